```python
import jax, jax.numpy as jnp
from jax import lax
import numpy as np

D_MODEL = 1024
BATCH = 32
SEQ = 256
DEPTH = 2
DEC_BATCH = 8
DEC_SEQ = 1024
PAST_LEN = 512

GRID_W = 64
N_MIXERS = 2
N_HGRN_LAYERS = (DEPTH + N_MIXERS - 1) // N_MIXERS
N_NA_LAYERS = DEPTH // N_MIXERS
HG_HEADS = 8
HG_DK = 128
HG_DV = D_MODEL // HG_HEADS
HG_KDIM = HG_HEADS * HG_DK
HG_VDIM = HG_HEADS * HG_DV
CHUNK = 64
NA_HEADS = 16
NA_HD = D_MODEL // NA_HEADS
WIN_R = 8
WIN_C = 16
N_COL_BLOCKS = GRID_W // WIN_C
KEY_COLS = 2 * WIN_C
N_EXPERTS = 64
TOP_K = 6
N_GROUPS = 8
TOPK_GROUPS = 4
D_EXPERT = 256
D_SHARED = 256
ROUTED_SCALE = 2.5
EPS = 1e-6

kernel_name = "hybrid_hgrn2_natten_moe_diffusion_step"


def _rmsnorm(x, w):
    x32 = x.astype(jnp.float32)
    y = x32 * lax.rsqrt(jnp.mean(x32 * x32, axis=-1, keepdims=True) + EPS)
    return (y * w.astype(jnp.float32)).astype(x.dtype)


def _modulation(cvec, ada_w, ada_b):
    m = jax.nn.silu(cvec) @ ada_w + ada_b
    return jnp.split(m[:, None, :], 6, axis=-1)


def _gla_scan(q, k, v, g, s0):
    B, T, H, K = q.shape
    V = v.shape[-1]
    n = T // CHUNK

    def to_chunks(a):
        return a.reshape(B, n, CHUNK, H, a.shape[-1]).transpose(1, 0, 3, 2, 4)

    tril = jnp.tril(jnp.ones((CHUNK, CHUNK), dtype=bool))[None, None, :, :, None]

    def step(S, inp):
        qc, kc, vc, gc = inp
        b = jnp.cumsum(gc, axis=2)
        o_inter = jnp.einsum('bhck,bhkv->bhcv', qc * jnp.exp(b), S)
        diff = b[:, :, :, None, :] - b[:, :, None, :, :]
        dec = jnp.exp(jnp.where(tril, diff, -jnp.inf))
        A = jnp.einsum('bhtk,bhsk,bhtsk->bhts', qc, kc, dec)
        o = o_inter + jnp.einsum('bhts,bhsv->bhtv', A, vc)
        b_last = b[:, :, -1:, :]
        S = jnp.exp(b_last[:, :, 0, :])[..., None] * S + jnp.einsum('bhsk,bhsv->bhkv', kc * jnp.exp(b_last - b), vc)
        return S, o

    S, o = lax.scan(step, s0, (to_chunks(q), to_chunks(k), to_chunks(v), to_chunks(g)))
    return o.transpose(1, 0, 3, 2, 4).reshape(B, T, H, V), S


def _hgrn_mixer(h, w_in, w_out, gn_w, lb, s0_f, s0_b):
    B, T, _ = h.shape
    proj = (h @ w_in).astype(jnp.float32)
    q, f_fw, f_bw, i_in, gate = jnp.split(
        proj, [HG_KDIM, 2 * HG_KDIM, 3 * HG_KDIM, 3 * HG_KDIM + HG_VDIM], axis=-1)
    q = q.reshape(B, T, HG_HEADS, HG_DK)
    v = i_in.reshape(B, T, HG_HEADS, HG_DV)
    lbh = lb.reshape(HG_HEADS, HG_DK).astype(jnp.float32)

    def key_and_log_decay(f_raw):
        f = lbh + (1.0 - lbh) * jax.nn.sigmoid(f_raw.reshape(B, T, HG_HEADS, HG_DK))
        return 1.0 - f, jnp.log(f)

    k_f, g_f = key_and_log_decay(f_fw)
    k_b, g_b = key_and_log_decay(f_bw)
    o_f, s_f = _gla_scan(q, k_f, v, g_f, s0_f.astype(jnp.float32))
    o_b, s_b = _gla_scan(q[:, ::-1], k_b[:, ::-1], v[:, ::-1], g_b[:, ::-1], s0_b.astype(jnp.float32))
    o = o_f + o_b[:, ::-1]
    o = o * lax.rsqrt(jnp.mean(o * o, axis=-1, keepdims=True) + EPS) * gn_w.astype(jnp.float32)
    o = o.reshape(B, T, HG_VDIM) * jax.nn.silu(gate)
    return o.astype(h.dtype) @ w_out, s_f.astype(h.dtype), s_b.astype(h.dtype)


def _na_context(h, w_qkv, w_o):
    B, S, _ = h.shape
    q, k, v = jnp.split(h @ w_qkv, 3, axis=-1)
    q = q.reshape(B, S, NA_HEADS, NA_HD) * (NA_HD ** -0.5)
    k = k.reshape(B, S, NA_HEADS, NA_HD)
    v = v.reshape(B, S, NA_HEADS, NA_HD)
    s = jnp.einsum('bqhd,bkhd->bhqk', q, k).astype(jnp.float32)
    p = jax.nn.softmax(s, axis=-1).astype(v.dtype)
    o = jnp.einsum('bhqk,bkhd->bqhd', p, v).reshape(B, S, D_MODEL)
    return o @ w_o, k, v


def _na_latent(h, w_qkv, w_o, rpb, k_ctx, v_ctx):
    B, T, _ = h.shape
    rows = T // GRID_W
    kr = min(WIN_R, rows)
    n_loc = kr * KEY_COLS
    q, k, v = jnp.split(h @ w_qkv, 3, axis=-1)
    q = q.reshape(B, rows, N_COL_BLOCKS, WIN_C, NA_HEADS, NA_HD) * (NA_HD ** -0.5)
    k = k.reshape(B, T, NA_HEADS, NA_HD)
    v = v.reshape(B, T, NA_HEADS, NA_HD)
    r = np.arange(rows)
    key_rows = np.clip(r - kr // 2, 0, rows - kr)[:, None] + np.arange(kr)
    jb = np.arange(N_COL_BLOCKS)
    key_cols = np.clip(jb * WIN_C - WIN_C // 2, 0, GRID_W - KEY_COLS)[:, None] + np.arange(KEY_COLS)
    q_cols = jb[:, None] * WIN_C + np.arange(WIN_C)
    win_start = np.clip(q_cols - WIN_C // 2, 0, GRID_W - WIN_C)
    col_ok = (key_cols[:, None, :] >= win_start[:, :, None]) & (key_cols[:, None, :] < win_start[:, :, None] + WIN_C)
    mask = np.broadcast_to(col_ok[:, :, None, :], (N_COL_BLOCKS, WIN_C, kr, KEY_COLS)).reshape(N_COL_BLOCKS, WIN_C, n_loc)
    idx = (key_rows[:, None, :, None] * GRID_W + key_cols[None, :, None, :]).reshape(rows, N_COL_BLOCKS, n_loc)
    dr = key_rows - r[:, None] + WIN_R - 1
    dc = np.clip(key_cols[:, None, :] - q_cols[:, :, None] + WIN_C - 1, 0, 2 * WIN_C - 2)
    bias = rpb[:, dr[:, None, None, :, None], dc[None, :, :, None, :]].reshape(
        NA_HEADS, rows, N_COL_BLOCKS, WIN_C, n_loc).astype(jnp.float32)
    kg = jnp.take(k, idx, axis=1)
    vg = jnp.take(v, idx, axis=1)
    s_loc = jnp.einsum('brjuhd,brjlhd->bhrjul', q, kg).astype(jnp.float32) + bias
    s_loc = jnp.where(mask, s_loc, -jnp.inf)
    s_ctx = jnp.einsum('brjuhd,bphd->bhrjup', q, k_ctx).astype(jnp.float32)
    p = jax.nn.softmax(jnp.concatenate([s_loc, s_ctx], axis=-1), axis=-1).astype(v.dtype)
    o = (jnp.einsum('bhrjul,brjlhd->brjuhd', p[..., :n_loc], vg)
         + jnp.einsum('bhrjup,bphd->brjuhd', p[..., n_loc:], v_ctx))
    return o.reshape(B, T, D_MODEL) @ w_o


def _moe(h, w_router, router_bias, w_gate, w_up, w_down, ws_gate, ws_up, ws_down):
    B, T, D = h.shape
    x = h.reshape(B * T, D)
    scores = jax.nn.sigmoid(x.astype(jnp.float32) @ w_router.astype(jnp.float32))
    sel = scores + router_bias.astype(jnp.float32)
    grp = sel.reshape(-1, N_GROUPS, N_EXPERTS // N_GROUPS)
    grp_score = lax.top_k(grp, 2)[0].sum(-1)
    _, gidx = lax.top_k(grp_score, TOPK_GROUPS)
    gmask = jax.nn.one_hot(gidx, N_GROUPS).sum(1) > 0
    emask = jnp.repeat(gmask, N_EXPERTS // N_GROUPS, axis=1)
    _, eidx = lax.top_k(jnp.where(emask, sel, -jnp.inf), TOP_K)
    wsel = jnp.take_along_axis(scores, eidx, axis=1)
    wsel = wsel / jnp.sum(wsel, axis=-1, keepdims=True) * ROUTED_SCALE
    combine = jnp.einsum('nk,nke->ne', wsel, jax.nn.one_hot(eidx, N_EXPERTS, dtype=jnp.float32)).astype(h.dtype)
    hid = jax.nn.silu(jnp.einsum('nd,edf->nef', x, w_gate)) * jnp.einsum('nd,edf->nef', x, w_up)
    routed = jnp.einsum('nef,efd->nd', hid * combine[:, :, None], w_down)
    shared = (jax.nn.silu(x @ ws_gate) * (x @ ws_up)) @ ws_down
    return (routed + shared).reshape(B, T, D)


def setup_inputs(seed: int = 0) -> dict:
    key = jax.random.key(seed)
    ks = iter(jax.random.split(key, 40))

    def nrm(shape, scale):
        return jax.random.normal(next(ks), shape, jnp.float32) * scale

    def gain(shape):
        return 1.0 + nrm(shape, 0.02)

    return {
        "x_prompt": nrm((BATCH, SEQ, D_MODEL), 1.0),
        "x_sample": nrm((DEC_BATCH, DEC_SEQ, D_MODEL), 1.0),
        "state_hgrn_fwd": nrm((DEC_BATCH, N_HGRN_LAYERS, HG_HEADS, HG_DK, HG_DV), 0.5),
        "state_hgrn_bwd": nrm((DEC_BATCH, N_HGRN_LAYERS, HG_HEADS, HG_DK, HG_DV), 0.5),
        "cache_na_k": nrm((DEC_BATCH, N_NA_LAYERS, PAST_LEN, NA_HEADS, NA_HD), 1.0),
        "cache_na_v": nrm((DEC_BATCH, N_NA_LAYERS, PAST_LEN, NA_HEADS, NA_HD), 1.0),
        "c": nrm((DEC_BATCH, D_MODEL), 1.0),
        "c_ctx": nrm((D_MODEL,), 1.0),
        "norm1_w": gain((DEPTH, D_MODEL)),
        "norm2_w": gain((DEPTH, D_MODEL)),
        "ada_w": nrm((DEPTH, D_MODEL, 6 * D_MODEL), 0.5 * D_MODEL ** -0.5),
        "ada_b": nrm((DEPTH, 6 * D_MODEL), 0.02),
        "hgrn_w_in": nrm((N_HGRN_LAYERS, D_MODEL, 3 * HG_KDIM + 2 * HG_VDIM), D_MODEL ** -0.5),
        "hgrn_lb_logits": nrm((DEPTH + 1, HG_KDIM), 0.1),
        "hgrn_gn_w": gain((N_HGRN_LAYERS, HG_DV)),
        "hgrn_w_out": nrm((N_HGRN_LAYERS, HG_VDIM, D_MODEL), HG_VDIM ** -0.5),
        "na_w_qkv": nrm((N_NA_LAYERS, D_MODEL, 3 * D_MODEL), D_MODEL ** -0.5),
        "na_rpb": nrm((N_NA_LAYERS, NA_HEADS, 2 * WIN_R - 1, 2 * WIN_C - 1), 0.1),
        "na_w_o": nrm((N_NA_LAYERS, D_MODEL, D_MODEL), D_MODEL ** -0.5),
        "moe_w_router": nrm((DEPTH, D_MODEL, N_EXPERTS), D_MODEL ** -0.5),
        "moe_router_bias": nrm((DEPTH, N_EXPERTS), 0.01),
        "moe_w_gate": nrm((DEPTH, N_EXPERTS, D_MODEL, D_EXPERT), D_MODEL ** -0.5),
        "moe_w_up": nrm((DEPTH, N_EXPERTS, D_MODEL, D_EXPERT), D_MODEL ** -0.5),
        "moe_w_down": nrm((DEPTH, N_EXPERTS, D_EXPERT, D_MODEL), D_EXPERT ** -0.5),
        "shared_w_gate": nrm((DEPTH, D_MODEL, D_SHARED), D_MODEL ** -0.5),
        "shared_w_up": nrm((DEPTH, D_MODEL, D_SHARED), D_MODEL ** -0.5),
        "shared_w_down": nrm((DEPTH, D_SHARED, D_MODEL), D_SHARED ** -0.5),
        "final_norm_w": gain((D_MODEL,)),
    }


def reference(x_prompt, x_sample, state_hgrn_fwd, state_hgrn_bwd, cache_na_k, cache_na_v, c, c_ctx,
              norm1_w, norm2_w, ada_w, ada_b, hgrn_w_in, hgrn_lb_logits, hgrn_gn_w, hgrn_w_out,
              na_w_qkv, na_rpb, na_w_o, moe_w_router, moe_router_bias, moe_w_gate, moe_w_up, moe_w_down,
              shared_w_gate, shared_w_up, shared_w_down, final_norm_w):
    xp, xs = x_prompt, x_sample
    lb_table = jnp.cumsum(jax.nn.softmax(hgrn_lb_logits.astype(jnp.float32), axis=0), axis=0)
    sf_list, sb_list, k_list, v_list = [], [], [], []
    for l in range(DEPTH):
        sh1p, sc1p, g1p, sh2p, sc2p, g2p = _modulation(c_ctx[None, :], ada_w[l], ada_b[l])
        sh1s, sc1s, g1s, sh2s, sc2s, g2s = _modulation(c, ada_w[l], ada_b[l])
        hp = _rmsnorm(xp, norm1_w[l]) * (1 + sc1p) + sh1p
        hs = _rmsnorm(xs, norm1_w[l]) * (1 + sc1s) + sh1s
        if l % N_MIXERS == 0:
            a = l // N_MIXERS
            zero = jnp.zeros((xp.shape[0], HG_HEADS, HG_DK, HG_DV), jnp.float32)
            op, s_f, s_b = _hgrn_mixer(hp, hgrn_w_in[a], hgrn_w_out[a], hgrn_gn_w[a], lb_table[l], zero, zero)
            os_, _, _ = _hgrn_mixer(hs, hgrn_w_in[a], hgrn_w_out[a], hgrn_gn_w[a], lb_table[l],
                                    state_hgrn_fwd[:, a], state_hgrn_bwd[:, a])
            sf_list.append(s_f)
            sb_list.append(s_b)
        else:
            n = l // N_MIXERS
            op, k_c, v_c = _na_context(hp, na_w_qkv[n], na_w_o[n])
            os_ = _na_latent(hs, na_w_qkv[n], na_w_o[n], na_rpb[n], cache_na_k[:, n], cache_na_v[:, n])
            k_list.append(k_c)
            v_list.append(v_c)
        xp = xp + g1p * op
        xs = xs + g1s * os_
        hp = _rmsnorm(xp, norm2_w[l]) * (1 + sc2p) + sh2p
        hs = _rmsnorm(xs, norm2_w[l]) * (1 + sc2s) + sh2s
        moe_args = (moe_w_router[l], moe_router_bias[l], moe_w_gate[l], moe_w_up[l], moe_w_down[l],
                    shared_w_gate[l], shared_w_up[l], shared_w_down[l])
        xp = xp + g2p * _moe(hp, *moe_args)
        xs = xs + g2s * _moe(hs, *moe_args)
    y_prompt = _rmsnorm(xp, final_norm_w)
    y_sample = _rmsnorm(xs, final_norm_w)
    new_state_hgrn_fwd = jnp.stack(sf_list, axis=1)
    new_state_hgrn_bwd = jnp.stack(sb_list, axis=1)
    new_cache_na_k = jnp.stack(k_list, axis=1)
    new_cache_na_v = jnp.stack(v_list, axis=1)
    return (y_prompt, y_sample, new_state_hgrn_fwd, new_state_hgrn_bwd, new_cache_na_k, new_cache_na_v)
```

```python
import functools

import numpy as np
import jax
import jax.numpy as jnp
from jax import lax
from jax.experimental import pallas as pl
from jax.experimental.pallas import tpu as pltpu

F32 = jnp.float32
BF16 = jnp.bfloat16
I32 = jnp.int32

D = 1024
BATCH = 32
SEQ = 256
DEPTH = 2
DEC_BATCH = 8
DEC_SEQ = 1024
PAST_LEN = 512
GRID_W = 64
HG_HEADS = 8
HG_DK = 128
HG_DV = 128
CHUNK = 64
NA_HEADS = 16
NA_HD = 64
WIN_R = 8
WIN_C = 16
N_EXPERTS = 64
TOP_K = 6
N_GROUPS = 8
TOPK_GROUPS = 4
D_EXPERT = 256
D_SHARED = 256
ROUTED_SCALE = 2.5
EPS = 1e-6

NP = BATCH * SEQ
NS = DEC_BATCH * DEC_SEQ
NT = NP + NS
N_MOD = 16
NEG = -1e30

LANES = 128
SUBLANES = 8
BF16_ROWS = 16
VMEM_LIMIT = 56 * 1024 * 1024

TM = 256
SUB = 16
MOE_TM = 256
PIECE = BF16_ROWS
R1 = MOE_TM * TOP_K + N_EXPERTS * (PIECE - 1) + 64
assert R1 % PIECE == 0 and R1 % LANES == 0
R1B = R1 // PIECE
N_TILES = NT // MOE_TM
P_MAX = N_TILES * R1B
G_PIECES = 16
NCH = P_MAX // G_PIECES + N_EXPERTS


def _cparams(sem):
    return pltpu.CompilerParams(dimension_semantics=sem, vmem_limit_bytes=VMEM_LIMIT)


def _dot(a, b):
    return jnp.dot(a, b, preferred_element_type=F32)


def _dot_nt(a, b):
    return lax.dot_general(a, b, (((1,), (1,)), ((), ())), preferred_element_type=F32)


def _dot_tn(a, b):
    return lax.dot_general(a, b, (((0,), (0,)), ((), ())), preferred_element_type=F32)


def _silu(x):
    return x * jax.nn.sigmoid(x)


def _mod_group(i, tm):
    r = i * tm
    return jnp.where(r < NP, 0, 1 + (r - NP) // DEC_SEQ)


def _norm_mod(x, nw, shift, scale):
    ms = jnp.mean(x * x, axis=-1, keepdims=True)
    y = x * lax.rsqrt(ms + EPS) * nw
    return y * (1.0 + scale) + shift


def _mod_kernel(c_ref, w_ref, b_ref, o_ref):
    s = _silu(c_ref[...]).astype(BF16)
    o_ref[0] = _dot(s, w_ref[0].astype(BF16)) + b_ref[0]


def _modulation(cvec, ada_w, ada_b):
    cw = 1536
    n = ada_w.shape[-1]
    out = pl.pallas_call(
        _mod_kernel,
        grid=(DEPTH, n // cw),
        in_specs=[
            pl.BlockSpec((N_MOD, D), lambda l, j: (0, 0)),
            pl.BlockSpec((1, D, cw), lambda l, j: (l, 0, j)),
            pl.BlockSpec((1, 1, cw), lambda l, j: (l, 0, j)),
        ],
        out_specs=pl.BlockSpec((1, N_MOD, cw), lambda l, j: (l, 0, j)),
        out_shape=jax.ShapeDtypeStruct((DEPTH, N_MOD, n), F32),
        compiler_params=_cparams(("arbitrary", "arbitrary")),
        name="modulation",
    )(cvec, ada_w, ada_b.reshape(DEPTH, 1, n))
    return out.reshape(DEPTH, N_MOD, 6, D)


def _proj_kernel(x_ref, nw_ref, mod_ref, w_ref, o_ref, *, cw):
    h = _norm_mod(x_ref[...], nw_ref[...], mod_ref[0, 0:1, :], mod_ref[0, 1:2, :]).astype(BF16)
    for j in range(w_ref.shape[1] // cw):
        o_ref[:, j * cw:(j + 1) * cw] = _dot(h, w_ref[:, j * cw:(j + 1) * cw])


def _norm_proj(x, nw, mod, w):
    n = w.shape[1]
    return pl.pallas_call(
        functools.partial(_proj_kernel, cw=512),
        grid=(NT // TM,),
        in_specs=[
            pl.BlockSpec((TM, D), lambda i: (i, 0)),
            pl.BlockSpec((1, D), lambda i: (0, 0)),
            pl.BlockSpec((1, 6, D), lambda i: (_mod_group(i, TM), 0, 0)),
            pl.BlockSpec((D, n), lambda i: (0, 0)),
        ],
        out_specs=pl.BlockSpec((TM, n), lambda i: (i, 0)),
        out_shape=jax.ShapeDtypeStruct((NT, n), F32),
        compiler_params=_cparams(("arbitrary",)),
        name="norm_proj",
    )(x, nw.reshape(1, D), mod, w)


def _split3(x):
    hi = x.astype(BF16)
    r = x - hi.astype(F32)
    mid = r.astype(BF16)
    lo = (r - mid.astype(F32)).astype(BF16)
    return hi, mid, lo


def _gla_direction(q, fraw, v, lb, st_ref, rev):
    f = lb + (1.0 - lb) * jax.nn.sigmoid(fraw)
    kk = 1.0 - f
    g = jnp.log(f)

    t_io = lax.broadcasted_iota(I32, (2 * CHUNK, CHUNK), 0)
    u_io = lax.broadcasted_iota(I32, (2 * CHUNK, CHUNK), 1)
    tt = jnp.where(t_io < CHUNK, t_io, t_io - CHUNK)
    blk = (tt // SUB) * SUB
    if rev:
        m_cum = u_io >= tt
        m_ref = u_io >= blk + SUB
    else:
        m_cum = u_io <= tt
        m_ref = u_io < blk
    top = t_io < CHUNK
    pick = jnp.logical_or(jnp.logical_and(top, m_cum), jnp.logical_and(jnp.logical_not(top), m_ref))
    tri = jnp.where(pick, 1.0, 0.0).astype(BF16)
    hi, mid, lo = _split3(g)
    cs = _dot(tri, hi) + _dot(tri, mid) + _dot(tri, lo)
    b = cs[:CHUNK]
    r = cs[CHUNK:]
    last = 0 if rev else CHUNK - 1
    tot = b[last:last + 1]

    qe = (q * jnp.exp(b)).astype(BF16)
    qd = (q * jnp.exp(b - r)).astype(BF16)
    kdec = (kk * jnp.exp(tot - b)).astype(BF16)
    vb = v.astype(BF16)
    dec_tot = jnp.exp(tot)

    s_io = lax.broadcasted_iota(I32, (CHUNK, LANES), 0)
    n_blk = CHUNK // SUB
    o_heads = []
    for h in range(HG_HEADS):
        sl = slice(h * LANES, (h + 1) * LANES)
        st = st_ref[sl, :]
        o_h = _dot_nt(qe[:, sl], st.astype(BF16))
        rows = []
        for i in range(n_blk):
            empty = (i == n_blk - 1) if rev else (i == 0)
            if empty:
                rows.append(jnp.zeros((SUB, CHUNK), F32))
                continue
            r_i = r[i * SUB:i * SUB + 1, sl]
            before = (s_io >= (i + 1) * SUB) if rev else (s_io < i * SUB)
            kd = jnp.where(before, kk[:, sl] * jnp.exp(r_i - b[:, sl]), 0.0).astype(BF16)
            rows.append(_dot_nt(qd[i * SUB:(i + 1) * SUB, sl], kd))
        a1 = jnp.concatenate(rows, axis=0).astype(BF16)
        o_h = o_h + _dot(a1, vb[:, sl])
        o_heads.append(o_h)
        st_ref[sl, :] = st * dec_tot[:, sl] + _dot_tn(vb[:, sl], kdec[:, sl])
    o = jnp.concatenate(o_heads, axis=1)

    row8 = lax.broadcasted_iota(I32, (SUBLANES, 1), 0)
    groups = []
    for tb in range(CHUNK // SUBLANES):
        t0 = tb * SUBLANES
        blk0 = (t0 // SUB) * SUB
        q_t = q[t0:t0 + SUBLANES]
        b_t = b[t0:t0 + SUBLANES]
        acc = jnp.zeros((SUBLANES, HG_HEADS * LANES), F32)
        s_range = range(t0, blk0 + SUB) if rev else range(blk0, t0 + SUBLANES)
        for s in s_range:
            e = jnp.exp(b_t - b[s:s + 1])
            if t0 <= s < t0 + SUBLANES:
                keep = (row8 <= s - t0) if rev else (row8 >= s - t0)
                e = jnp.where(keep, e, 0.0)
            p = q_t * e * kk[s:s + 1]
            red = jnp.concatenate(
                [jnp.broadcast_to(jnp.sum(p[:, h * LANES:(h + 1) * LANES], axis=-1, keepdims=True),
                                  (SUBLANES, LANES)) for h in range(HG_HEADS)], axis=1)
            acc = acc + red * v[s:s + 1]
        groups.append(acc)
    return o + jnp.concatenate(groups, axis=0)


def _gla_kernel(qf_ref, ff_ref, vf_ref, qb_ref, fb_ref, vb_ref, lb_ref, s0f_ref, s0b_ref,
                of_ref, ob_ref, sf_ref, sb_ref, stf, stb):
    i = pl.program_id(0)
    n_p = NP // CHUNK
    is_prompt = i < n_p
    c = jnp.where(is_prompt, i % (SEQ // CHUNK), (i - n_p) % (DEC_SEQ // CHUNK))
    n_c = jnp.where(is_prompt, SEQ // CHUNK, DEC_SEQ // CHUNK)

    @pl.when(jnp.logical_and(c == 0, is_prompt))
    def _():
        stf[...] = jnp.zeros_like(stf)
        stb[...] = jnp.zeros_like(stb)

    @pl.when(jnp.logical_and(c == 0, jnp.logical_not(is_prompt)))
    def _():
        for h in range(HG_HEADS):
            sl = slice(h * LANES, (h + 1) * LANES)
            stf[sl, :] = s0f_ref[0, sl, :].T
            stb[sl, :] = s0b_ref[0, sl, :].T

    lb = lb_ref[...]
    of_ref[...] = _gla_direction(qf_ref[...], ff_ref[...], vf_ref[...], lb, stf, False)
    ob_ref[...] = _gla_direction(qb_ref[...], fb_ref[...], vb_ref[...], lb, stb, True)

    @pl.when(jnp.logical_and(c == n_c - 1, is_prompt))
    def _():
        for h in range(HG_HEADS):
            sl = slice(h * LANES, (h + 1) * LANES)
            sf_ref[0, sl, :] = stf[sl, :].T
            sb_ref[0, sl, :] = stb[sl, :].T


def _gla(proj, lb, s0f, s0b):
    n_p = NP // CHUNK
    cp = SEQ // CHUNK
    cs = DEC_SEQ // CHUNK

    def bwd_blk(i):
        jp = (i // cp) * cp + (cp - 1 - i % cp)
        j = i - n_p
        js = n_p + (j // cs) * cs + (cs - 1 - j % cs)
        return jnp.where(i < n_p, jp, js)

    def req(i):
        return jnp.maximum(i - n_p, 0) // cs

    def preq(i):
        return jnp.minimum(i // cp, BATCH - 1)

    hk = HG_HEADS * HG_DK
    return pl.pallas_call(
        _gla_kernel,
        grid=(NT // CHUNK,),
        in_specs=[
            pl.BlockSpec((CHUNK, D), lambda i: (i, 0)),
            pl.BlockSpec((CHUNK, D), lambda i: (i, 1)),
            pl.BlockSpec((CHUNK, D), lambda i: (i, 3)),
            pl.BlockSpec((CHUNK, D), lambda i: (bwd_blk(i), 0)),
            pl.BlockSpec((CHUNK, D), lambda i: (bwd_blk(i), 2)),
            pl.BlockSpec((CHUNK, D), lambda i: (bwd_blk(i), 3)),
            pl.BlockSpec((1, hk), lambda i: (0, 0)),
            pl.BlockSpec((1, hk, HG_DV), lambda i: (req(i), 0, 0)),
            pl.BlockSpec((1, hk, HG_DV), lambda i: (req(i), 0, 0)),
        ],
        out_specs=[
            pl.BlockSpec((CHUNK, D), lambda i: (i, 0)),
            pl.BlockSpec((CHUNK, D), lambda i: (bwd_blk(i), 0)),
            pl.BlockSpec((1, hk, HG_DV), lambda i: (preq(i), 0, 0)),
            pl.BlockSpec((1, hk, HG_DV), lambda i: (preq(i), 0, 0)),
        ],
        out_shape=[
            jax.ShapeDtypeStruct((NT, D), F32),
            jax.ShapeDtypeStruct((NT, D), F32),
            jax.ShapeDtypeStruct((BATCH, hk, HG_DV), F32),
            jax.ShapeDtypeStruct((BATCH, hk, HG_DV), F32),
        ],
        scratch_shapes=[pltpu.VMEM((hk, HG_DV), F32), pltpu.VMEM((hk, HG_DV), F32)],
        compiler_params=_cparams(("arbitrary",)),
        name="gla_scan",
    )(proj, proj, proj, proj, proj, proj, lb.reshape(1, hk), s0f, s0b)


def _hgrn_out_kernel(of_ref, ob_ref, gate_ref, x_ref, mod_ref, gn_ref, w_ref, o_ref):
    o = of_ref[...] + ob_ref[...]
    gn = gn_ref[...]
    segs = []
    for h in range(HG_HEADS):
        seg = o[:, h * LANES:(h + 1) * LANES]
        ms = jnp.mean(seg * seg, axis=-1, keepdims=True)
        segs.append(seg * lax.rsqrt(ms + EPS) * gn)
    y = (jnp.concatenate(segs, axis=1) * _silu(gate_ref[...])).astype(BF16)
    o_ref[...] = x_ref[...] + mod_ref[0, 2:3, :] * _dot(y, w_ref[...])


def _hgrn_out(o_f, o_b, proj, x, mod, gn_w, w_out):
    return pl.pallas_call(
        _hgrn_out_kernel,
        grid=(NT // TM,),
        in_specs=[
            pl.BlockSpec((TM, D), lambda i: (i, 0)),
            pl.BlockSpec((TM, D), lambda i: (i, 0)),
            pl.BlockSpec((TM, D), lambda i: (i, 4)),
            pl.BlockSpec((TM, D), lambda i: (i, 0)),
            pl.BlockSpec((1, 6, D), lambda i: (_mod_group(i, TM), 0, 0)),
            pl.BlockSpec((1, HG_DV), lambda i: (0, 0)),
            pl.BlockSpec((D, D), lambda i: (0, 0)),
        ],
        out_specs=pl.BlockSpec((TM, D), lambda i: (i, 0)),
        out_shape=jax.ShapeDtypeStruct((NT, D), F32),
        compiler_params=_cparams(("arbitrary",)),
        name="hgrn_out",
    )(o_f, o_b, proj, x, mod, gn_w.reshape(1, HG_DV), w_out)


def _attn_out_kernel(ap_ref, as_ref, x_ref, mod_ref, w_ref, o_ref):
    a = jnp.where(pl.program_id(0) < NP // TM, ap_ref[...], as_ref[...]).astype(BF16)
    o_ref[...] = x_ref[...] + mod_ref[0, 2:3, :] * _dot(a, w_ref[...])


def _attn_out(a_p, a_s, x, mod, w_o):
    n_p = NP // TM
    return pl.pallas_call(
        _attn_out_kernel,
        grid=(NT // TM,),
        in_specs=[
            pl.BlockSpec((TM, D), lambda i: (jnp.minimum(i, n_p - 1), 0)),
            pl.BlockSpec((TM, D), lambda i: (jnp.maximum(i - n_p, 0), 0)),
            pl.BlockSpec((TM, D), lambda i: (i, 0)),
            pl.BlockSpec((1, 6, D), lambda i: (_mod_group(i, TM), 0, 0)),
            pl.BlockSpec((D, D), lambda i: (0, 0)),
        ],
        out_specs=pl.BlockSpec((TM, D), lambda i: (i, 0)),
        out_shape=jax.ShapeDtypeStruct((NT, D), F32),
        compiler_params=_cparams(("arbitrary",)),
        name="attn_out",
    )(a_p, a_s, x, mod, w_o)


def _head_pair_masks():
    upper = lax.broadcasted_iota(I32, (1, LANES), 1) >= NA_HD
    return upper


def _attn_ctx_kernel(q_ref, k_ref, v_ref, o_ref):
    upper = _head_pair_masks()
    for p in range(NA_HEADS // 2):
        sl = slice(p * LANES, (p + 1) * LANES)
        qp = q_ref[:, sl] * (NA_HD ** -0.5)
        kp = k_ref[:, sl].astype(BF16)
        vp = v_ref[:, sl].astype(BF16)
        outs = []
        for u in range(2):
            qm = jnp.where(upper if u else jnp.logical_not(upper), qp, 0.0).astype(BF16)
            s = _dot_nt(qm, kp)
            e = jnp.exp(s - jnp.max(s, axis=-1, keepdims=True))
            outs.append(_dot(e.astype(BF16), vp) / jnp.sum(e, axis=-1, keepdims=True))
        o_ref[:, sl] = jnp.where(upper, outs[1], outs[0])


def _attn_ctx(qkv):
    return pl.pallas_call(
        _attn_ctx_kernel,
        grid=(BATCH,),
        in_specs=[
            pl.BlockSpec((SEQ, D), lambda b: (b, 0)),
            pl.BlockSpec((SEQ, D), lambda b: (b, 1)),
            pl.BlockSpec((SEQ, D), lambda b: (b, 2)),
        ],
        out_specs=pl.BlockSpec((SEQ, D), lambda b: (b, 0)),
        out_shape=jax.ShapeDtypeStruct((NP, D), F32),
        compiler_params=_cparams(("arbitrary",)),
        name="attn_ctx",
    )(qkv, qkv, qkv)


def _attn_lat_kernel(q_ref, k_ref, v_ref, kc_ref, vc_ref, bias_ref, o_ref, *, tq):
    upper = _head_pair_masks()
    kl = k_ref[...].astype(BF16)
    vl = v_ref[...].astype(BF16)
    kc = kc_ref[0].astype(BF16)
    vc = vc_ref[0].astype(BF16)
    for t in range(DEC_SEQ // tq):
        rs = slice(t * tq, (t + 1) * tq)
        qp = q_ref[rs, :] * (NA_HD ** -0.5)
        outs = []
        for u in range(2):
            qm = jnp.where(upper if u else jnp.logical_not(upper), qp, 0.0).astype(BF16)
            s_l = _dot_nt(qm, kl) + bias_ref[u, rs, :]
            s_c = _dot_nt(qm, kc)
            m = jnp.maximum(jnp.max(s_l, axis=-1, keepdims=True), jnp.max(s_c, axis=-1, keepdims=True))
            e_l = jnp.exp(s_l - m)
            e_c = jnp.exp(s_c - m)
            den = jnp.sum(e_l, axis=-1, keepdims=True) + jnp.sum(e_c, axis=-1, keepdims=True)
            outs.append((_dot(e_l.astype(BF16), vl) + _dot(e_c.astype(BF16), vc)) / den)
        o_ref[rs, :] = jnp.where(upper, outs[1], outs[0])


def _attn_lat(qkv, k_ctx, v_ctx, bias):
    npair = NA_HEADS // 2
    rb0 = NP // DEC_SEQ
    return pl.pallas_call(
        functools.partial(_attn_lat_kernel, tq=256),
        grid=(npair, DEC_BATCH),
        in_specs=[
            pl.BlockSpec((DEC_SEQ, LANES), lambda p, b: (rb0 + b, p)),
            pl.BlockSpec((DEC_SEQ, LANES), lambda p, b: (rb0 + b, npair + p)),
            pl.BlockSpec((DEC_SEQ, LANES), lambda p, b: (rb0 + b, 2 * npair + p)),
            pl.BlockSpec((1, PAST_LEN, LANES), lambda p, b: (b, 0, p)),
            pl.BlockSpec((1, PAST_LEN, LANES), lambda p, b: (b, 0, p)),
            pl.BlockSpec((2, DEC_SEQ, DEC_SEQ), lambda p, b: (p, 0, 0)),
        ],
        out_specs=pl.BlockSpec((DEC_SEQ, LANES), lambda p, b: (b, p)),
        out_shape=jax.ShapeDtypeStruct((NS, D), F32),
        compiler_params=_cparams(("arbitrary", "arbitrary")),
        name="attn_lat",
    )(qkv, qkv, qkv, k_ctx, v_ctx, bias)


def _latent_bias(rpb):
    rows = DEC_SEQ // GRID_W
    kr = min(WIN_R, rows)
    ndc = 2 * WIN_C - 1
    qc = np.arange(GRID_W)
    kc = np.arange(GRID_W)
    ws = np.clip(qc - WIN_C // 2, 0, GRID_W - WIN_C)
    col_ok = (kc[None, :] >= ws[:, None]) & (kc[None, :] < ws[:, None] + WIN_C)
    dc = np.clip(kc[None, :] - qc[:, None] + WIN_C - 1, 0, ndc - 1)
    onehot = (dc[None, :, :] == np.arange(ndc)[:, None, None]).astype(np.float32)
    t2 = jnp.einsum('hrc,cqk->hrqk', rpb.astype(F32), jnp.asarray(onehot),
                    precision=lax.Precision.HIGHEST)
    t2 = jnp.where(jnp.asarray(col_ok)[None, None], t2, NEG)
    r = np.arange(rows)
    k0 = np.clip(r - kr // 2, 0, rows - kr)
    krow = np.arange(rows)
    row_ok = (krow[None, :] >= k0[:, None]) & (krow[None, :] < k0[:, None] + kr)
    dr = np.clip(krow[None, :] - r[:, None] + WIN_R - 1, 0, 2 * WIN_R - 2)
    g = jnp.take(t2, jnp.asarray(dr.reshape(-1)), axis=1).reshape(NA_HEADS, rows, rows, GRID_W, GRID_W)
    g = jnp.where(jnp.asarray(row_ok)[None, :, :, None, None], g, NEG)
    return g.transpose(0, 1, 3, 2, 4).reshape(NA_HEADS, DEC_SEQ, DEC_SEQ)


def _route_kernel(x_ref, nw_ref, mod_ref, wrh_ref, wrl_ref, rb_ref, xs_ref, pos_ref, wt_ref, npc_ref):
    h = _norm_mod(x_ref[...], nw_ref[...], mod_ref[0, 3:4, :], mod_ref[0, 4:5, :])
    hb = h.astype(BF16)
    hl = (h - hb.astype(F32)).astype(BF16)
    wrh = wrh_ref[...]
    logits = _dot_nt(wrh, hb) + _dot_nt(wrh, hl) + _dot_nt(wrl_ref[...], hb)
    scores = jax.nn.sigmoid(logits)
    sel = scores + rb_ref[...]

    gsz = N_EXPERTS // N_GROUPS
    sub = lax.broadcasted_iota(I32, (gsz, MOE_TM), 0)
    ninf = -jnp.inf
    gs_rows = []
    for gi in range(N_GROUPS):
        blk = sel[gi * gsz:(gi + 1) * gsz]
        m1 = jnp.max(blk, axis=0, keepdims=True)
        first = jnp.min(jnp.where(blk == m1, sub, gsz), axis=0, keepdims=True)
        m2 = jnp.max(jnp.where(sub == first, ninf, blk), axis=0, keepdims=True)
        gs_rows.append(m1 + m2)
    cur = jnp.concatenate(gs_rows, axis=0)
    gidx = lax.broadcasted_iota(I32, (N_GROUPS, MOE_TM), 0)
    gsel = jnp.zeros((N_GROUPS, MOE_TM), F32)
    for _ in range(TOPK_GROUPS):
        m = jnp.max(cur, axis=0, keepdims=True)
        first = jnp.min(jnp.where(cur == m, gidx, N_GROUPS), axis=0, keepdims=True)
        hit = gidx == first
        gsel = jnp.where(hit, 1.0, gsel)
        cur = jnp.where(hit, ninf, cur)
    emask = jnp.concatenate(
        [jnp.broadcast_to(gsel[gi:gi + 1], (gsz, MOE_TM)) for gi in range(N_GROUPS)], axis=0)
    masked = jnp.where(emask > 0.5, sel, ninf)
    eidx = lax.broadcasted_iota(I32, (N_EXPERTS, MOE_TM), 0)
    chosen = jnp.zeros((N_EXPERTS, MOE_TM), F32)
    hits, wsel = [], []
    for _ in range(TOP_K):
        m = jnp.max(masked, axis=0, keepdims=True)
        first = jnp.min(jnp.where(masked == m, eidx, N_EXPERTS), axis=0, keepdims=True)
        hit = eidx == first
        hits.append(hit)
        wsel.append(jnp.sum(jnp.where(hit, scores, 0.0), axis=0, keepdims=True))
        chosen = jnp.where(hit, 1.0, chosen)
        masked = jnp.where(hit, ninf, masked)
    wsum = wsel[0]
    for w in wsel[1:]:
        wsum = wsum + w

    n_io = lax.broadcasted_iota(I32, (MOE_TM, MOE_TM), 0)
    m_io = lax.broadcasted_iota(I32, (MOE_TM, MOE_TM), 1)
    earlier = jnp.where(n_io < m_io, 1.0, 0.0).astype(BF16)
    rank = _dot(chosen.astype(BF16), earlier)
    cnt = jnp.sum(chosen, axis=1, keepdims=True)
    npc = jnp.floor((cnt + (PIECE - 1)) * (1.0 / PIECE))
    e_io = lax.broadcasted_iota(I32, (N_EXPERTS, N_EXPERTS), 0)
    f_io = lax.broadcasted_iota(I32, (N_EXPERTS, N_EXPERTS), 1)
    below = jnp.where(f_io < e_io, 1.0, 0.0).astype(BF16)
    npc_l = jnp.broadcast_to(npc, (N_EXPERTS, LANES))
    start = _dot(below, npc_l.astype(BF16))[:, 0:1] * PIECE
    slot = start + rank

    j_io = lax.broadcasted_iota(I32, (R1, MOE_TM), 0)
    onehot = jnp.zeros((R1, MOE_TM), F32)
    pos_rows, wt_rows = [], []
    for k in range(TOP_K):
        pos_k = jnp.sum(jnp.where(hits[k], slot, 0.0), axis=0, keepdims=True).astype(I32)
        pos_rows.append(pos_k)
        wt_rows.append(wsel[k] / wsum * ROUTED_SCALE)
        onehot = jnp.where(j_io == pos_k, 1.0, onehot)
    xs_ref[...] = _dot(onehot.astype(BF16), hb).astype(BF16)
    pad = SUBLANES - TOP_K
    pos_ref[...] = jnp.concatenate(pos_rows + [jnp.full((pad, MOE_TM), -1, I32)], axis=0)
    wt_ref[...] = jnp.concatenate(wt_rows + [jnp.zeros((pad, MOE_TM), F32)], axis=0)
    npc_ref[0] = npc_l.astype(I32)


def _route(x, nw, mod, wr_hi, wr_lo, rbias):
    return pl.pallas_call(
        _route_kernel,
        grid=(N_TILES,),
        in_specs=[
            pl.BlockSpec((MOE_TM, D), lambda i: (i, 0)),
            pl.BlockSpec((1, D), lambda i: (0, 0)),
            pl.BlockSpec((1, 6, D), lambda i: (_mod_group(i, MOE_TM), 0, 0)),
            pl.BlockSpec((N_EXPERTS, D), lambda i: (0, 0)),
            pl.BlockSpec((N_EXPERTS, D), lambda i: (0, 0)),
            pl.BlockSpec((N_EXPERTS, 1), lambda i: (0, 0)),
        ],
        out_specs=[
            pl.BlockSpec((R1, D), lambda i: (i, 0)),
            pl.BlockSpec((SUBLANES, MOE_TM), lambda i: (0, i)),
            pl.BlockSpec((SUBLANES, MOE_TM), lambda i: (0, i)),
            pl.BlockSpec((1, N_EXPERTS, LANES), lambda i: (i, 0, 0)),
        ],
        out_shape=[
            jax.ShapeDtypeStruct((N_TILES * R1, D), BF16),
            jax.ShapeDtypeStruct((SUBLANES, NT), I32),
            jax.ShapeDtypeStruct((SUBLANES, NT), F32),
            jax.ShapeDtypeStruct((N_TILES, N_EXPERTS, LANES), I32),
        ],
        compiler_params=_cparams(("arbitrary",)),
        name="moe_route",
    )(x, nw.reshape(1, D), mod, wr_hi, wr_lo, rbias.reshape(N_EXPERTS, 1))


def _piece_lists(npc):
    t, e = npc.shape
    start = jnp.cumsum(npc, axis=1) - npc
    run_len = npc.T.reshape(-1)
    run_src = (jnp.arange(t, dtype=I32)[None, :] * R1B + start.T).reshape(-1)
    run_end = jnp.cumsum(run_len)
    p = jnp.arange(P_MAX, dtype=I32)
    run_id = jnp.minimum(jnp.searchsorted(run_end, p, side='right'), t * e - 1).astype(I32)
    src = (run_src[run_id] + (p - (run_end - run_len)[run_id])).astype(I32)
    src = jnp.clip(src, 0, P_MAX - 1)
    n_e = npc.sum(axis=0)
    pe_off = jnp.cumsum(n_e) - n_e
    nch = (n_e + G_PIECES - 1) // G_PIECES
    ch_end = jnp.cumsum(nch)
    c = jnp.arange(NCH, dtype=I32)
    ce = jnp.minimum(jnp.searchsorted(ch_end, c, side='right'), e - 1).astype(I32)
    k_in = c - (ch_end - nch)[ce]
    cs = (pe_off[ce] + G_PIECES * k_in).astype(I32)
    cn = jnp.where(c < ch_end[-1], jnp.clip(n_e[ce] - G_PIECES * k_in, 0, G_PIECES), 0).astype(I32)
    cs = jnp.where(cn > 0, cs, 0)
    return src, ce, cs, cn


def _ffn_kernel(src_ref, ce_ref, cs_ref, cn_ref, xs_in, wg_ref, wu_ref, wd_ref, xs_out,
                xbuf, ybuf, wgb, wub, wdb, gsem, ssem):
    c = pl.program_id(0)
    slot = lax.rem(c, 2)

    def gather_copy(ch, sl, i):
        return pltpu.make_async_copy(xs_in.at[src_ref[cs_ref[ch] + i]],
                                     xbuf.at[sl, pl.ds(i * PIECE, PIECE)], gsem.at[sl])

    def scatter_copy(ch, sl, i):
        return pltpu.make_async_copy(ybuf.at[sl, pl.ds(i * PIECE, PIECE)],
                                     xs_out.at[src_ref[cs_ref[ch] + i]], ssem.at[sl])

    def for_pieces(ch, fn):
        def body(i, carry):
            fn(i)
            return carry
        lax.fori_loop(0, cn_ref[ch], body, 0)

    @pl.when(c == 0)
    def _():
        xbuf[...] = jnp.zeros_like(xbuf)
        for_pieces(0, lambda i: gather_copy(0, 0, i).start())

    for_pieces(c, lambda i: gather_copy(c, slot, i).wait())

    @pl.when(c + 1 < NCH)
    def _():
        for_pieces(c + 1, lambda i: gather_copy(c + 1, 1 - slot, i).start())

    changed = jnp.logical_or(c == 0, ce_ref[c] != ce_ref[jnp.maximum(c - 1, 0)])

    @pl.when(changed)
    def _():
        wgb[...] = wg_ref[0].astype(BF16)
        wub[...] = wu_ref[0].astype(BF16)
        wdb[...] = wd_ref[0].astype(BF16)

    @pl.when(c >= 2)
    def _():
        for_pieces(c - 2, lambda i: scatter_copy(c - 2, slot, i).wait())

    @pl.when(cn_ref[c] > 0)
    def _():
        x = xbuf[slot]
        hid = (_silu(_dot(x, wgb[...])) * _dot(x, wub[...])).astype(BF16)
        ybuf[slot] = _dot(hid, wdb[...]).astype(BF16)

    for_pieces(c, lambda i: scatter_copy(c, slot, i).start())

    @pl.when(c == NCH - 1)
    def _():
        for_pieces(c - 1, lambda i: scatter_copy(c - 1, 1 - slot, i).wait())
        for_pieces(c, lambda i: scatter_copy(c, slot, i).wait())


def _expert_ffn(xs, lists, w_gate, w_up, w_down):
    src, ce, cs, cn = lists
    rows = G_PIECES * PIECE
    grid_spec = pltpu.PrefetchScalarGridSpec(
        num_scalar_prefetch=4,
        grid=(NCH,),
        in_specs=[
            pl.BlockSpec(memory_space=pl.ANY),
            pl.BlockSpec((1, D, D_EXPERT), lambda c, src, ce, cs, cn: (ce[c], 0, 0)),
            pl.BlockSpec((1, D, D_EXPERT), lambda c, src, ce, cs, cn: (ce[c], 0, 0)),
            pl.BlockSpec((1, D_EXPERT, D), lambda c, src, ce, cs, cn: (ce[c], 0, 0)),
        ],
        out_specs=pl.BlockSpec(memory_space=pl.ANY),
        scratch_shapes=[
            pltpu.VMEM((2, rows, D), BF16),
            pltpu.VMEM((2, rows, D), BF16),
            pltpu.VMEM((D, D_EXPERT), BF16),
            pltpu.VMEM((D, D_EXPERT), BF16),
            pltpu.VMEM((D_EXPERT, D), BF16),
            pltpu.SemaphoreType.DMA((2,)),
            pltpu.SemaphoreType.DMA((2,)),
        ],
    )
    out = pl.pallas_call(
        _ffn_kernel,
        grid_spec=grid_spec,
        out_shape=jax.ShapeDtypeStruct((P_MAX, PIECE, D), BF16),
        input_output_aliases={4: 0},
        compiler_params=_cparams(("arbitrary",)),
        name="moe_ffn",
    )(src, ce, cs, cn, xs.reshape(P_MAX, PIECE, D), w_gate, w_up, w_down)
    return out.reshape(N_TILES * R1, D)


def _combine_kernel(ys_ref, pos_ref, wt_ref, x_ref, nw_ref, mod_ref, wsg_ref, wsu_ref, wsd_ref, fw_ref,
                    o_ref, *, final):
    x = x_ref[...]
    h = _norm_mod(x, nw_ref[...], mod_ref[0, 3:4, :], mod_ref[0, 4:5, :]).astype(BF16)
    shared = _dot((_silu(_dot(h, wsg_ref[...])) * _dot(h, wsu_ref[...])).astype(BF16), wsd_ref[...])
    j_io = lax.broadcasted_iota(I32, (MOE_TM, R1), 1)
    pos = pos_ref[...]
    wt = wt_ref[...]
    wm = jnp.zeros((MOE_TM, R1), F32)
    for k in range(TOP_K):
        wm = jnp.where(j_io == pos[:, k:k + 1], wt[:, k:k + 1], wm)
    routed = _dot(wm.astype(BF16), ys_ref[...])
    y = x + mod_ref[0, 5:6, :] * (routed + shared)
    if final:
        ms = jnp.mean(y * y, axis=-1, keepdims=True)
        y = y * lax.rsqrt(ms + EPS) * fw_ref[...]
    o_ref[...] = y


def _combine(ys, pos_t, wt_t, x, nw, mod, wsg, wsu, wsd, fw, final):
    return pl.pallas_call(
        functools.partial(_combine_kernel, final=final),
        grid=(N_TILES,),
        in_specs=[
            pl.BlockSpec((R1, D), lambda i: (i, 0)),
            pl.BlockSpec((MOE_TM, SUBLANES), lambda i: (i, 0)),
            pl.BlockSpec((MOE_TM, SUBLANES), lambda i: (i, 0)),
            pl.BlockSpec((MOE_TM, D), lambda i: (i, 0)),
            pl.BlockSpec((1, D), lambda i: (0, 0)),
            pl.BlockSpec((1, 6, D), lambda i: (_mod_group(i, MOE_TM), 0, 0)),
            pl.BlockSpec((D, D_SHARED), lambda i: (0, 0)),
            pl.BlockSpec((D, D_SHARED), lambda i: (0, 0)),
            pl.BlockSpec((D_SHARED, D), lambda i: (0, 0)),
            pl.BlockSpec((1, D), lambda i: (0, 0)),
        ],
        out_specs=pl.BlockSpec((MOE_TM, D), lambda i: (i, 0)),
        out_shape=jax.ShapeDtypeStruct((NT, D), F32),
        compiler_params=_cparams(("arbitrary",)),
        name="moe_combine",
    )(ys, pos_t, wt_t, x, nw.reshape(1, D), mod, wsg, wsu, wsd, fw.reshape(1, D))


def _moe(x, nw, mod, w_router, rbias, w_gate, w_up, w_down, ws_gate, ws_up, ws_down, fw, final):
    wr = w_router.astype(F32).T
    wr_hi = wr.astype(BF16)
    wr_lo = (wr - wr_hi.astype(F32)).astype(BF16)
    xs, pos, wt, npc = _route(x, nw, mod, wr_hi, wr_lo, rbias.astype(F32))
    lists = _piece_lists(npc[:, :, 0])
    ys = _expert_ffn(xs, lists, w_gate, w_up, w_down)
    return _combine(ys, pos.T, wt.T, x, nw, mod, ws_gate.astype(BF16), ws_up.astype(BF16),
                    ws_down.astype(BF16), fw, final)


def kernel(x_prompt, x_sample, state_hgrn_fwd, state_hgrn_bwd, cache_na_k, cache_na_v, c, c_ctx,
           norm1_w, norm2_w, ada_w, ada_b, hgrn_w_in, hgrn_lb_logits, hgrn_gn_w, hgrn_w_out,
           na_w_qkv, na_rpb, na_w_o, moe_w_router, moe_router_bias, moe_w_gate, moe_w_up, moe_w_down,
           shared_w_gate, shared_w_up, shared_w_down, final_norm_w):
    x = jnp.concatenate([x_prompt.reshape(NP, D), x_sample.reshape(NS, D)], axis=0)
    cvec = jnp.concatenate([c_ctx[None, :], c, jnp.zeros((N_MOD - 1 - DEC_BATCH, D), F32)], axis=0)
    mod = _modulation(cvec, ada_w, ada_b)
    lb_table = jnp.cumsum(jax.nn.softmax(hgrn_lb_logits.astype(F32), axis=0), axis=0)
    hk = HG_HEADS * HG_DK

    sf = sb = k_c = v_c = None
    for l in range(DEPTH):
        if l % 2 == 0:
            a = l // 2
            proj = _norm_proj(x, norm1_w[l], mod[l], hgrn_w_in[a].astype(BF16))
            o_f, o_b, sf, sb = _gla(proj, lb_table[l],
                                    state_hgrn_fwd[:, a].reshape(DEC_BATCH, hk, HG_DV),
                                    state_hgrn_bwd[:, a].reshape(DEC_BATCH, hk, HG_DV))
            x = _hgrn_out(o_f, o_b, proj, x, mod[l], hgrn_gn_w[a], hgrn_w_out[a].astype(BF16))
        else:
            n = l // 2
            qkv = _norm_proj(x, norm1_w[l], mod[l], na_w_qkv[n].astype(BF16))
            att_p = _attn_ctx(qkv)
            att_s = _attn_lat(qkv, cache_na_k[:, n].reshape(DEC_BATCH, PAST_LEN, D),
                              cache_na_v[:, n].reshape(DEC_BATCH, PAST_LEN, D), _latent_bias(na_rpb[n]))
            x = _attn_out(att_p, att_s, x, mod[l], na_w_o[n].astype(BF16))
            k_c = qkv[:NP, D:2 * D].reshape(BATCH, SEQ, NA_HEADS, NA_HD)
            v_c = qkv[:NP, 2 * D:].reshape(BATCH, SEQ, NA_HEADS, NA_HD)
        x = _moe(x, norm2_w[l], mod[l], moe_w_router[l], moe_router_bias[l], moe_w_gate[l], moe_w_up[l],
                 moe_w_down[l], shared_w_gate[l], shared_w_up[l], shared_w_down[l], final_norm_w,
                 final=(l == DEPTH - 1))

    y_prompt = x[:NP].reshape(BATCH, SEQ, D)
    y_sample = x[NP:].reshape(DEC_BATCH, DEC_SEQ, D)
    new_sf = sf.reshape(BATCH, 1, HG_HEADS, HG_DK, HG_DV)
    new_sb = sb.reshape(BATCH, 1, HG_HEADS, HG_DK, HG_DV)
    return (y_prompt, y_sample, new_sf, new_sb, k_c[:, None], v_c[:, None])
```

```python
import functools

import numpy as np
import jax
import jax.numpy as jnp
from jax import lax
from jax.experimental import pallas as pl
from jax.experimental.pallas import tpu as pltpu

F32 = jnp.float32
BF16 = jnp.bfloat16
I32 = jnp.int32

D = 1024
BATCH = 32
SEQ = 256
DEPTH = 2
DEC_BATCH = 8
DEC_SEQ = 1024
PAST_LEN = 512
GRID_W = 64
HG_HEADS = 8
HG_DK = 128
HG_DV = 128
CHUNK = 64
NA_HEADS = 16
NA_HD = 64
WIN_R = 8
WIN_C = 16
N_EXPERTS = 64
TOP_K = 6
N_GROUPS = 8
TOPK_GROUPS = 4
D_EXPERT = 256
D_SHARED = 256
ROUTED_SCALE = 2.5
EPS = 1e-6

NP = BATCH * SEQ
NS = DEC_BATCH * DEC_SEQ
NT = NP + NS
N_MOD = 16
NEG = -1e30

LANES = 128
SUBLANES = 8
BF16_ROWS = 16
VMEM_LIMIT = 56 * 1024 * 1024

TM = 256
SUB = 16
MOE_TM = 256
PIECE = BF16_ROWS
R1 = MOE_TM * TOP_K + N_EXPERTS * (PIECE - 1) + 64
assert R1 % PIECE == 0 and R1 % LANES == 0
R1B = R1 // PIECE
N_TILES = NT // MOE_TM
P_MAX = N_TILES * R1B
G_PIECES = 16
NCH = P_MAX // G_PIECES + N_EXPERTS


def _cparams(sem):
    return pltpu.CompilerParams(dimension_semantics=sem, vmem_limit_bytes=VMEM_LIMIT)


def _dot(a, b):
    return jnp.dot(a, b, preferred_element_type=F32)


def _dot_nt(a, b):
    return lax.dot_general(a, b, (((1,), (1,)), ((), ())), preferred_element_type=F32)


def _dot_tn(a, b):
    return lax.dot_general(a, b, (((0,), (0,)), ((), ())), preferred_element_type=F32)


def _silu(x):
    return x * jax.nn.sigmoid(x)


def _mod_group(i, tm):
    r = i * tm
    return jnp.where(r < NP, 0, 1 + (r - NP) // DEC_SEQ)


def _norm_mod(x, nw, shift, scale):
    ms = jnp.mean(x * x, axis=-1, keepdims=True)
    y = x * lax.rsqrt(ms + EPS) * nw
    return y * (1.0 + scale) + shift


def _mod_kernel(c_ref, w_ref, b_ref, o_ref):
    s = _silu(c_ref[...]).astype(BF16)
    o_ref[0] = _dot(s, w_ref[0].astype(BF16)) + b_ref[0]


def _modulation(cvec, ada_w, ada_b):
    cw = 1536
    n = ada_w.shape[-1]
    out = pl.pallas_call(
        _mod_kernel,
        grid=(DEPTH, n // cw),
        in_specs=[
            pl.BlockSpec((N_MOD, D), lambda l, j: (0, 0)),
            pl.BlockSpec((1, D, cw), lambda l, j: (l, 0, j)),
            pl.BlockSpec((1, 1, cw), lambda l, j: (l, 0, j)),
        ],
        out_specs=pl.BlockSpec((1, N_MOD, cw), lambda l, j: (l, 0, j)),
        out_shape=jax.ShapeDtypeStruct((DEPTH, N_MOD, n), F32),
        compiler_params=_cparams(("arbitrary", "arbitrary")),
        name="modulation",
    )(cvec, ada_w, ada_b.reshape(DEPTH, 1, n))
    return out.reshape(DEPTH, N_MOD, 6, D)


def _proj_kernel(x_ref, nw_ref, mod_ref, w_ref, o_ref, *, cw):
    h = _norm_mod(x_ref[...], nw_ref[...], mod_ref[0, 0:1, :], mod_ref[0, 1:2, :]).astype(BF16)
    for j in range(w_ref.shape[1] // cw):
        o_ref[:, j * cw:(j + 1) * cw] = _dot(h, w_ref[:, j * cw:(j + 1) * cw])


def _norm_proj(x, nw, mod, w):
    n = w.shape[1]
    return pl.pallas_call(
        functools.partial(_proj_kernel, cw=512),
        grid=(NT // TM,),
        in_specs=[
            pl.BlockSpec((TM, D), lambda i: (i, 0)),
            pl.BlockSpec((1, D), lambda i: (0, 0)),
            pl.BlockSpec((1, 6, D), lambda i: (_mod_group(i, TM), 0, 0)),
            pl.BlockSpec((D, n), lambda i: (0, 0)),
        ],
        out_specs=pl.BlockSpec((TM, n), lambda i: (i, 0)),
        out_shape=jax.ShapeDtypeStruct((NT, n), F32),
        compiler_params=_cparams(("arbitrary",)),
        name="norm_proj",
    )(x, nw.reshape(1, D), mod, w)


def _split3(x):
    hi = x.astype(BF16)
    r = x - hi.astype(F32)
    mid = r.astype(BF16)
    lo = (r - mid.astype(F32)).astype(BF16)
    return hi, mid, lo


def _gla_direction(q, fraw, v, lb, st_ref, rev):
    f = lb + (1.0 - lb) * jax.nn.sigmoid(fraw)
    kk = 1.0 - f
    g = jnp.log(f)

    t_io = lax.broadcasted_iota(I32, (2 * CHUNK, CHUNK), 0)
    u_io = lax.broadcasted_iota(I32, (2 * CHUNK, CHUNK), 1)
    tt = jnp.where(t_io < CHUNK, t_io, t_io - CHUNK)
    blk = (tt // SUB) * SUB
    if rev:
        m_cum = u_io >= tt
        m_ref = u_io >= blk + SUB
    else:
        m_cum = u_io <= tt
        m_ref = u_io < blk
    top = t_io < CHUNK
    pick = jnp.logical_or(jnp.logical_and(top, m_cum), jnp.logical_and(jnp.logical_not(top), m_ref))
    tri = jnp.where(pick, 1.0, 0.0).astype(BF16)
    hi, mid, lo = _split3(g)
    cs = _dot(tri, hi) + _dot(tri, mid) + _dot(tri, lo)
    b = cs[:CHUNK]
    r = cs[CHUNK:]
    last = 0 if rev else CHUNK - 1
    tot = b[last:last + 1]

    qe = (q * jnp.exp(b)).astype(BF16)
    qd = (q * jnp.exp(b - r)).astype(BF16)
    kdec = (kk * jnp.exp(tot - b)).astype(BF16)
    vb = v.astype(BF16)
    dec_tot = jnp.exp(tot)

    s_io = lax.broadcasted_iota(I32, (CHUNK, LANES), 0)
    n_blk = CHUNK // SUB
    o_heads = []
    for h in range(HG_HEADS):
        sl = slice(h * LANES, (h + 1) * LANES)
        st = st_ref[sl, :]
        o_h = _dot_nt(qe[:, sl], st.astype(BF16))
        rows = []
        for i in range(n_blk):
            empty = (i == n_blk - 1) if rev else (i == 0)
            if empty:
                rows.append(jnp.zeros((SUB, CHUNK), F32))
                continue
            r_i = r[i * SUB:i * SUB + 1, sl]
            before = (s_io >= (i + 1) * SUB) if rev else (s_io < i * SUB)
            kd = jnp.where(before, kk[:, sl] * jnp.exp(r_i - b[:, sl]), 0.0).astype(BF16)
            rows.append(_dot_nt(qd[i * SUB:(i + 1) * SUB, sl], kd))
        a1 = jnp.concatenate(rows, axis=0).astype(BF16)
        o_h = o_h + _dot(a1, vb[:, sl])
        o_heads.append(o_h)
        st_ref[sl, :] = st * dec_tot[:, sl] + _dot_tn(vb[:, sl], kdec[:, sl])
    o = jnp.concatenate(o_heads, axis=1)

    row8 = lax.broadcasted_iota(I32, (SUBLANES, 1), 0)
    groups = []
    for tb in range(CHUNK // SUBLANES):
        t0 = tb * SUBLANES
        blk0 = (t0 // SUB) * SUB
        q_t = q[t0:t0 + SUBLANES]
        b_t = b[t0:t0 + SUBLANES]
        acc = jnp.zeros((SUBLANES, HG_HEADS * LANES), F32)
        s_range = range(t0, blk0 + SUB) if rev else range(blk0, t0 + SUBLANES)
        for s in s_range:
            e = jnp.exp(b_t - b[s:s + 1])
            if t0 <= s < t0 + SUBLANES:
                keep = (row8 <= s - t0) if rev else (row8 >= s - t0)
                e = jnp.where(keep, e, 0.0)
            p = q_t * e * kk[s:s + 1]
            red = jnp.concatenate(
                [jnp.broadcast_to(jnp.sum(p[:, h * LANES:(h + 1) * LANES], axis=-1, keepdims=True),
                                  (SUBLANES, LANES)) for h in range(HG_HEADS)], axis=1)
            acc = acc + red * v[s:s + 1]
        groups.append(acc)
    return o + jnp.concatenate(groups, axis=0)


def _gla_kernel(qf_ref, ff_ref, vf_ref, qb_ref, fb_ref, vb_ref, lb_ref, s0f_ref, s0b_ref,
                of_ref, ob_ref, sf_ref, sb_ref, stf, stb):
    i = pl.program_id(0)
    n_p = NP // CHUNK
    is_prompt = i < n_p
    c = jnp.where(is_prompt, i % (SEQ // CHUNK), (i - n_p) % (DEC_SEQ // CHUNK))
    n_c = jnp.where(is_prompt, SEQ // CHUNK, DEC_SEQ // CHUNK)

    @pl.when(jnp.logical_and(c == 0, is_prompt))
    def _():
        stf[...] = jnp.zeros_like(stf)
        stb[...] = jnp.zeros_like(stb)

    @pl.when(jnp.logical_and(c == 0, jnp.logical_not(is_prompt)))
    def _():
        for h in range(HG_HEADS):
            sl = slice(h * LANES, (h + 1) * LANES)
            stf[sl, :] = s0f_ref[0, sl, :].T
            stb[sl, :] = s0b_ref[0, sl, :].T

    lb = lb_ref[...]
    of_ref[...] = _gla_direction(qf_ref[...], ff_ref[...], vf_ref[...], lb, stf, False)
    ob_ref[...] = _gla_direction(qb_ref[...], fb_ref[...], vb_ref[...], lb, stb, True)

    @pl.when(jnp.logical_and(c == n_c - 1, is_prompt))
    def _():
        for h in range(HG_HEADS):
            sl = slice(h * LANES, (h + 1) * LANES)
            sf_ref[0, sl, :] = stf[sl, :].T
            sb_ref[0, sl, :] = stb[sl, :].T


def _gla(proj, lb, s0f, s0b):
    n_p = NP // CHUNK
    cp = SEQ // CHUNK
    cs = DEC_SEQ // CHUNK

    def bwd_blk(i):
        jp = (i // cp) * cp + (cp - 1 - i % cp)
        j = i - n_p
        js = n_p + (j // cs) * cs + (cs - 1 - j % cs)
        return jnp.where(i < n_p, jp, js)

    def req(i):
        return jnp.maximum(i - n_p, 0) // cs

    def preq(i):
        return jnp.minimum(i // cp, BATCH - 1)

    hk = HG_HEADS * HG_DK
    return pl.pallas_call(
        _gla_kernel,
        grid=(NT // CHUNK,),
        in_specs=[
            pl.BlockSpec((CHUNK, D), lambda i: (i, 0)),
            pl.BlockSpec((CHUNK, D), lambda i: (i, 1)),
            pl.BlockSpec((CHUNK, D), lambda i: (i, 3)),
            pl.BlockSpec((CHUNK, D), lambda i: (bwd_blk(i), 0)),
            pl.BlockSpec((CHUNK, D), lambda i: (bwd_blk(i), 2)),
            pl.BlockSpec((CHUNK, D), lambda i: (bwd_blk(i), 3)),
            pl.BlockSpec((1, hk), lambda i: (0, 0)),
            pl.BlockSpec((1, hk, HG_DV), lambda i: (req(i), 0, 0)),
            pl.BlockSpec((1, hk, HG_DV), lambda i: (req(i), 0, 0)),
        ],
        out_specs=[
            pl.BlockSpec((CHUNK, D), lambda i: (i, 0)),
            pl.BlockSpec((CHUNK, D), lambda i: (bwd_blk(i), 0)),
            pl.BlockSpec((1, hk, HG_DV), lambda i: (preq(i), 0, 0)),
            pl.BlockSpec((1, hk, HG_DV), lambda i: (preq(i), 0, 0)),
        ],
        out_shape=[
            jax.ShapeDtypeStruct((NT, D), F32),
            jax.ShapeDtypeStruct((NT, D), F32),
            jax.ShapeDtypeStruct((BATCH, hk, HG_DV), F32),
            jax.ShapeDtypeStruct((BATCH, hk, HG_DV), F32),
        ],
        scratch_shapes=[pltpu.VMEM((hk, HG_DV), F32), pltpu.VMEM((hk, HG_DV), F32)],
        compiler_params=_cparams(("arbitrary",)),
        name="gla_scan",
    )(proj, proj, proj, proj, proj, proj, lb.reshape(1, hk), s0f, s0b)


def _hgrn_out_kernel(of_ref, ob_ref, gate_ref, x_ref, mod_ref, gn_ref, w_ref, o_ref):
    o = of_ref[...] + ob_ref[...]
    gn = gn_ref[...]
    segs = []
    for h in range(HG_HEADS):
        seg = o[:, h * LANES:(h + 1) * LANES]
        ms = jnp.mean(seg * seg, axis=-1, keepdims=True)
        segs.append(seg * lax.rsqrt(ms + EPS) * gn)
    y = (jnp.concatenate(segs, axis=1) * _silu(gate_ref[...])).astype(BF16)
    o_ref[...] = x_ref[...] + mod_ref[0, 2:3, :] * _dot(y, w_ref[...])


def _hgrn_out(o_f, o_b, proj, x, mod, gn_w, w_out):
    return pl.pallas_call(
        _hgrn_out_kernel,
        grid=(NT // TM,),
        in_specs=[
            pl.BlockSpec((TM, D), lambda i: (i, 0)),
            pl.BlockSpec((TM, D), lambda i: (i, 0)),
            pl.BlockSpec((TM, D), lambda i: (i, 4)),
            pl.BlockSpec((TM, D), lambda i: (i, 0)),
            pl.BlockSpec((1, 6, D), lambda i: (_mod_group(i, TM), 0, 0)),
            pl.BlockSpec((1, HG_DV), lambda i: (0, 0)),
            pl.BlockSpec((D, D), lambda i: (0, 0)),
        ],
        out_specs=pl.BlockSpec((TM, D), lambda i: (i, 0)),
        out_shape=jax.ShapeDtypeStruct((NT, D), F32),
        compiler_params=_cparams(("arbitrary",)),
        name="hgrn_out",
    )(o_f, o_b, proj, x, mod, gn_w.reshape(1, HG_DV), w_out)


def _attn_out_kernel(ap_ref, as_ref, x_ref, mod_ref, w_ref, o_ref):
    a = jnp.where(pl.program_id(0) < NP // TM, ap_ref[...], as_ref[...]).astype(BF16)
    o_ref[...] = x_ref[...] + mod_ref[0, 2:3, :] * _dot(a, w_ref[...])


def _attn_out(a_p, a_s, x, mod, w_o):
    n_p = NP // TM
    return pl.pallas_call(
        _attn_out_kernel,
        grid=(NT // TM,),
        in_specs=[
            pl.BlockSpec((TM, D), lambda i: (jnp.minimum(i, n_p - 1), 0)),
            pl.BlockSpec((TM, D), lambda i: (jnp.maximum(i - n_p, 0), 0)),
            pl.BlockSpec((TM, D), lambda i: (i, 0)),
            pl.BlockSpec((1, 6, D), lambda i: (_mod_group(i, TM), 0, 0)),
            pl.BlockSpec((D, D), lambda i: (0, 0)),
        ],
        out_specs=pl.BlockSpec((TM, D), lambda i: (i, 0)),
        out_shape=jax.ShapeDtypeStruct((NT, D), F32),
        compiler_params=_cparams(("arbitrary",)),
        name="attn_out",
    )(a_p, a_s, x, mod, w_o)


def _head_pair_masks():
    upper = lax.broadcasted_iota(I32, (1, LANES), 1) >= NA_HD
    return upper


def _attn_ctx_kernel(q_ref, k_ref, v_ref, o_ref):
    upper = _head_pair_masks()
    for p in range(NA_HEADS // 2):
        sl = slice(p * LANES, (p + 1) * LANES)
        qp = q_ref[:, sl] * (NA_HD ** -0.5)
        kp = k_ref[:, sl].astype(BF16)
        vp = v_ref[:, sl].astype(BF16)
        outs = []
        for u in range(2):
            qm = jnp.where(upper if u else jnp.logical_not(upper), qp, 0.0).astype(BF16)
            s = _dot_nt(qm, kp)
            e = jnp.exp(s - jnp.max(s, axis=-1, keepdims=True))
            outs.append(_dot(e.astype(BF16), vp) / jnp.sum(e, axis=-1, keepdims=True))
        o_ref[:, sl] = jnp.where(upper, outs[1], outs[0])


def _attn_ctx(qkv):
    return pl.pallas_call(
        _attn_ctx_kernel,
        grid=(BATCH,),
        in_specs=[
            pl.BlockSpec((SEQ, D), lambda b: (b, 0)),
            pl.BlockSpec((SEQ, D), lambda b: (b, 1)),
            pl.BlockSpec((SEQ, D), lambda b: (b, 2)),
        ],
        out_specs=pl.BlockSpec((SEQ, D), lambda b: (b, 0)),
        out_shape=jax.ShapeDtypeStruct((NP, D), F32),
        compiler_params=_cparams(("arbitrary",)),
        name="attn_ctx",
    )(qkv, qkv, qkv)


def _attn_lat_kernel(q_ref, k_ref, v_ref, kc_ref, vc_ref, bias_ref, o_ref, *, tq):
    upper = _head_pair_masks()
    kl = k_ref[...].astype(BF16)
    vl = v_ref[...].astype(BF16)
    kc = kc_ref[0].astype(BF16)
    vc = vc_ref[0].astype(BF16)
    for t in range(DEC_SEQ // tq):
        rs = slice(t * tq, (t + 1) * tq)
        qp = q_ref[rs, :] * (NA_HD ** -0.5)
        outs = []
        for u in range(2):
            qm = jnp.where(upper if u else jnp.logical_not(upper), qp, 0.0).astype(BF16)
            s_l = _dot_nt(qm, kl) + bias_ref[u, rs, :]
            s_c = _dot_nt(qm, kc)
            m = jnp.maximum(jnp.max(s_l, axis=-1, keepdims=True), jnp.max(s_c, axis=-1, keepdims=True))
            e_l = jnp.exp(s_l - m)
            e_c = jnp.exp(s_c - m)
            den = jnp.sum(e_l, axis=-1, keepdims=True) + jnp.sum(e_c, axis=-1, keepdims=True)
            outs.append((_dot(e_l.astype(BF16), vl) + _dot(e_c.astype(BF16), vc)) / den)
        o_ref[rs, :] = jnp.where(upper, outs[1], outs[0])


def _attn_lat(qkv, k_ctx, v_ctx, bias):
    npair = NA_HEADS // 2
    rb0 = NP // DEC_SEQ
    return pl.pallas_call(
        functools.partial(_attn_lat_kernel, tq=256),
        grid=(npair, DEC_BATCH),
        in_specs=[
            pl.BlockSpec((DEC_SEQ, LANES), lambda p, b: (rb0 + b, p)),
            pl.BlockSpec((DEC_SEQ, LANES), lambda p, b: (rb0 + b, npair + p)),
            pl.BlockSpec((DEC_SEQ, LANES), lambda p, b: (rb0 + b, 2 * npair + p)),
            pl.BlockSpec((1, PAST_LEN, LANES), lambda p, b: (b, 0, p)),
            pl.BlockSpec((1, PAST_LEN, LANES), lambda p, b: (b, 0, p)),
            pl.BlockSpec((2, DEC_SEQ, DEC_SEQ), lambda p, b: (p, 0, 0)),
        ],
        out_specs=pl.BlockSpec((DEC_SEQ, LANES), lambda p, b: (b, p)),
        out_shape=jax.ShapeDtypeStruct((NS, D), F32),
        compiler_params=_cparams(("arbitrary", "arbitrary")),
        name="attn_lat",
    )(qkv, qkv, qkv, k_ctx, v_ctx, bias)


def _latent_bias(rpb):
    rows = DEC_SEQ // GRID_W
    kr = min(WIN_R, rows)
    ndc = 2 * WIN_C - 1
    qc = np.arange(GRID_W)
    kc = np.arange(GRID_W)
    ws = np.clip(qc - WIN_C // 2, 0, GRID_W - WIN_C)
    col_ok = (kc[None, :] >= ws[:, None]) & (kc[None, :] < ws[:, None] + WIN_C)
    dc = np.clip(kc[None, :] - qc[:, None] + WIN_C - 1, 0, ndc - 1)
    onehot = (dc[None, :, :] == np.arange(ndc)[:, None, None]).astype(np.float32)
    t2 = jnp.einsum('hrc,cqk->hrqk', rpb.astype(F32), jnp.asarray(onehot),
                    precision=lax.Precision.HIGHEST)
    t2 = jnp.where(jnp.asarray(col_ok)[None, None], t2, NEG)
    r = np.arange(rows)
    k0 = np.clip(r - kr // 2, 0, rows - kr)
    krow = np.arange(rows)
    row_ok = (krow[None, :] >= k0[:, None]) & (krow[None, :] < k0[:, None] + kr)
    dr = np.clip(krow[None, :] - r[:, None] + WIN_R - 1, 0, 2 * WIN_R - 2)
    g = jnp.take(t2, jnp.asarray(dr.reshape(-1)), axis=1).reshape(NA_HEADS, rows, rows, GRID_W, GRID_W)
    g = jnp.where(jnp.asarray(row_ok)[None, :, :, None, None], g, NEG)
    return g.transpose(0, 1, 3, 2, 4).reshape(NA_HEADS, DEC_SEQ, DEC_SEQ)


def _route_kernel(x_ref, nw_ref, mod_ref, wrh_ref, wrl_ref, rb_ref, xs_ref, pos_ref, wt_ref, npc_ref):
    h = _norm_mod(x_ref[...], nw_ref[...], mod_ref[0, 3:4, :], mod_ref[0, 4:5, :])
    hb = h.astype(BF16)
    hl = (h - hb.astype(F32)).astype(BF16)
    wrh = wrh_ref[...]
    logits = _dot_nt(wrh, hb) + _dot_nt(wrh, hl) + _dot_nt(wrl_ref[...], hb)
    scores = jax.nn.sigmoid(logits)
    sel = scores + rb_ref[...]

    gsz = N_EXPERTS // N_GROUPS
    sub = lax.broadcasted_iota(I32, (gsz, MOE_TM), 0)
    ninf = -jnp.inf
    gs_rows = []
    for gi in range(N_GROUPS):
        blk = sel[gi * gsz:(gi + 1) * gsz]
        m1 = jnp.max(blk, axis=0, keepdims=True)
        first = jnp.min(jnp.where(blk == m1, sub, gsz), axis=0, keepdims=True)
        m2 = jnp.max(jnp.where(sub == first, ninf, blk), axis=0, keepdims=True)
        gs_rows.append(m1 + m2)
    cur = jnp.concatenate(gs_rows, axis=0)
    gidx = lax.broadcasted_iota(I32, (N_GROUPS, MOE_TM), 0)
    gsel = jnp.zeros((N_GROUPS, MOE_TM), F32)
    for _ in range(TOPK_GROUPS):
        m = jnp.max(cur, axis=0, keepdims=True)
        first = jnp.min(jnp.where(cur == m, gidx, N_GROUPS), axis=0, keepdims=True)
        hit = gidx == first
        gsel = jnp.where(hit, 1.0, gsel)
        cur = jnp.where(hit, ninf, cur)
    emask = jnp.concatenate(
        [jnp.broadcast_to(gsel[gi:gi + 1], (gsz, MOE_TM)) for gi in range(N_GROUPS)], axis=0)
    masked = jnp.where(emask > 0.5, sel, ninf)
    eidx = lax.broadcasted_iota(I32, (N_EXPERTS, MOE_TM), 0)
    chosen = jnp.zeros((N_EXPERTS, MOE_TM), F32)
    hits, wsel = [], []
    for _ in range(TOP_K):
        m = jnp.max(masked, axis=0, keepdims=True)
        first = jnp.min(jnp.where(masked == m, eidx, N_EXPERTS), axis=0, keepdims=True)
        hit = eidx == first
        hits.append(hit)
        wsel.append(jnp.sum(jnp.where(hit, scores, 0.0), axis=0, keepdims=True))
        chosen = jnp.where(hit, 1.0, chosen)
        masked = jnp.where(hit, ninf, masked)
    wsum = wsel[0]
    for w in wsel[1:]:
        wsum = wsum + w

    n_io = lax.broadcasted_iota(I32, (MOE_TM, MOE_TM), 0)
    m_io = lax.broadcasted_iota(I32, (MOE_TM, MOE_TM), 1)
    earlier = jnp.where(n_io < m_io, 1.0, 0.0).astype(BF16)
    rank = _dot(chosen.astype(BF16), earlier)
    cnt = jnp.sum(chosen, axis=1, keepdims=True)
    npc = jnp.floor((cnt + (PIECE - 1)) * (1.0 / PIECE))
    e_io = lax.broadcasted_iota(I32, (N_EXPERTS, N_EXPERTS), 0)
    f_io = lax.broadcasted_iota(I32, (N_EXPERTS, N_EXPERTS), 1)
    below = jnp.where(f_io < e_io, 1.0, 0.0).astype(BF16)
    npc_l = jnp.broadcast_to(npc, (N_EXPERTS, LANES))
    start = _dot(below, npc_l.astype(BF16))[:, 0:1] * PIECE
    slot = start + rank

    j_io = lax.broadcasted_iota(I32, (R1, MOE_TM), 0)
    onehot = jnp.zeros((R1, MOE_TM), F32)
    pos_rows, wt_rows = [], []
    for k in range(TOP_K):
        pos_k = jnp.sum(jnp.where(hits[k], slot, 0.0), axis=0, keepdims=True).astype(I32)
        pos_rows.append(pos_k)
        wt_rows.append(wsel[k] / wsum * ROUTED_SCALE)
        onehot = jnp.where(j_io == pos_k, 1.0, onehot)
    xs_ref[...] = _dot(onehot.astype(BF16), hb).astype(BF16)
    pad = SUBLANES - TOP_K
    pos_ref[...] = jnp.concatenate(pos_rows + [jnp.full((pad, MOE_TM), -1, I32)], axis=0)
    wt_ref[...] = jnp.concatenate(wt_rows + [jnp.zeros((pad, MOE_TM), F32)], axis=0)
    npc_ref[0] = npc_l.astype(I32)


def _route(x, nw, mod, wr_hi, wr_lo, rbias):
    return pl.pallas_call(
        _route_kernel,
        grid=(N_TILES,),
        in_specs=[
            pl.BlockSpec((MOE_TM, D), lambda i: (i, 0)),
            pl.BlockSpec((1, D), lambda i: (0, 0)),
            pl.BlockSpec((1, 6, D), lambda i: (_mod_group(i, MOE_TM), 0, 0)),
            pl.BlockSpec((N_EXPERTS, D), lambda i: (0, 0)),
            pl.BlockSpec((N_EXPERTS, D), lambda i: (0, 0)),
            pl.BlockSpec((N_EXPERTS, 1), lambda i: (0, 0)),
        ],
        out_specs=[
            pl.BlockSpec((R1, D), lambda i: (i, 0)),
            pl.BlockSpec((SUBLANES, MOE_TM), lambda i: (0, i)),
            pl.BlockSpec((SUBLANES, MOE_TM), lambda i: (0, i)),
            pl.BlockSpec((1, N_EXPERTS, LANES), lambda i: (i, 0, 0)),
        ],
        out_shape=[
            jax.ShapeDtypeStruct((N_TILES * R1, D), BF16),
            jax.ShapeDtypeStruct((SUBLANES, NT), I32),
            jax.ShapeDtypeStruct((SUBLANES, NT), F32),
            jax.ShapeDtypeStruct((N_TILES, N_EXPERTS, LANES), I32),
        ],
        compiler_params=_cparams(("arbitrary",)),
        name="moe_route",
    )(x, nw.reshape(1, D), mod, wr_hi, wr_lo, rbias.reshape(N_EXPERTS, 1))


def _piece_lists(npc):
    t, e = npc.shape
    hp = lax.Precision.HIGHEST
    npc_t = npc.T.astype(F32)
    start_t = (jnp.cumsum(npc, axis=1) - npc).T.astype(F32)
    tile_end = jnp.cumsum(npc_t, axis=1)
    n_e = tile_end[:, -1]
    pe_end = jnp.cumsum(n_e)
    pe_off = pe_end - n_e
    p = jnp.arange(P_MAX, dtype=F32)
    e_p = jnp.minimum(jnp.sum((pe_end[None, :] <= p[:, None]).astype(I32), axis=1), e - 1)
    oh_e = (e_p[:, None] == jnp.arange(e, dtype=I32)[None, :]).astype(F32)
    tab = jnp.concatenate([tile_end, start_t, npc_t, pe_off[:, None]], axis=1)
    row = jnp.dot(oh_e, tab, precision=hp)
    te_p, st_p, np_p, off_p = row[:, :t], row[:, t:2 * t], row[:, 2 * t:3 * t], row[:, 3 * t]
    local = p - off_p
    t_p = jnp.minimum(jnp.sum((te_p <= local[:, None]).astype(I32), axis=1), t - 1)
    oh_t = t_p[:, None] == jnp.arange(t, dtype=I32)[None, :]

    def pick(a):
        return jnp.sum(jnp.where(oh_t, a, 0.0), axis=1)

    src = t_p * R1B + (pick(st_p) + local - (pick(te_p) - pick(np_p))).astype(I32)
    src = jnp.clip(src, 0, P_MAX - 1)

    nch = jnp.floor((n_e + (G_PIECES - 1)) * (1.0 / G_PIECES))
    ch_end = jnp.cumsum(nch)
    c = jnp.arange(NCH, dtype=F32)
    ce = jnp.minimum(jnp.sum((ch_end[None, :] <= c[:, None]).astype(I32), axis=1), e - 1)
    oh_c = (ce[:, None] == jnp.arange(e, dtype=I32)[None, :]).astype(F32)
    crow = jnp.dot(oh_c, jnp.stack([ch_end - nch, pe_off, n_e], axis=1), precision=hp)
    k_in = c - crow[:, 0]
    cn = jnp.where(c < ch_end[-1], jnp.clip(crow[:, 2] - G_PIECES * k_in, 0, G_PIECES), 0.0)
    cs = jnp.where(cn > 0, crow[:, 1] + G_PIECES * k_in, 0.0)
    return src, ce, cs.astype(I32), cn.astype(I32)


def _ffn_kernel(src_ref, ce_ref, cs_ref, cn_ref, xs_in, wg_ref, wu_ref, wd_ref, xs_out,
                xbuf, ybuf, wgb, wub, wdb, gsem, ssem):
    c = pl.program_id(0)
    slot = lax.rem(c, 2)

    def gather_copy(ch, sl, i):
        return pltpu.make_async_copy(xs_in.at[src_ref[cs_ref[ch] + i]],
                                     xbuf.at[sl, pl.ds(i * PIECE, PIECE)], gsem.at[sl])

    def scatter_copy(ch, sl, i):
        return pltpu.make_async_copy(ybuf.at[sl, pl.ds(i * PIECE, PIECE)],
                                     xs_out.at[src_ref[cs_ref[ch] + i]], ssem.at[sl])

    def for_pieces(ch, fn):
        def body(i, carry):
            fn(i)
            return carry
        lax.fori_loop(0, cn_ref[ch], body, 0)

    @pl.when(c == 0)
    def _():
        xbuf[...] = jnp.zeros_like(xbuf)
        for_pieces(0, lambda i: gather_copy(0, 0, i).start())

    for_pieces(c, lambda i: gather_copy(c, slot, i).wait())

    @pl.when(c + 1 < NCH)
    def _():
        for_pieces(c + 1, lambda i: gather_copy(c + 1, 1 - slot, i).start())

    changed = jnp.logical_or(c == 0, ce_ref[c] != ce_ref[jnp.maximum(c - 1, 0)])

    @pl.when(changed)
    def _():
        wgb[...] = wg_ref[0].astype(BF16)
        wub[...] = wu_ref[0].astype(BF16)
        wdb[...] = wd_ref[0].astype(BF16)

    @pl.when(c >= 2)
    def _():
        for_pieces(c - 2, lambda i: scatter_copy(c - 2, slot, i).wait())

    @pl.when(cn_ref[c] > 0)
    def _():
        x = xbuf[slot]
        hid = (_silu(_dot(x, wgb[...])) * _dot(x, wub[...])).astype(BF16)
        ybuf[slot] = _dot(hid, wdb[...]).astype(BF16)

    for_pieces(c, lambda i: scatter_copy(c, slot, i).start())

    @pl.when(c == NCH - 1)
    def _():
        for_pieces(c - 1, lambda i: scatter_copy(c - 1, 1 - slot, i).wait())
        for_pieces(c, lambda i: scatter_copy(c, slot, i).wait())


def _expert_ffn(xs, lists, w_gate, w_up, w_down):
    src, ce, cs, cn = lists
    rows = G_PIECES * PIECE
    grid_spec = pltpu.PrefetchScalarGridSpec(
        num_scalar_prefetch=4,
        grid=(NCH,),
        in_specs=[
            pl.BlockSpec(memory_space=pl.ANY),
            pl.BlockSpec((1, D, D_EXPERT), lambda c, src, ce, cs, cn: (ce[c], 0, 0)),
            pl.BlockSpec((1, D, D_EXPERT), lambda c, src, ce, cs, cn: (ce[c], 0, 0)),
            pl.BlockSpec((1, D_EXPERT, D), lambda c, src, ce, cs, cn: (ce[c], 0, 0)),
        ],
        out_specs=pl.BlockSpec(memory_space=pl.ANY),
        scratch_shapes=[
            pltpu.VMEM((2, rows, D), BF16),
            pltpu.VMEM((2, rows, D), BF16),
            pltpu.VMEM((D, D_EXPERT), BF16),
            pltpu.VMEM((D, D_EXPERT), BF16),
            pltpu.VMEM((D_EXPERT, D), BF16),
            pltpu.SemaphoreType.DMA((2,)),
            pltpu.SemaphoreType.DMA((2,)),
        ],
    )
    out = pl.pallas_call(
        _ffn_kernel,
        grid_spec=grid_spec,
        out_shape=jax.ShapeDtypeStruct((P_MAX, PIECE, D), BF16),
        input_output_aliases={4: 0},
        compiler_params=_cparams(("arbitrary",)),
        name="moe_ffn",
    )(src, ce, cs, cn, xs.reshape(P_MAX, PIECE, D), w_gate, w_up, w_down)
    return out.reshape(N_TILES * R1, D)


def _combine_kernel(ys_ref, pos_ref, wt_ref, x_ref, nw_ref, mod_ref, wsg_ref, wsu_ref, wsd_ref, fw_ref,
                    o_ref, *, final):
    x = x_ref[...]
    h = _norm_mod(x, nw_ref[...], mod_ref[0, 3:4, :], mod_ref[0, 4:5, :]).astype(BF16)
    shared = _dot((_silu(_dot(h, wsg_ref[...])) * _dot(h, wsu_ref[...])).astype(BF16), wsd_ref[...])
    j_io = lax.broadcasted_iota(I32, (MOE_TM, R1), 1)
    pos = pos_ref[...]
    wt = wt_ref[...]
    wm = jnp.zeros((MOE_TM, R1), F32)
    for k in range(TOP_K):
        wm = jnp.where(j_io == pos[:, k:k + 1], wt[:, k:k + 1], wm)
    routed = _dot(wm.astype(BF16), ys_ref[...])
    y = x + mod_ref[0, 5:6, :] * (routed + shared)
    if final:
        ms = jnp.mean(y * y, axis=-1, keepdims=True)
        y = y * lax.rsqrt(ms + EPS) * fw_ref[...]
    o_ref[...] = y


def _combine(ys, pos_t, wt_t, x, nw, mod, wsg, wsu, wsd, fw, final):
    return pl.pallas_call(
        functools.partial(_combine_kernel, final=final),
        grid=(N_TILES,),
        in_specs=[
            pl.BlockSpec((R1, D), lambda i: (i, 0)),
            pl.BlockSpec((MOE_TM, SUBLANES), lambda i: (i, 0)),
            pl.BlockSpec((MOE_TM, SUBLANES), lambda i: (i, 0)),
            pl.BlockSpec((MOE_TM, D), lambda i: (i, 0)),
            pl.BlockSpec((1, D), lambda i: (0, 0)),
            pl.BlockSpec((1, 6, D), lambda i: (_mod_group(i, MOE_TM), 0, 0)),
            pl.BlockSpec((D, D_SHARED), lambda i: (0, 0)),
            pl.BlockSpec((D, D_SHARED), lambda i: (0, 0)),
            pl.BlockSpec((D_SHARED, D), lambda i: (0, 0)),
            pl.BlockSpec((1, D), lambda i: (0, 0)),
        ],
        out_specs=pl.BlockSpec((MOE_TM, D), lambda i: (i, 0)),
        out_shape=jax.ShapeDtypeStruct((NT, D), F32),
        compiler_params=_cparams(("arbitrary",)),
        name="moe_combine",
    )(ys, pos_t, wt_t, x, nw.reshape(1, D), mod, wsg, wsu, wsd, fw.reshape(1, D))


def _moe(x, nw, mod, w_router, rbias, w_gate, w_up, w_down, ws_gate, ws_up, ws_down, fw, final):
    wr = w_router.astype(F32).T
    wr_hi = wr.astype(BF16)
    wr_lo = (wr - wr_hi.astype(F32)).astype(BF16)
    xs, pos, wt, npc = _route(x, nw, mod, wr_hi, wr_lo, rbias.astype(F32))
    lists = _piece_lists(npc[:, :, 0])
    ys = _expert_ffn(xs, lists, w_gate, w_up, w_down)
    return _combine(ys, pos.T, wt.T, x, nw, mod, ws_gate.astype(BF16), ws_up.astype(BF16),
                    ws_down.astype(BF16), fw, final)


def kernel(x_prompt, x_sample, state_hgrn_fwd, state_hgrn_bwd, cache_na_k, cache_na_v, c, c_ctx,
           norm1_w, norm2_w, ada_w, ada_b, hgrn_w_in, hgrn_lb_logits, hgrn_gn_w, hgrn_w_out,
           na_w_qkv, na_rpb, na_w_o, moe_w_router, moe_router_bias, moe_w_gate, moe_w_up, moe_w_down,
           shared_w_gate, shared_w_up, shared_w_down, final_norm_w):
    x = jnp.concatenate([x_prompt.reshape(NP, D), x_sample.reshape(NS, D)], axis=0)
    cvec = jnp.concatenate([c_ctx[None, :], c, jnp.zeros((N_MOD - 1 - DEC_BATCH, D), F32)], axis=0)
    mod = _modulation(cvec, ada_w, ada_b)
    lb_table = jnp.cumsum(jax.nn.softmax(hgrn_lb_logits.astype(F32), axis=0), axis=0)
    hk = HG_HEADS * HG_DK

    sf = sb = k_c = v_c = None
    for l in range(DEPTH):
        if l % 2 == 0:
            a = l // 2
            proj = _norm_proj(x, norm1_w[l], mod[l], hgrn_w_in[a].astype(BF16))
            o_f, o_b, sf, sb = _gla(proj, lb_table[l],
                                    state_hgrn_fwd[:, a].reshape(DEC_BATCH, hk, HG_DV),
                                    state_hgrn_bwd[:, a].reshape(DEC_BATCH, hk, HG_DV))
            x = _hgrn_out(o_f, o_b, proj, x, mod[l], hgrn_gn_w[a], hgrn_w_out[a].astype(BF16))
        else:
            n = l // 2
            qkv = _norm_proj(x, norm1_w[l], mod[l], na_w_qkv[n].astype(BF16))
            att_p = _attn_ctx(qkv)
            att_s = _attn_lat(qkv, cache_na_k[:, n].reshape(DEC_BATCH, PAST_LEN, D),
                              cache_na_v[:, n].reshape(DEC_BATCH, PAST_LEN, D), _latent_bias(na_rpb[n]))
            x = _attn_out(att_p, att_s, x, mod[l], na_w_o[n].astype(BF16))
            k_c = qkv[:NP, D:2 * D].reshape(BATCH, SEQ, NA_HEADS, NA_HD)
            v_c = qkv[:NP, 2 * D:].reshape(BATCH, SEQ, NA_HEADS, NA_HD)
        x = _moe(x, norm2_w[l], mod[l], moe_w_router[l], moe_router_bias[l], moe_w_gate[l], moe_w_up[l],
                 moe_w_down[l], shared_w_gate[l], shared_w_up[l], shared_w_down[l], final_norm_w,
                 final=(l == DEPTH - 1))

    y_prompt = x[:NP].reshape(BATCH, SEQ, D)
    y_sample = x[NP:].reshape(DEC_BATCH, DEC_SEQ, D)
    new_sf = sf.reshape(BATCH, 1, HG_HEADS, HG_DK, HG_DV)
    new_sb = sb.reshape(BATCH, 1, HG_HEADS, HG_DK, HG_DV)
    return (y_prompt, y_sample, new_sf, new_sb, k_c[:, None], v_c[:, None])
```

```python
import functools

import numpy as np
import jax
import jax.numpy as jnp
from jax import lax
from jax.experimental import pallas as pl
from jax.experimental.pallas import tpu as pltpu

F32 = jnp.float32
BF16 = jnp.bfloat16
I32 = jnp.int32

D = 1024
BATCH = 32
SEQ = 256
DEPTH = 2
DEC_BATCH = 8
DEC_SEQ = 1024
PAST_LEN = 512
GRID_W = 64
HG_HEADS = 8
HG_DK = 128
HG_DV = 128
CHUNK = 64
NA_HEADS = 16
NA_HD = 64
WIN_R = 8
WIN_C = 16
N_EXPERTS = 64
TOP_K = 6
N_GROUPS = 8
TOPK_GROUPS = 4
D_EXPERT = 256
D_SHARED = 256
ROUTED_SCALE = 2.5
EPS = 1e-6

NP = BATCH * SEQ
NS = DEC_BATCH * DEC_SEQ
NT = NP + NS
N_MOD = 16
NEG = -1e30

LANES = 128
SUBLANES = 8
BF16_ROWS = 16
VMEM_LIMIT = 56 * 1024 * 1024

TM = 256
SUB = 16
GLA_FAST_MAX = 60.0
MOE_TM = 256
PIECE = BF16_ROWS
R1 = MOE_TM * TOP_K + N_EXPERTS * (PIECE - 1) + 64
assert R1 % PIECE == 0 and R1 % LANES == 0
R1B = R1 // PIECE
N_TILES = NT // MOE_TM
P_MAX = N_TILES * R1B
G_PIECES = 32
NCH = P_MAX // G_PIECES + N_EXPERTS
N_SLOTS = 3
P_TOT = P_MAX + R1B
assert N_SLOTS * G_PIECES <= R1B


def _cparams(sem):
    return pltpu.CompilerParams(dimension_semantics=sem, vmem_limit_bytes=VMEM_LIMIT)


def _dot(a, b):
    return jnp.dot(a, b, preferred_element_type=F32)


def _dot_nt(a, b):
    return lax.dot_general(a, b, (((1,), (1,)), ((), ())), preferred_element_type=F32)


def _dot_tn(a, b):
    return lax.dot_general(a, b, (((0,), (0,)), ((), ())), preferred_element_type=F32)


def _silu(x):
    return x * jax.nn.sigmoid(x)


def _mod_group(i, tm):
    r = i * tm
    return jnp.where(r < NP, 0, 1 + (r - NP) // DEC_SEQ)


def _norm_mod(x, nw, shift, scale):
    ms = jnp.mean(x * x, axis=-1, keepdims=True)
    y = x * lax.rsqrt(ms + EPS) * nw
    return y * (1.0 + scale) + shift


def _mod_kernel(c_ref, w_ref, b_ref, o_ref):
    s = _silu(c_ref[...]).astype(BF16)
    o_ref[0] = _dot(s, w_ref[0].astype(BF16)) + b_ref[0]


def _modulation(cvec, ada_w, ada_b):
    cw = 1536
    n = ada_w.shape[-1]
    out = pl.pallas_call(
        _mod_kernel,
        grid=(DEPTH, n // cw),
        in_specs=[
            pl.BlockSpec((N_MOD, D), lambda l, j: (0, 0)),
            pl.BlockSpec((1, D, cw), lambda l, j: (l, 0, j)),
            pl.BlockSpec((1, 1, cw), lambda l, j: (l, 0, j)),
        ],
        out_specs=pl.BlockSpec((1, N_MOD, cw), lambda l, j: (l, 0, j)),
        out_shape=jax.ShapeDtypeStruct((DEPTH, N_MOD, n), F32),
        compiler_params=_cparams(("arbitrary", "arbitrary")),
        name="modulation",
    )(cvec, ada_w, ada_b.reshape(DEPTH, 1, n))
    return out.reshape(DEPTH, N_MOD, 6, D)


def _stream_specs(x, tm):
    if not isinstance(x, tuple):
        return [pl.BlockSpec((tm, D), lambda i: (i, 0))], (x,)
    n_p = NP // tm
    return [pl.BlockSpec((tm, D), lambda i: (jnp.minimum(i, n_p - 1), 0)),
            pl.BlockSpec((tm, D), lambda i: (jnp.maximum(i - n_p, 0), 0))], x


def _stream_tile(refs, tm):
    if len(refs) == 1:
        return refs[0][...]
    return jnp.where(pl.program_id(0) < NP // tm, refs[0][...], refs[1][...])


def _proj_kernel(*refs, cw, n_x):
    nw_ref, mod_ref, w_ref, o_ref = refs[n_x:]
    x = _stream_tile(refs[:n_x], TM)
    h = _norm_mod(x, nw_ref[...], mod_ref[0, 0:1, :], mod_ref[0, 1:2, :]).astype(BF16)
    for j in range(w_ref.shape[1] // cw):
        o_ref[:, j * cw:(j + 1) * cw] = _dot(h, w_ref[:, j * cw:(j + 1) * cw])


def _norm_proj(x, nw, mod, w):
    n = w.shape[1]
    x_specs, xs = _stream_specs(x, TM)
    return pl.pallas_call(
        functools.partial(_proj_kernel, cw=512, n_x=len(xs)),
        grid=(NT // TM,),
        in_specs=x_specs + [
            pl.BlockSpec((1, D), lambda i: (0, 0)),
            pl.BlockSpec((1, 6, D), lambda i: (_mod_group(i, TM), 0, 0)),
            pl.BlockSpec((D, n), lambda i: (0, 0)),
        ],
        out_specs=pl.BlockSpec((TM, n), lambda i: (i, 0)),
        out_shape=jax.ShapeDtypeStruct((NT, n), F32),
        compiler_params=_cparams(("arbitrary",)),
        name="norm_proj",
    )(*xs, nw.reshape(1, D), mod, w)


def _split3(x):
    hi = x.astype(BF16)
    r = x - hi.astype(F32)
    mid = r.astype(BF16)
    lo = (r - mid.astype(F32)).astype(BF16)
    return hi, mid, lo


def _gla_safe(q, kk, v, g3, st_ref, row_refs, rev):
    bs_ref, ks_ref, kd8_ref = row_refs

    n_stack = 4
    t_io = lax.broadcasted_iota(I32, (n_stack * CHUNK, CHUNK), 0)
    u_io = lax.broadcasted_iota(I32, (n_stack * CHUNK, CHUNK), 1)
    which = t_io // CHUNK
    tt = t_io - which * CHUNK
    b16 = (tt // SUB) * SUB
    b8 = (tt // SUBLANES) * SUBLANES
    if rev:
        lim = jnp.where(which == 0, tt, jnp.where(which == 1, b16 + SUB, jnp.where(which == 2, b8 + SUBLANES, b8)))
        pick = u_io >= lim
    else:
        lim = jnp.where(which == 0, tt, jnp.where(which == 1, b16 - 1,
                                                  jnp.where(which == 2, b8 - 1, b8 + SUBLANES - 1)))
        pick = u_io <= lim
    tri = jnp.where(pick, 1.0, 0.0).astype(BF16)
    hi, mid, lo = g3
    cs = _dot(tri, hi) + _dot(tri, mid) + _dot(tri, lo)
    b, r16, r8, e8 = (cs[j * CHUNK:(j + 1) * CHUNK] for j in range(n_stack))
    last = 0 if rev else CHUNK - 1
    tot = b[last:last + 1]

    qe = (q * jnp.exp(b)).astype(BF16)
    qd = (q * jnp.exp(b - r16)).astype(BF16)
    qd8 = q * jnp.exp(b - r8)
    kdec = (kk * jnp.exp(tot - b)).astype(BF16)
    vb = v.astype(BF16)
    dec_tot = jnp.exp(tot)
    kd8 = kk * jnp.exp(e8 - b)
    for h in range(HG_HEADS):
        sl = slice(h * LANES, (h + 1) * LANES)
        bs_ref[h] = b[:, sl]
        ks_ref[h] = kk[:, sl]
        kd8_ref[h] = kd8[:, sl]

    def row(ref, h, s):
        return jnp.broadcast_to(ref[h, s:s + 1, :], (SUBLANES, LANES))

    lane = lax.broadcasted_iota(I32, (SUBLANES, LANES), 1)
    row8 = lax.broadcasted_iota(I32, (SUBLANES, LANES), 0)
    a_in = [[] for _ in range(HG_HEADS)]
    for tb in range(CHUNK // SUBLANES):
        t0 = tb * SUBLANES
        blk0 = (t0 // SUB) * SUB
        other = blk0 + SUBLANES if t0 == blk0 else blk0
        keep = (row8 + t0 <= lane) if rev else (row8 + t0 >= lane)
        for h in range(HG_HEADS):
            sl = slice(h * LANES, (h + 1) * LANES)
            q_t = q[t0:t0 + SUBLANES, sl]
            b_t = b[t0:t0 + SUBLANES, sl]
            acc = jnp.zeros((SUBLANES, LANES), F32)
            for s in range(t0, t0 + SUBLANES):
                p = q_t * jnp.exp(b_t - row(bs_ref, h, s)) * row(ks_ref, h, s)
                acc = jnp.where(lane == s, jnp.sum(p, axis=-1, keepdims=True), acc)
            acc = jnp.where(keep, acc, 0.0)
            if (t0 == blk0) == rev:
                qd8_t = qd8[t0:t0 + SUBLANES, sl]
                for s in range(other, other + SUBLANES):
                    p = qd8_t * row(kd8_ref, h, s)
                    acc = jnp.where(lane == s, jnp.sum(p, axis=-1, keepdims=True), acc)
            a_in[h].append(acc)

    n_blk = CHUNK // SUB
    o_heads = []
    for h in range(HG_HEADS):
        sl = slice(h * LANES, (h + 1) * LANES)
        st = st_ref[sl, :]
        o_h = _dot_nt(qe[:, sl], st.astype(BF16))
        rows = []
        for i in range(n_blk):
            lo_r, hi_r = ((i + 1) * SUB, CHUNK) if rev else (0, i * SUB)
            if hi_r == lo_r:
                rows.append(jnp.zeros((SUB, CHUNK), F32))
                continue
            r_i = r16[i * SUB:i * SUB + 1, sl]
            kd = kk[lo_r:hi_r, sl] * jnp.exp(r_i - b[lo_r:hi_r, sl])
            pads = [jnp.zeros((lo_r, LANES), F32)] if lo_r else []
            pads_hi = [jnp.zeros((CHUNK - hi_r, LANES), F32)] if hi_r < CHUNK else []
            kd = jnp.concatenate(pads + [kd] + pads_hi, axis=0).astype(BF16)
            rows.append(_dot_nt(qd[i * SUB:(i + 1) * SUB, sl], kd))
        a1 = jnp.concatenate(rows, axis=0) + jnp.concatenate(a_in[h], axis=0)[:, :CHUNK]
        o_h = o_h + _dot(a1.astype(BF16), vb[:, sl])
        o_heads.append(o_h)
        st_ref[sl, :] = st * dec_tot[:, sl] + _dot_tn(vb[:, sl], kdec[:, sl])
    return jnp.concatenate(o_heads, axis=1)


def _gla_fast(q, kk, v, b, tot, st_ref, rev):
    qe = (q * jnp.exp(b)).astype(BF16)
    ke = (kk * jnp.exp(-b)).astype(BF16)
    kdec = (kk * jnp.exp(tot - b)).astype(BF16)
    vb = v.astype(BF16)
    dec_tot = jnp.exp(tot)
    t_io = lax.broadcasted_iota(I32, (CHUNK, CHUNK), 0)
    s_io = lax.broadcasted_iota(I32, (CHUNK, CHUNK), 1)
    keep = (s_io >= t_io) if rev else (s_io <= t_io)
    o_heads = []
    for h in range(HG_HEADS):
        sl = slice(h * LANES, (h + 1) * LANES)
        st = st_ref[sl, :]
        a = jnp.where(keep, _dot_nt(qe[:, sl], ke[:, sl]), 0.0).astype(BF16)
        o_heads.append(_dot_nt(qe[:, sl], st.astype(BF16)) + _dot(a, vb[:, sl]))
        st_ref[sl, :] = st * dec_tot[:, sl] + _dot_tn(vb[:, sl], kdec[:, sl])
    return jnp.concatenate(o_heads, axis=1)


def _gla_direction(q, fraw, v, lb, st_ref, row_refs, o_ref, rev):
    f = lb + (1.0 - lb) * jax.nn.sigmoid(fraw)
    kk = 1.0 - f
    g3 = _split3(jnp.log(f))
    t_io = lax.broadcasted_iota(I32, (CHUNK, CHUNK), 0)
    u_io = lax.broadcasted_iota(I32, (CHUNK, CHUNK), 1)
    tri = jnp.where((u_io >= t_io) if rev else (u_io <= t_io), 1.0, 0.0).astype(BF16)
    b = _dot(tri, g3[0]) + _dot(tri, g3[1]) + _dot(tri, g3[2])
    last = 0 if rev else CHUNK - 1
    tot = b[last:last + 1]
    mild = jnp.min(tot) >= -GLA_FAST_MAX

    @pl.when(mild)
    def _():
        o_ref[...] = _gla_fast(q, kk, v, b, tot, st_ref, rev)

    @pl.when(jnp.logical_not(mild))
    def _():
        o_ref[...] = _gla_safe(q, kk, v, g3, st_ref, row_refs, rev)


def _gla_kernel(qf_ref, ff_ref, vf_ref, qb_ref, fb_ref, vb_ref, lb_ref, s0f_ref, s0b_ref,
                of_ref, ob_ref, sf_ref, sb_ref, stf, stb, bs, ks, kd8):
    i = pl.program_id(0)
    n_p = NP // CHUNK
    is_prompt = i < n_p
    c = jnp.where(is_prompt, i % (SEQ // CHUNK), (i - n_p) % (DEC_SEQ // CHUNK))
    n_c = jnp.where(is_prompt, SEQ // CHUNK, DEC_SEQ // CHUNK)

    @pl.when(jnp.logical_and(c == 0, is_prompt))
    def _():
        stf[...] = jnp.zeros_like(stf)
        stb[...] = jnp.zeros_like(stb)

    @pl.when(jnp.logical_and(c == 0, jnp.logical_not(is_prompt)))
    def _():
        for h in range(HG_HEADS):
            sl = slice(h * LANES, (h + 1) * LANES)
            stf[sl, :] = s0f_ref[0, sl, :].T
            stb[sl, :] = s0b_ref[0, sl, :].T

    lb = lb_ref[...]
    rows = (bs, ks, kd8)
    _gla_direction(qf_ref[...], ff_ref[...], vf_ref[...], lb, stf, rows, of_ref, False)
    _gla_direction(qb_ref[...], fb_ref[...], vb_ref[...], lb, stb, rows, ob_ref, True)

    @pl.when(jnp.logical_and(c == n_c - 1, is_prompt))
    def _():
        for h in range(HG_HEADS):
            sl = slice(h * LANES, (h + 1) * LANES)
            sf_ref[0, sl, :] = stf[sl, :].T
            sb_ref[0, sl, :] = stb[sl, :].T


def _gla(proj, lb, s0f, s0b):
    n_p = NP // CHUNK
    cp = SEQ // CHUNK
    cs = DEC_SEQ // CHUNK

    def bwd_blk(i):
        jp = (i // cp) * cp + (cp - 1 - i % cp)
        j = i - n_p
        js = n_p + (j // cs) * cs + (cs - 1 - j % cs)
        return jnp.where(i < n_p, jp, js)

    def req(i):
        return jnp.maximum(i - n_p, 0) // cs

    def preq(i):
        return jnp.minimum(i // cp, BATCH - 1)

    hk = HG_HEADS * HG_DK
    return pl.pallas_call(
        _gla_kernel,
        grid=(NT // CHUNK,),
        in_specs=[
            pl.BlockSpec((CHUNK, D), lambda i: (i, 0)),
            pl.BlockSpec((CHUNK, D), lambda i: (i, 1)),
            pl.BlockSpec((CHUNK, D), lambda i: (i, 3)),
            pl.BlockSpec((CHUNK, D), lambda i: (bwd_blk(i), 0)),
            pl.BlockSpec((CHUNK, D), lambda i: (bwd_blk(i), 2)),
            pl.BlockSpec((CHUNK, D), lambda i: (bwd_blk(i), 3)),
            pl.BlockSpec((1, hk), lambda i: (0, 0)),
            pl.BlockSpec((1, hk, HG_DV), lambda i: (req(i), 0, 0)),
            pl.BlockSpec((1, hk, HG_DV), lambda i: (req(i), 0, 0)),
        ],
        out_specs=[
            pl.BlockSpec((CHUNK, D), lambda i: (i, 0)),
            pl.BlockSpec((CHUNK, D), lambda i: (bwd_blk(i), 0)),
            pl.BlockSpec((1, hk, HG_DV), lambda i: (preq(i), 0, 0)),
            pl.BlockSpec((1, hk, HG_DV), lambda i: (preq(i), 0, 0)),
        ],
        out_shape=[
            jax.ShapeDtypeStruct((NT, D), F32),
            jax.ShapeDtypeStruct((NT, D), F32),
            jax.ShapeDtypeStruct((BATCH, hk, HG_DV), F32),
            jax.ShapeDtypeStruct((BATCH, hk, HG_DV), F32),
        ],
        scratch_shapes=[pltpu.VMEM((hk, HG_DV), F32), pltpu.VMEM((hk, HG_DV), F32)]
        + [pltpu.VMEM((HG_HEADS, CHUNK, HG_DK), F32)] * 3,
        compiler_params=_cparams(("arbitrary",)),
        name="gla_scan",
    )(proj, proj, proj, proj, proj, proj, lb.reshape(1, hk), s0f, s0b)


def _hgrn_out_kernel(of_ref, ob_ref, gate_ref, mod_ref, gn_ref, w_ref, *refs):
    o_ref = refs[-1]
    x = _stream_tile(refs[:-1], TM)
    o = of_ref[...] + ob_ref[...]
    gn = gn_ref[...]
    segs = []
    for h in range(HG_HEADS):
        seg = o[:, h * LANES:(h + 1) * LANES]
        ms = jnp.mean(seg * seg, axis=-1, keepdims=True)
        segs.append(seg * lax.rsqrt(ms + EPS) * gn)
    y = (jnp.concatenate(segs, axis=1) * _silu(gate_ref[...])).astype(BF16)
    o_ref[...] = x + mod_ref[0, 2:3, :] * _dot(y, w_ref[...])


def _hgrn_out(o_f, o_b, proj, x, mod, gn_w, w_out):
    x_specs, xs = _stream_specs(x, TM)
    return pl.pallas_call(
        _hgrn_out_kernel,
        grid=(NT // TM,),
        in_specs=[
            pl.BlockSpec((TM, D), lambda i: (i, 0)),
            pl.BlockSpec((TM, D), lambda i: (i, 0)),
            pl.BlockSpec((TM, D), lambda i: (i, 4)),
            pl.BlockSpec((1, 6, D), lambda i: (_mod_group(i, TM), 0, 0)),
            pl.BlockSpec((1, HG_DV), lambda i: (0, 0)),
            pl.BlockSpec((D, D), lambda i: (0, 0)),
        ] + x_specs,
        out_specs=pl.BlockSpec((TM, D), lambda i: (i, 0)),
        out_shape=jax.ShapeDtypeStruct((NT, D), F32),
        compiler_params=_cparams(("arbitrary",)),
        name="hgrn_out",
    )(o_f, o_b, proj, mod, gn_w.reshape(1, HG_DV), w_out, *xs)


def _attn_out_kernel(ap_ref, as_ref, x_ref, mod_ref, w_ref, o_ref):
    a = jnp.where(pl.program_id(0) < NP // TM, ap_ref[...], as_ref[...]).astype(BF16)
    o_ref[...] = x_ref[...] + mod_ref[0, 2:3, :] * _dot(a, w_ref[...])


def _attn_out(a_p, a_s, x, mod, w_o):
    n_p = NP // TM
    return pl.pallas_call(
        _attn_out_kernel,
        grid=(NT // TM,),
        in_specs=[
            pl.BlockSpec((TM, D), lambda i: (jnp.minimum(i, n_p - 1), 0)),
            pl.BlockSpec((TM, D), lambda i: (jnp.maximum(i - n_p, 0), 0)),
            pl.BlockSpec((TM, D), lambda i: (i, 0)),
            pl.BlockSpec((1, 6, D), lambda i: (_mod_group(i, TM), 0, 0)),
            pl.BlockSpec((D, D), lambda i: (0, 0)),
        ],
        out_specs=pl.BlockSpec((TM, D), lambda i: (i, 0)),
        out_shape=jax.ShapeDtypeStruct((NT, D), F32),
        compiler_params=_cparams(("arbitrary",)),
        name="attn_out",
    )(a_p, a_s, x, mod, w_o)


def _head_pair_masks():
    upper = lax.broadcasted_iota(I32, (1, LANES), 1) >= NA_HD
    return upper


def _attn_ctx_kernel(q_ref, k_ref, v_ref, o_ref):
    upper = _head_pair_masks()
    for p in range(NA_HEADS // 2):
        sl = slice(p * LANES, (p + 1) * LANES)
        qp = q_ref[:, sl] * (NA_HD ** -0.5)
        kp = k_ref[:, sl].astype(BF16)
        vp = v_ref[:, sl].astype(BF16)
        outs = []
        for u in range(2):
            qm = jnp.where(upper if u else jnp.logical_not(upper), qp, 0.0).astype(BF16)
            s = _dot_nt(qm, kp)
            e = jnp.exp(s - jnp.max(s, axis=-1, keepdims=True))
            outs.append(_dot(e.astype(BF16), vp) / jnp.sum(e, axis=-1, keepdims=True))
        o_ref[:, sl] = jnp.where(upper, outs[1], outs[0])


def _attn_ctx(qkv):
    return pl.pallas_call(
        _attn_ctx_kernel,
        grid=(BATCH,),
        in_specs=[
            pl.BlockSpec((SEQ, D), lambda b: (b, 0)),
            pl.BlockSpec((SEQ, D), lambda b: (b, 1)),
            pl.BlockSpec((SEQ, D), lambda b: (b, 2)),
        ],
        out_specs=pl.BlockSpec((SEQ, D), lambda b: (b, 0)),
        out_shape=jax.ShapeDtypeStruct((NP, D), F32),
        compiler_params=_cparams(("arbitrary",)),
        name="attn_ctx",
    )(qkv, qkv, qkv)


def _attn_lat_kernel(q_ref, k_ref, v_ref, kc_ref, vc_ref, bias_ref, o_ref, *, tq):
    upper = _head_pair_masks()
    kl = k_ref[...].astype(BF16)
    vl = v_ref[...].astype(BF16)
    kc = kc_ref[0].astype(BF16)
    vc = vc_ref[0].astype(BF16)
    for t in range(DEC_SEQ // tq):
        rs = slice(t * tq, (t + 1) * tq)
        qp = q_ref[rs, :] * (NA_HD ** -0.5)
        outs = []
        for u in range(2):
            qm = jnp.where(upper if u else jnp.logical_not(upper), qp, 0.0).astype(BF16)
            s_l = _dot_nt(qm, kl) + bias_ref[u, rs, :]
            s_c = _dot_nt(qm, kc)
            m = jnp.maximum(jnp.max(s_l, axis=-1, keepdims=True), jnp.max(s_c, axis=-1, keepdims=True))
            e_l = jnp.exp(s_l - m)
            e_c = jnp.exp(s_c - m)
            den = jnp.sum(e_l, axis=-1, keepdims=True) + jnp.sum(e_c, axis=-1, keepdims=True)
            outs.append((_dot(e_l.astype(BF16), vl) + _dot(e_c.astype(BF16), vc)) / den)
        o_ref[rs, :] = jnp.where(upper, outs[1], outs[0])


def _attn_lat(qkv, k_ctx, v_ctx, bias):
    npair = NA_HEADS // 2
    rb0 = NP // DEC_SEQ
    return pl.pallas_call(
        functools.partial(_attn_lat_kernel, tq=256),
        grid=(npair, DEC_BATCH),
        in_specs=[
            pl.BlockSpec((DEC_SEQ, LANES), lambda p, b: (rb0 + b, p)),
            pl.BlockSpec((DEC_SEQ, LANES), lambda p, b: (rb0 + b, npair + p)),
            pl.BlockSpec((DEC_SEQ, LANES), lambda p, b: (rb0 + b, 2 * npair + p)),
            pl.BlockSpec((1, PAST_LEN, LANES), lambda p, b: (b, 0, p)),
            pl.BlockSpec((1, PAST_LEN, LANES), lambda p, b: (b, 0, p)),
            pl.BlockSpec((2, DEC_SEQ, DEC_SEQ), lambda p, b: (p, 0, 0)),
        ],
        out_specs=pl.BlockSpec((DEC_SEQ, LANES), lambda p, b: (b, p)),
        out_shape=jax.ShapeDtypeStruct((NS, D), F32),
        compiler_params=_cparams(("arbitrary", "arbitrary")),
        name="attn_lat",
    )(qkv, qkv, qkv, k_ctx, v_ctx, bias)


def _latent_bias(rpb):
    rows = DEC_SEQ // GRID_W
    kr = min(WIN_R, rows)
    ndc = 2 * WIN_C - 1
    qc = np.arange(GRID_W)
    kc = np.arange(GRID_W)
    ws = np.clip(qc - WIN_C // 2, 0, GRID_W - WIN_C)
    col_ok = (kc[None, :] >= ws[:, None]) & (kc[None, :] < ws[:, None] + WIN_C)
    dc = np.clip(kc[None, :] - qc[:, None] + WIN_C - 1, 0, ndc - 1)
    onehot = (dc[None, :, :] == np.arange(ndc)[:, None, None]).astype(np.float32)
    t2 = jnp.einsum('hrc,cqk->hrqk', rpb.astype(F32), jnp.asarray(onehot),
                    precision=lax.Precision.HIGHEST)
    t2 = jnp.where(jnp.asarray(col_ok)[None, None], t2, NEG)
    r = np.arange(rows)
    k0 = np.clip(r - kr // 2, 0, rows - kr)
    krow = np.arange(rows)
    row_ok = (krow[None, :] >= k0[:, None]) & (krow[None, :] < k0[:, None] + kr)
    dr = np.clip(krow[None, :] - r[:, None] + WIN_R - 1, 0, 2 * WIN_R - 2)
    g = jnp.take(t2, jnp.asarray(dr.reshape(-1)), axis=1).reshape(NA_HEADS, rows, rows, GRID_W, GRID_W)
    g = jnp.where(jnp.asarray(row_ok)[None, :, :, None, None], g, NEG)
    return g.transpose(0, 1, 3, 2, 4).reshape(NA_HEADS, DEC_SEQ, DEC_SEQ)


def _route_kernel(x_ref, nw_ref, mod_ref, wrh_ref, wrl_ref, rb_ref, xs_ref, pos_ref, wt_ref, npc_ref):
    @pl.when(pl.program_id(0) == N_TILES)
    def _():
        xs_ref[...] = jnp.zeros_like(xs_ref)

    @pl.when(pl.program_id(0) < N_TILES)
    def _():
        _route_tile(x_ref, nw_ref, mod_ref, wrh_ref, wrl_ref, rb_ref, xs_ref, pos_ref, wt_ref, npc_ref)


def _route_tile(x_ref, nw_ref, mod_ref, wrh_ref, wrl_ref, rb_ref, xs_ref, pos_ref, wt_ref, npc_ref):
    h = _norm_mod(x_ref[...], nw_ref[...], mod_ref[0, 3:4, :], mod_ref[0, 4:5, :])
    hb = h.astype(BF16)
    hl = (h - hb.astype(F32)).astype(BF16)
    wrh = wrh_ref[...]
    logits = _dot_nt(wrh, hb) + _dot_nt(wrh, hl) + _dot_nt(wrl_ref[...], hb)
    scores = jax.nn.sigmoid(logits)
    sel = scores + rb_ref[...]

    gsz = N_EXPERTS // N_GROUPS
    sub = lax.broadcasted_iota(I32, (gsz, MOE_TM), 0)
    ninf = -jnp.inf
    gs_rows = []
    for gi in range(N_GROUPS):
        blk = sel[gi * gsz:(gi + 1) * gsz]
        m1 = jnp.max(blk, axis=0, keepdims=True)
        first = jnp.min(jnp.where(blk == m1, sub, gsz), axis=0, keepdims=True)
        m2 = jnp.max(jnp.where(sub == first, ninf, blk), axis=0, keepdims=True)
        gs_rows.append(m1 + m2)
    cur = jnp.concatenate(gs_rows, axis=0)
    gidx = lax.broadcasted_iota(I32, (N_GROUPS, MOE_TM), 0)
    gsel = jnp.zeros((N_GROUPS, MOE_TM), F32)
    for _ in range(TOPK_GROUPS):
        m = jnp.max(cur, axis=0, keepdims=True)
        first = jnp.min(jnp.where(cur == m, gidx, N_GROUPS), axis=0, keepdims=True)
        hit = gidx == first
        gsel = jnp.where(hit, 1.0, gsel)
        cur = jnp.where(hit, ninf, cur)
    emask = jnp.concatenate(
        [jnp.broadcast_to(gsel[gi:gi + 1], (gsz, MOE_TM)) for gi in range(N_GROUPS)], axis=0)
    masked = jnp.where(emask > 0.5, sel, ninf)
    eidx = lax.broadcasted_iota(I32, (N_EXPERTS, MOE_TM), 0)
    chosen = jnp.zeros((N_EXPERTS, MOE_TM), F32)
    hits, wsel = [], []
    for _ in range(TOP_K):
        m = jnp.max(masked, axis=0, keepdims=True)
        first = jnp.min(jnp.where(masked == m, eidx, N_EXPERTS), axis=0, keepdims=True)
        hit = eidx == first
        hits.append(hit)
        wsel.append(jnp.sum(jnp.where(hit, scores, 0.0), axis=0, keepdims=True))
        chosen = jnp.where(hit, 1.0, chosen)
        masked = jnp.where(hit, ninf, masked)
    wsum = wsel[0]
    for w in wsel[1:]:
        wsum = wsum + w

    n_io = lax.broadcasted_iota(I32, (MOE_TM, MOE_TM), 0)
    m_io = lax.broadcasted_iota(I32, (MOE_TM, MOE_TM), 1)
    earlier = jnp.where(n_io < m_io, 1.0, 0.0).astype(BF16)
    rank = _dot(chosen.astype(BF16), earlier)
    cnt = jnp.sum(chosen, axis=1, keepdims=True)
    npc = jnp.floor((cnt + (PIECE - 1)) * (1.0 / PIECE))
    e_io = lax.broadcasted_iota(I32, (N_EXPERTS, N_EXPERTS), 0)
    f_io = lax.broadcasted_iota(I32, (N_EXPERTS, N_EXPERTS), 1)
    below = jnp.where(f_io < e_io, 1.0, 0.0).astype(BF16)
    npc_l = jnp.broadcast_to(npc, (N_EXPERTS, LANES))
    start = _dot(below, npc_l.astype(BF16))[:, 0:1] * PIECE
    slot = start + rank

    j_io = lax.broadcasted_iota(I32, (R1, MOE_TM), 0)
    onehot = jnp.zeros((R1, MOE_TM), F32)
    pos_rows, wt_rows = [], []
    for k in range(TOP_K):
        pos_k = jnp.sum(jnp.where(hits[k], slot, 0.0), axis=0, keepdims=True).astype(I32)
        pos_rows.append(pos_k)
        wt_rows.append(wsel[k] / wsum * ROUTED_SCALE)
        onehot = jnp.where(j_io == pos_k, 1.0, onehot)
    xs_ref[...] = _dot(onehot.astype(BF16), hb).astype(BF16)
    pad = SUBLANES - TOP_K
    pos_ref[...] = jnp.concatenate(pos_rows + [jnp.full((pad, MOE_TM), -1, I32)], axis=0)
    wt_ref[...] = jnp.concatenate(wt_rows + [jnp.zeros((pad, MOE_TM), F32)], axis=0)
    npc_ref[0] = npc_l.astype(I32)


def _route(x, nw, mod, wr_hi, wr_lo, rbias):
    def tile(i):
        return jnp.minimum(i, N_TILES - 1)

    return pl.pallas_call(
        _route_kernel,
        grid=(N_TILES + 1,),
        in_specs=[
            pl.BlockSpec((MOE_TM, D), lambda i: (tile(i), 0)),
            pl.BlockSpec((1, D), lambda i: (0, 0)),
            pl.BlockSpec((1, 6, D), lambda i: (_mod_group(tile(i), MOE_TM), 0, 0)),
            pl.BlockSpec((N_EXPERTS, D), lambda i: (0, 0)),
            pl.BlockSpec((N_EXPERTS, D), lambda i: (0, 0)),
            pl.BlockSpec((N_EXPERTS, 1), lambda i: (0, 0)),
        ],
        out_specs=[
            pl.BlockSpec((R1, D), lambda i: (i, 0)),
            pl.BlockSpec((SUBLANES, MOE_TM), lambda i: (0, tile(i))),
            pl.BlockSpec((SUBLANES, MOE_TM), lambda i: (0, tile(i))),
            pl.BlockSpec((1, N_EXPERTS, LANES), lambda i: (tile(i), 0, 0)),
        ],
        out_shape=[
            jax.ShapeDtypeStruct((P_TOT * PIECE, D), BF16),
            jax.ShapeDtypeStruct((SUBLANES, NT), I32),
            jax.ShapeDtypeStruct((SUBLANES, NT), F32),
            jax.ShapeDtypeStruct((N_TILES, N_EXPERTS, LANES), I32),
        ],
        compiler_params=_cparams(("arbitrary",)),
        name="moe_route",
    )(x, nw.reshape(1, D), mod, wr_hi, wr_lo, rbias.reshape(N_EXPERTS, 1))


def _piece_lists(npc):
    t, e = npc.shape
    hp = lax.Precision.HIGHEST
    npc_t = npc.T.astype(F32)
    start_t = (jnp.cumsum(npc, axis=1) - npc).T.astype(F32)
    tile_end = jnp.cumsum(npc_t, axis=1)
    n_e = tile_end[:, -1]
    pe_end = jnp.cumsum(n_e)
    pe_off = pe_end - n_e
    p = jnp.arange(P_MAX, dtype=F32)
    e_p = jnp.minimum(jnp.sum((pe_end[None, :] <= p[:, None]).astype(I32), axis=1), e - 1)
    oh_e = (e_p[:, None] == jnp.arange(e, dtype=I32)[None, :]).astype(F32)
    tab = jnp.concatenate([tile_end, start_t, npc_t, pe_off[:, None]], axis=1)
    row = jnp.dot(oh_e, tab, precision=hp)
    te_p, st_p, np_p, off_p = row[:, :t], row[:, t:2 * t], row[:, 2 * t:3 * t], row[:, 3 * t]
    local = p - off_p
    t_p = jnp.minimum(jnp.sum((te_p <= local[:, None]).astype(I32), axis=1), t - 1)
    oh_t = t_p[:, None] == jnp.arange(t, dtype=I32)[None, :]

    def pick(a):
        return jnp.sum(jnp.where(oh_t, a, 0.0), axis=1)

    src = t_p * R1B + (pick(st_p) + local - (pick(te_p) - pick(np_p))).astype(I32)
    src = jnp.concatenate([jnp.clip(src, 0, P_MAX - 1), jnp.zeros((G_PIECES,), I32)])

    nch = jnp.floor((n_e + (G_PIECES - 1)) * (1.0 / G_PIECES))
    ch_end = jnp.cumsum(nch)
    ch_off = jnp.concatenate([jnp.zeros((1,), F32), ch_end]).astype(I32)
    c = jnp.arange(-N_SLOTS, NCH + 2, dtype=F32)
    ce = jnp.minimum(jnp.sum((ch_end[None, :] <= c[:, None]).astype(I32), axis=1), e - 1)
    oh_c = (ce[:, None] == jnp.arange(e, dtype=I32)[None, :]).astype(F32)
    crow = jnp.dot(oh_c, jnp.stack([ch_end - nch, pe_off, n_e], axis=1), precision=hp)
    k_in = c - crow[:, 0]
    live = jnp.logical_and(c >= 0, c < ch_end[-1])
    cn = jnp.where(live, jnp.clip(crow[:, 2] - G_PIECES * k_in, 0, G_PIECES), 0.0)
    cs = jnp.where(cn > 0, crow[:, 1] + G_PIECES * k_in, 0.0)
    return src, ch_off, cs.astype(I32), cn.astype(I32)


def _ffn_kernel(src_ref, choff_ref, cs_ref, cn_ref, xs_in, wg_ref, wu_ref, wd_ref, xs_out,
                xbuf, ybuf, wgb, wub, wdb, gsem, ssem):
    e = pl.program_id(0)
    total = choff_ref[N_EXPERTS]

    def start_gather(ch):
        sl = lax.rem(ch + N_SLOTS, N_SLOTS)
        base = cs_ref[ch + N_SLOTS]
        n = cn_ref[ch + N_SLOTS]
        for i in range(G_PIECES):
            idx = src_ref[base + jnp.where(i < n, i, 0)]
            pltpu.make_async_copy(xs_in.at[idx], xbuf.at[sl, i], gsem.at[sl]).start()

    def wait_gather(ch):
        sl = lax.rem(ch + N_SLOTS, N_SLOTS)
        pltpu.make_async_copy(xs_in.at[pl.ds(0, G_PIECES)], xbuf.at[sl], gsem.at[sl]).wait()

    def start_scatter(ch):
        sl = lax.rem(ch + N_SLOTS, N_SLOTS)
        base = cs_ref[ch + N_SLOTS]
        n = cn_ref[ch + N_SLOTS]
        for i in range(G_PIECES):
            idx = jnp.where(i < n, src_ref[base + i], P_MAX + sl * G_PIECES + i)
            pltpu.make_async_copy(ybuf.at[sl, i], xs_out.at[idx], ssem.at[sl]).start()

    def wait_scatter(ch):
        sl = lax.rem(ch + N_SLOTS, N_SLOTS)
        pltpu.make_async_copy(ybuf.at[sl], xs_out.at[pl.ds(0, G_PIECES)], ssem.at[sl]).wait()

    @pl.when(e == 0)
    def _():
        ybuf[...] = jnp.zeros_like(ybuf)
        start_gather(0)
        start_gather(1)
        start_scatter(-3)
        start_scatter(-2)

    wgb[...] = wg_ref[0].astype(BF16)
    wub[...] = wu_ref[0].astype(BF16)
    wdb[...] = wd_ref[0].astype(BF16)

    def chunk(c, carry):
        sl = lax.rem(c, N_SLOTS)
        wait_gather(c)
        wait_scatter(c - 3)
        x = xbuf[sl].reshape(G_PIECES * PIECE, D)
        hid = (_silu(_dot(x, wgb[...])) * _dot(x, wub[...])).astype(BF16)
        start_gather(c + 2)
        start_scatter(c - 1)
        ybuf[sl] = _dot(hid, wdb[...]).astype(BF16).reshape(G_PIECES, PIECE, D)
        return carry

    lax.fori_loop(choff_ref[e], choff_ref[e + 1], chunk, 0)

    @pl.when(e == N_EXPERTS - 1)
    def _():
        start_scatter(total - 1)
        wait_gather(total)
        wait_gather(total + 1)
        wait_scatter(total - 3)
        wait_scatter(total - 2)
        wait_scatter(total - 1)


def _expert_ffn(xs, lists, w_gate, w_up, w_down):
    src, ch_off, cs, cn = lists
    grid_spec = pltpu.PrefetchScalarGridSpec(
        num_scalar_prefetch=4,
        grid=(N_EXPERTS,),
        in_specs=[
            pl.BlockSpec(memory_space=pl.ANY),
            pl.BlockSpec((1, D, D_EXPERT), lambda e, *_: (e, 0, 0)),
            pl.BlockSpec((1, D, D_EXPERT), lambda e, *_: (e, 0, 0)),
            pl.BlockSpec((1, D_EXPERT, D), lambda e, *_: (e, 0, 0)),
        ],
        out_specs=pl.BlockSpec(memory_space=pl.ANY),
        scratch_shapes=[
            pltpu.VMEM((N_SLOTS, G_PIECES, PIECE, D), BF16),
            pltpu.VMEM((N_SLOTS, G_PIECES, PIECE, D), BF16),
            pltpu.VMEM((D, D_EXPERT), BF16),
            pltpu.VMEM((D, D_EXPERT), BF16),
            pltpu.VMEM((D_EXPERT, D), BF16),
            pltpu.SemaphoreType.DMA((N_SLOTS,)),
            pltpu.SemaphoreType.DMA((N_SLOTS,)),
        ],
    )
    out = pl.pallas_call(
        _ffn_kernel,
        grid_spec=grid_spec,
        out_shape=jax.ShapeDtypeStruct((P_TOT, PIECE, D), BF16),
        input_output_aliases={4: 0},
        compiler_params=_cparams(("arbitrary",)),
        name="moe_ffn",
    )(src, ch_off, cs, cn, xs.reshape(P_TOT, PIECE, D), w_gate, w_up, w_down)
    return out.reshape(P_TOT * PIECE, D)


def _combine_kernel(ys_ref, pos_ref, wt_ref, x_ref, nw_ref, mod_ref, wsg_ref, wsu_ref, wsd_ref, fw_ref,
                    *o_refs, final):
    x = x_ref[...]
    h = _norm_mod(x, nw_ref[...], mod_ref[0, 3:4, :], mod_ref[0, 4:5, :]).astype(BF16)
    shared = _dot((_silu(_dot(h, wsg_ref[...])) * _dot(h, wsu_ref[...])).astype(BF16), wsd_ref[...])
    j_io = lax.broadcasted_iota(I32, (MOE_TM, R1), 1)
    pos = pos_ref[...]
    wt = wt_ref[...]
    wm = jnp.zeros((MOE_TM, R1), F32)
    for k in range(TOP_K):
        wm = jnp.where(j_io == pos[:, k:k + 1], wt[:, k:k + 1], wm)
    routed = _dot(wm.astype(BF16), ys_ref[...])
    y = x + mod_ref[0, 5:6, :] * (routed + shared)
    if not final:
        o_refs[0][...] = y
        return
    ms = jnp.mean(y * y, axis=-1, keepdims=True)
    y = y * lax.rsqrt(ms + EPS) * fw_ref[...]
    is_prompt = pl.program_id(0) < NP // MOE_TM

    @pl.when(is_prompt)
    def _():
        o_refs[0][...] = y

    @pl.when(jnp.logical_not(is_prompt))
    def _():
        o_refs[1][...] = y


def _combine(ys, pos_t, wt_t, x, nw, mod, wsg, wsu, wsd, fw, final):
    n_p = NP // MOE_TM
    if final:
        out_specs = [pl.BlockSpec((MOE_TM, D), lambda i: (jnp.minimum(i, n_p - 1), 0)),
                     pl.BlockSpec((MOE_TM, D), lambda i: (jnp.maximum(i - n_p, 0), 0))]
        out_shape = [jax.ShapeDtypeStruct((NP, D), F32), jax.ShapeDtypeStruct((NS, D), F32)]
    else:
        out_specs = pl.BlockSpec((MOE_TM, D), lambda i: (i, 0))
        out_shape = jax.ShapeDtypeStruct((NT, D), F32)
    return pl.pallas_call(
        functools.partial(_combine_kernel, final=final),
        grid=(N_TILES,),
        in_specs=[
            pl.BlockSpec((R1, D), lambda i: (i, 0)),
            pl.BlockSpec((MOE_TM, SUBLANES), lambda i: (i, 0)),
            pl.BlockSpec((MOE_TM, SUBLANES), lambda i: (i, 0)),
            pl.BlockSpec((MOE_TM, D), lambda i: (i, 0)),
            pl.BlockSpec((1, D), lambda i: (0, 0)),
            pl.BlockSpec((1, 6, D), lambda i: (_mod_group(i, MOE_TM), 0, 0)),
            pl.BlockSpec((D, D_SHARED), lambda i: (0, 0)),
            pl.BlockSpec((D, D_SHARED), lambda i: (0, 0)),
            pl.BlockSpec((D_SHARED, D), lambda i: (0, 0)),
            pl.BlockSpec((1, D), lambda i: (0, 0)),
        ],
        out_specs=out_specs,
        out_shape=out_shape,
        compiler_params=_cparams(("arbitrary",)),
        name="moe_combine",
    )(ys, pos_t, wt_t, x, nw.reshape(1, D), mod, wsg, wsu, wsd, fw.reshape(1, D))


def _moe(x, nw, mod, w_router, rbias, w_gate, w_up, w_down, ws_gate, ws_up, ws_down, fw, final):
    wr = w_router.astype(F32).T
    wr_hi = wr.astype(BF16)
    wr_lo = (wr - wr_hi.astype(F32)).astype(BF16)
    xs, pos, wt, npc = _route(x, nw, mod, wr_hi, wr_lo, rbias.astype(F32))
    lists = _piece_lists(npc[:, :, 0])
    ys = _expert_ffn(xs, lists, w_gate, w_up, w_down)
    return _combine(ys, pos.T, wt.T, x, nw, mod, ws_gate.astype(BF16), ws_up.astype(BF16),
                    ws_down.astype(BF16), fw, final)


def kernel(x_prompt, x_sample, state_hgrn_fwd, state_hgrn_bwd, cache_na_k, cache_na_v, c, c_ctx,
           norm1_w, norm2_w, ada_w, ada_b, hgrn_w_in, hgrn_lb_logits, hgrn_gn_w, hgrn_w_out,
           na_w_qkv, na_rpb, na_w_o, moe_w_router, moe_router_bias, moe_w_gate, moe_w_up, moe_w_down,
           shared_w_gate, shared_w_up, shared_w_down, final_norm_w):
    x = (x_prompt.reshape(NP, D), x_sample.reshape(NS, D))
    cvec = jnp.concatenate([c_ctx[None, :], c, jnp.zeros((N_MOD - 1 - DEC_BATCH, D), F32)], axis=0)
    mod = _modulation(cvec, ada_w, ada_b)
    lb_table = jnp.cumsum(jax.nn.softmax(hgrn_lb_logits.astype(F32), axis=0), axis=0)
    hk = HG_HEADS * HG_DK

    sf = sb = k_c = v_c = None
    for l in range(DEPTH):
        if l % 2 == 0:
            a = l // 2
            proj = _norm_proj(x, norm1_w[l], mod[l], hgrn_w_in[a].astype(BF16))
            o_f, o_b, sf, sb = _gla(proj, lb_table[l],
                                    state_hgrn_fwd[:, a].reshape(DEC_BATCH, hk, HG_DV),
                                    state_hgrn_bwd[:, a].reshape(DEC_BATCH, hk, HG_DV))
            x = _hgrn_out(o_f, o_b, proj, x, mod[l], hgrn_gn_w[a], hgrn_w_out[a].astype(BF16))
        else:
            n = l // 2
            qkv = _norm_proj(x, norm1_w[l], mod[l], na_w_qkv[n].astype(BF16))
            att_p = _attn_ctx(qkv)
            att_s = _attn_lat(qkv, cache_na_k[:, n].reshape(DEC_BATCH, PAST_LEN, D),
                              cache_na_v[:, n].reshape(DEC_BATCH, PAST_LEN, D), _latent_bias(na_rpb[n]))
            x = _attn_out(att_p, att_s, x, mod[l], na_w_o[n].astype(BF16))
            k_c = qkv[:NP, D:2 * D].reshape(BATCH, SEQ, NA_HEADS, NA_HD)
            v_c = qkv[:NP, 2 * D:].reshape(BATCH, SEQ, NA_HEADS, NA_HD)
        x = _moe(x, norm2_w[l], mod[l], moe_w_router[l], moe_router_bias[l], moe_w_gate[l], moe_w_up[l],
                 moe_w_down[l], shared_w_gate[l], shared_w_up[l], shared_w_down[l], final_norm_w,
                 final=(l == DEPTH - 1))

    y_prompt = x[0].reshape(BATCH, SEQ, D)
    y_sample = x[1].reshape(DEC_BATCH, DEC_SEQ, D)
    new_sf = sf.reshape(BATCH, 1, HG_HEADS, HG_DK, HG_DV)
    new_sb = sb.reshape(BATCH, 1, HG_HEADS, HG_DK, HG_DV)
    return (y_prompt, y_sample, new_sf, new_sb, k_c[:, None], v_c[:, None])
```

```python
import functools

import numpy as np
import jax
import jax.numpy as jnp
from jax import lax
from jax.experimental import pallas as pl
from jax.experimental.pallas import tpu as pltpu

F32 = jnp.float32
BF16 = jnp.bfloat16
I32 = jnp.int32

D = 1024
BATCH = 32
SEQ = 256
DEPTH = 2
DEC_BATCH = 8
DEC_SEQ = 1024
PAST_LEN = 512
GRID_W = 64
HG_HEADS = 8
HG_DK = 128
HG_DV = 128
CHUNK = 64
NA_HEADS = 16
NA_HD = 64
WIN_R = 8
WIN_C = 16
N_EXPERTS = 64
TOP_K = 6
N_GROUPS = 8
TOPK_GROUPS = 4
D_EXPERT = 256
D_SHARED = 256
ROUTED_SCALE = 2.5
EPS = 1e-6

NP = BATCH * SEQ
NS = DEC_BATCH * DEC_SEQ
NT = NP + NS
N_MOD = 16
STRIP_W = 2048
assert (2 * (DEC_SEQ // GRID_W) - 1) * GRID_W <= STRIP_W
NEG = -1e30

LANES = 128
SUBLANES = 8
BF16_ROWS = 16
VMEM_LIMIT = 56 * 1024 * 1024

TM = 256
SUB = 16
GLA_FAST_MAX = 60.0
MOE_TM = 256
PIECE = BF16_ROWS
R1 = MOE_TM * TOP_K + N_EXPERTS * (PIECE - 1) + 64
assert R1 % PIECE == 0 and R1 % LANES == 0
R1B = R1 // PIECE
N_TILES = NT // MOE_TM
P_MAX = N_TILES * R1B
G_PIECES = 32
NCH = P_MAX // G_PIECES + N_EXPERTS
N_SLOTS = 3
P_TOT = P_MAX + R1B
assert N_SLOTS * G_PIECES <= R1B


def _cparams(sem):
    return pltpu.CompilerParams(dimension_semantics=sem, vmem_limit_bytes=VMEM_LIMIT)


def _dot(a, b):
    return jnp.dot(a, b, preferred_element_type=F32)


def _dot_nt(a, b):
    return lax.dot_general(a, b, (((1,), (1,)), ((), ())), preferred_element_type=F32)


def _dot_tn(a, b):
    return lax.dot_general(a, b, (((0,), (0,)), ((), ())), preferred_element_type=F32)


def _silu(x):
    return x * jax.nn.sigmoid(x)


def _mod_group(i, tm):
    r = i * tm
    return jnp.where(r < NP, 0, 1 + (r - NP) // DEC_SEQ)


def _norm_mod(x, nw, shift, scale):
    ms = jnp.mean(x * x, axis=-1, keepdims=True)
    y = x * lax.rsqrt(ms + EPS) * nw
    return y * (1.0 + scale) + shift


def _mod_kernel(c_ref, w_ref, b_ref, o_ref):
    s = _silu(c_ref[...]).astype(BF16)
    o_ref[0] = _dot(s, w_ref[0].astype(BF16)) + b_ref[0]


def _modulation(cvec, ada_w, ada_b):
    cw = 1536
    n = ada_w.shape[-1]
    out = pl.pallas_call(
        _mod_kernel,
        grid=(DEPTH, n // cw),
        in_specs=[
            pl.BlockSpec((N_MOD, D), lambda l, j: (0, 0)),
            pl.BlockSpec((1, D, cw), lambda l, j: (l, 0, j)),
            pl.BlockSpec((1, 1, cw), lambda l, j: (l, 0, j)),
        ],
        out_specs=pl.BlockSpec((1, N_MOD, cw), lambda l, j: (l, 0, j)),
        out_shape=jax.ShapeDtypeStruct((DEPTH, N_MOD, n), F32),
        compiler_params=_cparams(("arbitrary", "arbitrary")),
        name="modulation",
    )(cvec, ada_w, ada_b.reshape(DEPTH, 1, n))
    return out.reshape(DEPTH, N_MOD, 6, D)


def _stream_specs(x, tm):
    if not isinstance(x, tuple):
        return [pl.BlockSpec((tm, D), lambda i: (i, 0))], (x,)
    n_p = NP // tm
    return [pl.BlockSpec((tm, D), lambda i: (jnp.minimum(i, n_p - 1), 0)),
            pl.BlockSpec((tm, D), lambda i: (jnp.maximum(i - n_p, 0), 0))], x


def _stream_tile(refs, tm):
    if len(refs) == 1:
        return refs[0][...]
    return jnp.where(pl.program_id(0) < NP // tm, refs[0][...], refs[1][...])


def _proj_kernel(*refs, cw, n_x, n_copy):
    nw_ref, mod_ref, w_ref, o_ref = refs[n_x:n_x + 4]
    copy_refs = refs[n_x + 4:]
    x = _stream_tile(refs[:n_x], TM)
    h = _norm_mod(x, nw_ref[...], mod_ref[0, 0:1, :], mod_ref[0, 1:2, :]).astype(BF16)
    is_prompt = pl.program_id(0) < NP // TM
    for j in range(w_ref.shape[1] // cw):
        val = _dot(h, w_ref[:, j * cw:(j + 1) * cw])
        o_ref[:, j * cw:(j + 1) * cw] = val
        col = j * cw - D
        if 0 <= col < n_copy * D:
            c_ref = copy_refs[col // D]

            @pl.when(is_prompt)
            def _(c_ref=c_ref, val=val, col=col):
                c_ref[:, col % D:col % D + cw] = val


def _norm_proj(x, nw, mod, w, n_copy=0):
    n = w.shape[1]
    n_p = NP // TM
    x_specs, xs = _stream_specs(x, TM)
    outs = pl.pallas_call(
        functools.partial(_proj_kernel, cw=512, n_x=len(xs), n_copy=n_copy),
        grid=(NT // TM,),
        in_specs=x_specs + [
            pl.BlockSpec((1, D), lambda i: (0, 0)),
            pl.BlockSpec((1, 6, D), lambda i: (_mod_group(i, TM), 0, 0)),
            pl.BlockSpec((D, n), lambda i: (0, 0)),
        ],
        out_specs=[pl.BlockSpec((TM, n), lambda i: (i, 0))]
        + [pl.BlockSpec((TM, D), lambda i: (jnp.minimum(i, n_p - 1), 0))] * n_copy,
        out_shape=[jax.ShapeDtypeStruct((NT, n), F32)] + [jax.ShapeDtypeStruct((NP, D), F32)] * n_copy,
        compiler_params=_cparams(("arbitrary",)),
        name="norm_proj",
    )(*xs, nw.reshape(1, D), mod, w)
    return outs if n_copy else outs[0]


def _split3(x):
    hi = x.astype(BF16)
    r = x - hi.astype(F32)
    mid = r.astype(BF16)
    lo = (r - mid.astype(F32)).astype(BF16)
    return hi, mid, lo


def _gla_safe(q, kk, v, g3, st_ref, row_refs, rev):
    bs_ref, ks_ref, kd8_ref = row_refs

    n_stack = 4
    t_io = lax.broadcasted_iota(I32, (n_stack * CHUNK, CHUNK), 0)
    u_io = lax.broadcasted_iota(I32, (n_stack * CHUNK, CHUNK), 1)
    which = t_io // CHUNK
    tt = t_io - which * CHUNK
    b16 = (tt // SUB) * SUB
    b8 = (tt // SUBLANES) * SUBLANES
    if rev:
        lim = jnp.where(which == 0, tt, jnp.where(which == 1, b16 + SUB, jnp.where(which == 2, b8 + SUBLANES, b8)))
        pick = u_io >= lim
    else:
        lim = jnp.where(which == 0, tt, jnp.where(which == 1, b16 - 1,
                                                  jnp.where(which == 2, b8 - 1, b8 + SUBLANES - 1)))
        pick = u_io <= lim
    tri = jnp.where(pick, 1.0, 0.0).astype(BF16)
    hi, mid, lo = g3
    cs = _dot(tri, hi) + _dot(tri, mid) + _dot(tri, lo)
    b, r16, r8, e8 = (cs[j * CHUNK:(j + 1) * CHUNK] for j in range(n_stack))
    last = 0 if rev else CHUNK - 1
    tot = b[last:last + 1]

    qe = (q * jnp.exp(b)).astype(BF16)
    qd = (q * jnp.exp(b - r16)).astype(BF16)
    qd8 = q * jnp.exp(b - r8)
    kdec = (kk * jnp.exp(tot - b)).astype(BF16)
    vb = v.astype(BF16)
    dec_tot = jnp.exp(tot)
    kd8 = kk * jnp.exp(e8 - b)
    for h in range(HG_HEADS):
        sl = slice(h * LANES, (h + 1) * LANES)
        bs_ref[h] = b[:, sl]
        ks_ref[h] = kk[:, sl]
        kd8_ref[h] = kd8[:, sl]

    def row(ref, h, s):
        return jnp.broadcast_to(ref[h, s:s + 1, :], (SUBLANES, LANES))

    lane = lax.broadcasted_iota(I32, (SUBLANES, LANES), 1)
    row8 = lax.broadcasted_iota(I32, (SUBLANES, LANES), 0)
    a_in = [[] for _ in range(HG_HEADS)]
    for tb in range(CHUNK // SUBLANES):
        t0 = tb * SUBLANES
        blk0 = (t0 // SUB) * SUB
        other = blk0 + SUBLANES if t0 == blk0 else blk0
        keep = (row8 + t0 <= lane) if rev else (row8 + t0 >= lane)
        for h in range(HG_HEADS):
            sl = slice(h * LANES, (h + 1) * LANES)
            q_t = q[t0:t0 + SUBLANES, sl]
            b_t = b[t0:t0 + SUBLANES, sl]
            acc = jnp.zeros((SUBLANES, LANES), F32)
            for s in range(t0, t0 + SUBLANES):
                p = q_t * jnp.exp(b_t - row(bs_ref, h, s)) * row(ks_ref, h, s)
                acc = jnp.where(lane == s, jnp.sum(p, axis=-1, keepdims=True), acc)
            acc = jnp.where(keep, acc, 0.0)
            if (t0 == blk0) == rev:
                qd8_t = qd8[t0:t0 + SUBLANES, sl]
                for s in range(other, other + SUBLANES):
                    p = qd8_t * row(kd8_ref, h, s)
                    acc = jnp.where(lane == s, jnp.sum(p, axis=-1, keepdims=True), acc)
            a_in[h].append(acc)

    n_blk = CHUNK // SUB
    o_heads = []
    for h in range(HG_HEADS):
        sl = slice(h * LANES, (h + 1) * LANES)
        st = st_ref[sl, :]
        o_h = _dot_nt(qe[:, sl], st.astype(BF16))
        rows = []
        for i in range(n_blk):
            lo_r, hi_r = ((i + 1) * SUB, CHUNK) if rev else (0, i * SUB)
            if hi_r == lo_r:
                rows.append(jnp.zeros((SUB, CHUNK), F32))
                continue
            r_i = r16[i * SUB:i * SUB + 1, sl]
            kd = kk[lo_r:hi_r, sl] * jnp.exp(r_i - b[lo_r:hi_r, sl])
            pads = [jnp.zeros((lo_r, LANES), F32)] if lo_r else []
            pads_hi = [jnp.zeros((CHUNK - hi_r, LANES), F32)] if hi_r < CHUNK else []
            kd = jnp.concatenate(pads + [kd] + pads_hi, axis=0).astype(BF16)
            rows.append(_dot_nt(qd[i * SUB:(i + 1) * SUB, sl], kd))
        a1 = jnp.concatenate(rows, axis=0) + jnp.concatenate(a_in[h], axis=0)[:, :CHUNK]
        o_h = o_h + _dot(a1.astype(BF16), vb[:, sl])
        o_heads.append(o_h)
        st_ref[sl, :] = st * dec_tot[:, sl] + _dot_tn(vb[:, sl], kdec[:, sl])
    return jnp.concatenate(o_heads, axis=1)


def _gla_fast(q, kk, v, b, tot, st_ref, rev):
    qe = (q * jnp.exp(b)).astype(BF16)
    ke = (kk * jnp.exp(-b)).astype(BF16)
    kdec = (kk * jnp.exp(tot - b)).astype(BF16)
    vb = v.astype(BF16)
    dec_tot = jnp.exp(tot)
    t_io = lax.broadcasted_iota(I32, (CHUNK, CHUNK), 0)
    s_io = lax.broadcasted_iota(I32, (CHUNK, CHUNK), 1)
    keep = (s_io >= t_io) if rev else (s_io <= t_io)
    o_heads = []
    for h in range(HG_HEADS):
        sl = slice(h * LANES, (h + 1) * LANES)
        st = st_ref[sl, :]
        a = jnp.where(keep, _dot_nt(qe[:, sl], ke[:, sl]), 0.0).astype(BF16)
        o_heads.append(_dot_nt(qe[:, sl], st.astype(BF16)) + _dot(a, vb[:, sl]))
        st_ref[sl, :] = st * dec_tot[:, sl] + _dot_tn(vb[:, sl], kdec[:, sl])
    return jnp.concatenate(o_heads, axis=1)


def _gla_prep(fraw, lb, rev):
    f = lb + (1.0 - lb) * jax.nn.sigmoid(fraw)
    kk = 1.0 - f
    g3 = _split3(jnp.log(f))
    t_io = lax.broadcasted_iota(I32, (CHUNK, CHUNK), 0)
    u_io = lax.broadcasted_iota(I32, (CHUNK, CHUNK), 1)
    tri = jnp.where((u_io >= t_io) if rev else (u_io <= t_io), 1.0, 0.0).astype(BF16)
    b = _dot(tri, g3[0]) + _dot(tri, g3[1]) + _dot(tri, g3[2])
    last = 0 if rev else CHUNK - 1
    return kk, g3, b, b[last:last + 1]


def _gla_kernel(qf_ref, ff_ref, vf_ref, qb_ref, fb_ref, vb_ref, lb_ref, s0f_ref, s0b_ref,
                of_ref, ob_ref, sf_ref, sb_ref, stf, stb, bs, ks, kd8):
    i = pl.program_id(0)
    n_p = NP // CHUNK
    is_prompt = i < n_p
    c = jnp.where(is_prompt, i % (SEQ // CHUNK), (i - n_p) % (DEC_SEQ // CHUNK))
    n_c = jnp.where(is_prompt, SEQ // CHUNK, DEC_SEQ // CHUNK)

    @pl.when(jnp.logical_and(c == 0, is_prompt))
    def _():
        stf[...] = jnp.zeros_like(stf)
        stb[...] = jnp.zeros_like(stb)

    @pl.when(jnp.logical_and(c == 0, jnp.logical_not(is_prompt)))
    def _():
        for h in range(HG_HEADS):
            sl = slice(h * LANES, (h + 1) * LANES)
            stf[sl, :] = s0f_ref[0, sl, :].T
            stb[sl, :] = s0b_ref[0, sl, :].T

    lb = lb_ref[...]
    rows = (bs, ks, kd8)
    kk_f, g3_f, b_f, tot_f = _gla_prep(ff_ref[...], lb, False)
    kk_b, g3_b, b_b, tot_b = _gla_prep(fb_ref[...], lb, True)
    mild = jnp.min(jnp.minimum(tot_f, tot_b)) >= -GLA_FAST_MAX

    @pl.when(mild)
    def _():
        of_ref[...] = _gla_fast(qf_ref[...], kk_f, vf_ref[...], b_f, tot_f, stf, False)
        ob_ref[...] = _gla_fast(qb_ref[...], kk_b, vb_ref[...], b_b, tot_b, stb, True)

    @pl.when(jnp.logical_not(mild))
    def _():
        of_ref[...] = _gla_safe(qf_ref[...], kk_f, vf_ref[...], g3_f, stf, rows, False)
        ob_ref[...] = _gla_safe(qb_ref[...], kk_b, vb_ref[...], g3_b, stb, rows, True)

    @pl.when(jnp.logical_and(c == n_c - 1, is_prompt))
    def _():
        for h in range(HG_HEADS):
            sl = slice(h * LANES, (h + 1) * LANES)
            sf_ref[0, sl, :] = stf[sl, :].T
            sb_ref[0, sl, :] = stb[sl, :].T


def _gla(proj, lb, s0f, s0b):
    n_p = NP // CHUNK
    cp = SEQ // CHUNK
    cs = DEC_SEQ // CHUNK

    def bwd_blk(i):
        jp = (i // cp) * cp + (cp - 1 - i % cp)
        j = i - n_p
        js = n_p + (j // cs) * cs + (cs - 1 - j % cs)
        return jnp.where(i < n_p, jp, js)

    def req(i):
        return jnp.maximum(i - n_p, 0) // cs

    def preq(i):
        return jnp.minimum(i // cp, BATCH - 1)

    hk = HG_HEADS * HG_DK
    return pl.pallas_call(
        _gla_kernel,
        grid=(NT // CHUNK,),
        in_specs=[
            pl.BlockSpec((CHUNK, D), lambda i: (i, 0)),
            pl.BlockSpec((CHUNK, D), lambda i: (i, 1)),
            pl.BlockSpec((CHUNK, D), lambda i: (i, 3)),
            pl.BlockSpec((CHUNK, D), lambda i: (bwd_blk(i), 0)),
            pl.BlockSpec((CHUNK, D), lambda i: (bwd_blk(i), 2)),
            pl.BlockSpec((CHUNK, D), lambda i: (bwd_blk(i), 3)),
            pl.BlockSpec((1, hk), lambda i: (0, 0)),
            pl.BlockSpec((1, hk, HG_DV), lambda i: (req(i), 0, 0)),
            pl.BlockSpec((1, hk, HG_DV), lambda i: (req(i), 0, 0)),
        ],
        out_specs=[
            pl.BlockSpec((CHUNK, D), lambda i: (i, 0)),
            pl.BlockSpec((CHUNK, D), lambda i: (bwd_blk(i), 0)),
            pl.BlockSpec((1, hk, HG_DV), lambda i: (preq(i), 0, 0)),
            pl.BlockSpec((1, hk, HG_DV), lambda i: (preq(i), 0, 0)),
        ],
        out_shape=[
            jax.ShapeDtypeStruct((NT, D), F32),
            jax.ShapeDtypeStruct((NT, D), F32),
            jax.ShapeDtypeStruct((BATCH, hk, HG_DV), F32),
            jax.ShapeDtypeStruct((BATCH, hk, HG_DV), F32),
        ],
        scratch_shapes=[pltpu.VMEM((hk, HG_DV), F32), pltpu.VMEM((hk, HG_DV), F32)]
        + [pltpu.VMEM((HG_HEADS, CHUNK, HG_DK), F32)] * 3,
        compiler_params=_cparams(("arbitrary",)),
        name="gla_scan",
    )(proj, proj, proj, proj, proj, proj, lb.reshape(1, hk), s0f, s0b)


def _hgrn_out_kernel(of_ref, ob_ref, gate_ref, mod_ref, gn_ref, w_ref, *refs):
    o_ref = refs[-1]
    x = _stream_tile(refs[:-1], TM)
    o = of_ref[...] + ob_ref[...]
    gn = gn_ref[...]
    segs = []
    for h in range(HG_HEADS):
        seg = o[:, h * LANES:(h + 1) * LANES]
        ms = jnp.mean(seg * seg, axis=-1, keepdims=True)
        segs.append(seg * lax.rsqrt(ms + EPS) * gn)
    y = (jnp.concatenate(segs, axis=1) * _silu(gate_ref[...])).astype(BF16)
    o_ref[...] = x + mod_ref[0, 2:3, :] * _dot(y, w_ref[...])


def _hgrn_out(o_f, o_b, proj, x, mod, gn_w, w_out):
    x_specs, xs = _stream_specs(x, TM)
    return pl.pallas_call(
        _hgrn_out_kernel,
        grid=(NT // TM,),
        in_specs=[
            pl.BlockSpec((TM, D), lambda i: (i, 0)),
            pl.BlockSpec((TM, D), lambda i: (i, 0)),
            pl.BlockSpec((TM, D), lambda i: (i, 4)),
            pl.BlockSpec((1, 6, D), lambda i: (_mod_group(i, TM), 0, 0)),
            pl.BlockSpec((1, HG_DV), lambda i: (0, 0)),
            pl.BlockSpec((D, D), lambda i: (0, 0)),
        ] + x_specs,
        out_specs=pl.BlockSpec((TM, D), lambda i: (i, 0)),
        out_shape=jax.ShapeDtypeStruct((NT, D), F32),
        compiler_params=_cparams(("arbitrary",)),
        name="hgrn_out",
    )(o_f, o_b, proj, mod, gn_w.reshape(1, HG_DV), w_out, *xs)


def _attn_out_kernel(ap_ref, as_ref, x_ref, mod_ref, w_ref, o_ref):
    a = jnp.where(pl.program_id(0) < NP // TM, ap_ref[...], as_ref[...]).astype(BF16)
    o_ref[...] = x_ref[...] + mod_ref[0, 2:3, :] * _dot(a, w_ref[...])


def _attn_out(a_p, a_s, x, mod, w_o):
    n_p = NP // TM
    return pl.pallas_call(
        _attn_out_kernel,
        grid=(NT // TM,),
        in_specs=[
            pl.BlockSpec((TM, D), lambda i: (jnp.minimum(i, n_p - 1), 0)),
            pl.BlockSpec((TM, D), lambda i: (jnp.maximum(i - n_p, 0), 0)),
            pl.BlockSpec((TM, D), lambda i: (i, 0)),
            pl.BlockSpec((1, 6, D), lambda i: (_mod_group(i, TM), 0, 0)),
            pl.BlockSpec((D, D), lambda i: (0, 0)),
        ],
        out_specs=pl.BlockSpec((TM, D), lambda i: (i, 0)),
        out_shape=jax.ShapeDtypeStruct((NT, D), F32),
        compiler_params=_cparams(("arbitrary",)),
        name="attn_out",
    )(a_p, a_s, x, mod, w_o)


def _head_pair_masks():
    upper = lax.broadcasted_iota(I32, (1, LANES), 1) >= NA_HD
    return upper


def _attn_ctx_kernel(q_ref, k_ref, v_ref, o_ref):
    upper = _head_pair_masks()
    for p in range(NA_HEADS // 2):
        sl = slice(p * LANES, (p + 1) * LANES)
        qp = q_ref[:, sl] * (NA_HD ** -0.5)
        kp = k_ref[:, sl].astype(BF16)
        vp = v_ref[:, sl].astype(BF16)
        outs = []
        for u in range(2):
            qm = jnp.where(upper if u else jnp.logical_not(upper), qp, 0.0).astype(BF16)
            s = _dot_nt(qm, kp)
            e = jnp.exp(s - jnp.max(s, axis=-1, keepdims=True))
            outs.append(_dot(e.astype(BF16), vp) / jnp.sum(e, axis=-1, keepdims=True))
        o_ref[:, sl] = jnp.where(upper, outs[1], outs[0])


def _attn_ctx(qkv):
    return pl.pallas_call(
        _attn_ctx_kernel,
        grid=(BATCH,),
        in_specs=[
            pl.BlockSpec((SEQ, D), lambda b: (b, 0)),
            pl.BlockSpec((SEQ, D), lambda b: (b, 1)),
            pl.BlockSpec((SEQ, D), lambda b: (b, 2)),
        ],
        out_specs=pl.BlockSpec((SEQ, D), lambda b: (b, 0)),
        out_shape=jax.ShapeDtypeStruct((NP, D), F32),
        compiler_params=_cparams(("arbitrary",)),
        name="attn_ctx",
    )(qkv, qkv, qkv)


def _attn_lat_kernel(q_ref, k_ref, v_ref, kc_ref, vc_ref, strip_ref, rowmask_ref, o_ref, *, tq):
    rows = DEC_SEQ // GRID_W

    def bias_rows(u, r):
        first = rows - 1 - r
        if first % 2 == 0:
            strip = strip_ref[u, 0, :, first * GRID_W:first * GRID_W + DEC_SEQ]
        else:
            strip = strip_ref[u, 1, :, (first - 1) * GRID_W:(first - 1) * GRID_W + DEC_SEQ]
        return strip + rowmask_ref[r:r + 1, :]

    upper = _head_pair_masks()
    kl = k_ref[...].astype(BF16)
    vl = v_ref[...].astype(BF16)
    kc = kc_ref[0].astype(BF16)
    vc = vc_ref[0].astype(BF16)
    for t in range(DEC_SEQ // tq):
        rs = slice(t * tq, (t + 1) * tq)
        qp = q_ref[rs, :] * (NA_HD ** -0.5)
        outs = []
        for u in range(2):
            qm = jnp.where(upper if u else jnp.logical_not(upper), qp, 0.0).astype(BF16)
            bias = jnp.concatenate(
                [bias_rows(u, t * (tq // GRID_W) + a) for a in range(tq // GRID_W)], axis=0)
            s_l = _dot_nt(qm, kl) + bias
            s_c = _dot_nt(qm, kc)
            m = jnp.maximum(jnp.max(s_l, axis=-1, keepdims=True), jnp.max(s_c, axis=-1, keepdims=True))
            e_l = jnp.exp(s_l - m)
            e_c = jnp.exp(s_c - m)
            den = jnp.sum(e_l, axis=-1, keepdims=True) + jnp.sum(e_c, axis=-1, keepdims=True)
            outs.append((_dot(e_l.astype(BF16), vl) + _dot(e_c.astype(BF16), vc)) / den)
        o_ref[rs, :] = jnp.where(upper, outs[1], outs[0])


def _attn_lat(qkv, k_ctx, v_ctx, strips, rowmask):
    npair = NA_HEADS // 2
    rb0 = NP // DEC_SEQ
    return pl.pallas_call(
        functools.partial(_attn_lat_kernel, tq=256),
        grid=(npair, DEC_BATCH),
        in_specs=[
            pl.BlockSpec((DEC_SEQ, LANES), lambda p, b: (rb0 + b, p)),
            pl.BlockSpec((DEC_SEQ, LANES), lambda p, b: (rb0 + b, npair + p)),
            pl.BlockSpec((DEC_SEQ, LANES), lambda p, b: (rb0 + b, 2 * npair + p)),
            pl.BlockSpec((1, PAST_LEN, LANES), lambda p, b: (b, 0, p)),
            pl.BlockSpec((1, PAST_LEN, LANES), lambda p, b: (b, 0, p)),
            pl.BlockSpec((2, 2, GRID_W, STRIP_W), lambda p, b: (p, 0, 0, 0)),
            pl.BlockSpec((DEC_SEQ // GRID_W, DEC_SEQ), lambda p, b: (0, 0)),
        ],
        out_specs=pl.BlockSpec((DEC_SEQ, LANES), lambda p, b: (b, p)),
        out_shape=jax.ShapeDtypeStruct((NS, D), F32),
        compiler_params=_cparams(("arbitrary", "arbitrary")),
        name="attn_lat",
    )(qkv, qkv, qkv, k_ctx, v_ctx, strips, rowmask)


def _latent_bias_tables(rpb):
    rows = DEC_SEQ // GRID_W
    kr = min(WIN_R, rows)
    ndr, ndc = 2 * WIN_R - 1, 2 * WIN_C - 1
    qc = np.arange(GRID_W)
    kc = np.arange(GRID_W)
    ws = np.clip(qc - WIN_C // 2, 0, GRID_W - WIN_C)
    col_ok = (kc[None, :] >= ws[:, None]) & (kc[None, :] < ws[:, None] + WIN_C)
    dc = np.clip(kc[None, :] - qc[:, None] + WIN_C - 1, 0, ndc - 1)
    onehot = (dc[None, :, :] == np.arange(ndc)[:, None, None]).astype(np.float32)
    t2 = jnp.einsum('hrc,cqk->hqrk', rpb.astype(F32), jnp.asarray(onehot),
                    precision=lax.Precision.HIGHEST)
    t2 = jnp.where(jnp.asarray(col_ok)[None, :, None, :], t2, NEG)
    lead = rows - WIN_R
    n_tiles = 2 * rows - 1

    def neg(n):
        return jnp.full((NA_HEADS, GRID_W, n, GRID_W), NEG, F32)

    strip = jnp.concatenate([neg(lead), t2, neg(n_tiles - lead - ndr)], axis=2)
    strip = strip.reshape(NA_HEADS, GRID_W, n_tiles * GRID_W)

    def pad(a):
        return jnp.pad(a, ((0, 0), (0, 0), (0, STRIP_W - a.shape[-1])), constant_values=NEG)

    strips = jnp.stack([pad(strip), pad(strip[:, :, GRID_W:])], axis=1)
    r = np.arange(rows)
    k0 = np.clip(r - kr // 2, 0, rows - kr)
    krow = np.arange(DEC_SEQ) // GRID_W
    row_ok = (krow[None, :] >= k0[:, None]) & (krow[None, :] < k0[:, None] + kr)
    rowmask = jnp.asarray(np.where(row_ok, 0.0, NEG).astype(np.float32))
    return strips, rowmask


def _route_kernel(x_ref, nw_ref, mod_ref, wrh_ref, wrl_ref, rb_ref, xs_ref, pos_ref, wt_ref, npc_ref):
    @pl.when(pl.program_id(0) == N_TILES)
    def _():
        xs_ref[...] = jnp.zeros_like(xs_ref)

    @pl.when(pl.program_id(0) < N_TILES)
    def _():
        _route_tile(x_ref, nw_ref, mod_ref, wrh_ref, wrl_ref, rb_ref, xs_ref, pos_ref, wt_ref, npc_ref)


def _route_tile(x_ref, nw_ref, mod_ref, wrh_ref, wrl_ref, rb_ref, xs_ref, pos_ref, wt_ref, npc_ref):
    h = _norm_mod(x_ref[...], nw_ref[...], mod_ref[0, 3:4, :], mod_ref[0, 4:5, :])
    hb = h.astype(BF16)
    hl = (h - hb.astype(F32)).astype(BF16)
    wrh = wrh_ref[...]
    logits = _dot_nt(wrh, hb) + _dot_nt(wrh, hl) + _dot_nt(wrl_ref[...], hb)
    scores = jax.nn.sigmoid(logits)
    sel = scores + rb_ref[...]

    gsz = N_EXPERTS // N_GROUPS
    sub = lax.broadcasted_iota(I32, (gsz, MOE_TM), 0)
    ninf = -jnp.inf
    gs_rows = []
    for gi in range(N_GROUPS):
        blk = sel[gi * gsz:(gi + 1) * gsz]
        m1 = jnp.max(blk, axis=0, keepdims=True)
        first = jnp.min(jnp.where(blk == m1, sub, gsz), axis=0, keepdims=True)
        m2 = jnp.max(jnp.where(sub == first, ninf, blk), axis=0, keepdims=True)
        gs_rows.append(m1 + m2)
    cur = jnp.concatenate(gs_rows, axis=0)
    gidx = lax.broadcasted_iota(I32, (N_GROUPS, MOE_TM), 0)
    gsel = jnp.zeros((N_GROUPS, MOE_TM), F32)
    for _ in range(TOPK_GROUPS):
        m = jnp.max(cur, axis=0, keepdims=True)
        first = jnp.min(jnp.where(cur == m, gidx, N_GROUPS), axis=0, keepdims=True)
        hit = gidx == first
        gsel = jnp.where(hit, 1.0, gsel)
        cur = jnp.where(hit, ninf, cur)
    emask = jnp.concatenate(
        [jnp.broadcast_to(gsel[gi:gi + 1], (gsz, MOE_TM)) for gi in range(N_GROUPS)], axis=0)
    masked = jnp.where(emask > 0.5, sel, ninf)
    eidx = lax.broadcasted_iota(I32, (N_EXPERTS, MOE_TM), 0)
    chosen = jnp.zeros((N_EXPERTS, MOE_TM), F32)
    hits, wsel = [], []
    for _ in range(TOP_K):
        m = jnp.max(masked, axis=0, keepdims=True)
        first = jnp.min(jnp.where(masked == m, eidx, N_EXPERTS), axis=0, keepdims=True)
        hit = eidx == first
        hits.append(hit)
        wsel.append(jnp.sum(jnp.where(hit, scores, 0.0), axis=0, keepdims=True))
        chosen = jnp.where(hit, 1.0, chosen)
        masked = jnp.where(hit, ninf, masked)
    wsum = wsel[0]
    for w in wsel[1:]:
        wsum = wsum + w

    n_io = lax.broadcasted_iota(I32, (MOE_TM, MOE_TM), 0)
    m_io = lax.broadcasted_iota(I32, (MOE_TM, MOE_TM), 1)
    earlier = jnp.where(n_io < m_io, 1.0, 0.0).astype(BF16)
    rank = _dot(chosen.astype(BF16), earlier)
    cnt = jnp.sum(chosen, axis=1, keepdims=True)
    npc = jnp.floor((cnt + (PIECE - 1)) * (1.0 / PIECE))
    e_io = lax.broadcasted_iota(I32, (N_EXPERTS, N_EXPERTS), 0)
    f_io = lax.broadcasted_iota(I32, (N_EXPERTS, N_EXPERTS), 1)
    below = jnp.where(f_io < e_io, 1.0, 0.0).astype(BF16)
    npc_l = jnp.broadcast_to(npc, (N_EXPERTS, LANES))
    start = _dot(below, npc_l.astype(BF16))[:, 0:1] * PIECE
    slot = start + rank

    j_io = lax.broadcasted_iota(I32, (R1, MOE_TM), 0)
    onehot = jnp.zeros((R1, MOE_TM), F32)
    pos_rows, wt_rows = [], []
    for k in range(TOP_K):
        pos_k = jnp.sum(jnp.where(hits[k], slot, 0.0), axis=0, keepdims=True).astype(I32)
        pos_rows.append(pos_k)
        wt_rows.append(wsel[k] / wsum * ROUTED_SCALE)
        onehot = jnp.where(j_io == pos_k, 1.0, onehot)
    xs_ref[...] = _dot(onehot.astype(BF16), hb).astype(BF16)
    pad = SUBLANES - TOP_K
    pos_ref[...] = jnp.concatenate(pos_rows + [jnp.full((pad, MOE_TM), -1, I32)], axis=0)
    wt_ref[...] = jnp.concatenate(wt_rows + [jnp.zeros((pad, MOE_TM), F32)], axis=0)
    npc_ref[0] = npc_l.astype(I32)


def _route(x, nw, mod, wr_hi, wr_lo, rbias):
    def tile(i):
        return jnp.minimum(i, N_TILES - 1)

    return pl.pallas_call(
        _route_kernel,
        grid=(N_TILES + 1,),
        in_specs=[
            pl.BlockSpec((MOE_TM, D), lambda i: (tile(i), 0)),
            pl.BlockSpec((1, D), lambda i: (0, 0)),
            pl.BlockSpec((1, 6, D), lambda i: (_mod_group(tile(i), MOE_TM), 0, 0)),
            pl.BlockSpec((N_EXPERTS, D), lambda i: (0, 0)),
            pl.BlockSpec((N_EXPERTS, D), lambda i: (0, 0)),
            pl.BlockSpec((N_EXPERTS, 1), lambda i: (0, 0)),
        ],
        out_specs=[
            pl.BlockSpec((R1, D), lambda i: (i, 0)),
            pl.BlockSpec((SUBLANES, MOE_TM), lambda i: (0, tile(i))),
            pl.BlockSpec((SUBLANES, MOE_TM), lambda i: (0, tile(i))),
            pl.BlockSpec((1, N_EXPERTS, LANES), lambda i: (tile(i), 0, 0)),
        ],
        out_shape=[
            jax.ShapeDtypeStruct((P_TOT * PIECE, D), BF16),
            jax.ShapeDtypeStruct((SUBLANES, NT), I32),
            jax.ShapeDtypeStruct((SUBLANES, NT), F32),
            jax.ShapeDtypeStruct((N_TILES, N_EXPERTS, LANES), I32),
        ],
        compiler_params=_cparams(("arbitrary",)),
        name="moe_route",
    )(x, nw.reshape(1, D), mod, wr_hi, wr_lo, rbias.reshape(N_EXPERTS, 1))


def _piece_lists(npc):
    t, e = npc.shape
    hp = lax.Precision.HIGHEST
    npc_t = npc.T.astype(F32)
    start_t = (jnp.cumsum(npc, axis=1) - npc).T.astype(F32)
    tile_end = jnp.cumsum(npc_t, axis=1)
    n_e = tile_end[:, -1]
    pe_end = jnp.cumsum(n_e)
    pe_off = pe_end - n_e
    p = jnp.arange(P_MAX, dtype=F32)
    e_p = jnp.minimum(jnp.sum((pe_end[None, :] <= p[:, None]).astype(I32), axis=1), e - 1)
    oh_e = (e_p[:, None] == jnp.arange(e, dtype=I32)[None, :]).astype(F32)
    tab = jnp.concatenate([tile_end, start_t, npc_t, pe_off[:, None]], axis=1)
    row = jnp.dot(oh_e, tab, precision=hp)
    te_p, st_p, np_p, off_p = row[:, :t], row[:, t:2 * t], row[:, 2 * t:3 * t], row[:, 3 * t]
    local = p - off_p
    t_p = jnp.minimum(jnp.sum((te_p <= local[:, None]).astype(I32), axis=1), t - 1)
    oh_t = t_p[:, None] == jnp.arange(t, dtype=I32)[None, :]

    def pick(a):
        return jnp.sum(jnp.where(oh_t, a, 0.0), axis=1)

    src = t_p * R1B + (pick(st_p) + local - (pick(te_p) - pick(np_p))).astype(I32)
    src = jnp.concatenate([jnp.clip(src, 0, P_MAX - 1), jnp.zeros((G_PIECES,), I32)])

    nch = jnp.floor((n_e + (G_PIECES - 1)) * (1.0 / G_PIECES))
    ch_end = jnp.cumsum(nch)
    ch_off = jnp.concatenate([jnp.zeros((1,), F32), ch_end]).astype(I32)
    c = jnp.arange(-N_SLOTS, NCH + 2, dtype=F32)
    ce = jnp.minimum(jnp.sum((ch_end[None, :] <= c[:, None]).astype(I32), axis=1), e - 1)
    oh_c = (ce[:, None] == jnp.arange(e, dtype=I32)[None, :]).astype(F32)
    crow = jnp.dot(oh_c, jnp.stack([ch_end - nch, pe_off, n_e], axis=1), precision=hp)
    k_in = c - crow[:, 0]
    live = jnp.logical_and(c >= 0, c < ch_end[-1])
    cn = jnp.where(live, jnp.clip(crow[:, 2] - G_PIECES * k_in, 0, G_PIECES), 0.0)
    cs = jnp.where(cn > 0, crow[:, 1] + G_PIECES * k_in, 0.0)
    return src, ch_off, cs.astype(I32), cn.astype(I32)


def _ffn_kernel(src_ref, choff_ref, cs_ref, cn_ref, xs_in, wg_ref, wu_ref, wd_ref, xs_out,
                xbuf, ybuf, wgb, wub, wdb, gsem, ssem):
    e = pl.program_id(0)
    total = choff_ref[N_EXPERTS]

    def start_gather(ch):
        sl = lax.rem(ch + N_SLOTS, N_SLOTS)
        base = cs_ref[ch + N_SLOTS]
        n = cn_ref[ch + N_SLOTS]
        for i in range(G_PIECES):
            idx = src_ref[base + jnp.where(i < n, i, 0)]
            pltpu.make_async_copy(xs_in.at[idx], xbuf.at[sl, i], gsem.at[sl]).start(priority=i % 2)

    def wait_gather(ch):
        sl = lax.rem(ch + N_SLOTS, N_SLOTS)
        pltpu.make_async_copy(xs_in.at[pl.ds(0, G_PIECES)], xbuf.at[sl], gsem.at[sl]).wait()

    def start_scatter(ch):
        sl = lax.rem(ch + N_SLOTS, N_SLOTS)
        base = cs_ref[ch + N_SLOTS]
        n = cn_ref[ch + N_SLOTS]
        for i in range(G_PIECES):
            idx = jnp.where(i < n, src_ref[base + i], P_MAX + sl * G_PIECES + i)
            pltpu.make_async_copy(ybuf.at[sl, i], xs_out.at[idx], ssem.at[sl]).start(priority=i % 2)

    def wait_scatter(ch):
        sl = lax.rem(ch + N_SLOTS, N_SLOTS)
        pltpu.make_async_copy(ybuf.at[sl], xs_out.at[pl.ds(0, G_PIECES)], ssem.at[sl]).wait()

    @pl.when(e == 0)
    def _():
        ybuf[...] = jnp.zeros_like(ybuf)
        start_gather(0)
        start_gather(1)
        start_scatter(-3)
        start_scatter(-2)

    wgb[...] = wg_ref[0].astype(BF16)
    wub[...] = wu_ref[0].astype(BF16)
    wdb[...] = wd_ref[0].astype(BF16)

    def chunk(c, carry):
        sl = lax.rem(c, N_SLOTS)
        wait_gather(c)
        wait_scatter(c - 3)
        x = xbuf[sl].reshape(G_PIECES * PIECE, D)
        hid = (_silu(_dot(x, wgb[...])) * _dot(x, wub[...])).astype(BF16)
        start_gather(c + 2)
        start_scatter(c - 1)
        ybuf[sl] = _dot(hid, wdb[...]).astype(BF16).reshape(G_PIECES, PIECE, D)
        return carry

    lax.fori_loop(choff_ref[e], choff_ref[e + 1], chunk, 0)

    @pl.when(e == N_EXPERTS - 1)
    def _():
        start_scatter(total - 1)
        wait_gather(total)
        wait_gather(total + 1)
        wait_scatter(total - 3)
        wait_scatter(total - 2)
        wait_scatter(total - 1)


def _expert_ffn(xs, lists, layer, w_gate, w_up, w_down):
    src, ch_off, cs, cn = lists
    grid_spec = pltpu.PrefetchScalarGridSpec(
        num_scalar_prefetch=4,
        grid=(N_EXPERTS,),
        in_specs=[
            pl.BlockSpec(memory_space=pl.ANY),
            pl.BlockSpec((None, 1, D, D_EXPERT), lambda e, *_: (layer, e, 0, 0)),
            pl.BlockSpec((None, 1, D, D_EXPERT), lambda e, *_: (layer, e, 0, 0)),
            pl.BlockSpec((None, 1, D_EXPERT, D), lambda e, *_: (layer, e, 0, 0)),
        ],
        out_specs=pl.BlockSpec(memory_space=pl.ANY),
        scratch_shapes=[
            pltpu.VMEM((N_SLOTS, G_PIECES, PIECE, D), BF16),
            pltpu.VMEM((N_SLOTS, G_PIECES, PIECE, D), BF16),
            pltpu.VMEM((D, D_EXPERT), BF16),
            pltpu.VMEM((D, D_EXPERT), BF16),
            pltpu.VMEM((D_EXPERT, D), BF16),
            pltpu.SemaphoreType.DMA((N_SLOTS,)),
            pltpu.SemaphoreType.DMA((N_SLOTS,)),
        ],
    )
    out = pl.pallas_call(
        _ffn_kernel,
        grid_spec=grid_spec,
        out_shape=jax.ShapeDtypeStruct((P_TOT, PIECE, D), BF16),
        input_output_aliases={4: 0},
        compiler_params=_cparams(("arbitrary",)),
        name="moe_ffn",
    )(src, ch_off, cs, cn, xs.reshape(P_TOT, PIECE, D), w_gate, w_up, w_down)
    return out.reshape(P_TOT * PIECE, D)


def _combine_kernel(ys_ref, pos_ref, wt_ref, x_ref, nw_ref, mod_ref, wsg_ref, wsu_ref, wsd_ref, fw_ref,
                    *o_refs, final):
    x = x_ref[...]
    h = _norm_mod(x, nw_ref[...], mod_ref[0, 3:4, :], mod_ref[0, 4:5, :]).astype(BF16)
    shared = _dot((_silu(_dot(h, wsg_ref[...])) * _dot(h, wsu_ref[...])).astype(BF16), wsd_ref[...])
    j_io = lax.broadcasted_iota(I32, (MOE_TM, R1), 1)
    pos = pos_ref[...]
    wt = wt_ref[...]
    wm = jnp.zeros((MOE_TM, R1), F32)
    for k in range(TOP_K):
        wm = jnp.where(j_io == pos[:, k:k + 1], wt[:, k:k + 1], wm)
    routed = _dot(wm.astype(BF16), ys_ref[...])
    y = x + mod_ref[0, 5:6, :] * (routed + shared)
    if not final:
        o_refs[0][...] = y
        return
    ms = jnp.mean(y * y, axis=-1, keepdims=True)
    y = y * lax.rsqrt(ms + EPS) * fw_ref[...]
    is_prompt = pl.program_id(0) < NP // MOE_TM

    @pl.when(is_prompt)
    def _():
        o_refs[0][...] = y

    @pl.when(jnp.logical_not(is_prompt))
    def _():
        o_refs[1][...] = y


def _combine(ys, pos_t, wt_t, x, nw, mod, wsg, wsu, wsd, fw, final):
    n_p = NP // MOE_TM
    if final:
        out_specs = [pl.BlockSpec((MOE_TM, D), lambda i: (jnp.minimum(i, n_p - 1), 0)),
                     pl.BlockSpec((MOE_TM, D), lambda i: (jnp.maximum(i - n_p, 0), 0))]
        out_shape = [jax.ShapeDtypeStruct((NP, D), F32), jax.ShapeDtypeStruct((NS, D), F32)]
    else:
        out_specs = pl.BlockSpec((MOE_TM, D), lambda i: (i, 0))
        out_shape = jax.ShapeDtypeStruct((NT, D), F32)
    return pl.pallas_call(
        functools.partial(_combine_kernel, final=final),
        grid=(N_TILES,),
        in_specs=[
            pl.BlockSpec((R1, D), lambda i: (i, 0)),
            pl.BlockSpec((MOE_TM, SUBLANES), lambda i: (i, 0)),
            pl.BlockSpec((MOE_TM, SUBLANES), lambda i: (i, 0)),
            pl.BlockSpec((MOE_TM, D), lambda i: (i, 0)),
            pl.BlockSpec((1, D), lambda i: (0, 0)),
            pl.BlockSpec((1, 6, D), lambda i: (_mod_group(i, MOE_TM), 0, 0)),
            pl.BlockSpec((D, D_SHARED), lambda i: (0, 0)),
            pl.BlockSpec((D, D_SHARED), lambda i: (0, 0)),
            pl.BlockSpec((D_SHARED, D), lambda i: (0, 0)),
            pl.BlockSpec((1, D), lambda i: (0, 0)),
        ],
        out_specs=out_specs,
        out_shape=out_shape,
        compiler_params=_cparams(("arbitrary",)),
        name="moe_combine",
    )(ys, pos_t, wt_t, x, nw.reshape(1, D), mod, wsg, wsu, wsd, fw.reshape(1, D))


def _moe(x, nw, mod, w_router, rbias, layer, w_gate, w_up, w_down, ws_gate, ws_up, ws_down, fw, final):
    wr = w_router.astype(F32).T
    wr_hi = wr.astype(BF16)
    wr_lo = (wr - wr_hi.astype(F32)).astype(BF16)
    xs, pos, wt, npc = _route(x, nw, mod, wr_hi, wr_lo, rbias.astype(F32))
    lists = _piece_lists(npc[:, :, 0])
    ys = _expert_ffn(xs, lists, layer, w_gate, w_up, w_down)
    return _combine(ys, pos.T, wt.T, x, nw, mod, ws_gate.astype(BF16), ws_up.astype(BF16),
                    ws_down.astype(BF16), fw, final)


def kernel(x_prompt, x_sample, state_hgrn_fwd, state_hgrn_bwd, cache_na_k, cache_na_v, c, c_ctx,
           norm1_w, norm2_w, ada_w, ada_b, hgrn_w_in, hgrn_lb_logits, hgrn_gn_w, hgrn_w_out,
           na_w_qkv, na_rpb, na_w_o, moe_w_router, moe_router_bias, moe_w_gate, moe_w_up, moe_w_down,
           shared_w_gate, shared_w_up, shared_w_down, final_norm_w):
    x = (x_prompt.reshape(NP, D), x_sample.reshape(NS, D))
    cvec = jnp.concatenate([c_ctx[None, :], c, jnp.zeros((N_MOD - 1 - DEC_BATCH, D), F32)], axis=0)
    mod = _modulation(cvec, ada_w, ada_b)
    lb_table = jnp.cumsum(jax.nn.softmax(hgrn_lb_logits.astype(F32), axis=0), axis=0)
    hk = HG_HEADS * HG_DK

    sf = sb = k_c = v_c = None
    for l in range(DEPTH):
        if l % 2 == 0:
            a = l // 2
            proj = _norm_proj(x, norm1_w[l], mod[l], hgrn_w_in[a].astype(BF16))
            o_f, o_b, sf, sb = _gla(proj, lb_table[l],
                                    state_hgrn_fwd[:, a].reshape(DEC_BATCH, hk, HG_DV),
                                    state_hgrn_bwd[:, a].reshape(DEC_BATCH, hk, HG_DV))
            x = _hgrn_out(o_f, o_b, proj, x, mod[l], hgrn_gn_w[a], hgrn_w_out[a].astype(BF16))
        else:
            n = l // 2
            qkv, k_p, v_p = _norm_proj(x, norm1_w[l], mod[l], na_w_qkv[n].astype(BF16), n_copy=2)
            att_p = _attn_ctx(qkv)
            att_s = _attn_lat(qkv, cache_na_k[:, n].reshape(DEC_BATCH, PAST_LEN, D),
                              cache_na_v[:, n].reshape(DEC_BATCH, PAST_LEN, D),
                              *_latent_bias_tables(na_rpb[n]))
            x = _attn_out(att_p, att_s, x, mod[l], na_w_o[n].astype(BF16))
            k_c = k_p.reshape(BATCH, SEQ, NA_HEADS, NA_HD)
            v_c = v_p.reshape(BATCH, SEQ, NA_HEADS, NA_HD)
        x = _moe(x, norm2_w[l], mod[l], moe_w_router[l], moe_router_bias[l], l, moe_w_gate, moe_w_up,
                 moe_w_down, shared_w_gate[l], shared_w_up[l], shared_w_down[l], final_norm_w,
                 final=(l == DEPTH - 1))

    y_prompt = x[0].reshape(BATCH, SEQ, D)
    y_sample = x[1].reshape(DEC_BATCH, DEC_SEQ, D)
    new_sf = sf.reshape(BATCH, 1, HG_HEADS, HG_DK, HG_DV)
    new_sb = sb.reshape(BATCH, 1, HG_HEADS, HG_DK, HG_DV)
    return (y_prompt, y_sample, new_sf, new_sb, k_c[:, None], v_c[:, None])
```

```python
import functools

import numpy as np
import jax
import jax.numpy as jnp
from jax import lax
from jax.experimental import pallas as pl
from jax.experimental.pallas import tpu as pltpu

F32 = jnp.float32
BF16 = jnp.bfloat16
I32 = jnp.int32

D = 1024
BATCH = 32
SEQ = 256
DEPTH = 2
DEC_BATCH = 8
DEC_SEQ = 1024
PAST_LEN = 512
GRID_W = 64
HG_HEADS = 8
HG_DK = 128
HG_DV = 128
CHUNK = 64
NA_HEADS = 16
NA_HD = 64
WIN_R = 8
WIN_C = 16
N_EXPERTS = 64
TOP_K = 6
N_GROUPS = 8
TOPK_GROUPS = 4
D_EXPERT = 256
D_SHARED = 256
ROUTED_SCALE = 2.5
EPS = 1e-6

NP = BATCH * SEQ
NS = DEC_BATCH * DEC_SEQ
NT = NP + NS
N_MOD = 16
STRIP_W = 2048
assert (2 * (DEC_SEQ // GRID_W) - 1) * GRID_W <= STRIP_W
NEG = -1e30

LANES = 128
SUBLANES = 8
BF16_ROWS = 16
VMEM_LIMIT = 56 * 1024 * 1024

TM = 256
SUB = 16
GLA_FAST_MAX = 60.0
MOE_TM = 256
PIECE = BF16_ROWS
R1 = MOE_TM * TOP_K + N_EXPERTS * (PIECE - 1) + 64
assert R1 % PIECE == 0 and R1 % LANES == 0
R1B = R1 // PIECE
ROWS_MAIN = 2048
ROW_BLK = 256
assert (R1 - ROWS_MAIN) % ROW_BLK == 0
N_TILES = NT // MOE_TM
P_MAX = N_TILES * R1B
G_PIECES = 32
NCH = P_MAX // G_PIECES + N_EXPERTS
N_SLOTS = 3
P_TOT = P_MAX + R1B
assert N_SLOTS * G_PIECES <= R1B


def _cparams(sem):
    return pltpu.CompilerParams(dimension_semantics=sem, vmem_limit_bytes=VMEM_LIMIT)


def _dot(a, b):
    return jnp.dot(a, b, preferred_element_type=F32)


def _dot_nt(a, b):
    return lax.dot_general(a, b, (((1,), (1,)), ((), ())), preferred_element_type=F32)


def _dot_tn(a, b):
    return lax.dot_general(a, b, (((0,), (0,)), ((), ())), preferred_element_type=F32)


def _silu(x):
    return x * jax.nn.sigmoid(x)


def _mod_group(i, tm):
    r = i * tm
    return jnp.where(r < NP, 0, 1 + (r - NP) // DEC_SEQ)


def _norm_mod(x, nw, shift, scale):
    ms = jnp.mean(x * x, axis=-1, keepdims=True)
    y = x * lax.rsqrt(ms + EPS) * nw
    return y * (1.0 + scale) + shift


def _mod_kernel(c_ref, w_ref, b_ref, o_ref):
    s = _silu(c_ref[...]).astype(BF16)
    o_ref[0] = _dot(s, w_ref[0].astype(BF16)) + b_ref[0]


def _modulation(cvec, ada_w, ada_b):
    cw = 1536
    n = ada_w.shape[-1]
    out = pl.pallas_call(
        _mod_kernel,
        grid=(DEPTH, n // cw),
        in_specs=[
            pl.BlockSpec((N_MOD, D), lambda l, j: (0, 0)),
            pl.BlockSpec((1, D, cw), lambda l, j: (l, 0, j)),
            pl.BlockSpec((1, 1, cw), lambda l, j: (l, 0, j)),
        ],
        out_specs=pl.BlockSpec((1, N_MOD, cw), lambda l, j: (l, 0, j)),
        out_shape=jax.ShapeDtypeStruct((DEPTH, N_MOD, n), F32),
        compiler_params=_cparams(("arbitrary", "arbitrary")),
        name="modulation",
    )(cvec, ada_w, ada_b.reshape(DEPTH, 1, n))
    return out.reshape(DEPTH, N_MOD, 6, D)


def _stream_specs(x, tm):
    if not isinstance(x, tuple):
        return [pl.BlockSpec((tm, D), lambda i: (i, 0))], (x,)
    n_p = NP // tm
    return [pl.BlockSpec((tm, D), lambda i: (jnp.minimum(i, n_p - 1), 0)),
            pl.BlockSpec((tm, D), lambda i: (jnp.maximum(i - n_p, 0), 0))], x


def _stream_tile(refs, tm):
    if len(refs) == 1:
        return refs[0][...]
    return jnp.where(pl.program_id(0) < NP // tm, refs[0][...], refs[1][...])


def _proj_kernel(*refs, cw, n_x, n_copy):
    nw_ref, mod_ref, w_ref, o_ref = refs[n_x:n_x + 4]
    copy_refs = refs[n_x + 4:]
    x = _stream_tile(refs[:n_x], TM)
    h = _norm_mod(x, nw_ref[...], mod_ref[0, 0:1, :], mod_ref[0, 1:2, :]).astype(BF16)
    is_prompt = pl.program_id(0) < NP // TM
    for j in range(w_ref.shape[1] // cw):
        val = _dot(h, w_ref[:, j * cw:(j + 1) * cw])
        o_ref[:, j * cw:(j + 1) * cw] = val
        col = j * cw - D
        if 0 <= col < n_copy * D:
            c_ref = copy_refs[col // D]

            @pl.when(is_prompt)
            def _(c_ref=c_ref, val=val, col=col):
                c_ref[:, col % D:col % D + cw] = val


def _norm_proj(x, nw, mod, w, n_copy=0):
    n = w.shape[1]
    n_p = NP // TM
    x_specs, xs = _stream_specs(x, TM)
    outs = pl.pallas_call(
        functools.partial(_proj_kernel, cw=512, n_x=len(xs), n_copy=n_copy),
        grid=(NT // TM,),
        in_specs=x_specs + [
            pl.BlockSpec((1, D), lambda i: (0, 0)),
            pl.BlockSpec((1, 6, D), lambda i: (_mod_group(i, TM), 0, 0)),
            pl.BlockSpec((D, n), lambda i: (0, 0)),
        ],
        out_specs=[pl.BlockSpec((TM, n), lambda i: (i, 0))]
        + [pl.BlockSpec((TM, D), lambda i: (jnp.minimum(i, n_p - 1), 0))] * n_copy,
        out_shape=[jax.ShapeDtypeStruct((NT, n), F32)] + [jax.ShapeDtypeStruct((NP, D), F32)] * n_copy,
        compiler_params=_cparams(("arbitrary",)),
        name="norm_proj",
    )(*xs, nw.reshape(1, D), mod, w)
    return outs if n_copy else outs[0]


def _split3(x):
    hi = x.astype(BF16)
    r = x - hi.astype(F32)
    mid = r.astype(BF16)
    lo = (r - mid.astype(F32)).astype(BF16)
    return hi, mid, lo


def _gla_safe(q, kk, v, g3, st_ref, row_refs, rev):
    bs_ref, ks_ref, kd8_ref = row_refs

    n_stack = 4
    t_io = lax.broadcasted_iota(I32, (n_stack * CHUNK, CHUNK), 0)
    u_io = lax.broadcasted_iota(I32, (n_stack * CHUNK, CHUNK), 1)
    which = t_io // CHUNK
    tt = t_io - which * CHUNK
    b16 = (tt // SUB) * SUB
    b8 = (tt // SUBLANES) * SUBLANES
    if rev:
        lim = jnp.where(which == 0, tt, jnp.where(which == 1, b16 + SUB, jnp.where(which == 2, b8 + SUBLANES, b8)))
        pick = u_io >= lim
    else:
        lim = jnp.where(which == 0, tt, jnp.where(which == 1, b16 - 1,
                                                  jnp.where(which == 2, b8 - 1, b8 + SUBLANES - 1)))
        pick = u_io <= lim
    tri = jnp.where(pick, 1.0, 0.0).astype(BF16)
    hi, mid, lo = g3
    cs = _dot(tri, hi) + _dot(tri, mid) + _dot(tri, lo)
    b, r16, r8, e8 = (cs[j * CHUNK:(j + 1) * CHUNK] for j in range(n_stack))
    last = 0 if rev else CHUNK - 1
    tot = b[last:last + 1]

    qe = (q * jnp.exp(b)).astype(BF16)
    qd = (q * jnp.exp(b - r16)).astype(BF16)
    qd8 = q * jnp.exp(b - r8)
    kdec = (kk * jnp.exp(tot - b)).astype(BF16)
    vb = v.astype(BF16)
    dec_tot = jnp.exp(tot)
    kd8 = kk * jnp.exp(e8 - b)
    for h in range(HG_HEADS):
        sl = slice(h * LANES, (h + 1) * LANES)
        bs_ref[h] = b[:, sl]
        ks_ref[h] = kk[:, sl]
        kd8_ref[h] = kd8[:, sl]

    def row(ref, h, s):
        return jnp.broadcast_to(ref[h, s:s + 1, :], (SUBLANES, LANES))

    lane = lax.broadcasted_iota(I32, (SUBLANES, LANES), 1)
    row8 = lax.broadcasted_iota(I32, (SUBLANES, LANES), 0)
    a_in = [[] for _ in range(HG_HEADS)]
    for tb in range(CHUNK // SUBLANES):
        t0 = tb * SUBLANES
        blk0 = (t0 // SUB) * SUB
        other = blk0 + SUBLANES if t0 == blk0 else blk0
        keep = (row8 + t0 <= lane) if rev else (row8 + t0 >= lane)
        for h in range(HG_HEADS):
            sl = slice(h * LANES, (h + 1) * LANES)
            q_t = q[t0:t0 + SUBLANES, sl]
            b_t = b[t0:t0 + SUBLANES, sl]
            acc = jnp.zeros((SUBLANES, LANES), F32)
            for s in range(t0, t0 + SUBLANES):
                p = q_t * jnp.exp(b_t - row(bs_ref, h, s)) * row(ks_ref, h, s)
                acc = jnp.where(lane == s, jnp.sum(p, axis=-1, keepdims=True), acc)
            acc = jnp.where(keep, acc, 0.0)
            if (t0 == blk0) == rev:
                qd8_t = qd8[t0:t0 + SUBLANES, sl]
                for s in range(other, other + SUBLANES):
                    p = qd8_t * row(kd8_ref, h, s)
                    acc = jnp.where(lane == s, jnp.sum(p, axis=-1, keepdims=True), acc)
            a_in[h].append(acc)

    n_blk = CHUNK // SUB
    o_heads = []
    for h in range(HG_HEADS):
        sl = slice(h * LANES, (h + 1) * LANES)
        st = st_ref[sl, :]
        o_h = _dot_nt(qe[:, sl], st.astype(BF16))
        rows = []
        for i in range(n_blk):
            lo_r, hi_r = ((i + 1) * SUB, CHUNK) if rev else (0, i * SUB)
            if hi_r == lo_r:
                rows.append(jnp.zeros((SUB, CHUNK), F32))
                continue
            r_i = r16[i * SUB:i * SUB + 1, sl]
            kd = kk[lo_r:hi_r, sl] * jnp.exp(r_i - b[lo_r:hi_r, sl])
            pads = [jnp.zeros((lo_r, LANES), F32)] if lo_r else []
            pads_hi = [jnp.zeros((CHUNK - hi_r, LANES), F32)] if hi_r < CHUNK else []
            kd = jnp.concatenate(pads + [kd] + pads_hi, axis=0).astype(BF16)
            rows.append(_dot_nt(qd[i * SUB:(i + 1) * SUB, sl], kd))
        a1 = jnp.concatenate(rows, axis=0) + jnp.concatenate(a_in[h], axis=0)[:, :CHUNK]
        o_h = o_h + _dot(a1.astype(BF16), vb[:, sl])
        o_heads.append(o_h)
        st_ref[sl, :] = st * dec_tot[:, sl] + _dot_tn(vb[:, sl], kdec[:, sl])
    return jnp.concatenate(o_heads, axis=1)


def _gla_fast(q, kk, v, b, tot, st_ref, rev):
    qe = (q * jnp.exp(b)).astype(BF16)
    ke32 = kk * jnp.exp(-b)
    ke = ke32.astype(BF16)
    dec_tot = jnp.exp(tot)
    kdec = (ke32 * dec_tot).astype(BF16)
    vb = v.astype(BF16)
    t_io = lax.broadcasted_iota(I32, (CHUNK, CHUNK), 0)
    s_io = lax.broadcasted_iota(I32, (CHUNK, CHUNK), 1)
    keep = (s_io >= t_io) if rev else (s_io <= t_io)
    o_heads = []
    for h in range(HG_HEADS):
        sl = slice(h * LANES, (h + 1) * LANES)
        st = st_ref[sl, :]
        a = jnp.where(keep, _dot_nt(qe[:, sl], ke[:, sl]), 0.0).astype(BF16)
        o_heads.append(_dot_nt(qe[:, sl], st.astype(BF16)) + _dot(a, vb[:, sl]))
        st_ref[sl, :] = st * dec_tot[:, sl] + _dot_tn(vb[:, sl], kdec[:, sl])
    return jnp.concatenate(o_heads, axis=1)


def _gla_prep(fraw, lb, rev):
    f = lb + (1.0 - lb) * jax.nn.sigmoid(fraw)
    kk = 1.0 - f
    g3 = _split3(jnp.log(f))
    t_io = lax.broadcasted_iota(I32, (CHUNK, CHUNK), 0)
    u_io = lax.broadcasted_iota(I32, (CHUNK, CHUNK), 1)
    tri = jnp.where((u_io >= t_io) if rev else (u_io <= t_io), 1.0, 0.0).astype(BF16)
    b = _dot(tri, g3[0]) + _dot(tri, g3[1]) + _dot(tri, g3[2])
    last = 0 if rev else CHUNK - 1
    return kk, g3, b, b[last:last + 1]


def _gla_kernel(qf_ref, ff_ref, vf_ref, qb_ref, fb_ref, vb_ref, lb_ref, s0f_ref, s0b_ref,
                of_ref, ob_ref, sf_ref, sb_ref, stf, stb, bs, ks, kd8):
    i = pl.program_id(0)
    n_p = NP // CHUNK
    is_prompt = i < n_p
    c = jnp.where(is_prompt, i % (SEQ // CHUNK), (i - n_p) % (DEC_SEQ // CHUNK))
    n_c = jnp.where(is_prompt, SEQ // CHUNK, DEC_SEQ // CHUNK)

    @pl.when(jnp.logical_and(c == 0, is_prompt))
    def _():
        stf[...] = jnp.zeros_like(stf)
        stb[...] = jnp.zeros_like(stb)

    @pl.when(jnp.logical_and(c == 0, jnp.logical_not(is_prompt)))
    def _():
        for h in range(HG_HEADS):
            sl = slice(h * LANES, (h + 1) * LANES)
            stf[sl, :] = s0f_ref[0, sl, :].T
            stb[sl, :] = s0b_ref[0, sl, :].T

    lb = lb_ref[...]
    rows = (bs, ks, kd8)
    kk_f, g3_f, b_f, tot_f = _gla_prep(ff_ref[...], lb, False)
    kk_b, g3_b, b_b, tot_b = _gla_prep(fb_ref[...], lb, True)
    mild = jnp.min(jnp.minimum(tot_f, tot_b)) >= -GLA_FAST_MAX

    @pl.when(mild)
    def _():
        of_ref[...] = _gla_fast(qf_ref[...], kk_f, vf_ref[...], b_f, tot_f, stf, False)
        ob_ref[...] = _gla_fast(qb_ref[...], kk_b, vb_ref[...], b_b, tot_b, stb, True)

    @pl.when(jnp.logical_not(mild))
    def _():
        of_ref[...] = _gla_safe(qf_ref[...], kk_f, vf_ref[...], g3_f, stf, rows, False)
        ob_ref[...] = _gla_safe(qb_ref[...], kk_b, vb_ref[...], g3_b, stb, rows, True)

    @pl.when(jnp.logical_and(c == n_c - 1, is_prompt))
    def _():
        for h in range(HG_HEADS):
            sl = slice(h * LANES, (h + 1) * LANES)
            sf_ref[0, sl, :] = stf[sl, :].T
            sb_ref[0, sl, :] = stb[sl, :].T


def _gla(proj, lb, s0f, s0b):
    n_p = NP // CHUNK
    cp = SEQ // CHUNK
    cs = DEC_SEQ // CHUNK

    def bwd_blk(i):
        jp = (i // cp) * cp + (cp - 1 - i % cp)
        j = i - n_p
        js = n_p + (j // cs) * cs + (cs - 1 - j % cs)
        return jnp.where(i < n_p, jp, js)

    def req(i):
        return jnp.maximum(i - n_p, 0) // cs

    def preq(i):
        return jnp.minimum(i // cp, BATCH - 1)

    hk = HG_HEADS * HG_DK
    return pl.pallas_call(
        _gla_kernel,
        grid=(NT // CHUNK,),
        in_specs=[
            pl.BlockSpec((CHUNK, D), lambda i: (i, 0)),
            pl.BlockSpec((CHUNK, D), lambda i: (i, 1)),
            pl.BlockSpec((CHUNK, D), lambda i: (i, 3)),
            pl.BlockSpec((CHUNK, D), lambda i: (bwd_blk(i), 0)),
            pl.BlockSpec((CHUNK, D), lambda i: (bwd_blk(i), 2)),
            pl.BlockSpec((CHUNK, D), lambda i: (bwd_blk(i), 3)),
            pl.BlockSpec((1, hk), lambda i: (0, 0)),
            pl.BlockSpec((1, hk, HG_DV), lambda i: (req(i), 0, 0)),
            pl.BlockSpec((1, hk, HG_DV), lambda i: (req(i), 0, 0)),
        ],
        out_specs=[
            pl.BlockSpec((CHUNK, D), lambda i: (i, 0)),
            pl.BlockSpec((CHUNK, D), lambda i: (bwd_blk(i), 0)),
            pl.BlockSpec((1, hk, HG_DV), lambda i: (preq(i), 0, 0)),
            pl.BlockSpec((1, hk, HG_DV), lambda i: (preq(i), 0, 0)),
        ],
        out_shape=[
            jax.ShapeDtypeStruct((NT, D), F32),
            jax.ShapeDtypeStruct((NT, D), F32),
            jax.ShapeDtypeStruct((BATCH, hk, HG_DV), F32),
            jax.ShapeDtypeStruct((BATCH, hk, HG_DV), F32),
        ],
        scratch_shapes=[pltpu.VMEM((hk, HG_DV), F32), pltpu.VMEM((hk, HG_DV), F32)]
        + [pltpu.VMEM((HG_HEADS, CHUNK, HG_DK), F32)] * 3,
        compiler_params=_cparams(("arbitrary",)),
        name="gla_scan",
    )(proj, proj, proj, proj, proj, proj, lb.reshape(1, hk), s0f, s0b)


def _hgrn_out_kernel(of_ref, ob_ref, gate_ref, mod_ref, gn_ref, w_ref, *refs):
    o_ref = refs[-1]
    x = _stream_tile(refs[:-1], TM)
    o = of_ref[...] + ob_ref[...]
    gn = gn_ref[...]
    segs = []
    for h in range(HG_HEADS):
        seg = o[:, h * LANES:(h + 1) * LANES]
        ms = jnp.mean(seg * seg, axis=-1, keepdims=True)
        segs.append(seg * lax.rsqrt(ms + EPS) * gn)
    y = (jnp.concatenate(segs, axis=1) * _silu(gate_ref[...])).astype(BF16)
    o_ref[...] = x + mod_ref[0, 2:3, :] * _dot(y, w_ref[...])


def _hgrn_out(o_f, o_b, proj, x, mod, gn_w, w_out):
    x_specs, xs = _stream_specs(x, TM)
    return pl.pallas_call(
        _hgrn_out_kernel,
        grid=(NT // TM,),
        in_specs=[
            pl.BlockSpec((TM, D), lambda i: (i, 0)),
            pl.BlockSpec((TM, D), lambda i: (i, 0)),
            pl.BlockSpec((TM, D), lambda i: (i, 4)),
            pl.BlockSpec((1, 6, D), lambda i: (_mod_group(i, TM), 0, 0)),
            pl.BlockSpec((1, HG_DV), lambda i: (0, 0)),
            pl.BlockSpec((D, D), lambda i: (0, 0)),
        ] + x_specs,
        out_specs=pl.BlockSpec((TM, D), lambda i: (i, 0)),
        out_shape=jax.ShapeDtypeStruct((NT, D), F32),
        compiler_params=_cparams(("arbitrary",)),
        name="hgrn_out",
    )(o_f, o_b, proj, mod, gn_w.reshape(1, HG_DV), w_out, *xs)


def _attn_out_kernel(ap_ref, as_ref, x_ref, mod_ref, w_ref, o_ref):
    a = jnp.where(pl.program_id(0) < NP // TM, ap_ref[...], as_ref[...]).astype(BF16)
    o_ref[...] = x_ref[...] + mod_ref[0, 2:3, :] * _dot(a, w_ref[...])


def _attn_out(a_p, a_s, x, mod, w_o):
    n_p = NP // TM
    return pl.pallas_call(
        _attn_out_kernel,
        grid=(NT // TM,),
        in_specs=[
            pl.BlockSpec((TM, D), lambda i: (jnp.minimum(i, n_p - 1), 0)),
            pl.BlockSpec((TM, D), lambda i: (jnp.maximum(i - n_p, 0), 0)),
            pl.BlockSpec((TM, D), lambda i: (i, 0)),
            pl.BlockSpec((1, 6, D), lambda i: (_mod_group(i, TM), 0, 0)),
            pl.BlockSpec((D, D), lambda i: (0, 0)),
        ],
        out_specs=pl.BlockSpec((TM, D), lambda i: (i, 0)),
        out_shape=jax.ShapeDtypeStruct((NT, D), F32),
        compiler_params=_cparams(("arbitrary",)),
        name="attn_out",
    )(a_p, a_s, x, mod, w_o)


def _head_pair_masks():
    upper = lax.broadcasted_iota(I32, (1, LANES), 1) >= NA_HD
    return upper


def _attn_ctx_kernel(q_ref, k_ref, v_ref, o_ref):
    upper = _head_pair_masks()
    for p in range(NA_HEADS // 2):
        sl = slice(p * LANES, (p + 1) * LANES)
        qp = q_ref[:, sl] * (NA_HD ** -0.5)
        kp = k_ref[:, sl].astype(BF16)
        vp = v_ref[:, sl].astype(BF16)
        outs = []
        for u in range(2):
            qm = jnp.where(upper if u else jnp.logical_not(upper), qp, 0.0).astype(BF16)
            s = _dot_nt(qm, kp)
            e = jnp.exp(s - jnp.max(s, axis=-1, keepdims=True))
            outs.append(_dot(e.astype(BF16), vp) / jnp.sum(e, axis=-1, keepdims=True))
        o_ref[:, sl] = jnp.where(upper, outs[1], outs[0])


def _attn_ctx(qkv):
    return pl.pallas_call(
        _attn_ctx_kernel,
        grid=(BATCH,),
        in_specs=[
            pl.BlockSpec((SEQ, D), lambda b: (b, 0)),
            pl.BlockSpec((SEQ, D), lambda b: (b, 1)),
            pl.BlockSpec((SEQ, D), lambda b: (b, 2)),
        ],
        out_specs=pl.BlockSpec((SEQ, D), lambda b: (b, 0)),
        out_shape=jax.ShapeDtypeStruct((NP, D), F32),
        compiler_params=_cparams(("arbitrary",)),
        name="attn_ctx",
    )(qkv, qkv, qkv)


def _attn_lat_kernel(q_ref, k_ref, v_ref, kc_ref, vc_ref, strip_ref, rowmask_ref, o_ref, *, tq):
    rows = DEC_SEQ // GRID_W
    kr = min(WIN_R, rows)
    rpt = tq // GRID_W

    def bias_rows(u, r, k_lo, k_hi):
        first = rows - 1 - r + k_lo
        var = first % 2
        strip = strip_ref[u, var, :, (first - var) * GRID_W:(first - var + k_hi - k_lo) * GRID_W]
        return strip + rowmask_ref[r:r + 1, k_lo * GRID_W:k_hi * GRID_W]

    upper = _head_pair_masks()
    mine = (jnp.logical_not(upper), upper)
    kl = k_ref[...].astype(BF16)
    kc = kc_ref[0].astype(BF16)
    vl = [jnp.where(mine[u], v_ref[...], 1.0).astype(BF16) for u in range(2)]
    vc = [jnp.where(mine[u], vc_ref[0], 1.0).astype(BF16) for u in range(2)]
    for t in range(DEC_SEQ // tq):
        rs = slice(t * tq, (t + 1) * tq)
        k_lo = min(max(t * rpt - kr // 2, 0), rows - kr) // 2 * 2
        k_hi = -(-(min(max(t * rpt + rpt - 1 - kr // 2, 0), rows - kr) + kr) // 2) * 2
        ks = slice(k_lo * GRID_W, k_hi * GRID_W)
        qp = q_ref[rs, :] * (NA_HD ** -0.5)
        outs = []
        for u in range(2):
            qm = jnp.where(mine[u], qp, 0.0).astype(BF16)
            bias = jnp.concatenate([bias_rows(u, t * rpt + a, k_lo, k_hi) for a in range(rpt)], axis=0)
            s_l = _dot_nt(qm, kl[ks]) + bias
            s_c = _dot_nt(qm, kc)
            m = jnp.maximum(jnp.max(s_l, axis=-1, keepdims=True), jnp.max(s_c, axis=-1, keepdims=True))
            acc = (_dot(jnp.exp(s_l - m).astype(BF16), vl[u][ks])
                   + _dot(jnp.exp(s_c - m).astype(BF16), vc[u]))
            outs.append(acc / pltpu.roll(acc, NA_HD, 1))
        o_ref[rs, :] = jnp.where(upper, outs[1], outs[0])


def _attn_lat(qkv, k_ctx, v_ctx, strips, rowmask):
    npair = NA_HEADS // 2
    rb0 = NP // DEC_SEQ
    return pl.pallas_call(
        functools.partial(_attn_lat_kernel, tq=256),
        grid=(npair, DEC_BATCH),
        in_specs=[
            pl.BlockSpec((DEC_SEQ, LANES), lambda p, b: (rb0 + b, p)),
            pl.BlockSpec((DEC_SEQ, LANES), lambda p, b: (rb0 + b, npair + p)),
            pl.BlockSpec((DEC_SEQ, LANES), lambda p, b: (rb0 + b, 2 * npair + p)),
            pl.BlockSpec((1, PAST_LEN, LANES), lambda p, b: (b, 0, p)),
            pl.BlockSpec((1, PAST_LEN, LANES), lambda p, b: (b, 0, p)),
            pl.BlockSpec((2, 2, GRID_W, STRIP_W), lambda p, b: (p, 0, 0, 0)),
            pl.BlockSpec((DEC_SEQ // GRID_W, DEC_SEQ), lambda p, b: (0, 0)),
        ],
        out_specs=pl.BlockSpec((DEC_SEQ, LANES), lambda p, b: (b, p)),
        out_shape=jax.ShapeDtypeStruct((NS, D), F32),
        compiler_params=_cparams(("arbitrary", "arbitrary")),
        name="attn_lat",
    )(qkv, qkv, qkv, k_ctx, v_ctx, strips, rowmask)


def _latent_bias_tables(rpb):
    rows = DEC_SEQ // GRID_W
    kr = min(WIN_R, rows)
    ndr, ndc = 2 * WIN_R - 1, 2 * WIN_C - 1
    qc = np.arange(GRID_W)
    kc = np.arange(GRID_W)
    ws = np.clip(qc - WIN_C // 2, 0, GRID_W - WIN_C)
    col_ok = (kc[None, :] >= ws[:, None]) & (kc[None, :] < ws[:, None] + WIN_C)
    dc = np.clip(kc[None, :] - qc[:, None] + WIN_C - 1, 0, ndc - 1)
    onehot = (dc[None, :, :] == np.arange(ndc)[:, None, None]).astype(np.float32)
    t2 = jnp.einsum('hrc,cqk->hqrk', rpb.astype(F32), jnp.asarray(onehot),
                    precision=lax.Precision.HIGHEST)
    t2 = jnp.where(jnp.asarray(col_ok)[None, :, None, :], t2, NEG)
    lead = rows - WIN_R
    n_tiles = 2 * rows - 1

    def neg(n):
        return jnp.full((NA_HEADS, GRID_W, n, GRID_W), NEG, F32)

    strip = jnp.concatenate([neg(lead), t2, neg(n_tiles - lead - ndr)], axis=2)
    strip = strip.reshape(NA_HEADS, GRID_W, n_tiles * GRID_W)

    def pad(a):
        return jnp.pad(a, ((0, 0), (0, 0), (0, STRIP_W - a.shape[-1])), constant_values=NEG)

    strips = jnp.stack([pad(strip), pad(strip[:, :, GRID_W:])], axis=1)
    r = np.arange(rows)
    k0 = np.clip(r - kr // 2, 0, rows - kr)
    krow = np.arange(DEC_SEQ) // GRID_W
    row_ok = (krow[None, :] >= k0[:, None]) & (krow[None, :] < k0[:, None] + kr)
    rowmask = jnp.asarray(np.where(row_ok, 0.0, NEG).astype(np.float32))
    return strips, rowmask


def _route_kernel(x_ref, nw_ref, mod_ref, wrh_ref, wrl_ref, rb_ref, xs_ref, pos_ref, wt_ref, npc_ref):
    @pl.when(pl.program_id(0) == N_TILES)
    def _():
        xs_ref[...] = jnp.zeros_like(xs_ref)

    @pl.when(pl.program_id(0) < N_TILES)
    def _():
        _route_tile(x_ref, nw_ref, mod_ref, wrh_ref, wrl_ref, rb_ref, xs_ref, pos_ref, wt_ref, npc_ref)


def _route_tile(x_ref, nw_ref, mod_ref, wrh_ref, wrl_ref, rb_ref, xs_ref, pos_ref, wt_ref, npc_ref):
    h = _norm_mod(x_ref[...], nw_ref[...], mod_ref[0, 3:4, :], mod_ref[0, 4:5, :])
    hb = h.astype(BF16)
    hl = (h - hb.astype(F32)).astype(BF16)
    wrh = wrh_ref[...]
    logits = _dot_nt(wrh, hb) + _dot_nt(wrh, hl) + _dot_nt(wrl_ref[...], hb)
    scores = jax.nn.sigmoid(logits)
    sel = scores + rb_ref[...]

    gsz = N_EXPERTS // N_GROUPS
    sub = lax.broadcasted_iota(I32, (gsz, MOE_TM), 0)
    ninf = -jnp.inf
    gs_rows = []
    for gi in range(N_GROUPS):
        blk = sel[gi * gsz:(gi + 1) * gsz]
        m1 = jnp.max(blk, axis=0, keepdims=True)
        first = jnp.min(jnp.where(blk == m1, sub, gsz), axis=0, keepdims=True)
        m2 = jnp.max(jnp.where(sub == first, ninf, blk), axis=0, keepdims=True)
        gs_rows.append(m1 + m2)
    cur = jnp.concatenate(gs_rows, axis=0)
    gidx = lax.broadcasted_iota(I32, (N_GROUPS, MOE_TM), 0)
    gsel = jnp.zeros((N_GROUPS, MOE_TM), F32)
    for _ in range(TOPK_GROUPS):
        m = jnp.max(cur, axis=0, keepdims=True)
        first = jnp.min(jnp.where(cur == m, gidx, N_GROUPS), axis=0, keepdims=True)
        hit = gidx == first
        gsel = jnp.where(hit, 1.0, gsel)
        cur = jnp.where(hit, ninf, cur)
    emask = jnp.concatenate(
        [jnp.broadcast_to(gsel[gi:gi + 1], (gsz, MOE_TM)) for gi in range(N_GROUPS)], axis=0)
    masked = jnp.where(emask > 0.5, sel, ninf)
    eidx = lax.broadcasted_iota(I32, (N_EXPERTS, MOE_TM), 0)
    chosen = jnp.zeros((N_EXPERTS, MOE_TM), F32)
    hits, wsel = [], []
    for _ in range(TOP_K):
        m = jnp.max(masked, axis=0, keepdims=True)
        first = jnp.min(jnp.where(masked == m, eidx, N_EXPERTS), axis=0, keepdims=True)
        hit = eidx == first
        hits.append(hit)
        wsel.append(jnp.sum(jnp.where(hit, scores, 0.0), axis=0, keepdims=True))
        chosen = jnp.where(hit, 1.0, chosen)
        masked = jnp.where(hit, ninf, masked)
    wsum = wsel[0]
    for w in wsel[1:]:
        wsum = wsum + w

    n_io = lax.broadcasted_iota(I32, (MOE_TM, MOE_TM), 0)
    m_io = lax.broadcasted_iota(I32, (MOE_TM, MOE_TM), 1)
    earlier = jnp.where(n_io < m_io, 1.0, 0.0).astype(BF16)
    rank = _dot(chosen.astype(BF16), earlier)
    cnt = jnp.sum(chosen, axis=1, keepdims=True)
    npc = jnp.floor((cnt + (PIECE - 1)) * (1.0 / PIECE))
    e_io = lax.broadcasted_iota(I32, (N_EXPERTS, N_EXPERTS), 0)
    f_io = lax.broadcasted_iota(I32, (N_EXPERTS, N_EXPERTS), 1)
    below = jnp.where(f_io < e_io, 1.0, 0.0).astype(BF16)
    npc_l = jnp.broadcast_to(npc, (N_EXPERTS, LANES))
    start = _dot(below, npc_l.astype(BF16))[:, 0:1] * PIECE
    slot = start + rank

    pos_rows, wt_rows = [], []
    for k in range(TOP_K):
        pos_k = jnp.sum(jnp.where(hits[k], slot, 0.0), axis=0, keepdims=True).astype(I32)
        pos_rows.append(pos_k)
        wt_rows.append(wsel[k] / wsum * ROUTED_SCALE)

    used = jnp.sum(npc) * PIECE
    pos16 = [p.astype(jnp.int16) for p in pos_rows]

    def fill(r0, n):
        j16 = (lax.broadcasted_iota(I32, (n, MOE_TM), 0) + r0).astype(jnp.int16)
        onehot = jnp.zeros((n, MOE_TM), BF16)
        for k in range(TOP_K):
            onehot = jnp.where(j16 == pos16[k], jnp.ones((), BF16), onehot)
        xs_ref[r0:r0 + n, :] = _dot(onehot, hb).astype(BF16)

    fill(0, ROWS_MAIN)
    for r0 in range(ROWS_MAIN, R1, ROW_BLK):
        pl.when(used > r0)(functools.partial(fill, r0, ROW_BLK))

        @pl.when(used <= r0)
        def _(r0=r0):
            xs_ref[r0:r0 + ROW_BLK, :] = jnp.zeros((ROW_BLK, D), BF16)
    pad = SUBLANES - TOP_K
    pos_ref[...] = jnp.concatenate(pos_rows + [jnp.full((pad, MOE_TM), -1, I32)], axis=0)
    wt_ref[...] = jnp.concatenate(wt_rows + [jnp.zeros((pad, MOE_TM), F32)], axis=0)
    npc_ref[0] = npc_l.astype(I32)


def _route(x, nw, mod, wr_hi, wr_lo, rbias):
    def tile(i):
        return jnp.minimum(i, N_TILES - 1)

    return pl.pallas_call(
        _route_kernel,
        grid=(N_TILES + 1,),
        in_specs=[
            pl.BlockSpec((MOE_TM, D), lambda i: (tile(i), 0)),
            pl.BlockSpec((1, D), lambda i: (0, 0)),
            pl.BlockSpec((1, 6, D), lambda i: (_mod_group(tile(i), MOE_TM), 0, 0)),
            pl.BlockSpec((N_EXPERTS, D), lambda i: (0, 0)),
            pl.BlockSpec((N_EXPERTS, D), lambda i: (0, 0)),
            pl.BlockSpec((N_EXPERTS, 1), lambda i: (0, 0)),
        ],
        out_specs=[
            pl.BlockSpec((R1, D), lambda i: (i, 0)),
            pl.BlockSpec((SUBLANES, MOE_TM), lambda i: (0, tile(i))),
            pl.BlockSpec((SUBLANES, MOE_TM), lambda i: (0, tile(i))),
            pl.BlockSpec((1, N_EXPERTS, LANES), lambda i: (tile(i), 0, 0)),
        ],
        out_shape=[
            jax.ShapeDtypeStruct((P_TOT * PIECE, D), BF16),
            jax.ShapeDtypeStruct((SUBLANES, NT), I32),
            jax.ShapeDtypeStruct((SUBLANES, NT), F32),
            jax.ShapeDtypeStruct((N_TILES, N_EXPERTS, LANES), I32),
        ],
        compiler_params=_cparams(("arbitrary",)),
        name="moe_route",
    )(x, nw.reshape(1, D), mod, wr_hi, wr_lo, rbias.reshape(N_EXPERTS, 1))


def _piece_lists(npc):
    t, e = npc.shape
    hp = lax.Precision.HIGHEST
    npc_t = npc.T.astype(F32)
    start_t = (jnp.cumsum(npc, axis=1) - npc).T.astype(F32)
    tile_end = jnp.cumsum(npc_t, axis=1)
    n_e = tile_end[:, -1]
    pe_end = jnp.cumsum(n_e)
    pe_off = pe_end - n_e
    p = jnp.arange(P_MAX, dtype=F32)
    e_p = jnp.minimum(jnp.sum((pe_end[None, :] <= p[:, None]).astype(I32), axis=1), e - 1)
    oh_e = (e_p[:, None] == jnp.arange(e, dtype=I32)[None, :]).astype(F32)
    tab = jnp.concatenate([tile_end, start_t, npc_t, pe_off[:, None]], axis=1)
    row = jnp.dot(oh_e, tab, precision=hp)
    te_p, st_p, np_p, off_p = row[:, :t], row[:, t:2 * t], row[:, 2 * t:3 * t], row[:, 3 * t]
    local = p - off_p
    t_p = jnp.minimum(jnp.sum((te_p <= local[:, None]).astype(I32), axis=1), t - 1)
    oh_t = t_p[:, None] == jnp.arange(t, dtype=I32)[None, :]

    def pick(a):
        return jnp.sum(jnp.where(oh_t, a, 0.0), axis=1)

    src = t_p * R1B + (pick(st_p) + local - (pick(te_p) - pick(np_p))).astype(I32)
    src = jnp.concatenate([jnp.clip(src, 0, P_MAX - 1), jnp.zeros((G_PIECES,), I32)])

    nch = jnp.floor((n_e + (G_PIECES - 1)) * (1.0 / G_PIECES))
    ch_end = jnp.cumsum(nch)
    ch_off = jnp.concatenate([jnp.zeros((1,), F32), ch_end]).astype(I32)
    c = jnp.arange(-N_SLOTS, NCH + 2, dtype=F32)
    ce = jnp.minimum(jnp.sum((ch_end[None, :] <= c[:, None]).astype(I32), axis=1), e - 1)
    oh_c = (ce[:, None] == jnp.arange(e, dtype=I32)[None, :]).astype(F32)
    crow = jnp.dot(oh_c, jnp.stack([ch_end - nch, pe_off, n_e], axis=1), precision=hp)
    k_in = c - crow[:, 0]
    live = jnp.logical_and(c >= 0, c < ch_end[-1])
    cn = jnp.where(live, jnp.clip(crow[:, 2] - G_PIECES * k_in, 0, G_PIECES), 0.0)
    cs = jnp.where(cn > 0, crow[:, 1] + G_PIECES * k_in, 0.0)
    return src, ch_off, cs.astype(I32), cn.astype(I32)


def _ffn_kernel(src_ref, choff_ref, cs_ref, cn_ref, xs_in, wg_ref, wu_ref, wd_ref, xs_out,
                xbuf, ybuf, wgb, wub, wdb, gsem, ssem):
    e = pl.program_id(0)
    total = choff_ref[N_EXPERTS]

    def start_gather(ch):
        sl = lax.rem(ch + N_SLOTS, N_SLOTS)
        base = cs_ref[ch + N_SLOTS]
        n = cn_ref[ch + N_SLOTS]
        for i in range(G_PIECES):
            idx = src_ref[base + jnp.where(i < n, i, 0)]
            pltpu.make_async_copy(xs_in.at[idx], xbuf.at[sl, i], gsem.at[sl]).start(priority=i % 2)

    def wait_gather(ch):
        sl = lax.rem(ch + N_SLOTS, N_SLOTS)
        pltpu.make_async_copy(xs_in.at[pl.ds(0, G_PIECES)], xbuf.at[sl], gsem.at[sl]).wait()

    def start_scatter(ch):
        sl = lax.rem(ch + N_SLOTS, N_SLOTS)
        base = cs_ref[ch + N_SLOTS]
        n = cn_ref[ch + N_SLOTS]
        for i in range(G_PIECES):
            idx = jnp.where(i < n, src_ref[base + i], P_MAX + sl * G_PIECES + i)
            pltpu.make_async_copy(ybuf.at[sl, i], xs_out.at[idx], ssem.at[sl]).start(priority=i % 2)

    def wait_scatter(ch):
        sl = lax.rem(ch + N_SLOTS, N_SLOTS)
        pltpu.make_async_copy(ybuf.at[sl], xs_out.at[pl.ds(0, G_PIECES)], ssem.at[sl]).wait()

    @pl.when(e == 0)
    def _():
        ybuf[...] = jnp.zeros_like(ybuf)
        start_gather(0)
        start_gather(1)
        start_scatter(-3)
        start_scatter(-2)

    wgb[...] = wg_ref[0].astype(BF16)
    wub[...] = wu_ref[0].astype(BF16)
    wdb[...] = wd_ref[0].astype(BF16)

    def chunk(c, carry):
        sl = lax.rem(c, N_SLOTS)
        wait_gather(c)
        wait_scatter(c - 3)
        x = xbuf[sl].reshape(G_PIECES * PIECE, D)
        hid = (_silu(_dot(x, wgb[...])) * _dot(x, wub[...])).astype(BF16)
        start_gather(c + 2)
        start_scatter(c - 1)
        ybuf[sl] = _dot(hid, wdb[...]).astype(BF16).reshape(G_PIECES, PIECE, D)
        return carry

    lax.fori_loop(choff_ref[e], choff_ref[e + 1], chunk, 0)

    @pl.when(e == N_EXPERTS - 1)
    def _():
        start_scatter(total - 1)
        wait_gather(total)
        wait_gather(total + 1)
        wait_scatter(total - 3)
        wait_scatter(total - 2)
        wait_scatter(total - 1)


def _expert_ffn(xs, lists, layer, w_gate, w_up, w_down):
    src, ch_off, cs, cn = lists
    grid_spec = pltpu.PrefetchScalarGridSpec(
        num_scalar_prefetch=4,
        grid=(N_EXPERTS,),
        in_specs=[
            pl.BlockSpec(memory_space=pl.ANY),
            pl.BlockSpec((None, 1, D, D_EXPERT), lambda e, *_: (layer, e, 0, 0)),
            pl.BlockSpec((None, 1, D, D_EXPERT), lambda e, *_: (layer, e, 0, 0)),
            pl.BlockSpec((None, 1, D_EXPERT, D), lambda e, *_: (layer, e, 0, 0)),
        ],
        out_specs=pl.BlockSpec(memory_space=pl.ANY),
        scratch_shapes=[
            pltpu.VMEM((N_SLOTS, G_PIECES, PIECE, D), BF16),
            pltpu.VMEM((N_SLOTS, G_PIECES, PIECE, D), BF16),
            pltpu.VMEM((D, D_EXPERT), BF16),
            pltpu.VMEM((D, D_EXPERT), BF16),
            pltpu.VMEM((D_EXPERT, D), BF16),
            pltpu.SemaphoreType.DMA((N_SLOTS,)),
            pltpu.SemaphoreType.DMA((N_SLOTS,)),
        ],
    )
    out = pl.pallas_call(
        _ffn_kernel,
        grid_spec=grid_spec,
        out_shape=jax.ShapeDtypeStruct((P_TOT, PIECE, D), BF16),
        input_output_aliases={4: 0},
        compiler_params=_cparams(("arbitrary",)),
        name="moe_ffn",
    )(src, ch_off, cs, cn, xs.reshape(P_TOT, PIECE, D), w_gate, w_up, w_down)
    return out.reshape(P_TOT * PIECE, D)


def _combine_kernel(used_ref, ys_ref, pos_ref, wt_ref, x_ref, nw_ref, mod_ref, wsg_ref, wsu_ref, wsd_ref,
                    fw_ref, *rest, final):
    o_refs, acc_ref = rest[:-1], rest[-1]
    i = pl.program_id(0)
    x = x_ref[...]
    h = _norm_mod(x, nw_ref[...], mod_ref[0, 3:4, :], mod_ref[0, 4:5, :]).astype(BF16)
    shared = _dot((_silu(_dot(h, wsg_ref[...])) * _dot(h, wsu_ref[...])).astype(BF16), wsd_ref[...])

    pos16 = pos_ref[...].astype(jnp.int16)
    wt16 = wt_ref[...].astype(BF16)

    def block(r0, n):
        j16 = (lax.broadcasted_iota(I32, (MOE_TM, n), 1) + r0).astype(jnp.int16)
        wm = jnp.zeros((MOE_TM, n), BF16)
        for k in range(TOP_K):
            wm = jnp.where(j16 == pos16[:, k:k + 1], wt16[:, k:k + 1], wm)
        return _dot(wm, ys_ref[r0:r0 + n, :])

    acc_ref[...] = block(0, ROWS_MAIN)
    for r0 in range(ROWS_MAIN, R1, ROW_BLK):
        @pl.when(used_ref[i] > r0)
        def _(r0=r0):
            acc_ref[...] += block(r0, ROW_BLK)

    y = x + mod_ref[0, 5:6, :] * (acc_ref[...] + shared)
    if not final:
        o_refs[0][...] = y
        return
    ms = jnp.mean(y * y, axis=-1, keepdims=True)
    y = y * lax.rsqrt(ms + EPS) * fw_ref[...]
    is_prompt = i < NP // MOE_TM

    @pl.when(is_prompt)
    def _():
        o_refs[0][...] = y

    @pl.when(jnp.logical_not(is_prompt))
    def _():
        o_refs[1][...] = y


def _combine(ys, used, pos_t, wt_t, x, nw, mod, wsg, wsu, wsd, fw, final):
    n_p = NP // MOE_TM
    if final:
        out_specs = [pl.BlockSpec((MOE_TM, D), lambda i, u: (jnp.minimum(i, n_p - 1), 0)),
                     pl.BlockSpec((MOE_TM, D), lambda i, u: (jnp.maximum(i - n_p, 0), 0))]
        out_shape = [jax.ShapeDtypeStruct((NP, D), F32), jax.ShapeDtypeStruct((NS, D), F32)]
    else:
        out_specs = pl.BlockSpec((MOE_TM, D), lambda i, u: (i, 0))
        out_shape = jax.ShapeDtypeStruct((NT, D), F32)
    grid_spec = pltpu.PrefetchScalarGridSpec(
        num_scalar_prefetch=1,
        grid=(N_TILES,),
        in_specs=[
            pl.BlockSpec((R1, D), lambda i, u: (i, 0)),
            pl.BlockSpec((MOE_TM, SUBLANES), lambda i, u: (i, 0)),
            pl.BlockSpec((MOE_TM, SUBLANES), lambda i, u: (i, 0)),
            pl.BlockSpec((MOE_TM, D), lambda i, u: (i, 0)),
            pl.BlockSpec((1, D), lambda i, u: (0, 0)),
            pl.BlockSpec((1, 6, D), lambda i, u: (_mod_group(i, MOE_TM), 0, 0)),
            pl.BlockSpec((D, D_SHARED), lambda i, u: (0, 0)),
            pl.BlockSpec((D, D_SHARED), lambda i, u: (0, 0)),
            pl.BlockSpec((D_SHARED, D), lambda i, u: (0, 0)),
            pl.BlockSpec((1, D), lambda i, u: (0, 0)),
        ],
        out_specs=out_specs,
        scratch_shapes=[pltpu.VMEM((MOE_TM, D), F32)],
    )
    return pl.pallas_call(
        functools.partial(_combine_kernel, final=final),
        grid_spec=grid_spec,
        out_shape=out_shape,
        compiler_params=_cparams(("arbitrary",)),
        name="moe_combine",
    )(used, ys, pos_t, wt_t, x, nw.reshape(1, D), mod, wsg, wsu, wsd, fw.reshape(1, D))


def _moe(x, nw, mod, w_router, rbias, layer, w_gate, w_up, w_down, ws_gate, ws_up, ws_down, fw, final):
    wr = w_router.astype(F32).T
    wr_hi = wr.astype(BF16)
    wr_lo = (wr - wr_hi.astype(F32)).astype(BF16)
    xs, pos, wt, npc = _route(x, nw, mod, wr_hi, wr_lo, rbias.astype(F32))
    npc = npc[:, :, 0]
    lists = _piece_lists(npc)
    ys = _expert_ffn(xs, lists, layer, w_gate, w_up, w_down)
    used = (npc.sum(axis=1) * PIECE).astype(I32)
    return _combine(ys, used, pos.T, wt.T, x, nw, mod, ws_gate.astype(BF16), ws_up.astype(BF16),
                    ws_down.astype(BF16), fw, final)


def kernel(x_prompt, x_sample, state_hgrn_fwd, state_hgrn_bwd, cache_na_k, cache_na_v, c, c_ctx,
           norm1_w, norm2_w, ada_w, ada_b, hgrn_w_in, hgrn_lb_logits, hgrn_gn_w, hgrn_w_out,
           na_w_qkv, na_rpb, na_w_o, moe_w_router, moe_router_bias, moe_w_gate, moe_w_up, moe_w_down,
           shared_w_gate, shared_w_up, shared_w_down, final_norm_w):
    x = (x_prompt.reshape(NP, D), x_sample.reshape(NS, D))
    cvec = jnp.concatenate([c_ctx[None, :], c, jnp.zeros((N_MOD - 1 - DEC_BATCH, D), F32)], axis=0)
    mod = _modulation(cvec, ada_w, ada_b)
    lb_table = jnp.cumsum(jax.nn.softmax(hgrn_lb_logits.astype(F32), axis=0), axis=0)
    hk = HG_HEADS * HG_DK

    sf = sb = k_c = v_c = None
    for l in range(DEPTH):
        if l % 2 == 0:
            a = l // 2
            proj = _norm_proj(x, norm1_w[l], mod[l], hgrn_w_in[a].astype(BF16))
            o_f, o_b, sf, sb = _gla(proj, lb_table[l],
                                    state_hgrn_fwd[:, a].reshape(DEC_BATCH, hk, HG_DV),
                                    state_hgrn_bwd[:, a].reshape(DEC_BATCH, hk, HG_DV))
            x = _hgrn_out(o_f, o_b, proj, x, mod[l], hgrn_gn_w[a], hgrn_w_out[a].astype(BF16))
        else:
            n = l // 2
            qkv, k_p, v_p = _norm_proj(x, norm1_w[l], mod[l], na_w_qkv[n].astype(BF16), n_copy=2)
            att_p = _attn_ctx(qkv)
            att_s = _attn_lat(qkv, cache_na_k[:, n].reshape(DEC_BATCH, PAST_LEN, D),
                              cache_na_v[:, n].reshape(DEC_BATCH, PAST_LEN, D),
                              *_latent_bias_tables(na_rpb[n]))
            x = _attn_out(att_p, att_s, x, mod[l], na_w_o[n].astype(BF16))
            k_c = k_p.reshape(BATCH, SEQ, NA_HEADS, NA_HD)
            v_c = v_p.reshape(BATCH, SEQ, NA_HEADS, NA_HD)
        x = _moe(x, norm2_w[l], mod[l], moe_w_router[l], moe_router_bias[l], l, moe_w_gate, moe_w_up,
                 moe_w_down, shared_w_gate[l], shared_w_up[l], shared_w_down[l], final_norm_w,
                 final=(l == DEPTH - 1))

    y_prompt = x[0].reshape(BATCH, SEQ, D)
    y_sample = x[1].reshape(DEC_BATCH, DEC_SEQ, D)
    new_sf = sf.reshape(BATCH, 1, HG_HEADS, HG_DK, HG_DV)
    new_sb = sb.reshape(BATCH, 1, HG_HEADS, HG_DK, HG_DV)
    return (y_prompt, y_sample, new_sf, new_sb, k_c[:, None], v_c[:, None])
```

```python
import functools

import numpy as np
import jax
import jax.numpy as jnp
from jax import lax
from jax.experimental import pallas as pl
from jax.experimental.pallas import tpu as pltpu

F32 = jnp.float32
BF16 = jnp.bfloat16
I32 = jnp.int32

D = 1024
BATCH = 32
SEQ = 256
DEPTH = 2
DEC_BATCH = 8
DEC_SEQ = 1024
PAST_LEN = 512
GRID_W = 64
HG_HEADS = 8
HG_DK = 128
HG_DV = 128
CHUNK = 64
NA_HEADS = 16
NA_HD = 64
WIN_R = 8
WIN_C = 16
N_EXPERTS = 64
TOP_K = 6
N_GROUPS = 8
TOPK_GROUPS = 4
D_EXPERT = 256
D_SHARED = 256
ROUTED_SCALE = 2.5
EPS = 1e-6

NP = BATCH * SEQ
NS = DEC_BATCH * DEC_SEQ
NT = NP + NS
N_MOD = 16
STRIP_W = 2048
assert (2 * (DEC_SEQ // GRID_W) - 1) * GRID_W <= STRIP_W
NEG = -1e30

LANES = 128
SUBLANES = 8
BF16_ROWS = 16
VMEM_LIMIT = 56 * 1024 * 1024

TM = 256
SUB = 16
GLA_FAST_MAX = 60.0
MOE_TM = 256
PIECE = BF16_ROWS
R1 = MOE_TM * TOP_K + N_EXPERTS * (PIECE - 1) + 64
assert R1 % PIECE == 0 and R1 % LANES == 0
R1B = R1 // PIECE
ROWS_MAIN = 2048
ROW_BLK = 256
assert (R1 - ROWS_MAIN) % ROW_BLK == 0
N_TILES = NT // MOE_TM
P_MAX = N_TILES * R1B
G_PIECES = 32
NCH = P_MAX // G_PIECES + N_EXPERTS
N_SLOTS = 3
P_TOT = P_MAX + R1B
assert N_SLOTS * G_PIECES <= R1B


def _cparams(sem):
    return pltpu.CompilerParams(dimension_semantics=sem, vmem_limit_bytes=VMEM_LIMIT)


def _dot(a, b):
    return jnp.dot(a, b, preferred_element_type=F32)


def _dot_nt(a, b):
    return lax.dot_general(a, b, (((1,), (1,)), ((), ())), preferred_element_type=F32)


def _dot_tn(a, b):
    return lax.dot_general(a, b, (((0,), (0,)), ((), ())), preferred_element_type=F32)


def _silu(x):
    return x * jax.nn.sigmoid(x)


def _mod_group(i, tm):
    r = i * tm
    return jnp.where(r < NP, 0, 1 + (r - NP) // DEC_SEQ)


def _norm_mod(x, nw, shift, scale):
    ms = jnp.mean(x * x, axis=-1, keepdims=True)
    y = x * lax.rsqrt(ms + EPS) * nw
    return y * (1.0 + scale) + shift


def _mod_kernel(c_ref, w_ref, b_ref, o_ref):
    s = _silu(c_ref[...]).astype(BF16)
    o_ref[0] = _dot(s, w_ref[0].astype(BF16)) + b_ref[0]


def _modulation(cvec, ada_w, ada_b):
    cw = 1536
    n = ada_w.shape[-1]
    out = pl.pallas_call(
        _mod_kernel,
        grid=(DEPTH, n // cw),
        in_specs=[
            pl.BlockSpec((N_MOD, D), lambda l, j: (0, 0)),
            pl.BlockSpec((1, D, cw), lambda l, j: (l, 0, j)),
            pl.BlockSpec((1, 1, cw), lambda l, j: (l, 0, j)),
        ],
        out_specs=pl.BlockSpec((1, N_MOD, cw), lambda l, j: (l, 0, j)),
        out_shape=jax.ShapeDtypeStruct((DEPTH, N_MOD, n), F32),
        compiler_params=_cparams(("arbitrary", "arbitrary")),
        name="modulation",
    )(cvec, ada_w, ada_b.reshape(DEPTH, 1, n))
    return out.reshape(DEPTH, N_MOD, 6, D)


def _stream_specs(x, tm):
    if not isinstance(x, tuple):
        return [pl.BlockSpec((tm, D), lambda i: (i, 0))], (x,)
    n_p = NP // tm
    return [pl.BlockSpec((tm, D), lambda i: (jnp.minimum(i, n_p - 1), 0)),
            pl.BlockSpec((tm, D), lambda i: (jnp.maximum(i - n_p, 0), 0))], x


def _stream_tile(refs, tm):
    if len(refs) == 1:
        return refs[0][...]
    return jnp.where(pl.program_id(0) < NP // tm, refs[0][...], refs[1][...])


def _proj_kernel(*refs, cw, n_x, n_copy):
    nw_ref, mod_ref, w_ref, o_ref = refs[n_x:n_x + 4]
    copy_refs = refs[n_x + 4:]
    x = _stream_tile(refs[:n_x], TM)
    h = _norm_mod(x, nw_ref[...], mod_ref[0, 0:1, :], mod_ref[0, 1:2, :]).astype(BF16)
    is_prompt = pl.program_id(0) < NP // TM
    for j in range(w_ref.shape[1] // cw):
        val = _dot(h, w_ref[:, j * cw:(j + 1) * cw])
        o_ref[:, j * cw:(j + 1) * cw] = val.astype(o_ref.dtype)
        col = j * cw - D
        if 0 <= col < n_copy * D:
            c_ref = copy_refs[col // D]

            @pl.when(is_prompt)
            def _(c_ref=c_ref, val=val, col=col):
                c_ref[:, col % D:col % D + cw] = val


def _norm_proj(x, nw, mod, w, n_copy=0, out_dtype=F32):
    n = w.shape[1]
    n_p = NP // TM
    x_specs, xs = _stream_specs(x, TM)
    outs = pl.pallas_call(
        functools.partial(_proj_kernel, cw=512, n_x=len(xs), n_copy=n_copy),
        grid=(NT // TM,),
        in_specs=x_specs + [
            pl.BlockSpec((1, D), lambda i: (0, 0)),
            pl.BlockSpec((1, 6, D), lambda i: (_mod_group(i, TM), 0, 0)),
            pl.BlockSpec((D, n), lambda i: (0, 0)),
        ],
        out_specs=[pl.BlockSpec((TM, n), lambda i: (i, 0))]
        + [pl.BlockSpec((TM, D), lambda i: (jnp.minimum(i, n_p - 1), 0))] * n_copy,
        out_shape=[jax.ShapeDtypeStruct((NT, n), out_dtype)] + [jax.ShapeDtypeStruct((NP, D), F32)] * n_copy,
        compiler_params=_cparams(("arbitrary",)),
        name="norm_proj",
    )(*xs, nw.reshape(1, D), mod, w)
    return outs if n_copy else outs[0]


def _split3(x):
    hi = x.astype(BF16)
    r = x - hi.astype(F32)
    mid = r.astype(BF16)
    lo = (r - mid.astype(F32)).astype(BF16)
    return hi, mid, lo


def _gla_safe(q, kk, v, g3, st_ref, row_refs, rev):
    bs_ref, ks_ref, kd8_ref = row_refs

    n_stack = 4
    t_io = lax.broadcasted_iota(I32, (n_stack * CHUNK, CHUNK), 0)
    u_io = lax.broadcasted_iota(I32, (n_stack * CHUNK, CHUNK), 1)
    which = t_io // CHUNK
    tt = t_io - which * CHUNK
    b16 = (tt // SUB) * SUB
    b8 = (tt // SUBLANES) * SUBLANES
    if rev:
        lim = jnp.where(which == 0, tt, jnp.where(which == 1, b16 + SUB, jnp.where(which == 2, b8 + SUBLANES, b8)))
        pick = u_io >= lim
    else:
        lim = jnp.where(which == 0, tt, jnp.where(which == 1, b16 - 1,
                                                  jnp.where(which == 2, b8 - 1, b8 + SUBLANES - 1)))
        pick = u_io <= lim
    tri = jnp.where(pick, 1.0, 0.0).astype(BF16)
    hi, mid, lo = g3
    cs = _dot(tri, hi) + _dot(tri, mid) + _dot(tri, lo)
    b, r16, r8, e8 = (cs[j * CHUNK:(j + 1) * CHUNK] for j in range(n_stack))
    last = 0 if rev else CHUNK - 1
    tot = b[last:last + 1]

    qe = (q * jnp.exp(b)).astype(BF16)
    qd = (q * jnp.exp(b - r16)).astype(BF16)
    qd8 = q * jnp.exp(b - r8)
    kdec = (kk * jnp.exp(tot - b)).astype(BF16)
    vb = v.astype(BF16)
    dec_tot = jnp.exp(tot)
    kd8 = kk * jnp.exp(e8 - b)
    for h in range(HG_HEADS):
        sl = slice(h * LANES, (h + 1) * LANES)
        bs_ref[h] = b[:, sl]
        ks_ref[h] = kk[:, sl]
        kd8_ref[h] = kd8[:, sl]

    def row(ref, h, s):
        return jnp.broadcast_to(ref[h, s:s + 1, :], (SUBLANES, LANES))

    lane = lax.broadcasted_iota(I32, (SUBLANES, LANES), 1)
    row8 = lax.broadcasted_iota(I32, (SUBLANES, LANES), 0)
    a_in = [[] for _ in range(HG_HEADS)]
    for tb in range(CHUNK // SUBLANES):
        t0 = tb * SUBLANES
        blk0 = (t0 // SUB) * SUB
        other = blk0 + SUBLANES if t0 == blk0 else blk0
        keep = (row8 + t0 <= lane) if rev else (row8 + t0 >= lane)
        for h in range(HG_HEADS):
            sl = slice(h * LANES, (h + 1) * LANES)
            q_t = q[t0:t0 + SUBLANES, sl]
            b_t = b[t0:t0 + SUBLANES, sl]
            acc = jnp.zeros((SUBLANES, LANES), F32)
            for s in range(t0, t0 + SUBLANES):
                p = q_t * jnp.exp(b_t - row(bs_ref, h, s)) * row(ks_ref, h, s)
                acc = jnp.where(lane == s, jnp.sum(p, axis=-1, keepdims=True), acc)
            acc = jnp.where(keep, acc, 0.0)
            if (t0 == blk0) == rev:
                qd8_t = qd8[t0:t0 + SUBLANES, sl]
                for s in range(other, other + SUBLANES):
                    p = qd8_t * row(kd8_ref, h, s)
                    acc = jnp.where(lane == s, jnp.sum(p, axis=-1, keepdims=True), acc)
            a_in[h].append(acc)

    n_blk = CHUNK // SUB
    o_heads = []
    for h in range(HG_HEADS):
        sl = slice(h * LANES, (h + 1) * LANES)
        st = st_ref[sl, :]
        o_h = _dot_nt(qe[:, sl], st.astype(BF16))
        rows = []
        for i in range(n_blk):
            lo_r, hi_r = ((i + 1) * SUB, CHUNK) if rev else (0, i * SUB)
            if hi_r == lo_r:
                rows.append(jnp.zeros((SUB, CHUNK), F32))
                continue
            r_i = r16[i * SUB:i * SUB + 1, sl]
            kd = kk[lo_r:hi_r, sl] * jnp.exp(r_i - b[lo_r:hi_r, sl])
            pads = [jnp.zeros((lo_r, LANES), F32)] if lo_r else []
            pads_hi = [jnp.zeros((CHUNK - hi_r, LANES), F32)] if hi_r < CHUNK else []
            kd = jnp.concatenate(pads + [kd] + pads_hi, axis=0).astype(BF16)
            rows.append(_dot_nt(qd[i * SUB:(i + 1) * SUB, sl], kd))
        a1 = jnp.concatenate(rows, axis=0) + jnp.concatenate(a_in[h], axis=0)[:, :CHUNK]
        o_h = o_h + _dot(a1.astype(BF16), vb[:, sl])
        o_heads.append(o_h)
        st_ref[sl, :] = st * dec_tot[:, sl] + _dot_tn(vb[:, sl], kdec[:, sl])
    return jnp.concatenate(o_heads, axis=1)


def _gla_fast(q, kk, v, b, tot, st_ref, rev):
    qe = (q * jnp.exp(b)).astype(BF16)
    ke32 = kk * jnp.exp(-b)
    ke = ke32.astype(BF16)
    dec_tot = jnp.exp(tot)
    kdec = (ke32 * dec_tot).astype(BF16)
    vb = v.astype(BF16)
    t_io = lax.broadcasted_iota(I32, (CHUNK, CHUNK), 0)
    s_io = lax.broadcasted_iota(I32, (CHUNK, CHUNK), 1)
    keep = (s_io >= t_io) if rev else (s_io <= t_io)
    o_heads = []
    for h in range(HG_HEADS):
        sl = slice(h * LANES, (h + 1) * LANES)
        st = st_ref[sl, :]
        a = jnp.where(keep, _dot_nt(qe[:, sl], ke[:, sl]), 0.0).astype(BF16)
        o_heads.append(_dot_nt(qe[:, sl], st.astype(BF16)) + _dot(a, vb[:, sl]))
        st_ref[sl, :] = st * dec_tot[:, sl] + _dot_tn(vb[:, sl], kdec[:, sl])
    return jnp.concatenate(o_heads, axis=1)


def _gla_prep(fraw, lb, rev):
    f = lb + (1.0 - lb) * jax.nn.sigmoid(fraw)
    kk = 1.0 - f
    g3 = _split3(jnp.log(f))
    t_io = lax.broadcasted_iota(I32, (CHUNK, CHUNK), 0)
    u_io = lax.broadcasted_iota(I32, (CHUNK, CHUNK), 1)
    tri = jnp.where((u_io >= t_io) if rev else (u_io <= t_io), 1.0, 0.0).astype(BF16)
    b = _dot(tri, g3[0]) + _dot(tri, g3[1]) + _dot(tri, g3[2])
    last = 0 if rev else CHUNK - 1
    return kk, g3, b, b[last:last + 1]


def _gla_kernel(qf_ref, ff_ref, vf_ref, qb_ref, fb_ref, vb_ref, lb_ref, s0f_ref, s0b_ref,
                of_ref, ob_ref, sf_ref, sb_ref, stf, stb, bs, ks, kd8):
    i = pl.program_id(0)
    n_p = NP // CHUNK
    is_prompt = i < n_p
    c = jnp.where(is_prompt, i % (SEQ // CHUNK), (i - n_p) % (DEC_SEQ // CHUNK))
    n_c = jnp.where(is_prompt, SEQ // CHUNK, DEC_SEQ // CHUNK)

    @pl.when(jnp.logical_and(c == 0, is_prompt))
    def _():
        stf[...] = jnp.zeros_like(stf)
        stb[...] = jnp.zeros_like(stb)

    @pl.when(jnp.logical_and(c == 0, jnp.logical_not(is_prompt)))
    def _():
        for h in range(HG_HEADS):
            sl = slice(h * LANES, (h + 1) * LANES)
            stf[sl, :] = s0f_ref[0, sl, :].T
            stb[sl, :] = s0b_ref[0, sl, :].T

    lb = lb_ref[...]
    rows = (bs, ks, kd8)
    kk_f, g3_f, b_f, tot_f = _gla_prep(ff_ref[...], lb, False)
    kk_b, g3_b, b_b, tot_b = _gla_prep(fb_ref[...], lb, True)
    mild = jnp.min(jnp.minimum(tot_f, tot_b)) >= -GLA_FAST_MAX

    @pl.when(mild)
    def _():
        of_ref[...] = _gla_fast(qf_ref[...], kk_f, vf_ref[...], b_f, tot_f, stf, False)
        ob_ref[...] = _gla_fast(qb_ref[...], kk_b, vb_ref[...], b_b, tot_b, stb, True)

    @pl.when(jnp.logical_not(mild))
    def _():
        of_ref[...] = _gla_safe(qf_ref[...], kk_f, vf_ref[...], g3_f, stf, rows, False)
        ob_ref[...] = _gla_safe(qb_ref[...], kk_b, vb_ref[...], g3_b, stb, rows, True)

    @pl.when(jnp.logical_and(c == n_c - 1, is_prompt))
    def _():
        for h in range(HG_HEADS):
            sl = slice(h * LANES, (h + 1) * LANES)
            sf_ref[0, sl, :] = stf[sl, :].T
            sb_ref[0, sl, :] = stb[sl, :].T


def _gla(proj, lb, s0f, s0b):
    n_p = NP // CHUNK
    cp = SEQ // CHUNK
    cs = DEC_SEQ // CHUNK

    def bwd_blk(i):
        jp = (i // cp) * cp + (cp - 1 - i % cp)
        j = i - n_p
        js = n_p + (j // cs) * cs + (cs - 1 - j % cs)
        return jnp.where(i < n_p, jp, js)

    def req(i):
        return jnp.maximum(i - n_p, 0) // cs

    def preq(i):
        return jnp.minimum(i // cp, BATCH - 1)

    hk = HG_HEADS * HG_DK
    return pl.pallas_call(
        _gla_kernel,
        grid=(NT // CHUNK,),
        in_specs=[
            pl.BlockSpec((CHUNK, D), lambda i: (i, 0)),
            pl.BlockSpec((CHUNK, D), lambda i: (i, 1)),
            pl.BlockSpec((CHUNK, D), lambda i: (i, 3)),
            pl.BlockSpec((CHUNK, D), lambda i: (bwd_blk(i), 0)),
            pl.BlockSpec((CHUNK, D), lambda i: (bwd_blk(i), 2)),
            pl.BlockSpec((CHUNK, D), lambda i: (bwd_blk(i), 3)),
            pl.BlockSpec((1, hk), lambda i: (0, 0)),
            pl.BlockSpec((1, hk, HG_DV), lambda i: (req(i), 0, 0)),
            pl.BlockSpec((1, hk, HG_DV), lambda i: (req(i), 0, 0)),
        ],
        out_specs=[
            pl.BlockSpec((CHUNK, D), lambda i: (i, 0)),
            pl.BlockSpec((CHUNK, D), lambda i: (bwd_blk(i), 0)),
            pl.BlockSpec((1, hk, HG_DV), lambda i: (preq(i), 0, 0)),
            pl.BlockSpec((1, hk, HG_DV), lambda i: (preq(i), 0, 0)),
        ],
        out_shape=[
            jax.ShapeDtypeStruct((NT, D), F32),
            jax.ShapeDtypeStruct((NT, D), F32),
            jax.ShapeDtypeStruct((BATCH, hk, HG_DV), F32),
            jax.ShapeDtypeStruct((BATCH, hk, HG_DV), F32),
        ],
        scratch_shapes=[pltpu.VMEM((hk, HG_DV), F32), pltpu.VMEM((hk, HG_DV), F32)]
        + [pltpu.VMEM((HG_HEADS, CHUNK, HG_DK), F32)] * 3,
        compiler_params=_cparams(("arbitrary",)),
        name="gla_scan",
    )(proj, proj, proj, proj, proj, proj, lb.reshape(1, hk), s0f, s0b)


def _hgrn_out_kernel(of_ref, ob_ref, gate_ref, mod_ref, gn_ref, w_ref, *refs):
    o_ref = refs[-1]
    x = _stream_tile(refs[:-1], TM)
    o = of_ref[...] + ob_ref[...]
    gn = gn_ref[...]
    segs = []
    for h in range(HG_HEADS):
        seg = o[:, h * LANES:(h + 1) * LANES]
        ms = jnp.mean(seg * seg, axis=-1, keepdims=True)
        segs.append(seg * lax.rsqrt(ms + EPS) * gn)
    y = (jnp.concatenate(segs, axis=1) * _silu(gate_ref[...])).astype(BF16)
    o_ref[...] = x + mod_ref[0, 2:3, :] * _dot(y, w_ref[...])


def _hgrn_out(o_f, o_b, proj, x, mod, gn_w, w_out):
    x_specs, xs = _stream_specs(x, TM)
    return pl.pallas_call(
        _hgrn_out_kernel,
        grid=(NT // TM,),
        in_specs=[
            pl.BlockSpec((TM, D), lambda i: (i, 0)),
            pl.BlockSpec((TM, D), lambda i: (i, 0)),
            pl.BlockSpec((TM, D), lambda i: (i, 4)),
            pl.BlockSpec((1, 6, D), lambda i: (_mod_group(i, TM), 0, 0)),
            pl.BlockSpec((1, HG_DV), lambda i: (0, 0)),
            pl.BlockSpec((D, D), lambda i: (0, 0)),
        ] + x_specs,
        out_specs=pl.BlockSpec((TM, D), lambda i: (i, 0)),
        out_shape=jax.ShapeDtypeStruct((NT, D), F32),
        compiler_params=_cparams(("arbitrary",)),
        name="hgrn_out",
    )(o_f, o_b, proj, mod, gn_w.reshape(1, HG_DV), w_out, *xs)


def _attn_out_kernel(ap_ref, as_ref, x_ref, mod_ref, w_ref, o_ref):
    a = jnp.where(pl.program_id(0) < NP // TM, ap_ref[...], as_ref[...]).astype(BF16)
    o_ref[...] = x_ref[...] + mod_ref[0, 2:3, :] * _dot(a, w_ref[...])


def _attn_out(a_p, a_s, x, mod, w_o):
    n_p = NP // TM
    return pl.pallas_call(
        _attn_out_kernel,
        grid=(NT // TM,),
        in_specs=[
            pl.BlockSpec((TM, D), lambda i: (jnp.minimum(i, n_p - 1), 0)),
            pl.BlockSpec((TM, D), lambda i: (jnp.maximum(i - n_p, 0), 0)),
            pl.BlockSpec((TM, D), lambda i: (i, 0)),
            pl.BlockSpec((1, 6, D), lambda i: (_mod_group(i, TM), 0, 0)),
            pl.BlockSpec((D, D), lambda i: (0, 0)),
        ],
        out_specs=pl.BlockSpec((TM, D), lambda i: (i, 0)),
        out_shape=jax.ShapeDtypeStruct((NT, D), F32),
        compiler_params=_cparams(("arbitrary",)),
        name="attn_out",
    )(a_p, a_s, x, mod, w_o)


def _head_pair_masks():
    upper = lax.broadcasted_iota(I32, (1, LANES), 1) >= NA_HD
    return upper


def _attn_ctx_kernel(q_ref, k_ref, v_ref, o_ref):
    upper = _head_pair_masks()
    for p in range(NA_HEADS // 2):
        sl = slice(p * LANES, (p + 1) * LANES)
        qp = q_ref[:, sl] * (NA_HD ** -0.5)
        kp = k_ref[:, sl].astype(BF16)
        vp = v_ref[:, sl].astype(BF16)
        outs = []
        for u in range(2):
            qm = jnp.where(upper if u else jnp.logical_not(upper), qp, 0.0).astype(BF16)
            s = _dot_nt(qm, kp)
            e = jnp.exp(s - jnp.max(s, axis=-1, keepdims=True))
            outs.append(_dot(e.astype(BF16), vp) / jnp.sum(e, axis=-1, keepdims=True))
        o_ref[:, sl] = jnp.where(upper, outs[1], outs[0]).astype(o_ref.dtype)


def _attn_ctx(qkv):
    return pl.pallas_call(
        _attn_ctx_kernel,
        grid=(BATCH,),
        in_specs=[
            pl.BlockSpec((SEQ, D), lambda b: (b, 0)),
            pl.BlockSpec((SEQ, D), lambda b: (b, 1)),
            pl.BlockSpec((SEQ, D), lambda b: (b, 2)),
        ],
        out_specs=pl.BlockSpec((SEQ, D), lambda b: (b, 0)),
        out_shape=jax.ShapeDtypeStruct((NP, D), BF16),
        compiler_params=_cparams(("arbitrary",)),
        name="attn_ctx",
    )(qkv, qkv, qkv)


def _attn_lat_kernel(q_ref, k_ref, v_ref, kc_ref, vc_ref, strip_ref, rowmask_ref, o_ref, *, tq):
    rows = DEC_SEQ // GRID_W
    kr = min(WIN_R, rows)
    rpt = tq // GRID_W

    def bias_rows(u, r, k_lo, k_hi):
        first = rows - 1 - r + k_lo
        var = first % 2
        strip = strip_ref[u, var, :, (first - var) * GRID_W:(first - var + k_hi - k_lo) * GRID_W]
        return strip + rowmask_ref[r:r + 1, k_lo * GRID_W:k_hi * GRID_W]

    upper = _head_pair_masks()
    mine = (jnp.logical_not(upper), upper)
    kl = k_ref[...].astype(BF16)
    kc = kc_ref[0].astype(BF16)
    vl = [jnp.where(mine[u], v_ref[...], 1.0).astype(BF16) for u in range(2)]
    vc = [jnp.where(mine[u], vc_ref[0], 1.0).astype(BF16) for u in range(2)]
    for t in range(DEC_SEQ // tq):
        rs = slice(t * tq, (t + 1) * tq)
        k_lo = min(max(t * rpt - kr // 2, 0), rows - kr) // 2 * 2
        k_hi = -(-(min(max(t * rpt + rpt - 1 - kr // 2, 0), rows - kr) + kr) // 2) * 2
        ks = slice(k_lo * GRID_W, k_hi * GRID_W)
        qp = q_ref[rs, :] * (NA_HD ** -0.5)
        outs = []
        for u in range(2):
            qm = jnp.where(mine[u], qp, 0.0).astype(BF16)
            bias = jnp.concatenate([bias_rows(u, t * rpt + a, k_lo, k_hi) for a in range(rpt)], axis=0)
            s_l = _dot_nt(qm, kl[ks]) + bias
            s_c = _dot_nt(qm, kc)
            m = jnp.maximum(jnp.max(s_l, axis=-1, keepdims=True), jnp.max(s_c, axis=-1, keepdims=True))
            acc = (_dot(jnp.exp(s_l - m).astype(BF16), vl[u][ks])
                   + _dot(jnp.exp(s_c - m).astype(BF16), vc[u]))
            outs.append(acc / pltpu.roll(acc, NA_HD, 1))
        o_ref[rs, :] = jnp.where(upper, outs[1], outs[0]).astype(o_ref.dtype)


def _attn_lat(qkv, k_ctx, v_ctx, strips, rowmask):
    npair = NA_HEADS // 2
    rb0 = NP // DEC_SEQ
    return pl.pallas_call(
        functools.partial(_attn_lat_kernel, tq=256),
        grid=(npair, DEC_BATCH),
        in_specs=[
            pl.BlockSpec((DEC_SEQ, LANES), lambda p, b: (rb0 + b, p)),
            pl.BlockSpec((DEC_SEQ, LANES), lambda p, b: (rb0 + b, npair + p)),
            pl.BlockSpec((DEC_SEQ, LANES), lambda p, b: (rb0 + b, 2 * npair + p)),
            pl.BlockSpec((1, PAST_LEN, LANES), lambda p, b: (b, 0, p)),
            pl.BlockSpec((1, PAST_LEN, LANES), lambda p, b: (b, 0, p)),
            pl.BlockSpec((2, 2, GRID_W, STRIP_W), lambda p, b: (p, 0, 0, 0)),
            pl.BlockSpec((DEC_SEQ // GRID_W, DEC_SEQ), lambda p, b: (0, 0)),
        ],
        out_specs=pl.BlockSpec((DEC_SEQ, LANES), lambda p, b: (b, p)),
        out_shape=jax.ShapeDtypeStruct((NS, D), BF16),
        compiler_params=_cparams(("arbitrary", "arbitrary")),
        name="attn_lat",
    )(qkv, qkv, qkv, k_ctx, v_ctx, strips, rowmask)


def _latent_bias_tables(rpb):
    rows = DEC_SEQ // GRID_W
    kr = min(WIN_R, rows)
    ndr, ndc = 2 * WIN_R - 1, 2 * WIN_C - 1
    qc = np.arange(GRID_W)
    kc = np.arange(GRID_W)
    ws = np.clip(qc - WIN_C // 2, 0, GRID_W - WIN_C)
    col_ok = (kc[None, :] >= ws[:, None]) & (kc[None, :] < ws[:, None] + WIN_C)
    dc = np.clip(kc[None, :] - qc[:, None] + WIN_C - 1, 0, ndc - 1)
    onehot = (dc[None, :, :] == np.arange(ndc)[:, None, None]).astype(np.float32)
    t2 = jnp.einsum('hrc,cqk->hqrk', rpb.astype(F32), jnp.asarray(onehot),
                    precision=lax.Precision.HIGHEST)
    t2 = jnp.where(jnp.asarray(col_ok)[None, :, None, :], t2, NEG)
    lead = rows - WIN_R
    n_tiles = 2 * rows - 1

    def neg(n):
        return jnp.full((NA_HEADS, GRID_W, n, GRID_W), NEG, F32)

    strip = jnp.concatenate([neg(lead), t2, neg(n_tiles - lead - ndr)], axis=2)
    strip = strip.reshape(NA_HEADS, GRID_W, n_tiles * GRID_W)

    def pad(a):
        return jnp.pad(a, ((0, 0), (0, 0), (0, STRIP_W - a.shape[-1])), constant_values=NEG)

    strips = jnp.stack([pad(strip), pad(strip[:, :, GRID_W:])], axis=1)
    r = np.arange(rows)
    k0 = np.clip(r - kr // 2, 0, rows - kr)
    krow = np.arange(DEC_SEQ) // GRID_W
    row_ok = (krow[None, :] >= k0[:, None]) & (krow[None, :] < k0[:, None] + kr)
    rowmask = jnp.asarray(np.where(row_ok, 0.0, NEG).astype(np.float32))
    return strips, rowmask


def _route_kernel(x_ref, nw_ref, mod_ref, wrh_ref, wrl_ref, rb_ref, xs_ref, pos_ref, wt_ref, npc_ref,
                  hb_s, pos_s, used_s):
    i = pl.program_id(0)
    cur = lax.rem(i, 2)

    @pl.when(i == 0)
    def _():
        hb_s[...] = jnp.zeros_like(hb_s)
        pos_s[...] = jnp.full(pos_s.shape, -1, I32)
        used_s[0] = 0.0
        used_s[1] = 0.0

    live = i < N_TILES

    def step(slot):
        hb, pos, used = _route_select(x_ref, nw_ref, mod_ref, wrh_ref, wrl_ref, rb_ref, pos_ref, wt_ref,
                                      npc_ref)
        hb_s[slot] = hb
        pos_s[slot] = jnp.where(live, pos, -1)
        used_s[slot] = jnp.where(live, used, 0.0)
        _route_sort(xs_ref, hb_s[1 - slot], pos_s[1 - slot], used_s[1 - slot])

    for slot in range(2):
        pl.when(cur == slot)(functools.partial(step, slot))


def _route_sort(xs_ref, hb, pos, used):
    pos16 = [pos[k:k + 1].astype(jnp.int16) for k in range(TOP_K)]

    def fill(r0, n):
        j16 = (lax.broadcasted_iota(I32, (n, MOE_TM), 0) + r0).astype(jnp.int16)
        onehot = jnp.zeros((n, MOE_TM), BF16)
        for k in range(TOP_K):
            onehot = jnp.where(j16 == pos16[k], jnp.ones((), BF16), onehot)
        xs_ref[r0:r0 + n, :] = _dot(onehot, hb).astype(BF16)

    fill(0, ROWS_MAIN)
    for r0 in range(ROWS_MAIN, R1, ROW_BLK):
        pl.when(used > r0)(functools.partial(fill, r0, ROW_BLK))

        @pl.when(used <= r0)
        def _(r0=r0):
            xs_ref[r0:r0 + ROW_BLK, :] = jnp.zeros((ROW_BLK, D), BF16)


def _route_select(x_ref, nw_ref, mod_ref, wrh_ref, wrl_ref, rb_ref, pos_ref, wt_ref, npc_ref):
    h = _norm_mod(x_ref[...], nw_ref[...], mod_ref[0, 3:4, :], mod_ref[0, 4:5, :])
    hb = h.astype(BF16)
    hl = (h - hb.astype(F32)).astype(BF16)
    wrh = wrh_ref[...]
    logits = _dot_nt(wrh, hb) + _dot_nt(wrh, hl) + _dot_nt(wrl_ref[...], hb)
    scores = jax.nn.sigmoid(logits)
    sel = scores + rb_ref[...]

    gsz = N_EXPERTS // N_GROUPS
    sub = lax.broadcasted_iota(I32, (gsz, MOE_TM), 0)
    ninf = -jnp.inf
    gs_rows = []
    for gi in range(N_GROUPS):
        blk = sel[gi * gsz:(gi + 1) * gsz]
        m1 = jnp.max(blk, axis=0, keepdims=True)
        first = jnp.min(jnp.where(blk == m1, sub, gsz), axis=0, keepdims=True)
        m2 = jnp.max(jnp.where(sub == first, ninf, blk), axis=0, keepdims=True)
        gs_rows.append(m1 + m2)
    cur = jnp.concatenate(gs_rows, axis=0)
    gidx = lax.broadcasted_iota(I32, (N_GROUPS, MOE_TM), 0)
    gsel = jnp.zeros((N_GROUPS, MOE_TM), F32)
    for _ in range(TOPK_GROUPS):
        m = jnp.max(cur, axis=0, keepdims=True)
        first = jnp.min(jnp.where(cur == m, gidx, N_GROUPS), axis=0, keepdims=True)
        hit = gidx == first
        gsel = jnp.where(hit, 1.0, gsel)
        cur = jnp.where(hit, ninf, cur)
    emask = jnp.concatenate(
        [jnp.broadcast_to(gsel[gi:gi + 1], (gsz, MOE_TM)) for gi in range(N_GROUPS)], axis=0)
    masked = jnp.where(emask > 0.5, sel, ninf)
    eidx = lax.broadcasted_iota(I32, (N_EXPERTS, MOE_TM), 0)
    chosen = jnp.zeros((N_EXPERTS, MOE_TM), F32)
    hits, wsel = [], []
    for _ in range(TOP_K):
        m = jnp.max(masked, axis=0, keepdims=True)
        first = jnp.min(jnp.where(masked == m, eidx, N_EXPERTS), axis=0, keepdims=True)
        hit = eidx == first
        hits.append(hit)
        wsel.append(jnp.sum(jnp.where(hit, scores, 0.0), axis=0, keepdims=True))
        chosen = jnp.where(hit, 1.0, chosen)
        masked = jnp.where(hit, ninf, masked)
    wsum = wsel[0]
    for w in wsel[1:]:
        wsum = wsum + w

    n_io = lax.broadcasted_iota(I32, (MOE_TM, MOE_TM), 0)
    m_io = lax.broadcasted_iota(I32, (MOE_TM, MOE_TM), 1)
    earlier = jnp.where(n_io < m_io, 1.0, 0.0).astype(BF16)
    rank = _dot(chosen.astype(BF16), earlier)
    cnt = jnp.sum(chosen, axis=1, keepdims=True)
    npc = jnp.floor((cnt + (PIECE - 1)) * (1.0 / PIECE))
    e_io = lax.broadcasted_iota(I32, (N_EXPERTS, N_EXPERTS), 0)
    f_io = lax.broadcasted_iota(I32, (N_EXPERTS, N_EXPERTS), 1)
    below = jnp.where(f_io < e_io, 1.0, 0.0).astype(BF16)
    npc_l = jnp.broadcast_to(npc, (N_EXPERTS, LANES))
    start = _dot(below, npc_l.astype(BF16))[:, 0:1] * PIECE
    slot = start + rank

    pos_rows, wt_rows = [], []
    for k in range(TOP_K):
        pos_k = jnp.sum(jnp.where(hits[k], slot, 0.0), axis=0, keepdims=True).astype(I32)
        pos_rows.append(pos_k)
        wt_rows.append(wsel[k] / wsum * ROUTED_SCALE)

    pad = SUBLANES - TOP_K
    pos = jnp.concatenate(pos_rows + [jnp.full((pad, MOE_TM), -1, I32)], axis=0)
    pos_ref[...] = pos
    wt_ref[...] = jnp.concatenate(wt_rows + [jnp.zeros((pad, MOE_TM), F32)], axis=0)
    npc_ref[0] = npc_l.astype(I32)
    return hb, pos, jnp.sum(npc) * PIECE


def _route(x, nw, mod, wr_hi, wr_lo, rbias):
    def tile(i):
        return jnp.minimum(i, N_TILES - 1)

    return pl.pallas_call(
        _route_kernel,
        grid=(N_TILES + 2,),
        in_specs=[
            pl.BlockSpec((MOE_TM, D), lambda i: (tile(i), 0)),
            pl.BlockSpec((1, D), lambda i: (0, 0)),
            pl.BlockSpec((1, 6, D), lambda i: (_mod_group(tile(i), MOE_TM), 0, 0)),
            pl.BlockSpec((N_EXPERTS, D), lambda i: (0, 0)),
            pl.BlockSpec((N_EXPERTS, D), lambda i: (0, 0)),
            pl.BlockSpec((N_EXPERTS, 1), lambda i: (0, 0)),
        ],
        out_specs=[
            pl.BlockSpec((R1, D), lambda i: (jnp.maximum(i - 1, 0), 0)),
            pl.BlockSpec((SUBLANES, MOE_TM), lambda i: (0, tile(i))),
            pl.BlockSpec((SUBLANES, MOE_TM), lambda i: (0, tile(i))),
            pl.BlockSpec((1, N_EXPERTS, LANES), lambda i: (tile(i), 0, 0)),
        ],
        out_shape=[
            jax.ShapeDtypeStruct((P_TOT * PIECE, D), BF16),
            jax.ShapeDtypeStruct((SUBLANES, NT), I32),
            jax.ShapeDtypeStruct((SUBLANES, NT), F32),
            jax.ShapeDtypeStruct((N_TILES, N_EXPERTS, LANES), I32),
        ],
        scratch_shapes=[pltpu.VMEM((2, MOE_TM, D), BF16), pltpu.VMEM((2, SUBLANES, MOE_TM), I32),
                        pltpu.SMEM((2,), F32)],
        compiler_params=_cparams(("arbitrary",)),
        name="moe_route",
    )(x, nw.reshape(1, D), mod, wr_hi, wr_lo, rbias.reshape(N_EXPERTS, 1))


def _piece_lists(npc):
    t, e = npc.shape
    hp = lax.Precision.HIGHEST
    npc_t = npc.T.astype(F32)
    start_t = (jnp.cumsum(npc, axis=1) - npc).T.astype(F32)
    tile_end = jnp.cumsum(npc_t, axis=1)
    n_e = tile_end[:, -1]
    pe_end = jnp.cumsum(n_e)
    pe_off = pe_end - n_e
    p = jnp.arange(P_MAX, dtype=F32)
    e_p = jnp.minimum(jnp.sum((pe_end[None, :] <= p[:, None]).astype(I32), axis=1), e - 1)
    oh_e = (e_p[:, None] == jnp.arange(e, dtype=I32)[None, :]).astype(F32)
    tab = jnp.concatenate([tile_end, start_t, npc_t, pe_off[:, None]], axis=1)
    row = jnp.dot(oh_e, tab, precision=hp)
    te_p, st_p, np_p, off_p = row[:, :t], row[:, t:2 * t], row[:, 2 * t:3 * t], row[:, 3 * t]
    local = p - off_p
    t_p = jnp.minimum(jnp.sum((te_p <= local[:, None]).astype(I32), axis=1), t - 1)
    oh_t = t_p[:, None] == jnp.arange(t, dtype=I32)[None, :]

    def pick(a):
        return jnp.sum(jnp.where(oh_t, a, 0.0), axis=1)

    src = t_p * R1B + (pick(st_p) + local - (pick(te_p) - pick(np_p))).astype(I32)
    src = jnp.concatenate([jnp.clip(src, 0, P_MAX - 1), jnp.zeros((G_PIECES,), I32)])

    nch = jnp.floor((n_e + (G_PIECES - 1)) * (1.0 / G_PIECES))
    ch_end = jnp.cumsum(nch)
    ch_off = jnp.concatenate([jnp.zeros((1,), F32), ch_end]).astype(I32)
    c = jnp.arange(-N_SLOTS, NCH + 2, dtype=F32)
    ce = jnp.minimum(jnp.sum((ch_end[None, :] <= c[:, None]).astype(I32), axis=1), e - 1)
    oh_c = (ce[:, None] == jnp.arange(e, dtype=I32)[None, :]).astype(F32)
    crow = jnp.dot(oh_c, jnp.stack([ch_end - nch, pe_off, n_e], axis=1), precision=hp)
    k_in = c - crow[:, 0]
    live = jnp.logical_and(c >= 0, c < ch_end[-1])
    cn = jnp.where(live, jnp.clip(crow[:, 2] - G_PIECES * k_in, 0, G_PIECES), 0.0)
    cs = jnp.where(cn > 0, crow[:, 1] + G_PIECES * k_in, 0.0)
    return src, ch_off, cs.astype(I32), cn.astype(I32)


def _ffn_kernel(src_ref, choff_ref, cs_ref, cn_ref, xs_in, wg_ref, wu_ref, wd_ref, xs_out,
                xbuf, ybuf, wgb, wub, wdb, gsem, ssem):
    e = pl.program_id(0)
    total = choff_ref[N_EXPERTS]

    def start_gather(ch):
        sl = lax.rem(ch + N_SLOTS, N_SLOTS)
        base = cs_ref[ch + N_SLOTS]
        n = cn_ref[ch + N_SLOTS]
        for i in range(G_PIECES):
            idx = src_ref[base + jnp.where(i < n, i, 0)]
            pltpu.make_async_copy(xs_in.at[idx], xbuf.at[sl, i], gsem.at[sl]).start(priority=i % 2)

    def wait_gather(ch):
        sl = lax.rem(ch + N_SLOTS, N_SLOTS)
        pltpu.make_async_copy(xs_in.at[pl.ds(0, G_PIECES)], xbuf.at[sl], gsem.at[sl]).wait()

    def start_scatter(ch):
        sl = lax.rem(ch + N_SLOTS, N_SLOTS)
        base = cs_ref[ch + N_SLOTS]
        n = cn_ref[ch + N_SLOTS]
        for i in range(G_PIECES):
            idx = jnp.where(i < n, src_ref[base + i], P_MAX + sl * G_PIECES + i)
            pltpu.make_async_copy(ybuf.at[sl, i], xs_out.at[idx], ssem.at[sl]).start(priority=i % 2)

    def wait_scatter(ch):
        sl = lax.rem(ch + N_SLOTS, N_SLOTS)
        pltpu.make_async_copy(ybuf.at[sl], xs_out.at[pl.ds(0, G_PIECES)], ssem.at[sl]).wait()

    @pl.when(e == 0)
    def _():
        ybuf[...] = jnp.zeros_like(ybuf)
        start_gather(0)
        start_gather(1)
        start_scatter(-3)
        start_scatter(-2)

    wgb[...] = wg_ref[0].astype(BF16)
    wub[...] = wu_ref[0].astype(BF16)
    wdb[...] = wd_ref[0].astype(BF16)

    def chunk(c, carry):
        sl = lax.rem(c, N_SLOTS)
        wait_gather(c)
        wait_scatter(c - 3)
        x = xbuf[sl].reshape(G_PIECES * PIECE, D)
        hid = (_silu(_dot(x, wgb[...])) * _dot(x, wub[...])).astype(BF16)
        start_gather(c + 2)
        start_scatter(c - 1)
        ybuf[sl] = _dot(hid, wdb[...]).astype(BF16).reshape(G_PIECES, PIECE, D)
        return carry

    lax.fori_loop(choff_ref[e], choff_ref[e + 1], chunk, 0)

    @pl.when(e == N_EXPERTS - 1)
    def _():
        start_scatter(total - 1)
        wait_gather(total)
        wait_gather(total + 1)
        wait_scatter(total - 3)
        wait_scatter(total - 2)
        wait_scatter(total - 1)


def _expert_ffn(xs, lists, layer, w_gate, w_up, w_down):
    src, ch_off, cs, cn = lists
    grid_spec = pltpu.PrefetchScalarGridSpec(
        num_scalar_prefetch=4,
        grid=(N_EXPERTS,),
        in_specs=[
            pl.BlockSpec(memory_space=pl.ANY),
            pl.BlockSpec((None, 1, D, D_EXPERT), lambda e, *_: (layer, e, 0, 0)),
            pl.BlockSpec((None, 1, D, D_EXPERT), lambda e, *_: (layer, e, 0, 0)),
            pl.BlockSpec((None, 1, D_EXPERT, D), lambda e, *_: (layer, e, 0, 0)),
        ],
        out_specs=pl.BlockSpec(memory_space=pl.ANY),
        scratch_shapes=[
            pltpu.VMEM((N_SLOTS, G_PIECES, PIECE, D), BF16),
            pltpu.VMEM((N_SLOTS, G_PIECES, PIECE, D), BF16),
            pltpu.VMEM((D, D_EXPERT), BF16),
            pltpu.VMEM((D, D_EXPERT), BF16),
            pltpu.VMEM((D_EXPERT, D), BF16),
            pltpu.SemaphoreType.DMA((N_SLOTS,)),
            pltpu.SemaphoreType.DMA((N_SLOTS,)),
        ],
    )
    out = pl.pallas_call(
        _ffn_kernel,
        grid_spec=grid_spec,
        out_shape=jax.ShapeDtypeStruct((P_TOT, PIECE, D), BF16),
        input_output_aliases={4: 0},
        compiler_params=_cparams(("arbitrary",)),
        name="moe_ffn",
    )(src, ch_off, cs, cn, xs.reshape(P_TOT, PIECE, D), w_gate, w_up, w_down)
    return out.reshape(P_TOT * PIECE, D)


def _combine_kernel(used_ref, ys_ref, pos_ref, wt_ref, x_ref, nw_ref, mod_ref, wsg_ref, wsu_ref, wsd_ref,
                    fw_ref, *rest, final):
    o_refs, acc_ref = rest[:-1], rest[-1]
    i = pl.program_id(0)
    x = x_ref[...]
    h = _norm_mod(x, nw_ref[...], mod_ref[0, 3:4, :], mod_ref[0, 4:5, :]).astype(BF16)
    shared = _dot((_silu(_dot(h, wsg_ref[...])) * _dot(h, wsu_ref[...])).astype(BF16), wsd_ref[...])

    pos16 = pos_ref[...].astype(jnp.int16)
    wt16 = wt_ref[...].astype(BF16)

    def block(r0, n):
        j16 = (lax.broadcasted_iota(I32, (MOE_TM, n), 1) + r0).astype(jnp.int16)
        wm = jnp.zeros((MOE_TM, n), BF16)
        for k in range(TOP_K):
            wm = jnp.where(j16 == pos16[:, k:k + 1], wt16[:, k:k + 1], wm)
        return _dot(wm, ys_ref[r0:r0 + n, :])

    acc_ref[...] = block(0, ROWS_MAIN)
    for r0 in range(ROWS_MAIN, R1, ROW_BLK):
        @pl.when(used_ref[i] > r0)
        def _(r0=r0):
            acc_ref[...] += block(r0, ROW_BLK)

    y = x + mod_ref[0, 5:6, :] * (acc_ref[...] + shared)
    if not final:
        o_refs[0][...] = y
        return
    ms = jnp.mean(y * y, axis=-1, keepdims=True)
    y = y * lax.rsqrt(ms + EPS) * fw_ref[...]
    is_prompt = i < NP // MOE_TM

    @pl.when(is_prompt)
    def _():
        o_refs[0][...] = y

    @pl.when(jnp.logical_not(is_prompt))
    def _():
        o_refs[1][...] = y


def _combine(ys, used, pos_t, wt_t, x, nw, mod, wsg, wsu, wsd, fw, final):
    n_p = NP // MOE_TM
    if final:
        out_specs = [pl.BlockSpec((MOE_TM, D), lambda i, u: (jnp.minimum(i, n_p - 1), 0)),
                     pl.BlockSpec((MOE_TM, D), lambda i, u: (jnp.maximum(i - n_p, 0), 0))]
        out_shape = [jax.ShapeDtypeStruct((NP, D), F32), jax.ShapeDtypeStruct((NS, D), F32)]
    else:
        out_specs = pl.BlockSpec((MOE_TM, D), lambda i, u: (i, 0))
        out_shape = jax.ShapeDtypeStruct((NT, D), F32)
    grid_spec = pltpu.PrefetchScalarGridSpec(
        num_scalar_prefetch=1,
        grid=(N_TILES,),
        in_specs=[
            pl.BlockSpec((R1, D), lambda i, u: (i, 0)),
            pl.BlockSpec((MOE_TM, SUBLANES), lambda i, u: (i, 0)),
            pl.BlockSpec((MOE_TM, SUBLANES), lambda i, u: (i, 0)),
            pl.BlockSpec((MOE_TM, D), lambda i, u: (i, 0)),
            pl.BlockSpec((1, D), lambda i, u: (0, 0)),
            pl.BlockSpec((1, 6, D), lambda i, u: (_mod_group(i, MOE_TM), 0, 0)),
            pl.BlockSpec((D, D_SHARED), lambda i, u: (0, 0)),
            pl.BlockSpec((D, D_SHARED), lambda i, u: (0, 0)),
            pl.BlockSpec((D_SHARED, D), lambda i, u: (0, 0)),
            pl.BlockSpec((1, D), lambda i, u: (0, 0)),
        ],
        out_specs=out_specs,
        scratch_shapes=[pltpu.VMEM((MOE_TM, D), F32)],
    )
    return pl.pallas_call(
        functools.partial(_combine_kernel, final=final),
        grid_spec=grid_spec,
        out_shape=out_shape,
        compiler_params=_cparams(("arbitrary",)),
        name="moe_combine",
    )(used, ys, pos_t, wt_t, x, nw.reshape(1, D), mod, wsg, wsu, wsd, fw.reshape(1, D))


def _moe(x, nw, mod, w_router, rbias, layer, w_gate, w_up, w_down, ws_gate, ws_up, ws_down, fw, final):
    wr = w_router.astype(F32).T
    wr_hi = wr.astype(BF16)
    wr_lo = (wr - wr_hi.astype(F32)).astype(BF16)
    xs, pos, wt, npc = _route(x, nw, mod, wr_hi, wr_lo, rbias.astype(F32))
    npc = npc[:, :, 0]
    lists = _piece_lists(npc)
    ys = _expert_ffn(xs, lists, layer, w_gate, w_up, w_down)
    used = (npc.sum(axis=1) * PIECE).astype(I32)
    return _combine(ys, used, pos.T, wt.T, x, nw, mod, ws_gate.astype(BF16), ws_up.astype(BF16),
                    ws_down.astype(BF16), fw, final)


def kernel(x_prompt, x_sample, state_hgrn_fwd, state_hgrn_bwd, cache_na_k, cache_na_v, c, c_ctx,
           norm1_w, norm2_w, ada_w, ada_b, hgrn_w_in, hgrn_lb_logits, hgrn_gn_w, hgrn_w_out,
           na_w_qkv, na_rpb, na_w_o, moe_w_router, moe_router_bias, moe_w_gate, moe_w_up, moe_w_down,
           shared_w_gate, shared_w_up, shared_w_down, final_norm_w):
    x = (x_prompt.reshape(NP, D), x_sample.reshape(NS, D))
    cvec = jnp.concatenate([c_ctx[None, :], c, jnp.zeros((N_MOD - 1 - DEC_BATCH, D), F32)], axis=0)
    mod = _modulation(cvec, ada_w, ada_b)
    lb_table = jnp.cumsum(jax.nn.softmax(hgrn_lb_logits.astype(F32), axis=0), axis=0)
    hk = HG_HEADS * HG_DK

    sf = sb = k_c = v_c = None
    for l in range(DEPTH):
        if l % 2 == 0:
            a = l // 2
            proj = _norm_proj(x, norm1_w[l], mod[l], hgrn_w_in[a].astype(BF16))
            o_f, o_b, sf, sb = _gla(proj, lb_table[l],
                                    state_hgrn_fwd[:, a].reshape(DEC_BATCH, hk, HG_DV),
                                    state_hgrn_bwd[:, a].reshape(DEC_BATCH, hk, HG_DV))
            x = _hgrn_out(o_f, o_b, proj, x, mod[l], hgrn_gn_w[a], hgrn_w_out[a].astype(BF16))
        else:
            n = l // 2
            qkv, k_p, v_p = _norm_proj(x, norm1_w[l], mod[l], na_w_qkv[n].astype(BF16), n_copy=2,
                                        out_dtype=BF16)
            att_p = _attn_ctx(qkv)
            att_s = _attn_lat(qkv, cache_na_k[:, n].reshape(DEC_BATCH, PAST_LEN, D),
                              cache_na_v[:, n].reshape(DEC_BATCH, PAST_LEN, D),
                              *_latent_bias_tables(na_rpb[n]))
            x = _attn_out(att_p, att_s, x, mod[l], na_w_o[n].astype(BF16))
            k_c = k_p.reshape(BATCH, SEQ, NA_HEADS, NA_HD)
            v_c = v_p.reshape(BATCH, SEQ, NA_HEADS, NA_HD)
        x = _moe(x, norm2_w[l], mod[l], moe_w_router[l], moe_router_bias[l], l, moe_w_gate, moe_w_up,
                 moe_w_down, shared_w_gate[l], shared_w_up[l], shared_w_down[l], final_norm_w,
                 final=(l == DEPTH - 1))

    y_prompt = x[0].reshape(BATCH, SEQ, D)
    y_sample = x[1].reshape(DEC_BATCH, DEC_SEQ, D)
    new_sf = sf.reshape(BATCH, 1, HG_HEADS, HG_DK, HG_DV)
    new_sb = sb.reshape(BATCH, 1, HG_HEADS, HG_DK, HG_DV)
    return (y_prompt, y_sample, new_sf, new_sb, k_c[:, None], v_c[:, None])
```

```python
import functools

import numpy as np
import jax
import jax.numpy as jnp
from jax import lax
from jax.experimental import pallas as pl
from jax.experimental.pallas import tpu as pltpu

F32 = jnp.float32
BF16 = jnp.bfloat16
I32 = jnp.int32

D = 1024
BATCH = 32
SEQ = 256
DEPTH = 2
DEC_BATCH = 8
DEC_SEQ = 1024
PAST_LEN = 512
GRID_W = 64
HG_HEADS = 8
HG_DK = 128
HG_DV = 128
CHUNK = 64
NA_HEADS = 16
NA_HD = 64
WIN_R = 8
WIN_C = 16
N_EXPERTS = 64
TOP_K = 6
N_GROUPS = 8
TOPK_GROUPS = 4
D_EXPERT = 256
D_SHARED = 256
ROUTED_SCALE = 2.5
EPS = 1e-6

NP = BATCH * SEQ
NS = DEC_BATCH * DEC_SEQ
NT = NP + NS
N_MOD = 16
STRIP_W = 2048
assert (2 * (DEC_SEQ // GRID_W) - 1) * GRID_W <= STRIP_W
NEG = -1e30

LANES = 128
SUBLANES = 8
BF16_ROWS = 16
VMEM_LIMIT = 56 * 1024 * 1024

TM = 256
SUB = 16
GLA_FAST_MAX = 60.0
MOE_TM = 256
PIECE = BF16_ROWS
R1 = MOE_TM * TOP_K + N_EXPERTS * (PIECE - 1) + 64
assert R1 % PIECE == 0 and R1 % LANES == 0
R1B = R1 // PIECE
ROWS_MAIN = 2048
ROW_BLK = 256
assert (R1 - ROWS_MAIN) % ROW_BLK == 0
N_TILES = NT // MOE_TM
P_MAX = N_TILES * R1B
G_PIECES = 32
NCH = P_MAX // G_PIECES + N_EXPERTS
N_SLOTS = 3
P_TOT = P_MAX + R1B
assert N_SLOTS * G_PIECES <= R1B


def _cparams(sem):
    return pltpu.CompilerParams(dimension_semantics=sem, vmem_limit_bytes=VMEM_LIMIT)


def _dot(a, b):
    return jnp.dot(a, b, preferred_element_type=F32)


def _dot_nt(a, b):
    return lax.dot_general(a, b, (((1,), (1,)), ((), ())), preferred_element_type=F32)


def _dot_tn(a, b):
    return lax.dot_general(a, b, (((0,), (0,)), ((), ())), preferred_element_type=F32)


def _silu(x):
    return x * jax.nn.sigmoid(x)


def _mod_group(i, tm):
    r = i * tm
    return jnp.where(r < NP, 0, 1 + (r - NP) // DEC_SEQ)


def _norm_mod(x, nw, shift, scale):
    ms = jnp.mean(x * x, axis=-1, keepdims=True)
    y = x * lax.rsqrt(ms + EPS) * nw
    return y * (1.0 + scale) + shift


def _mod_kernel(c_ref, w_ref, b_ref, o_ref):
    s = _silu(c_ref[...]).astype(BF16)
    o_ref[0] = _dot(s, w_ref[0].astype(BF16)) + b_ref[0]


def _modulation(cvec, ada_w, ada_b):
    cw = 1536
    n = ada_w.shape[-1]
    out = pl.pallas_call(
        _mod_kernel,
        grid=(DEPTH, n // cw),
        in_specs=[
            pl.BlockSpec((N_MOD, D), lambda l, j: (0, 0)),
            pl.BlockSpec((1, D, cw), lambda l, j: (l, 0, j)),
            pl.BlockSpec((1, 1, cw), lambda l, j: (l, 0, j)),
        ],
        out_specs=pl.BlockSpec((1, N_MOD, cw), lambda l, j: (l, 0, j)),
        out_shape=jax.ShapeDtypeStruct((DEPTH, N_MOD, n), F32),
        compiler_params=_cparams(("arbitrary", "arbitrary")),
        name="modulation",
    )(cvec, ada_w, ada_b.reshape(DEPTH, 1, n))
    return out.reshape(DEPTH, N_MOD, 6, D)


def _stream_specs(x, tm):
    if not isinstance(x, tuple):
        return [pl.BlockSpec((tm, D), lambda i: (i, 0))], (x,)
    n_p = NP // tm
    return [pl.BlockSpec((tm, D), lambda i: (jnp.minimum(i, n_p - 1), 0)),
            pl.BlockSpec((tm, D), lambda i: (jnp.maximum(i - n_p, 0), 0))], x


def _stream_tile(refs, tm):
    if len(refs) == 1:
        return refs[0][...]
    return jnp.where(pl.program_id(0) < NP // tm, refs[0][...], refs[1][...])


def _proj_kernel(x_ref, nw_ref, mod_ref, w_ref, o_ref, *copy_refs, cw):
    h = _norm_mod(x_ref[...], nw_ref[...], mod_ref[0, 0:1, :], mod_ref[0, 1:2, :]).astype(BF16)
    for j in range(w_ref.shape[1] // cw):
        val = _dot(h, w_ref[:, j * cw:(j + 1) * cw])
        o_ref[:, j * cw:(j + 1) * cw] = val.astype(o_ref.dtype)
        col = j * cw - D
        if 0 <= col < len(copy_refs) * D:
            copy_refs[col // D][:, col % D:col % D + cw] = val


def _norm_proj(x, nw, mod, w, tile0, n_tiles, n_copy=0, out_dtype=F32):
    n = w.shape[1]
    rows = n_tiles * TM
    outs = pl.pallas_call(
        functools.partial(_proj_kernel, cw=512),
        grid=(n_tiles,),
        in_specs=[
            pl.BlockSpec((TM, D), lambda i: (tile0 + i, 0)),
            pl.BlockSpec((1, D), lambda i: (0, 0)),
            pl.BlockSpec((1, 6, D), lambda i: (_mod_group(tile0 + i, TM), 0, 0)),
            pl.BlockSpec((D, n), lambda i: (0, 0)),
        ],
        out_specs=[pl.BlockSpec((TM, n), lambda i: (i, 0))] + [pl.BlockSpec((TM, D), lambda i: (i, 0))] * n_copy,
        out_shape=[jax.ShapeDtypeStruct((rows, n), out_dtype)] + [jax.ShapeDtypeStruct((rows, D), F32)] * n_copy,
        compiler_params=_cparams(("arbitrary",)),
        name="norm_proj",
    )(x, nw.reshape(1, D), mod, w)
    return outs if n_copy else outs[0]


def _hgrn_proj_kernel(*refs, cw, n_x):
    nw_ref, mod_ref, lb_ref, w_ref, o_ref, tot_ref = refs[n_x:]
    x = _stream_tile(refs[:n_x], TM)
    h = _norm_mod(x, nw_ref[...], mod_ref[0, 0:1, :], mod_ref[0, 1:2, :]).astype(BF16)
    n_ck = TM // CHUNK
    mins = {}
    for j in range(w_ref.shape[1] // cw):
        val = _dot(h, w_ref[:, j * cw:(j + 1) * cw])
        sec, col = divmod(j * cw, D)
        if sec in (1, 2):
            lb = lb_ref[:, col:col + cw]
            val = jnp.log(lb + (1.0 - lb) * jax.nn.sigmoid(val))
            for c in range(n_ck):
                tot = jnp.sum(val[c * CHUNK:(c + 1) * CHUNK], axis=0, keepdims=True)
                m = jnp.min(tot, axis=-1, keepdims=True)
                key = (sec - 1, c)
                mins[key] = m if key not in mins else jnp.minimum(mins[key], m)
        o_ref[:, j * cw:(j + 1) * cw] = val
    tot_ref[0] = jnp.concatenate(
        [jnp.broadcast_to(mins[(d, c)], (1, LANES)) for c in range(n_ck) for d in range(2)], axis=0)


def _hgrn_proj(x, nw, mod, lb, w):
    n = w.shape[1]
    n_ck = TM // CHUNK
    assert 2 * n_ck == SUBLANES
    x_specs, xs = _stream_specs(x, TM)
    proj, tot = pl.pallas_call(
        functools.partial(_hgrn_proj_kernel, cw=512, n_x=len(xs)),
        grid=(NT // TM,),
        in_specs=x_specs + [
            pl.BlockSpec((1, D), lambda i: (0, 0)),
            pl.BlockSpec((1, 6, D), lambda i: (_mod_group(i, TM), 0, 0)),
            pl.BlockSpec((1, HG_HEADS * HG_DK), lambda i: (0, 0)),
            pl.BlockSpec((D, n), lambda i: (0, 0)),
        ],
        out_specs=[pl.BlockSpec((TM, n), lambda i: (i, 0)),
                   pl.BlockSpec((1, SUBLANES, LANES), lambda i: (i, 0, 0))],
        out_shape=[jax.ShapeDtypeStruct((NT, n), F32),
                   jax.ShapeDtypeStruct((NT // TM, SUBLANES, LANES), F32)],
        compiler_params=_cparams(("arbitrary",)),
        name="hgrn_proj",
    )(*xs, nw.reshape(1, D), mod, lb.reshape(1, HG_HEADS * HG_DK), w)
    return proj, tot[:, :, 0].reshape(NT // CHUNK, 2)


def _split3(x):
    hi = x.astype(BF16)
    r = x - hi.astype(F32)
    mid = r.astype(BF16)
    lo = (r - mid.astype(F32)).astype(BF16)
    return hi, mid, lo


def _gla_safe(q, kk, v, g3, st_ref, row_refs, rev):
    bs_ref, ks_ref, kd8_ref = row_refs

    n_stack = 4
    t_io = lax.broadcasted_iota(I32, (n_stack * CHUNK, CHUNK), 0)
    u_io = lax.broadcasted_iota(I32, (n_stack * CHUNK, CHUNK), 1)
    which = t_io // CHUNK
    tt = t_io - which * CHUNK
    b16 = (tt // SUB) * SUB
    b8 = (tt // SUBLANES) * SUBLANES
    if rev:
        lim = jnp.where(which == 0, tt, jnp.where(which == 1, b16 + SUB, jnp.where(which == 2, b8 + SUBLANES, b8)))
        pick = u_io >= lim
    else:
        lim = jnp.where(which == 0, tt, jnp.where(which == 1, b16 - 1,
                                                  jnp.where(which == 2, b8 - 1, b8 + SUBLANES - 1)))
        pick = u_io <= lim
    tri = jnp.where(pick, 1.0, 0.0).astype(BF16)
    hi, mid, lo = g3
    cs = _dot(tri, hi) + _dot(tri, mid) + _dot(tri, lo)
    b, r16, r8, e8 = (cs[j * CHUNK:(j + 1) * CHUNK] for j in range(n_stack))
    last = 0 if rev else CHUNK - 1
    tot = b[last:last + 1]

    qe = (q * jnp.exp(b)).astype(BF16)
    qd = (q * jnp.exp(b - r16)).astype(BF16)
    qd8 = q * jnp.exp(b - r8)
    kdec = (kk * jnp.exp(tot - b)).astype(BF16)
    vb = v.astype(BF16)
    dec_tot = jnp.exp(tot)
    kd8 = kk * jnp.exp(e8 - b)
    for h in range(HG_HEADS):
        sl = slice(h * LANES, (h + 1) * LANES)
        bs_ref[h] = b[:, sl]
        ks_ref[h] = kk[:, sl]
        kd8_ref[h] = kd8[:, sl]

    def row(ref, h, s):
        return jnp.broadcast_to(ref[h, s:s + 1, :], (SUBLANES, LANES))

    lane = lax.broadcasted_iota(I32, (SUBLANES, LANES), 1)
    row8 = lax.broadcasted_iota(I32, (SUBLANES, LANES), 0)
    a_in = [[] for _ in range(HG_HEADS)]
    for tb in range(CHUNK // SUBLANES):
        t0 = tb * SUBLANES
        blk0 = (t0 // SUB) * SUB
        other = blk0 + SUBLANES if t0 == blk0 else blk0
        keep = (row8 + t0 <= lane) if rev else (row8 + t0 >= lane)
        for h in range(HG_HEADS):
            sl = slice(h * LANES, (h + 1) * LANES)
            q_t = q[t0:t0 + SUBLANES, sl]
            b_t = b[t0:t0 + SUBLANES, sl]
            acc = jnp.zeros((SUBLANES, LANES), F32)
            for s in range(t0, t0 + SUBLANES):
                p = q_t * jnp.exp(b_t - row(bs_ref, h, s)) * row(ks_ref, h, s)
                acc = jnp.where(lane == s, jnp.sum(p, axis=-1, keepdims=True), acc)
            acc = jnp.where(keep, acc, 0.0)
            if (t0 == blk0) == rev:
                qd8_t = qd8[t0:t0 + SUBLANES, sl]
                for s in range(other, other + SUBLANES):
                    p = qd8_t * row(kd8_ref, h, s)
                    acc = jnp.where(lane == s, jnp.sum(p, axis=-1, keepdims=True), acc)
            a_in[h].append(acc)

    n_blk = CHUNK // SUB
    o_heads = []
    for h in range(HG_HEADS):
        sl = slice(h * LANES, (h + 1) * LANES)
        st = st_ref[sl, :]
        o_h = _dot_nt(qe[:, sl], st.astype(BF16))
        rows = []
        for i in range(n_blk):
            lo_r, hi_r = ((i + 1) * SUB, CHUNK) if rev else (0, i * SUB)
            if hi_r == lo_r:
                rows.append(jnp.zeros((SUB, CHUNK), F32))
                continue
            r_i = r16[i * SUB:i * SUB + 1, sl]
            kd = kk[lo_r:hi_r, sl] * jnp.exp(r_i - b[lo_r:hi_r, sl])
            pads = [jnp.zeros((lo_r, LANES), F32)] if lo_r else []
            pads_hi = [jnp.zeros((CHUNK - hi_r, LANES), F32)] if hi_r < CHUNK else []
            kd = jnp.concatenate(pads + [kd] + pads_hi, axis=0).astype(BF16)
            rows.append(_dot_nt(qd[i * SUB:(i + 1) * SUB, sl], kd))
        a1 = jnp.concatenate(rows, axis=0) + jnp.concatenate(a_in[h], axis=0)[:, :CHUNK]
        o_h = o_h + _dot(a1.astype(BF16), vb[:, sl])
        o_heads.append(o_h)
        st_ref[sl, :] = st * dec_tot[:, sl] + _dot_tn(vb[:, sl], kdec[:, sl])
    return jnp.concatenate(o_heads, axis=1)


def _gla_fast(q, kk, v, b, tot, st_ref, rev):
    qe = (q * jnp.exp(b)).astype(BF16)
    ke32 = kk * jnp.exp(-b)
    ke = ke32.astype(BF16)
    dec_tot = jnp.exp(tot)
    kdec = (ke32 * dec_tot).astype(BF16)
    vb = v.astype(BF16)
    t_io = lax.broadcasted_iota(I32, (CHUNK, CHUNK), 0)
    s_io = lax.broadcasted_iota(I32, (CHUNK, CHUNK), 1)
    keep = (s_io >= t_io) if rev else (s_io <= t_io)
    o_heads = []
    for h in range(HG_HEADS):
        sl = slice(h * LANES, (h + 1) * LANES)
        st = st_ref[sl, :]
        a = jnp.where(keep, _dot_nt(qe[:, sl], ke[:, sl]), 0.0).astype(BF16)
        o_heads.append(_dot_nt(qe[:, sl], st.astype(BF16)) + _dot(a, vb[:, sl]))
        st_ref[sl, :] = st * dec_tot[:, sl] + _dot_tn(vb[:, sl], kdec[:, sl])
    return jnp.concatenate(o_heads, axis=1)


def _gla_prep(g, rev):
    kk = 1.0 - jnp.exp(g)
    g3 = _split3(g)
    t_io = lax.broadcasted_iota(I32, (CHUNK, CHUNK), 0)
    u_io = lax.broadcasted_iota(I32, (CHUNK, CHUNK), 1)
    tri = jnp.where((u_io >= t_io) if rev else (u_io <= t_io), 1.0, 0.0).astype(BF16)
    b = _dot(tri, g3[0]) + _dot(tri, g3[1]) + _dot(tri, g3[2])
    last = 0 if rev else CHUNK - 1
    return kk, g3, b, b[last:last + 1]


def _gla_kernel(mild_ref, qf_ref, gf_ref, vf_ref, qb_ref, gb_ref, vb_ref, s0f_ref, s0b_ref,
                of_ref, ob_ref, sf_ref, sb_ref, stf, stb, bs, ks, kd8):
    i = pl.program_id(0)
    n_p = NP // CHUNK
    is_prompt = i < n_p
    c = jnp.where(is_prompt, i % (SEQ // CHUNK), (i - n_p) % (DEC_SEQ // CHUNK))
    n_c = jnp.where(is_prompt, SEQ // CHUNK, DEC_SEQ // CHUNK)

    @pl.when(jnp.logical_and(c == 0, is_prompt))
    def _():
        stf[...] = jnp.zeros_like(stf)
        stb[...] = jnp.zeros_like(stb)

    @pl.when(jnp.logical_and(c == 0, jnp.logical_not(is_prompt)))
    def _():
        for h in range(HG_HEADS):
            sl = slice(h * LANES, (h + 1) * LANES)
            stf[sl, :] = s0f_ref[0, sl, :].T
            stb[sl, :] = s0b_ref[0, sl, :].T

    rows = (bs, ks, kd8)
    kk_f, g3_f, b_f, tot_f = _gla_prep(gf_ref[...], False)
    kk_b, g3_b, b_b, tot_b = _gla_prep(gb_ref[...], True)
    mild = mild_ref[i] != 0

    @pl.when(mild)
    def _():
        of_ref[...] = _gla_fast(qf_ref[...], kk_f, vf_ref[...], b_f, tot_f, stf, False)
        ob_ref[...] = _gla_fast(qb_ref[...], kk_b, vb_ref[...], b_b, tot_b, stb, True)

    @pl.when(jnp.logical_not(mild))
    def _():
        of_ref[...] = _gla_safe(qf_ref[...], kk_f, vf_ref[...], g3_f, stf, rows, False)
        ob_ref[...] = _gla_safe(qb_ref[...], kk_b, vb_ref[...], g3_b, stb, rows, True)

    @pl.when(jnp.logical_and(c == n_c - 1, is_prompt))
    def _():
        for h in range(HG_HEADS):
            sl = slice(h * LANES, (h + 1) * LANES)
            sf_ref[0, sl, :] = stf[sl, :].T
            sb_ref[0, sl, :] = stb[sl, :].T


def _gla(proj, chunk_tot, s0f, s0b):
    n_p = NP // CHUNK
    cp = SEQ // CHUNK
    cs = DEC_SEQ // CHUNK

    def bwd_blk(i):
        jp = (i // cp) * cp + (cp - 1 - i % cp)
        j = i - n_p
        js = n_p + (j // cs) * cs + (cs - 1 - j % cs)
        return jnp.where(i < n_p, jp, js)

    def req(i):
        return jnp.maximum(i - n_p, 0) // cs

    def preq(i):
        return jnp.minimum(i // cp, BATCH - 1)

    steps = jnp.arange(NT // CHUNK, dtype=I32)
    mild = jnp.logical_and(chunk_tot[:, 0] >= -GLA_FAST_MAX,
                           chunk_tot[bwd_blk(steps), 1] >= -GLA_FAST_MAX).astype(I32)
    hk = HG_HEADS * HG_DK
    grid_spec = pltpu.PrefetchScalarGridSpec(
        num_scalar_prefetch=1,
        grid=(NT // CHUNK,),
        in_specs=[
            pl.BlockSpec((CHUNK, D), lambda i, m: (i, 0)),
            pl.BlockSpec((CHUNK, D), lambda i, m: (i, 1)),
            pl.BlockSpec((CHUNK, D), lambda i, m: (i, 3)),
            pl.BlockSpec((CHUNK, D), lambda i, m: (bwd_blk(i), 0)),
            pl.BlockSpec((CHUNK, D), lambda i, m: (bwd_blk(i), 2)),
            pl.BlockSpec((CHUNK, D), lambda i, m: (bwd_blk(i), 3)),
            pl.BlockSpec((1, hk, HG_DV), lambda i, m: (req(i), 0, 0)),
            pl.BlockSpec((1, hk, HG_DV), lambda i, m: (req(i), 0, 0)),
        ],
        out_specs=[
            pl.BlockSpec((CHUNK, D), lambda i, m: (i, 0)),
            pl.BlockSpec((CHUNK, D), lambda i, m: (bwd_blk(i), 0)),
            pl.BlockSpec((1, hk, HG_DV), lambda i, m: (preq(i), 0, 0)),
            pl.BlockSpec((1, hk, HG_DV), lambda i, m: (preq(i), 0, 0)),
        ],
        scratch_shapes=[pltpu.VMEM((hk, HG_DV), F32), pltpu.VMEM((hk, HG_DV), F32)]
        + [pltpu.VMEM((HG_HEADS, CHUNK, HG_DK), F32)] * 3,
    )
    return pl.pallas_call(
        _gla_kernel,
        grid_spec=grid_spec,
        out_shape=[
            jax.ShapeDtypeStruct((NT, D), F32),
            jax.ShapeDtypeStruct((NT, D), F32),
            jax.ShapeDtypeStruct((BATCH, hk, HG_DV), F32),
            jax.ShapeDtypeStruct((BATCH, hk, HG_DV), F32),
        ],
        compiler_params=_cparams(("arbitrary",)),
        name="gla_scan",
    )(mild, proj, proj, proj, proj, proj, proj, s0f, s0b)


def _hgrn_out_kernel(of_ref, ob_ref, gate_ref, mod_ref, gn_ref, w_ref, *refs):
    o_ref = refs[-1]
    x = _stream_tile(refs[:-1], TM)
    o = of_ref[...] + ob_ref[...]
    gn = gn_ref[...]
    segs = []
    for h in range(HG_HEADS):
        seg = o[:, h * LANES:(h + 1) * LANES]
        ms = jnp.mean(seg * seg, axis=-1, keepdims=True)
        segs.append(seg * lax.rsqrt(ms + EPS) * gn)
    y = (jnp.concatenate(segs, axis=1) * _silu(gate_ref[...])).astype(BF16)
    o_ref[...] = x + mod_ref[0, 2:3, :] * _dot(y, w_ref[...])


def _hgrn_out(o_f, o_b, proj, x, mod, gn_w, w_out):
    x_specs, xs = _stream_specs(x, TM)
    return pl.pallas_call(
        _hgrn_out_kernel,
        grid=(NT // TM,),
        in_specs=[
            pl.BlockSpec((TM, D), lambda i: (i, 0)),
            pl.BlockSpec((TM, D), lambda i: (i, 0)),
            pl.BlockSpec((TM, D), lambda i: (i, 4)),
            pl.BlockSpec((1, 6, D), lambda i: (_mod_group(i, TM), 0, 0)),
            pl.BlockSpec((1, HG_DV), lambda i: (0, 0)),
            pl.BlockSpec((D, D), lambda i: (0, 0)),
        ] + x_specs,
        out_specs=pl.BlockSpec((TM, D), lambda i: (i, 0)),
        out_shape=jax.ShapeDtypeStruct((NT, D), F32),
        compiler_params=_cparams(("arbitrary",)),
        name="hgrn_out",
    )(o_f, o_b, proj, mod, gn_w.reshape(1, HG_DV), w_out, *xs)


def _attn_out_kernel(ap_ref, as_ref, x_ref, mod_ref, w_ref, o_ref):
    a = jnp.where(pl.program_id(0) < NP // TM, ap_ref[...], as_ref[...]).astype(BF16)
    o_ref[...] = x_ref[...] + mod_ref[0, 2:3, :] * _dot(a, w_ref[...])


def _attn_out(a_p, a_s, x, mod, w_o):
    n_p = NP // TM
    return pl.pallas_call(
        _attn_out_kernel,
        grid=(NT // TM,),
        in_specs=[
            pl.BlockSpec((TM, D), lambda i: (jnp.minimum(i, n_p - 1), 0)),
            pl.BlockSpec((TM, D), lambda i: (jnp.maximum(i - n_p, 0), 0)),
            pl.BlockSpec((TM, D), lambda i: (i, 0)),
            pl.BlockSpec((1, 6, D), lambda i: (_mod_group(i, TM), 0, 0)),
            pl.BlockSpec((D, D), lambda i: (0, 0)),
        ],
        out_specs=pl.BlockSpec((TM, D), lambda i: (i, 0)),
        out_shape=jax.ShapeDtypeStruct((NT, D), F32),
        compiler_params=_cparams(("arbitrary",)),
        name="attn_out",
    )(a_p, a_s, x, mod, w_o)


def _head_pair_masks():
    upper = lax.broadcasted_iota(I32, (1, LANES), 1) >= NA_HD
    return upper


def _attn_ctx_kernel(q_ref, k_ref, v_ref, o_ref):
    upper = _head_pair_masks()
    for p in range(NA_HEADS // 2):
        sl = slice(p * LANES, (p + 1) * LANES)
        qp = q_ref[:, sl] * (NA_HD ** -0.5)
        kp = k_ref[:, sl].astype(BF16)
        vp = v_ref[:, sl].astype(BF16)
        outs = []
        for u in range(2):
            qm = jnp.where(upper if u else jnp.logical_not(upper), qp, 0.0).astype(BF16)
            s = _dot_nt(qm, kp)
            e = jnp.exp(s - jnp.max(s, axis=-1, keepdims=True))
            outs.append(_dot(e.astype(BF16), vp) / jnp.sum(e, axis=-1, keepdims=True))
        o_ref[:, sl] = jnp.where(upper, outs[1], outs[0]).astype(o_ref.dtype)


def _attn_ctx(qkv):
    return pl.pallas_call(
        _attn_ctx_kernel,
        grid=(BATCH,),
        in_specs=[
            pl.BlockSpec((SEQ, D), lambda b: (b, 0)),
            pl.BlockSpec((SEQ, D), lambda b: (b, 1)),
            pl.BlockSpec((SEQ, D), lambda b: (b, 2)),
        ],
        out_specs=pl.BlockSpec((SEQ, D), lambda b: (b, 0)),
        out_shape=jax.ShapeDtypeStruct((NP, D), BF16),
        compiler_params=_cparams(("arbitrary",)),
        name="attn_ctx",
    )(qkv, qkv, qkv)


def _attn_lat_kernel(q_ref, k_ref, v_ref, kc_ref, vc_ref, strip_ref, rowmask_ref, o_ref, *, tq):
    rows = DEC_SEQ // GRID_W
    kr = min(WIN_R, rows)
    rpt = tq // GRID_W

    def bias_rows(u, r, k_lo, k_hi):
        first = rows - 1 - r + k_lo
        var = first % 2
        strip = strip_ref[u, var, :, (first - var) * GRID_W:(first - var + k_hi - k_lo) * GRID_W]
        return strip + rowmask_ref[r:r + 1, k_lo * GRID_W:k_hi * GRID_W]

    upper = _head_pair_masks()
    mine = (jnp.logical_not(upper), upper)
    kl = k_ref[...].astype(BF16)
    kc = kc_ref[0].astype(BF16)
    vl = [jnp.where(mine[u], v_ref[...], 1.0).astype(BF16) for u in range(2)]
    vc = [jnp.where(mine[u], vc_ref[0], 1.0).astype(BF16) for u in range(2)]
    for t in range(DEC_SEQ // tq):
        rs = slice(t * tq, (t + 1) * tq)
        k_lo = min(max(t * rpt - kr // 2, 0), rows - kr) // 2 * 2
        k_hi = -(-(min(max(t * rpt + rpt - 1 - kr // 2, 0), rows - kr) + kr) // 2) * 2
        ks = slice(k_lo * GRID_W, k_hi * GRID_W)
        qp = q_ref[rs, :] * (NA_HD ** -0.5)
        outs = []
        for u in range(2):
            qm = jnp.where(mine[u], qp, 0.0).astype(BF16)
            bias = jnp.concatenate([bias_rows(u, t * rpt + a, k_lo, k_hi) for a in range(rpt)], axis=0)
            s_l = _dot_nt(qm, kl[ks]) + bias
            s_c = _dot_nt(qm, kc)
            m = jnp.maximum(jnp.max(s_l, axis=-1, keepdims=True), jnp.max(s_c, axis=-1, keepdims=True))
            acc = (_dot(jnp.exp(s_l - m).astype(BF16), vl[u][ks])
                   + _dot(jnp.exp(s_c - m).astype(BF16), vc[u]))
            outs.append(acc / pltpu.roll(acc, NA_HD, 1))
        o_ref[rs, :] = jnp.where(upper, outs[1], outs[0]).astype(o_ref.dtype)


def _attn_lat(qkv, k_ctx, v_ctx, strips, rowmask):
    npair = NA_HEADS // 2
    return pl.pallas_call(
        functools.partial(_attn_lat_kernel, tq=256),
        grid=(npair, DEC_BATCH),
        in_specs=[
            pl.BlockSpec((DEC_SEQ, LANES), lambda p, b: (b, p)),
            pl.BlockSpec((DEC_SEQ, LANES), lambda p, b: (b, npair + p)),
            pl.BlockSpec((DEC_SEQ, LANES), lambda p, b: (b, 2 * npair + p)),
            pl.BlockSpec((1, PAST_LEN, LANES), lambda p, b: (b, 0, p)),
            pl.BlockSpec((1, PAST_LEN, LANES), lambda p, b: (b, 0, p)),
            pl.BlockSpec((2, 2, GRID_W, STRIP_W), lambda p, b: (p, 0, 0, 0)),
            pl.BlockSpec((DEC_SEQ // GRID_W, DEC_SEQ), lambda p, b: (0, 0)),
        ],
        out_specs=pl.BlockSpec((DEC_SEQ, LANES), lambda p, b: (b, p)),
        out_shape=jax.ShapeDtypeStruct((NS, D), BF16),
        compiler_params=_cparams(("arbitrary", "arbitrary")),
        name="attn_lat",
    )(qkv, qkv, qkv, k_ctx, v_ctx, strips, rowmask)


def _latent_bias_tables(rpb):
    rows = DEC_SEQ // GRID_W
    kr = min(WIN_R, rows)
    ndr, ndc = 2 * WIN_R - 1, 2 * WIN_C - 1
    qc = np.arange(GRID_W)
    kc = np.arange(GRID_W)
    ws = np.clip(qc - WIN_C // 2, 0, GRID_W - WIN_C)
    col_ok = (kc[None, :] >= ws[:, None]) & (kc[None, :] < ws[:, None] + WIN_C)
    dc = np.clip(kc[None, :] - qc[:, None] + WIN_C - 1, 0, ndc - 1)
    onehot = (dc[None, :, :] == np.arange(ndc)[:, None, None]).astype(np.float32)
    t2 = jnp.einsum('hrc,cqk->hqrk', rpb.astype(F32), jnp.asarray(onehot),
                    precision=lax.Precision.HIGHEST)
    t2 = jnp.where(jnp.asarray(col_ok)[None, :, None, :], t2, NEG)
    lead = rows - WIN_R
    n_tiles = 2 * rows - 1

    def neg(n):
        return jnp.full((NA_HEADS, GRID_W, n, GRID_W), NEG, F32)

    strip = jnp.concatenate([neg(lead), t2, neg(n_tiles - lead - ndr)], axis=2)
    strip = strip.reshape(NA_HEADS, GRID_W, n_tiles * GRID_W)

    def pad(a):
        return jnp.pad(a, ((0, 0), (0, 0), (0, STRIP_W - a.shape[-1])), constant_values=NEG)

    strips = jnp.stack([pad(strip), pad(strip[:, :, GRID_W:])], axis=1)
    r = np.arange(rows)
    k0 = np.clip(r - kr // 2, 0, rows - kr)
    krow = np.arange(DEC_SEQ) // GRID_W
    row_ok = (krow[None, :] >= k0[:, None]) & (krow[None, :] < k0[:, None] + kr)
    rowmask = jnp.asarray(np.where(row_ok, 0.0, NEG).astype(np.float32))
    return strips, rowmask


def _route_kernel(x_ref, nw_ref, mod_ref, wrh_ref, wrl_ref, rb_ref, xs_ref, pos_ref, wt_ref, npc_ref,
                  hb_s, pos_s, used_s):
    i = pl.program_id(0)
    cur = lax.rem(i, 2)

    @pl.when(i == 0)
    def _():
        hb_s[...] = jnp.zeros_like(hb_s)
        pos_s[...] = jnp.full(pos_s.shape, -1, I32)
        used_s[0] = 0.0
        used_s[1] = 0.0

    live = i < N_TILES

    def step(slot):
        hb, pos, used = _route_select(x_ref, nw_ref, mod_ref, wrh_ref, wrl_ref, rb_ref, pos_ref, wt_ref,
                                      npc_ref)
        hb_s[slot] = hb
        pos_s[slot] = jnp.where(live, pos, -1)
        used_s[slot] = jnp.where(live, used, 0.0)
        _route_sort(xs_ref, hb_s[1 - slot], pos_s[1 - slot], used_s[1 - slot])

    for slot in range(2):
        pl.when(cur == slot)(functools.partial(step, slot))


def _route_sort(xs_ref, hb, pos, used):
    pos16 = [pos[k:k + 1].astype(jnp.int16) for k in range(TOP_K)]

    def fill(r0, n):
        j16 = (lax.broadcasted_iota(I32, (n, MOE_TM), 0) + r0).astype(jnp.int16)
        onehot = jnp.zeros((n, MOE_TM), BF16)
        for k in range(TOP_K):
            onehot = jnp.where(j16 == pos16[k], jnp.ones((), BF16), onehot)
        xs_ref[r0:r0 + n, :] = _dot(onehot, hb).astype(BF16)

    fill(0, ROWS_MAIN)
    for r0 in range(ROWS_MAIN, R1, ROW_BLK):
        pl.when(used > r0)(functools.partial(fill, r0, ROW_BLK))

        @pl.when(used <= r0)
        def _(r0=r0):
            xs_ref[r0:r0 + ROW_BLK, :] = jnp.zeros((ROW_BLK, D), BF16)


def _route_select(x_ref, nw_ref, mod_ref, wrh_ref, wrl_ref, rb_ref, pos_ref, wt_ref, npc_ref):
    h = _norm_mod(x_ref[...], nw_ref[...], mod_ref[0, 3:4, :], mod_ref[0, 4:5, :])
    hb = h.astype(BF16)
    hl = (h - hb.astype(F32)).astype(BF16)
    wrh = wrh_ref[...]
    logits = _dot_nt(wrh, hb) + _dot_nt(wrh, hl) + _dot_nt(wrl_ref[...], hb)
    scores = jax.nn.sigmoid(logits)
    sel = scores + rb_ref[...]

    gsz = N_EXPERTS // N_GROUPS
    sub = lax.broadcasted_iota(I32, (gsz, MOE_TM), 0)
    ninf = -jnp.inf
    gs_rows = []
    for gi in range(N_GROUPS):
        blk = sel[gi * gsz:(gi + 1) * gsz]
        m1 = jnp.max(blk, axis=0, keepdims=True)
        first = jnp.min(jnp.where(blk == m1, sub, gsz), axis=0, keepdims=True)
        m2 = jnp.max(jnp.where(sub == first, ninf, blk), axis=0, keepdims=True)
        gs_rows.append(m1 + m2)
    cur = jnp.concatenate(gs_rows, axis=0)
    gidx = lax.broadcasted_iota(I32, (N_GROUPS, MOE_TM), 0)
    gsel = jnp.zeros((N_GROUPS, MOE_TM), F32)
    for _ in range(TOPK_GROUPS):
        m = jnp.max(cur, axis=0, keepdims=True)
        first = jnp.min(jnp.where(cur == m, gidx, N_GROUPS), axis=0, keepdims=True)
        hit = gidx == first
        gsel = jnp.where(hit, 1.0, gsel)
        cur = jnp.where(hit, ninf, cur)
    emask = jnp.concatenate(
        [jnp.broadcast_to(gsel[gi:gi + 1], (gsz, MOE_TM)) for gi in range(N_GROUPS)], axis=0)
    masked = jnp.where(emask > 0.5, sel, ninf)
    eidx = lax.broadcasted_iota(I32, (N_EXPERTS, MOE_TM), 0)
    chosen = jnp.zeros((N_EXPERTS, MOE_TM), F32)
    hits, wsel = [], []
    for _ in range(TOP_K):
        m = jnp.max(masked, axis=0, keepdims=True)
        first = jnp.min(jnp.where(masked == m, eidx, N_EXPERTS), axis=0, keepdims=True)
        hit = eidx == first
        hits.append(hit)
        wsel.append(jnp.sum(jnp.where(hit, scores, 0.0), axis=0, keepdims=True))
        chosen = jnp.where(hit, 1.0, chosen)
        masked = jnp.where(hit, ninf, masked)
    wsum = wsel[0]
    for w in wsel[1:]:
        wsum = wsum + w

    n_io = lax.broadcasted_iota(I32, (MOE_TM, MOE_TM), 0)
    m_io = lax.broadcasted_iota(I32, (MOE_TM, MOE_TM), 1)
    earlier = jnp.where(n_io < m_io, 1.0, 0.0).astype(BF16)
    rank = _dot(chosen.astype(BF16), earlier)
    cnt = jnp.sum(chosen, axis=1, keepdims=True)
    npc = jnp.floor((cnt + (PIECE - 1)) * (1.0 / PIECE))
    e_io = lax.broadcasted_iota(I32, (N_EXPERTS, N_EXPERTS), 0)
    f_io = lax.broadcasted_iota(I32, (N_EXPERTS, N_EXPERTS), 1)
    below = jnp.where(f_io < e_io, 1.0, 0.0).astype(BF16)
    npc_l = jnp.broadcast_to(npc, (N_EXPERTS, LANES))
    start = _dot(below, npc_l.astype(BF16))[:, 0:1] * PIECE
    slot = start + rank

    pos_rows, wt_rows = [], []
    for k in range(TOP_K):
        pos_k = jnp.sum(jnp.where(hits[k], slot, 0.0), axis=0, keepdims=True).astype(I32)
        pos_rows.append(pos_k)
        wt_rows.append(wsel[k] / wsum * ROUTED_SCALE)

    pad = SUBLANES - TOP_K
    pos = jnp.concatenate(pos_rows + [jnp.full((pad, MOE_TM), -1, I32)], axis=0)
    pos_ref[...] = pos
    wt_ref[...] = jnp.concatenate(wt_rows + [jnp.zeros((pad, MOE_TM), F32)], axis=0)
    npc_ref[0] = npc_l.astype(I32)
    return hb, pos, jnp.sum(npc) * PIECE


def _route(x, nw, mod, wr_hi, wr_lo, rbias):
    def tile(i):
        return jnp.minimum(i, N_TILES - 1)

    return pl.pallas_call(
        _route_kernel,
        grid=(N_TILES + 2,),
        in_specs=[
            pl.BlockSpec((MOE_TM, D), lambda i: (tile(i), 0)),
            pl.BlockSpec((1, D), lambda i: (0, 0)),
            pl.BlockSpec((1, 6, D), lambda i: (_mod_group(tile(i), MOE_TM), 0, 0)),
            pl.BlockSpec((N_EXPERTS, D), lambda i: (0, 0)),
            pl.BlockSpec((N_EXPERTS, D), lambda i: (0, 0)),
            pl.BlockSpec((N_EXPERTS, 1), lambda i: (0, 0)),
        ],
        out_specs=[
            pl.BlockSpec((R1, D), lambda i: (jnp.maximum(i - 1, 0), 0)),
            pl.BlockSpec((SUBLANES, MOE_TM), lambda i: (0, tile(i))),
            pl.BlockSpec((SUBLANES, MOE_TM), lambda i: (0, tile(i))),
            pl.BlockSpec((1, N_EXPERTS, LANES), lambda i: (tile(i), 0, 0)),
        ],
        out_shape=[
            jax.ShapeDtypeStruct((P_TOT * PIECE, D), BF16),
            jax.ShapeDtypeStruct((SUBLANES, NT), I32),
            jax.ShapeDtypeStruct((SUBLANES, NT), F32),
            jax.ShapeDtypeStruct((N_TILES, N_EXPERTS, LANES), I32),
        ],
        scratch_shapes=[pltpu.VMEM((2, MOE_TM, D), BF16), pltpu.VMEM((2, SUBLANES, MOE_TM), I32),
                        pltpu.SMEM((2,), F32)],
        compiler_params=_cparams(("arbitrary",)),
        name="moe_route",
    )(x, nw.reshape(1, D), mod, wr_hi, wr_lo, rbias.reshape(N_EXPERTS, 1))


def _piece_lists(npc):
    t, e = npc.shape
    hp = lax.Precision.HIGHEST
    npc_t = npc.T.astype(F32)
    start_t = (jnp.cumsum(npc, axis=1) - npc).T.astype(F32)
    tile_end = jnp.cumsum(npc_t, axis=1)
    n_e = tile_end[:, -1]
    pe_end = jnp.cumsum(n_e)
    pe_off = pe_end - n_e
    p = jnp.arange(P_MAX, dtype=F32)
    e_p = jnp.minimum(jnp.sum((pe_end[None, :] <= p[:, None]).astype(I32), axis=1), e - 1)
    oh_e = (e_p[:, None] == jnp.arange(e, dtype=I32)[None, :]).astype(F32)
    tab = jnp.concatenate([tile_end, start_t, npc_t, pe_off[:, None]], axis=1)
    row = jnp.dot(oh_e, tab, precision=hp)
    te_p, st_p, np_p, off_p = row[:, :t], row[:, t:2 * t], row[:, 2 * t:3 * t], row[:, 3 * t]
    local = p - off_p
    t_p = jnp.minimum(jnp.sum((te_p <= local[:, None]).astype(I32), axis=1), t - 1)
    oh_t = t_p[:, None] == jnp.arange(t, dtype=I32)[None, :]

    def pick(a):
        return jnp.sum(jnp.where(oh_t, a, 0.0), axis=1)

    src = t_p * R1B + (pick(st_p) + local - (pick(te_p) - pick(np_p))).astype(I32)
    src = jnp.concatenate([jnp.clip(src, 0, P_MAX - 1), jnp.zeros((G_PIECES,), I32)])

    nch = jnp.floor((n_e + (G_PIECES - 1)) * (1.0 / G_PIECES))
    ch_end = jnp.cumsum(nch)
    ch_off = jnp.concatenate([jnp.zeros((1,), F32), ch_end]).astype(I32)
    c = jnp.arange(-N_SLOTS, NCH + 2, dtype=F32)
    ce = jnp.minimum(jnp.sum((ch_end[None, :] <= c[:, None]).astype(I32), axis=1), e - 1)
    oh_c = (ce[:, None] == jnp.arange(e, dtype=I32)[None, :]).astype(F32)
    crow = jnp.dot(oh_c, jnp.stack([ch_end - nch, pe_off, n_e], axis=1), precision=hp)
    k_in = c - crow[:, 0]
    live = jnp.logical_and(c >= 0, c < ch_end[-1])
    cn = jnp.where(live, jnp.clip(crow[:, 2] - G_PIECES * k_in, 0, G_PIECES), 0.0)
    cs = jnp.where(cn > 0, crow[:, 1] + G_PIECES * k_in, 0.0)
    return src, ch_off, cs.astype(I32), cn.astype(I32)


def _ffn_kernel(src_ref, choff_ref, cs_ref, cn_ref, xs_in, wg_ref, wu_ref, wd_ref, xs_out,
                xbuf, ybuf, wgb, wub, wdb, gsem, ssem):
    e = pl.program_id(0)
    total = choff_ref[N_EXPERTS]

    def start_gather(ch):
        sl = lax.rem(ch + N_SLOTS, N_SLOTS)
        base = cs_ref[ch + N_SLOTS]
        n = cn_ref[ch + N_SLOTS]
        for i in range(G_PIECES):
            idx = src_ref[base + jnp.where(i < n, i, 0)]
            pltpu.make_async_copy(xs_in.at[idx], xbuf.at[sl, i], gsem.at[sl]).start(priority=i % 2)

    def wait_gather(ch):
        sl = lax.rem(ch + N_SLOTS, N_SLOTS)
        pltpu.make_async_copy(xs_in.at[pl.ds(0, G_PIECES)], xbuf.at[sl], gsem.at[sl]).wait()

    def start_scatter(ch):
        sl = lax.rem(ch + N_SLOTS, N_SLOTS)
        base = cs_ref[ch + N_SLOTS]
        n = cn_ref[ch + N_SLOTS]
        for i in range(G_PIECES):
            idx = jnp.where(i < n, src_ref[base + i], P_MAX + sl * G_PIECES + i)
            pltpu.make_async_copy(ybuf.at[sl, i], xs_out.at[idx], ssem.at[sl]).start(priority=i % 2)

    def wait_scatter(ch):
        sl = lax.rem(ch + N_SLOTS, N_SLOTS)
        pltpu.make_async_copy(ybuf.at[sl], xs_out.at[pl.ds(0, G_PIECES)], ssem.at[sl]).wait()

    @pl.when(e == 0)
    def _():
        ybuf[...] = jnp.zeros_like(ybuf)
        start_gather(0)
        start_gather(1)
        start_scatter(-3)
        start_scatter(-2)

    wgb[...] = wg_ref[0].astype(BF16)
    wub[...] = wu_ref[0].astype(BF16)
    wdb[...] = wd_ref[0].astype(BF16)

    def chunk(c, carry):
        sl = lax.rem(c, N_SLOTS)
        wait_gather(c)
        wait_scatter(c - 3)
        x = xbuf[sl].reshape(G_PIECES * PIECE, D)
        hid = (_silu(_dot(x, wgb[...])) * _dot(x, wub[...])).astype(BF16)
        start_gather(c + 2)
        start_scatter(c - 1)
        ybuf[sl] = _dot(hid, wdb[...]).astype(BF16).reshape(G_PIECES, PIECE, D)
        return carry

    lax.fori_loop(choff_ref[e], choff_ref[e + 1], chunk, 0)

    @pl.when(e == N_EXPERTS - 1)
    def _():
        start_scatter(total - 1)
        wait_gather(total)
        wait_gather(total + 1)
        wait_scatter(total - 3)
        wait_scatter(total - 2)
        wait_scatter(total - 1)


def _expert_ffn(xs, lists, layer, w_gate, w_up, w_down):
    src, ch_off, cs, cn = lists
    grid_spec = pltpu.PrefetchScalarGridSpec(
        num_scalar_prefetch=4,
        grid=(N_EXPERTS,),
        in_specs=[
            pl.BlockSpec(memory_space=pl.ANY),
            pl.BlockSpec((None, 1, D, D_EXPERT), lambda e, *_: (layer, e, 0, 0)),
            pl.BlockSpec((None, 1, D, D_EXPERT), lambda e, *_: (layer, e, 0, 0)),
            pl.BlockSpec((None, 1, D_EXPERT, D), lambda e, *_: (layer, e, 0, 0)),
        ],
        out_specs=pl.BlockSpec(memory_space=pl.ANY),
        scratch_shapes=[
            pltpu.VMEM((N_SLOTS, G_PIECES, PIECE, D), BF16),
            pltpu.VMEM((N_SLOTS, G_PIECES, PIECE, D), BF16),
            pltpu.VMEM((D, D_EXPERT), BF16),
            pltpu.VMEM((D, D_EXPERT), BF16),
            pltpu.VMEM((D_EXPERT, D), BF16),
            pltpu.SemaphoreType.DMA((N_SLOTS,)),
            pltpu.SemaphoreType.DMA((N_SLOTS,)),
        ],
    )
    out = pl.pallas_call(
        _ffn_kernel,
        grid_spec=grid_spec,
        out_shape=jax.ShapeDtypeStruct((P_TOT, PIECE, D), BF16),
        input_output_aliases={4: 0},
        compiler_params=_cparams(("arbitrary",)),
        name="moe_ffn",
    )(src, ch_off, cs, cn, xs.reshape(P_TOT, PIECE, D), w_gate, w_up, w_down)
    return out.reshape(P_TOT * PIECE, D)


def _combine_kernel(used_ref, ys_ref, pos_ref, wt_ref, x_ref, nw_ref, mod_ref, wsg_ref, wsu_ref, wsd_ref,
                    fw_ref, *rest, final):
    o_refs, acc_ref = rest[:-1], rest[-1]
    i = pl.program_id(0)
    x = x_ref[...]
    h = _norm_mod(x, nw_ref[...], mod_ref[0, 3:4, :], mod_ref[0, 4:5, :]).astype(BF16)
    shared = _dot((_silu(_dot(h, wsg_ref[...])) * _dot(h, wsu_ref[...])).astype(BF16), wsd_ref[...])

    pos16 = pos_ref[...].astype(jnp.int16)
    wt16 = wt_ref[...].astype(BF16)

    def block(r0, n):
        j16 = (lax.broadcasted_iota(I32, (MOE_TM, n), 1) + r0).astype(jnp.int16)
        wm = jnp.zeros((MOE_TM, n), BF16)
        for k in range(TOP_K):
            wm = jnp.where(j16 == pos16[:, k:k + 1], wt16[:, k:k + 1], wm)
        return _dot(wm, ys_ref[r0:r0 + n, :])

    acc_ref[...] = block(0, ROWS_MAIN)
    for r0 in range(ROWS_MAIN, R1, ROW_BLK):
        @pl.when(used_ref[i] > r0)
        def _(r0=r0):
            acc_ref[...] += block(r0, ROW_BLK)

    y = x + mod_ref[0, 5:6, :] * (acc_ref[...] + shared)
    if not final:
        o_refs[0][...] = y
        return
    ms = jnp.mean(y * y, axis=-1, keepdims=True)
    y = y * lax.rsqrt(ms + EPS) * fw_ref[...]
    is_prompt = i < NP // MOE_TM

    @pl.when(is_prompt)
    def _():
        o_refs[0][...] = y

    @pl.when(jnp.logical_not(is_prompt))
    def _():
        o_refs[1][...] = y


def _combine(ys, used, pos_t, wt_t, x, nw, mod, wsg, wsu, wsd, fw, final):
    n_p = NP // MOE_TM
    if final:
        out_specs = [pl.BlockSpec((MOE_TM, D), lambda i, u: (jnp.minimum(i, n_p - 1), 0)),
                     pl.BlockSpec((MOE_TM, D), lambda i, u: (jnp.maximum(i - n_p, 0), 0))]
        out_shape = [jax.ShapeDtypeStruct((NP, D), F32), jax.ShapeDtypeStruct((NS, D), F32)]
    else:
        out_specs = pl.BlockSpec((MOE_TM, D), lambda i, u: (i, 0))
        out_shape = jax.ShapeDtypeStruct((NT, D), F32)
    grid_spec = pltpu.PrefetchScalarGridSpec(
        num_scalar_prefetch=1,
        grid=(N_TILES,),
        in_specs=[
            pl.BlockSpec((R1, D), lambda i, u: (i, 0)),
            pl.BlockSpec((MOE_TM, SUBLANES), lambda i, u: (i, 0)),
            pl.BlockSpec((MOE_TM, SUBLANES), lambda i, u: (i, 0)),
            pl.BlockSpec((MOE_TM, D), lambda i, u: (i, 0)),
            pl.BlockSpec((1, D), lambda i, u: (0, 0)),
            pl.BlockSpec((1, 6, D), lambda i, u: (_mod_group(i, MOE_TM), 0, 0)),
            pl.BlockSpec((D, D_SHARED), lambda i, u: (0, 0)),
            pl.BlockSpec((D, D_SHARED), lambda i, u: (0, 0)),
            pl.BlockSpec((D_SHARED, D), lambda i, u: (0, 0)),
            pl.BlockSpec((1, D), lambda i, u: (0, 0)),
        ],
        out_specs=out_specs,
        scratch_shapes=[pltpu.VMEM((MOE_TM, D), F32)],
    )
    return pl.pallas_call(
        functools.partial(_combine_kernel, final=final),
        grid_spec=grid_spec,
        out_shape=out_shape,
        compiler_params=_cparams(("arbitrary",)),
        name="moe_combine",
    )(used, ys, pos_t, wt_t, x, nw.reshape(1, D), mod, wsg, wsu, wsd, fw.reshape(1, D))


def _moe(x, nw, mod, w_router, rbias, layer, w_gate, w_up, w_down, ws_gate, ws_up, ws_down, fw, final):
    wr = w_router.astype(F32).T
    wr_hi = wr.astype(BF16)
    wr_lo = (wr - wr_hi.astype(F32)).astype(BF16)
    xs, pos, wt, npc = _route(x, nw, mod, wr_hi, wr_lo, rbias.astype(F32))
    npc = npc[:, :, 0]
    lists = _piece_lists(npc)
    ys = _expert_ffn(xs, lists, layer, w_gate, w_up, w_down)
    used = (npc.sum(axis=1) * PIECE).astype(I32)
    return _combine(ys, used, pos.T, wt.T, x, nw, mod, ws_gate.astype(BF16), ws_up.astype(BF16),
                    ws_down.astype(BF16), fw, final)


def kernel(x_prompt, x_sample, state_hgrn_fwd, state_hgrn_bwd, cache_na_k, cache_na_v, c, c_ctx,
           norm1_w, norm2_w, ada_w, ada_b, hgrn_w_in, hgrn_lb_logits, hgrn_gn_w, hgrn_w_out,
           na_w_qkv, na_rpb, na_w_o, moe_w_router, moe_router_bias, moe_w_gate, moe_w_up, moe_w_down,
           shared_w_gate, shared_w_up, shared_w_down, final_norm_w):
    x = (x_prompt.reshape(NP, D), x_sample.reshape(NS, D))
    cvec = jnp.concatenate([c_ctx[None, :], c, jnp.zeros((N_MOD - 1 - DEC_BATCH, D), F32)], axis=0)
    mod = _modulation(cvec, ada_w, ada_b)
    lb_table = jnp.cumsum(jax.nn.softmax(hgrn_lb_logits.astype(F32), axis=0), axis=0)
    hk = HG_HEADS * HG_DK

    sf = sb = k_c = v_c = None
    for l in range(DEPTH):
        if l % 2 == 0:
            a = l // 2
            proj, chunk_tot = _hgrn_proj(x, norm1_w[l], mod[l], lb_table[l], hgrn_w_in[a].astype(BF16))
            o_f, o_b, sf, sb = _gla(proj, chunk_tot,
                                    state_hgrn_fwd[:, a].reshape(DEC_BATCH, hk, HG_DV),
                                    state_hgrn_bwd[:, a].reshape(DEC_BATCH, hk, HG_DV))
            x = _hgrn_out(o_f, o_b, proj, x, mod[l], hgrn_gn_w[a], hgrn_w_out[a].astype(BF16))
        else:
            n = l // 2
            w_qkv = na_w_qkv[n].astype(BF16)
            qkv_p, k_p, v_p = _norm_proj(x, norm1_w[l], mod[l], w_qkv, 0, NP // TM, n_copy=2, out_dtype=BF16)
            qkv_s = _norm_proj(x, norm1_w[l], mod[l], w_qkv, NP // TM, NS // TM, out_dtype=BF16)
            att_p = _attn_ctx(qkv_p)
            att_s = _attn_lat(qkv_s, cache_na_k[:, n].reshape(DEC_BATCH, PAST_LEN, D),
                              cache_na_v[:, n].reshape(DEC_BATCH, PAST_LEN, D),
                              *_latent_bias_tables(na_rpb[n]))
            x = _attn_out(att_p, att_s, x, mod[l], na_w_o[n].astype(BF16))
            k_c = k_p.reshape(BATCH, SEQ, NA_HEADS, NA_HD)
            v_c = v_p.reshape(BATCH, SEQ, NA_HEADS, NA_HD)
        x = _moe(x, norm2_w[l], mod[l], moe_w_router[l], moe_router_bias[l], l, moe_w_gate, moe_w_up,
                 moe_w_down, shared_w_gate[l], shared_w_up[l], shared_w_down[l], final_norm_w,
                 final=(l == DEPTH - 1))

    y_prompt = x[0].reshape(BATCH, SEQ, D)
    y_sample = x[1].reshape(DEC_BATCH, DEC_SEQ, D)
    new_sf = sf.reshape(BATCH, 1, HG_HEADS, HG_DK, HG_DV)
    new_sb = sb.reshape(BATCH, 1, HG_HEADS, HG_DK, HG_DV)
    return (y_prompt, y_sample, new_sf, new_sb, k_c[:, None], v_c[:, None])
```

```python
import functools

import numpy as np
import jax
import jax.numpy as jnp
from jax import lax
from jax.experimental import pallas as pl
from jax.experimental.pallas import tpu as pltpu

F32 = jnp.float32
BF16 = jnp.bfloat16
I32 = jnp.int32

D = 1024
BATCH = 32
SEQ = 256
DEPTH = 2
DEC_BATCH = 8
DEC_SEQ = 1024
PAST_LEN = 512
GRID_W = 64
HG_HEADS = 8
HG_DK = 128
HG_DV = 128
CHUNK = 64
NA_HEADS = 16
NA_HD = 64
WIN_R = 8
WIN_C = 16
N_EXPERTS = 64
TOP_K = 6
N_GROUPS = 8
TOPK_GROUPS = 4
D_EXPERT = 256
D_SHARED = 256
ROUTED_SCALE = 2.5
EPS = 1e-6

NP = BATCH * SEQ
NS = DEC_BATCH * DEC_SEQ
NT = NP + NS
N_MOD = 16
STRIP_W = 2048
assert (2 * (DEC_SEQ // GRID_W) - 1) * GRID_W <= STRIP_W
NEG = -1e30

LANES = 128
SUBLANES = 8
BF16_ROWS = 16
VMEM_LIMIT = 56 * 1024 * 1024

TM = 256
SUB = 16
GLA_FAST_MAX = 60.0
GLA_CPS = 2
GLA_BLK = GLA_CPS * CHUNK
assert SEQ % GLA_BLK == 0 and DEC_SEQ % GLA_BLK == 0
MOE_TM = 256
PIECE = BF16_ROWS
R1 = MOE_TM * TOP_K + N_EXPERTS * (PIECE - 1) + 64
assert R1 % PIECE == 0 and R1 % LANES == 0
R1B = R1 // PIECE
ROWS_MAIN = 2048
ROW_BLK = 256
assert (R1 - ROWS_MAIN) % ROW_BLK == 0
N_TILES = NT // MOE_TM
P_MAX = N_TILES * R1B
G_PIECES = 32
NCH = P_MAX // G_PIECES + N_EXPERTS
N_SLOTS = 3
P_TOT = P_MAX + R1B
assert N_SLOTS * G_PIECES <= R1B


def _cparams(sem):
    return pltpu.CompilerParams(dimension_semantics=sem, vmem_limit_bytes=VMEM_LIMIT)


def _dot(a, b):
    return jnp.dot(a, b, preferred_element_type=F32)


def _dot_nt(a, b):
    return lax.dot_general(a, b, (((1,), (1,)), ((), ())), preferred_element_type=F32)


def _dot_tn(a, b):
    return lax.dot_general(a, b, (((0,), (0,)), ((), ())), preferred_element_type=F32)


def _silu(x):
    return x * jax.nn.sigmoid(x)


def _mod_group(i, tm):
    r = i * tm
    return jnp.where(r < NP, 0, 1 + (r - NP) // DEC_SEQ)


def _norm_mod(x, nw, shift, scale):
    ms = jnp.mean(x * x, axis=-1, keepdims=True)
    y = x * lax.rsqrt(ms + EPS) * nw
    return y * (1.0 + scale) + shift


def _mod_kernel(c_ref, w_ref, b_ref, o_ref):
    s = _silu(c_ref[...]).astype(BF16)
    o_ref[0] = _dot(s, w_ref[0].astype(BF16)) + b_ref[0]


def _modulation(cvec, ada_w, ada_b):
    cw = 1536
    n = ada_w.shape[-1]
    out = pl.pallas_call(
        _mod_kernel,
        grid=(DEPTH, n // cw),
        in_specs=[
            pl.BlockSpec((N_MOD, D), lambda l, j: (0, 0)),
            pl.BlockSpec((1, D, cw), lambda l, j: (l, 0, j)),
            pl.BlockSpec((1, 1, cw), lambda l, j: (l, 0, j)),
        ],
        out_specs=pl.BlockSpec((1, N_MOD, cw), lambda l, j: (l, 0, j)),
        out_shape=jax.ShapeDtypeStruct((DEPTH, N_MOD, n), F32),
        compiler_params=_cparams(("arbitrary", "arbitrary")),
        name="modulation",
    )(cvec, ada_w, ada_b.reshape(DEPTH, 1, n))
    return out.reshape(DEPTH, N_MOD, 6, D)


def _stream_specs(x, tm):
    if not isinstance(x, tuple):
        return [pl.BlockSpec((tm, D), lambda i: (i, 0))], (x,)
    n_p = NP // tm
    return [pl.BlockSpec((tm, D), lambda i: (jnp.minimum(i, n_p - 1), 0)),
            pl.BlockSpec((tm, D), lambda i: (jnp.maximum(i - n_p, 0), 0))], x


def _stream_tile(refs, tm):
    if len(refs) == 1:
        return refs[0][...]
    return jnp.where(pl.program_id(0) < NP // tm, refs[0][...], refs[1][...])


def _proj_kernel(x_ref, nw_ref, mod_ref, w_ref, o_ref, *copy_refs, cw):
    h = _norm_mod(x_ref[...], nw_ref[...], mod_ref[0, 0:1, :], mod_ref[0, 1:2, :]).astype(BF16)
    for j in range(w_ref.shape[1] // cw):
        val = _dot(h, w_ref[:, j * cw:(j + 1) * cw])
        o_ref[:, j * cw:(j + 1) * cw] = val.astype(o_ref.dtype)
        col = j * cw - D
        if 0 <= col < len(copy_refs) * D:
            copy_refs[col // D][:, col % D:col % D + cw] = val


def _norm_proj(x, nw, mod, w, tile0, n_tiles, n_copy=0, out_dtype=F32):
    n = w.shape[1]
    rows = n_tiles * TM
    outs = pl.pallas_call(
        functools.partial(_proj_kernel, cw=512),
        grid=(n_tiles,),
        in_specs=[
            pl.BlockSpec((TM, D), lambda i: (tile0 + i, 0)),
            pl.BlockSpec((1, D), lambda i: (0, 0)),
            pl.BlockSpec((1, 6, D), lambda i: (_mod_group(tile0 + i, TM), 0, 0)),
            pl.BlockSpec((D, n), lambda i: (0, 0)),
        ],
        out_specs=[pl.BlockSpec((TM, n), lambda i: (i, 0))] + [pl.BlockSpec((TM, D), lambda i: (i, 0))] * n_copy,
        out_shape=[jax.ShapeDtypeStruct((rows, n), out_dtype)] + [jax.ShapeDtypeStruct((rows, D), F32)] * n_copy,
        compiler_params=_cparams(("arbitrary",)),
        name="norm_proj",
    )(x, nw.reshape(1, D), mod, w)
    return outs if n_copy else outs[0]


def _hgrn_proj_kernel(*refs, cw, n_x):
    nw_ref, mod_ref, lb_ref, w_ref, o_ref, tot_ref = refs[n_x:]
    x = _stream_tile(refs[:n_x], TM)
    h = _norm_mod(x, nw_ref[...], mod_ref[0, 0:1, :], mod_ref[0, 1:2, :]).astype(BF16)
    n_ck = TM // CHUNK
    mins = {}
    for j in range(w_ref.shape[1] // cw):
        val = _dot(h, w_ref[:, j * cw:(j + 1) * cw])
        sec, col = divmod(j * cw, D)
        if sec in (1, 2):
            lb = lb_ref[:, col:col + cw]
            val = jnp.log(lb + (1.0 - lb) * jax.nn.sigmoid(val))
            for c in range(n_ck):
                tot = jnp.sum(val[c * CHUNK:(c + 1) * CHUNK], axis=0, keepdims=True)
                m = jnp.min(tot, axis=-1, keepdims=True)
                key = (sec - 1, c)
                mins[key] = m if key not in mins else jnp.minimum(mins[key], m)
        o_ref[:, j * cw:(j + 1) * cw] = val
    tot_ref[0] = jnp.concatenate(
        [jnp.broadcast_to(mins[(d, c)], (1, LANES)) for c in range(n_ck) for d in range(2)], axis=0)


def _hgrn_proj(x, nw, mod, lb, w):
    n = w.shape[1]
    n_ck = TM // CHUNK
    assert 2 * n_ck == SUBLANES
    x_specs, xs = _stream_specs(x, TM)
    proj, tot = pl.pallas_call(
        functools.partial(_hgrn_proj_kernel, cw=512, n_x=len(xs)),
        grid=(NT // TM,),
        in_specs=x_specs + [
            pl.BlockSpec((1, D), lambda i: (0, 0)),
            pl.BlockSpec((1, 6, D), lambda i: (_mod_group(i, TM), 0, 0)),
            pl.BlockSpec((1, HG_HEADS * HG_DK), lambda i: (0, 0)),
            pl.BlockSpec((D, n), lambda i: (0, 0)),
        ],
        out_specs=[pl.BlockSpec((TM, n), lambda i: (i, 0)),
                   pl.BlockSpec((1, SUBLANES, LANES), lambda i: (i, 0, 0))],
        out_shape=[jax.ShapeDtypeStruct((NT, n), F32),
                   jax.ShapeDtypeStruct((NT // TM, SUBLANES, LANES), F32)],
        compiler_params=_cparams(("arbitrary",)),
        name="hgrn_proj",
    )(*xs, nw.reshape(1, D), mod, lb.reshape(1, HG_HEADS * HG_DK), w)
    return proj, tot[:, :, 0].reshape(NT // CHUNK, 2)


def _split3(x):
    hi = x.astype(BF16)
    r = x - hi.astype(F32)
    mid = r.astype(BF16)
    lo = (r - mid.astype(F32)).astype(BF16)
    return hi, mid, lo


def _gla_safe(q, kk, v, g3, st_ref, row_refs, rev):
    bs_ref, ks_ref, kd8_ref = row_refs

    n_stack = 4
    t_io = lax.broadcasted_iota(I32, (n_stack * CHUNK, CHUNK), 0)
    u_io = lax.broadcasted_iota(I32, (n_stack * CHUNK, CHUNK), 1)
    which = t_io // CHUNK
    tt = t_io - which * CHUNK
    b16 = (tt // SUB) * SUB
    b8 = (tt // SUBLANES) * SUBLANES
    if rev:
        lim = jnp.where(which == 0, tt, jnp.where(which == 1, b16 + SUB, jnp.where(which == 2, b8 + SUBLANES, b8)))
        pick = u_io >= lim
    else:
        lim = jnp.where(which == 0, tt, jnp.where(which == 1, b16 - 1,
                                                  jnp.where(which == 2, b8 - 1, b8 + SUBLANES - 1)))
        pick = u_io <= lim
    tri = jnp.where(pick, 1.0, 0.0).astype(BF16)
    hi, mid, lo = g3
    cs = _dot(tri, hi) + _dot(tri, mid) + _dot(tri, lo)
    b, r16, r8, e8 = (cs[j * CHUNK:(j + 1) * CHUNK] for j in range(n_stack))
    last = 0 if rev else CHUNK - 1
    tot = b[last:last + 1]

    qe = (q * jnp.exp(b)).astype(BF16)
    qd = (q * jnp.exp(b - r16)).astype(BF16)
    qd8 = q * jnp.exp(b - r8)
    kdec = (kk * jnp.exp(tot - b)).astype(BF16)
    vb = v.astype(BF16)
    dec_tot = jnp.exp(tot)
    kd8 = kk * jnp.exp(e8 - b)
    for h in range(HG_HEADS):
        sl = slice(h * LANES, (h + 1) * LANES)
        bs_ref[h] = b[:, sl]
        ks_ref[h] = kk[:, sl]
        kd8_ref[h] = kd8[:, sl]

    def row(ref, h, s):
        return jnp.broadcast_to(ref[h, s:s + 1, :], (SUBLANES, LANES))

    lane = lax.broadcasted_iota(I32, (SUBLANES, LANES), 1)
    row8 = lax.broadcasted_iota(I32, (SUBLANES, LANES), 0)
    a_in = [[] for _ in range(HG_HEADS)]
    for tb in range(CHUNK // SUBLANES):
        t0 = tb * SUBLANES
        blk0 = (t0 // SUB) * SUB
        other = blk0 + SUBLANES if t0 == blk0 else blk0
        keep = (row8 + t0 <= lane) if rev else (row8 + t0 >= lane)
        for h in range(HG_HEADS):
            sl = slice(h * LANES, (h + 1) * LANES)
            q_t = q[t0:t0 + SUBLANES, sl]
            b_t = b[t0:t0 + SUBLANES, sl]
            acc = jnp.zeros((SUBLANES, LANES), F32)
            for s in range(t0, t0 + SUBLANES):
                p = q_t * jnp.exp(b_t - row(bs_ref, h, s)) * row(ks_ref, h, s)
                acc = jnp.where(lane == s, jnp.sum(p, axis=-1, keepdims=True), acc)
            acc = jnp.where(keep, acc, 0.0)
            if (t0 == blk0) == rev:
                qd8_t = qd8[t0:t0 + SUBLANES, sl]
                for s in range(other, other + SUBLANES):
                    p = qd8_t * row(kd8_ref, h, s)
                    acc = jnp.where(lane == s, jnp.sum(p, axis=-1, keepdims=True), acc)
            a_in[h].append(acc)

    n_blk = CHUNK // SUB
    o_heads = []
    for h in range(HG_HEADS):
        sl = slice(h * LANES, (h + 1) * LANES)
        st = st_ref[sl, :]
        o_h = _dot_nt(qe[:, sl], st.astype(BF16))
        rows = []
        for i in range(n_blk):
            lo_r, hi_r = ((i + 1) * SUB, CHUNK) if rev else (0, i * SUB)
            if hi_r == lo_r:
                rows.append(jnp.zeros((SUB, CHUNK), F32))
                continue
            r_i = r16[i * SUB:i * SUB + 1, sl]
            kd = kk[lo_r:hi_r, sl] * jnp.exp(r_i - b[lo_r:hi_r, sl])
            pads = [jnp.zeros((lo_r, LANES), F32)] if lo_r else []
            pads_hi = [jnp.zeros((CHUNK - hi_r, LANES), F32)] if hi_r < CHUNK else []
            kd = jnp.concatenate(pads + [kd] + pads_hi, axis=0).astype(BF16)
            rows.append(_dot_nt(qd[i * SUB:(i + 1) * SUB, sl], kd))
        a1 = jnp.concatenate(rows, axis=0) + jnp.concatenate(a_in[h], axis=0)[:, :CHUNK]
        o_h = o_h + _dot(a1.astype(BF16), vb[:, sl])
        o_heads.append(o_h)
        st_ref[sl, :] = st * dec_tot[:, sl] + _dot_tn(vb[:, sl], kdec[:, sl])
    return jnp.concatenate(o_heads, axis=1)


def _gla_fast(q, kk, v, b, tot, st_ref, rev):
    qe = (q * jnp.exp(b)).astype(BF16)
    ke32 = kk * jnp.exp(-b)
    ke = ke32.astype(BF16)
    dec_tot = jnp.exp(tot)
    kdec = (ke32 * dec_tot).astype(BF16)
    vb = v.astype(BF16)
    t_io = lax.broadcasted_iota(I32, (CHUNK, CHUNK), 0)
    s_io = lax.broadcasted_iota(I32, (CHUNK, CHUNK), 1)
    keep = (s_io >= t_io) if rev else (s_io <= t_io)
    o_heads = []
    for h in range(HG_HEADS):
        sl = slice(h * LANES, (h + 1) * LANES)
        st = st_ref[sl, :]
        a = jnp.where(keep, _dot_nt(qe[:, sl], ke[:, sl]), 0.0).astype(BF16)
        o_heads.append(_dot_nt(qe[:, sl], st.astype(BF16)) + _dot(a, vb[:, sl]))
        st_ref[sl, :] = st * dec_tot[:, sl] + _dot_tn(vb[:, sl], kdec[:, sl])
    return jnp.concatenate(o_heads, axis=1)


def _gla_prep(g, rev):
    kk = 1.0 - jnp.exp(g)
    g3 = _split3(g)
    t_io = lax.broadcasted_iota(I32, (CHUNK, CHUNK), 0)
    u_io = lax.broadcasted_iota(I32, (CHUNK, CHUNK), 1)
    tri = jnp.where((u_io >= t_io) if rev else (u_io <= t_io), 1.0, 0.0).astype(BF16)
    b = _dot(tri, g3[0]) + _dot(tri, g3[1]) + _dot(tri, g3[2])
    last = 0 if rev else CHUNK - 1
    return kk, g3, b, b[last:last + 1]


def _gla_kernel(mild_ref, qf_ref, gf_ref, vf_ref, qb_ref, gb_ref, vb_ref, s0f_ref, s0b_ref,
                of_ref, ob_ref, sf_ref, sb_ref, stf, stb, bs, ks, kd8):
    i = pl.program_id(0)
    n_p = NP // GLA_BLK
    is_prompt = i < n_p
    c = jnp.where(is_prompt, i % (SEQ // GLA_BLK), (i - n_p) % (DEC_SEQ // GLA_BLK))
    n_c = jnp.where(is_prompt, SEQ // GLA_BLK, DEC_SEQ // GLA_BLK)

    @pl.when(jnp.logical_and(c == 0, is_prompt))
    def _():
        stf[...] = jnp.zeros_like(stf)
        stb[...] = jnp.zeros_like(stb)

    @pl.when(jnp.logical_and(c == 0, jnp.logical_not(is_prompt)))
    def _():
        for h in range(HG_HEADS):
            sl = slice(h * LANES, (h + 1) * LANES)
            stf[sl, :] = s0f_ref[0, sl, :].T
            stb[sl, :] = s0b_ref[0, sl, :].T

    rows = (bs, ks, kd8)
    rf = [slice(k * CHUNK, (k + 1) * CHUNK) for k in range(GLA_CPS)]
    rb = rf[::-1]
    prep_f = [_gla_prep(gf_ref[r, :], False) for r in rf]
    prep_b = [_gla_prep(gb_ref[r, :], True) for r in rb]
    mild = mild_ref[i] != 0

    @pl.when(mild)
    def _():
        for k in range(GLA_CPS):
            kk, _, b, tot = prep_f[k]
            of_ref[rf[k], :] = _gla_fast(qf_ref[rf[k], :], kk, vf_ref[rf[k], :], b, tot, stf, False)
            kk, _, b, tot = prep_b[k]
            ob_ref[rb[k], :] = _gla_fast(qb_ref[rb[k], :], kk, vb_ref[rb[k], :], b, tot, stb, True)

    @pl.when(jnp.logical_not(mild))
    def _():
        for k in range(GLA_CPS):
            kk, g3, _, _ = prep_f[k]
            of_ref[rf[k], :] = _gla_safe(qf_ref[rf[k], :], kk, vf_ref[rf[k], :], g3, stf, rows, False)
            kk, g3, _, _ = prep_b[k]
            ob_ref[rb[k], :] = _gla_safe(qb_ref[rb[k], :], kk, vb_ref[rb[k], :], g3, stb, rows, True)

    @pl.when(jnp.logical_and(c == n_c - 1, is_prompt))
    def _():
        for h in range(HG_HEADS):
            sl = slice(h * LANES, (h + 1) * LANES)
            sf_ref[0, sl, :] = stf[sl, :].T
            sb_ref[0, sl, :] = stb[sl, :].T


def _gla(proj, chunk_tot, s0f, s0b):
    n_p = NP // GLA_BLK
    cp = SEQ // GLA_BLK
    cs = DEC_SEQ // GLA_BLK

    def bwd_blk(i):
        jp = (i // cp) * cp + (cp - 1 - i % cp)
        j = i - n_p
        js = n_p + (j // cs) * cs + (cs - 1 - j % cs)
        return jnp.where(i < n_p, jp, js)

    def req(i):
        return jnp.maximum(i - n_p, 0) // cs

    def preq(i):
        return jnp.minimum(i // cp, BATCH - 1)

    steps = jnp.arange(NT // GLA_BLK, dtype=I32)
    blk_tot = chunk_tot.reshape(NT // GLA_BLK, GLA_CPS, 2).min(axis=1)
    mild = jnp.logical_and(blk_tot[:, 0] >= -GLA_FAST_MAX,
                           blk_tot[bwd_blk(steps), 1] >= -GLA_FAST_MAX).astype(I32)
    hk = HG_HEADS * HG_DK
    grid_spec = pltpu.PrefetchScalarGridSpec(
        num_scalar_prefetch=1,
        grid=(NT // GLA_BLK,),
        in_specs=[
            pl.BlockSpec((GLA_BLK, D), lambda i, m: (i, 0)),
            pl.BlockSpec((GLA_BLK, D), lambda i, m: (i, 1)),
            pl.BlockSpec((GLA_BLK, D), lambda i, m: (i, 3)),
            pl.BlockSpec((GLA_BLK, D), lambda i, m: (bwd_blk(i), 0)),
            pl.BlockSpec((GLA_BLK, D), lambda i, m: (bwd_blk(i), 2)),
            pl.BlockSpec((GLA_BLK, D), lambda i, m: (bwd_blk(i), 3)),
            pl.BlockSpec((1, hk, HG_DV), lambda i, m: (req(i), 0, 0)),
            pl.BlockSpec((1, hk, HG_DV), lambda i, m: (req(i), 0, 0)),
        ],
        out_specs=[
            pl.BlockSpec((GLA_BLK, D), lambda i, m: (i, 0)),
            pl.BlockSpec((GLA_BLK, D), lambda i, m: (bwd_blk(i), 0)),
            pl.BlockSpec((1, hk, HG_DV), lambda i, m: (preq(i), 0, 0)),
            pl.BlockSpec((1, hk, HG_DV), lambda i, m: (preq(i), 0, 0)),
        ],
        scratch_shapes=[pltpu.VMEM((hk, HG_DV), F32), pltpu.VMEM((hk, HG_DV), F32)]
        + [pltpu.VMEM((HG_HEADS, CHUNK, HG_DK), F32)] * 3,
    )
    return pl.pallas_call(
        _gla_kernel,
        grid_spec=grid_spec,
        out_shape=[
            jax.ShapeDtypeStruct((NT, D), F32),
            jax.ShapeDtypeStruct((NT, D), F32),
            jax.ShapeDtypeStruct((BATCH, hk, HG_DV), F32),
            jax.ShapeDtypeStruct((BATCH, hk, HG_DV), F32),
        ],
        compiler_params=_cparams(("arbitrary",)),
        name="gla_scan",
    )(mild, proj, proj, proj, proj, proj, proj, s0f, s0b)


def _head_pair_masks():
    upper = lax.broadcasted_iota(I32, (1, LANES), 1) >= NA_HD
    return upper


def _attn_ctx_kernel(q_ref, k_ref, v_ref, o_ref):
    upper = _head_pair_masks()
    for p in range(NA_HEADS // 2):
        sl = slice(p * LANES, (p + 1) * LANES)
        qp = q_ref[:, sl] * (NA_HD ** -0.5)
        kp = k_ref[:, sl].astype(BF16)
        vp = v_ref[:, sl].astype(BF16)
        outs = []
        for u in range(2):
            qm = jnp.where(upper if u else jnp.logical_not(upper), qp, 0.0).astype(BF16)
            s = _dot_nt(qm, kp)
            e = jnp.exp(s - jnp.max(s, axis=-1, keepdims=True))
            outs.append(_dot(e.astype(BF16), vp) / jnp.sum(e, axis=-1, keepdims=True))
        o_ref[:, sl] = jnp.where(upper, outs[1], outs[0]).astype(o_ref.dtype)


def _attn_ctx(qkv):
    return pl.pallas_call(
        _attn_ctx_kernel,
        grid=(BATCH,),
        in_specs=[
            pl.BlockSpec((SEQ, D), lambda b: (b, 0)),
            pl.BlockSpec((SEQ, D), lambda b: (b, 1)),
            pl.BlockSpec((SEQ, D), lambda b: (b, 2)),
        ],
        out_specs=pl.BlockSpec((SEQ, D), lambda b: (b, 0)),
        out_shape=jax.ShapeDtypeStruct((NP, D), BF16),
        compiler_params=_cparams(("arbitrary",)),
        name="attn_ctx",
    )(qkv, qkv, qkv)


def _attn_lat_kernel(q_ref, k_ref, v_ref, kc_ref, vc_ref, strip_ref, rowmask_ref, o_ref, *, tq):
    rows = DEC_SEQ // GRID_W
    kr = min(WIN_R, rows)
    rpt = tq // GRID_W

    def bias_rows(u, r, k_lo, k_hi):
        first = rows - 1 - r + k_lo
        var = first % 2
        strip = strip_ref[u, var, :, (first - var) * GRID_W:(first - var + k_hi - k_lo) * GRID_W]
        return strip + rowmask_ref[r:r + 1, k_lo * GRID_W:k_hi * GRID_W]

    upper = _head_pair_masks()
    mine = (jnp.logical_not(upper), upper)
    kl = k_ref[...].astype(BF16)
    kc = kc_ref[0].astype(BF16)
    vl = [jnp.where(mine[u], v_ref[...], 1.0).astype(BF16) for u in range(2)]
    vc = [jnp.where(mine[u], vc_ref[0], 1.0).astype(BF16) for u in range(2)]
    for t in range(DEC_SEQ // tq):
        rs = slice(t * tq, (t + 1) * tq)
        k_lo = min(max(t * rpt - kr // 2, 0), rows - kr) // 2 * 2
        k_hi = -(-(min(max(t * rpt + rpt - 1 - kr // 2, 0), rows - kr) + kr) // 2) * 2
        ks = slice(k_lo * GRID_W, k_hi * GRID_W)
        qp = q_ref[rs, :] * (NA_HD ** -0.5)
        outs = []
        for u in range(2):
            qm = jnp.where(mine[u], qp, 0.0).astype(BF16)
            bias = jnp.concatenate([bias_rows(u, t * rpt + a, k_lo, k_hi) for a in range(rpt)], axis=0)
            s_l = _dot_nt(qm, kl[ks]) + bias
            s_c = _dot_nt(qm, kc)
            m = jnp.maximum(jnp.max(s_l, axis=-1, keepdims=True), jnp.max(s_c, axis=-1, keepdims=True))
            acc = (_dot(jnp.exp(s_l - m).astype(BF16), vl[u][ks])
                   + _dot(jnp.exp(s_c - m).astype(BF16), vc[u]))
            outs.append(acc / pltpu.roll(acc, NA_HD, 1))
        o_ref[rs, :] = jnp.where(upper, outs[1], outs[0]).astype(o_ref.dtype)


def _attn_lat(qkv, k_ctx, v_ctx, strips, rowmask):
    npair = NA_HEADS // 2
    return pl.pallas_call(
        functools.partial(_attn_lat_kernel, tq=256),
        grid=(npair, DEC_BATCH),
        in_specs=[
            pl.BlockSpec((DEC_SEQ, LANES), lambda p, b: (b, p)),
            pl.BlockSpec((DEC_SEQ, LANES), lambda p, b: (b, npair + p)),
            pl.BlockSpec((DEC_SEQ, LANES), lambda p, b: (b, 2 * npair + p)),
            pl.BlockSpec((1, PAST_LEN, LANES), lambda p, b: (b, 0, p)),
            pl.BlockSpec((1, PAST_LEN, LANES), lambda p, b: (b, 0, p)),
            pl.BlockSpec((2, 2, GRID_W, STRIP_W), lambda p, b: (p, 0, 0, 0)),
            pl.BlockSpec((DEC_SEQ // GRID_W, DEC_SEQ), lambda p, b: (0, 0)),
        ],
        out_specs=pl.BlockSpec((DEC_SEQ, LANES), lambda p, b: (b, p)),
        out_shape=jax.ShapeDtypeStruct((NS, D), BF16),
        compiler_params=_cparams(("arbitrary", "arbitrary")),
        name="attn_lat",
    )(qkv, qkv, qkv, k_ctx, v_ctx, strips, rowmask)


def _latent_bias_tables(rpb):
    rows = DEC_SEQ // GRID_W
    kr = min(WIN_R, rows)
    ndr, ndc = 2 * WIN_R - 1, 2 * WIN_C - 1
    qc = np.arange(GRID_W)
    kc = np.arange(GRID_W)
    ws = np.clip(qc - WIN_C // 2, 0, GRID_W - WIN_C)
    col_ok = (kc[None, :] >= ws[:, None]) & (kc[None, :] < ws[:, None] + WIN_C)
    dc = np.clip(kc[None, :] - qc[:, None] + WIN_C - 1, 0, ndc - 1)
    onehot = (dc[None, :, :] == np.arange(ndc)[:, None, None]).astype(np.float32)
    t2 = jnp.einsum('hrc,cqk->hqrk', rpb.astype(F32), jnp.asarray(onehot),
                    precision=lax.Precision.HIGHEST)
    t2 = jnp.where(jnp.asarray(col_ok)[None, :, None, :], t2, NEG)
    lead = rows - WIN_R
    n_tiles = 2 * rows - 1

    def neg(n):
        return jnp.full((NA_HEADS, GRID_W, n, GRID_W), NEG, F32)

    strip = jnp.concatenate([neg(lead), t2, neg(n_tiles - lead - ndr)], axis=2)
    strip = strip.reshape(NA_HEADS, GRID_W, n_tiles * GRID_W)

    def pad(a):
        return jnp.pad(a, ((0, 0), (0, 0), (0, STRIP_W - a.shape[-1])), constant_values=NEG)

    strips = jnp.stack([pad(strip), pad(strip[:, :, GRID_W:])], axis=1)
    r = np.arange(rows)
    k0 = np.clip(r - kr // 2, 0, rows - kr)
    krow = np.arange(DEC_SEQ) // GRID_W
    row_ok = (krow[None, :] >= k0[:, None]) & (krow[None, :] < k0[:, None] + kr)
    rowmask = jnp.asarray(np.where(row_ok, 0.0, NEG).astype(np.float32))
    return strips, rowmask


def _hgrn_mix_out(refs, mod_ref, tile):
    of_ref, ob_ref, gate_ref, xp_ref, xs_ref, gn_ref, w_ref = refs
    x = jnp.where(tile < NP // MOE_TM, xp_ref[...], xs_ref[...])
    o = of_ref[...] + ob_ref[...]
    gn = gn_ref[...]
    segs = []
    for h in range(HG_HEADS):
        seg = o[:, h * LANES:(h + 1) * LANES]
        ms = jnp.mean(seg * seg, axis=-1, keepdims=True)
        segs.append(seg * lax.rsqrt(ms + EPS) * gn)
    y = (jnp.concatenate(segs, axis=1) * _silu(gate_ref[...])).astype(BF16)
    return x + mod_ref[0, 2:3, :] * _dot(y, w_ref[...])


def _attn_mix_out(refs, mod_ref, tile):
    ap_ref, as_ref, x_ref, w_ref = refs
    a = jnp.where(tile < NP // MOE_TM, ap_ref[...], as_ref[...])
    return x_ref[...] + mod_ref[0, 2:3, :] * _dot(a, w_ref[...])


def _route_kernel(*refs, mix_out, n_mix):
    mix_refs = refs[:n_mix]
    (nw_ref, mod_ref, wrh_ref, wrl_ref, rb_ref, xnew_ref, xs_ref, pos_ref, wt_ref, npc_ref,
     hb_s, pos_s, used_s) = refs[n_mix:]
    i = pl.program_id(0)
    cur = lax.rem(i, 2)
    tile = jnp.minimum(i, N_TILES - 1)

    @pl.when(i == 0)
    def _():
        hb_s[...] = jnp.zeros_like(hb_s)
        pos_s[...] = jnp.full(pos_s.shape, -1, I32)
        used_s[0] = 0.0
        used_s[1] = 0.0

    live = i < N_TILES

    def step(slot):
        x = mix_out(mix_refs, mod_ref, tile)
        xnew_ref[...] = x
        hb, pos, used = _route_select(x, nw_ref, mod_ref, wrh_ref, wrl_ref, rb_ref, pos_ref, wt_ref, npc_ref)
        hb_s[slot] = hb
        pos_s[slot] = jnp.where(live, pos, -1)
        used_s[slot] = jnp.where(live, used, 0.0)
        _route_sort(xs_ref, hb_s[1 - slot], pos_s[1 - slot], used_s[1 - slot])

    for slot in range(2):
        pl.when(cur == slot)(functools.partial(step, slot))


def _route_sort(xs_ref, hb, pos, used):
    pos16 = [pos[k:k + 1].astype(jnp.int16) for k in range(TOP_K)]

    def fill(r0, n):
        j16 = (lax.broadcasted_iota(I32, (n, MOE_TM), 0) + r0).astype(jnp.int16)
        onehot = jnp.zeros((n, MOE_TM), BF16)
        for k in range(TOP_K):
            onehot = jnp.where(j16 == pos16[k], jnp.ones((), BF16), onehot)
        xs_ref[r0:r0 + n, :] = _dot(onehot, hb).astype(BF16)

    fill(0, ROWS_MAIN)
    for r0 in range(ROWS_MAIN, R1, ROW_BLK):
        pl.when(used > r0)(functools.partial(fill, r0, ROW_BLK))

        @pl.when(used <= r0)
        def _(r0=r0):
            xs_ref[r0:r0 + ROW_BLK, :] = jnp.zeros((ROW_BLK, D), BF16)


def _route_select(x, nw_ref, mod_ref, wrh_ref, wrl_ref, rb_ref, pos_ref, wt_ref, npc_ref):
    h = _norm_mod(x, nw_ref[...], mod_ref[0, 3:4, :], mod_ref[0, 4:5, :])
    hb = h.astype(BF16)
    hl = (h - hb.astype(F32)).astype(BF16)
    wrh = wrh_ref[...]
    logits = _dot_nt(wrh, hb) + _dot_nt(wrh, hl) + _dot_nt(wrl_ref[...], hb)
    scores = jax.nn.sigmoid(logits)
    sel = scores + rb_ref[...]

    gsz = N_EXPERTS // N_GROUPS
    sub = lax.broadcasted_iota(I32, (gsz, MOE_TM), 0)
    ninf = -jnp.inf
    gs_rows = []
    for gi in range(N_GROUPS):
        blk = sel[gi * gsz:(gi + 1) * gsz]
        m1 = jnp.max(blk, axis=0, keepdims=True)
        first = jnp.min(jnp.where(blk == m1, sub, gsz), axis=0, keepdims=True)
        m2 = jnp.max(jnp.where(sub == first, ninf, blk), axis=0, keepdims=True)
        gs_rows.append(m1 + m2)
    cur = jnp.concatenate(gs_rows, axis=0)
    gidx = lax.broadcasted_iota(I32, (N_GROUPS, MOE_TM), 0)
    gsel = jnp.zeros((N_GROUPS, MOE_TM), F32)
    for _ in range(TOPK_GROUPS):
        m = jnp.max(cur, axis=0, keepdims=True)
        first = jnp.min(jnp.where(cur == m, gidx, N_GROUPS), axis=0, keepdims=True)
        hit = gidx == first
        gsel = jnp.where(hit, 1.0, gsel)
        cur = jnp.where(hit, ninf, cur)
    emask = jnp.concatenate(
        [jnp.broadcast_to(gsel[gi:gi + 1], (gsz, MOE_TM)) for gi in range(N_GROUPS)], axis=0)
    masked = jnp.where(emask > 0.5, sel, ninf)
    eidx = lax.broadcasted_iota(I32, (N_EXPERTS, MOE_TM), 0)
    chosen = jnp.zeros((N_EXPERTS, MOE_TM), F32)
    hits, wsel = [], []
    for _ in range(TOP_K):
        m = jnp.max(masked, axis=0, keepdims=True)
        first = jnp.min(jnp.where(masked == m, eidx, N_EXPERTS), axis=0, keepdims=True)
        hit = eidx == first
        hits.append(hit)
        wsel.append(jnp.sum(jnp.where(hit, scores, 0.0), axis=0, keepdims=True))
        chosen = jnp.where(hit, 1.0, chosen)
        masked = jnp.where(hit, ninf, masked)
    wsum = wsel[0]
    for w in wsel[1:]:
        wsum = wsum + w

    n_io = lax.broadcasted_iota(I32, (MOE_TM, MOE_TM), 0)
    m_io = lax.broadcasted_iota(I32, (MOE_TM, MOE_TM), 1)
    earlier = jnp.where(n_io < m_io, 1.0, 0.0).astype(BF16)
    rank = _dot(chosen.astype(BF16), earlier)
    cnt = jnp.sum(chosen, axis=1, keepdims=True)
    npc = jnp.floor((cnt + (PIECE - 1)) * (1.0 / PIECE))
    e_io = lax.broadcasted_iota(I32, (N_EXPERTS, N_EXPERTS), 0)
    f_io = lax.broadcasted_iota(I32, (N_EXPERTS, N_EXPERTS), 1)
    below = jnp.where(f_io < e_io, 1.0, 0.0).astype(BF16)
    npc_l = jnp.broadcast_to(npc, (N_EXPERTS, LANES))
    start = _dot(below, npc_l.astype(BF16))[:, 0:1] * PIECE
    slot = start + rank

    pos_rows, wt_rows = [], []
    for k in range(TOP_K):
        pos_k = jnp.sum(jnp.where(hits[k], slot, 0.0), axis=0, keepdims=True).astype(I32)
        pos_rows.append(pos_k)
        wt_rows.append(wsel[k] / wsum * ROUTED_SCALE)

    pad = SUBLANES - TOP_K
    pos = jnp.concatenate(pos_rows + [jnp.full((pad, MOE_TM), -1, I32)], axis=0)
    pos_ref[...] = pos
    wt_ref[...] = jnp.concatenate(wt_rows + [jnp.zeros((pad, MOE_TM), F32)], axis=0)
    npc_ref[0] = npc_l.astype(I32)
    return hb, pos, jnp.sum(npc) * PIECE


def _hgrn_mix_inputs(o_f, o_b, proj, x_pair, gn_w, w_out):
    n_p = NP // MOE_TM

    def specs(tile):
        return [pl.BlockSpec((MOE_TM, D), lambda i: (tile(i), 0)),
                pl.BlockSpec((MOE_TM, D), lambda i: (tile(i), 0)),
                pl.BlockSpec((MOE_TM, D), lambda i: (tile(i), 4)),
                pl.BlockSpec((MOE_TM, D), lambda i: (jnp.minimum(tile(i), n_p - 1), 0)),
                pl.BlockSpec((MOE_TM, D), lambda i: (jnp.maximum(tile(i) - n_p, 0), 0)),
                pl.BlockSpec((1, HG_DV), lambda i: (0, 0)),
                pl.BlockSpec((D, D), lambda i: (0, 0))]

    return _hgrn_mix_out, (o_f, o_b, proj, x_pair[0], x_pair[1], gn_w.reshape(1, HG_DV), w_out), specs


def _attn_mix_inputs(a_p, a_s, x, w_o):
    n_p = NP // MOE_TM

    def specs(tile):
        return [pl.BlockSpec((MOE_TM, D), lambda i: (jnp.minimum(tile(i), n_p - 1), 0)),
                pl.BlockSpec((MOE_TM, D), lambda i: (jnp.maximum(tile(i) - n_p, 0), 0)),
                pl.BlockSpec((MOE_TM, D), lambda i: (tile(i), 0)),
                pl.BlockSpec((D, D), lambda i: (0, 0))]

    return _attn_mix_out, (a_p, a_s, x, w_o), specs


def _route(mix, nw, mod, wr_hi, wr_lo, rbias):
    mix_out, mix_args, mix_specs = mix

    def tile(i):
        return jnp.minimum(i, N_TILES - 1)

    return pl.pallas_call(
        functools.partial(_route_kernel, mix_out=mix_out, n_mix=len(mix_args)),
        grid=(N_TILES + 2,),
        in_specs=mix_specs(tile) + [
            pl.BlockSpec((1, D), lambda i: (0, 0)),
            pl.BlockSpec((1, 6, D), lambda i: (_mod_group(tile(i), MOE_TM), 0, 0)),
            pl.BlockSpec((N_EXPERTS, D), lambda i: (0, 0)),
            pl.BlockSpec((N_EXPERTS, D), lambda i: (0, 0)),
            pl.BlockSpec((N_EXPERTS, 1), lambda i: (0, 0)),
        ],
        out_specs=[
            pl.BlockSpec((MOE_TM, D), lambda i: (tile(i), 0)),
            pl.BlockSpec((R1, D), lambda i: (jnp.maximum(i - 1, 0), 0)),
            pl.BlockSpec((SUBLANES, MOE_TM), lambda i: (0, tile(i))),
            pl.BlockSpec((SUBLANES, MOE_TM), lambda i: (0, tile(i))),
            pl.BlockSpec((1, N_EXPERTS, LANES), lambda i: (tile(i), 0, 0)),
        ],
        out_shape=[
            jax.ShapeDtypeStruct((NT, D), F32),
            jax.ShapeDtypeStruct((P_TOT * PIECE, D), BF16),
            jax.ShapeDtypeStruct((SUBLANES, NT), I32),
            jax.ShapeDtypeStruct((SUBLANES, NT), F32),
            jax.ShapeDtypeStruct((N_TILES, N_EXPERTS, LANES), I32),
        ],
        scratch_shapes=[pltpu.VMEM((2, MOE_TM, D), BF16), pltpu.VMEM((2, SUBLANES, MOE_TM), I32),
                        pltpu.SMEM((2,), F32)],
        compiler_params=_cparams(("arbitrary",)),
        name="moe_route",
    )(*mix_args, nw.reshape(1, D), mod, wr_hi, wr_lo, rbias.reshape(N_EXPERTS, 1))


def _piece_lists(npc):
    t, e = npc.shape
    hp = lax.Precision.HIGHEST
    npc_t = npc.T.astype(F32)
    start_t = (jnp.cumsum(npc, axis=1) - npc).T.astype(F32)
    tile_end = jnp.cumsum(npc_t, axis=1)
    n_e = tile_end[:, -1]
    pe_end = jnp.cumsum(n_e)
    pe_off = pe_end - n_e
    p = jnp.arange(P_MAX, dtype=F32)
    e_p = jnp.minimum(jnp.sum((pe_end[None, :] <= p[:, None]).astype(I32), axis=1), e - 1)
    oh_e = (e_p[:, None] == jnp.arange(e, dtype=I32)[None, :]).astype(F32)
    tab = jnp.concatenate([tile_end, start_t, npc_t, pe_off[:, None]], axis=1)
    row = jnp.dot(oh_e, tab, precision=hp)
    te_p, st_p, np_p, off_p = row[:, :t], row[:, t:2 * t], row[:, 2 * t:3 * t], row[:, 3 * t]
    local = p - off_p
    t_p = jnp.minimum(jnp.sum((te_p <= local[:, None]).astype(I32), axis=1), t - 1)
    oh_t = t_p[:, None] == jnp.arange(t, dtype=I32)[None, :]

    def pick(a):
        return jnp.sum(jnp.where(oh_t, a, 0.0), axis=1)

    src = t_p * R1B + (pick(st_p) + local - (pick(te_p) - pick(np_p))).astype(I32)
    src = jnp.concatenate([jnp.clip(src, 0, P_MAX - 1), jnp.zeros((G_PIECES,), I32)])

    nch = jnp.floor((n_e + (G_PIECES - 1)) * (1.0 / G_PIECES))
    ch_end = jnp.cumsum(nch)
    ch_off = jnp.concatenate([jnp.zeros((1,), F32), ch_end]).astype(I32)
    c = jnp.arange(-N_SLOTS, NCH + 2, dtype=F32)
    ce = jnp.minimum(jnp.sum((ch_end[None, :] <= c[:, None]).astype(I32), axis=1), e - 1)
    oh_c = (ce[:, None] == jnp.arange(e, dtype=I32)[None, :]).astype(F32)
    crow = jnp.dot(oh_c, jnp.stack([ch_end - nch, pe_off, n_e], axis=1), precision=hp)
    k_in = c - crow[:, 0]
    live = jnp.logical_and(c >= 0, c < ch_end[-1])
    cn = jnp.where(live, jnp.clip(crow[:, 2] - G_PIECES * k_in, 0, G_PIECES), 0.0)
    cs = jnp.where(cn > 0, crow[:, 1] + G_PIECES * k_in, 0.0)
    return src, ch_off, cs.astype(I32), cn.astype(I32)


def _ffn_kernel(src_ref, choff_ref, cs_ref, cn_ref, xs_in, wg_ref, wu_ref, wd_ref, xs_out,
                xbuf, ybuf, wgb, wub, wdb, gsem, ssem):
    e = pl.program_id(0)
    total = choff_ref[N_EXPERTS]

    def start_gather(ch):
        sl = lax.rem(ch + N_SLOTS, N_SLOTS)
        base = cs_ref[ch + N_SLOTS]
        n = cn_ref[ch + N_SLOTS]
        for i in range(G_PIECES):
            idx = src_ref[base + jnp.where(i < n, i, 0)]
            pltpu.make_async_copy(xs_in.at[idx], xbuf.at[sl, i], gsem.at[sl]).start(priority=i % 2)

    def wait_gather(ch):
        sl = lax.rem(ch + N_SLOTS, N_SLOTS)
        pltpu.make_async_copy(xs_in.at[pl.ds(0, G_PIECES)], xbuf.at[sl], gsem.at[sl]).wait()

    def start_scatter(ch):
        sl = lax.rem(ch + N_SLOTS, N_SLOTS)
        base = cs_ref[ch + N_SLOTS]
        n = cn_ref[ch + N_SLOTS]
        for i in range(G_PIECES):
            idx = jnp.where(i < n, src_ref[base + i], P_MAX + sl * G_PIECES + i)
            pltpu.make_async_copy(ybuf.at[sl, i], xs_out.at[idx], ssem.at[sl]).start(priority=i % 2)

    def wait_scatter(ch):
        sl = lax.rem(ch + N_SLOTS, N_SLOTS)
        pltpu.make_async_copy(ybuf.at[sl], xs_out.at[pl.ds(0, G_PIECES)], ssem.at[sl]).wait()

    @pl.when(e == 0)
    def _():
        ybuf[...] = jnp.zeros_like(ybuf)
        start_gather(0)
        start_gather(1)
        start_scatter(-3)
        start_scatter(-2)

    wgb[...] = wg_ref[0].astype(BF16)
    wub[...] = wu_ref[0].astype(BF16)
    wdb[...] = wd_ref[0].astype(BF16)

    def chunk(c, carry):
        sl = lax.rem(c, N_SLOTS)
        wait_gather(c)
        wait_scatter(c - 3)
        x = xbuf[sl].reshape(G_PIECES * PIECE, D)
        hid = (_silu(_dot(x, wgb[...])) * _dot(x, wub[...])).astype(BF16)
        start_gather(c + 2)
        start_scatter(c - 1)
        ybuf[sl] = _dot(hid, wdb[...]).astype(BF16).reshape(G_PIECES, PIECE, D)
        return carry

    lax.fori_loop(choff_ref[e], choff_ref[e + 1], chunk, 0)

    @pl.when(e == N_EXPERTS - 1)
    def _():
        start_scatter(total - 1)
        wait_gather(total)
        wait_gather(total + 1)
        wait_scatter(total - 3)
        wait_scatter(total - 2)
        wait_scatter(total - 1)


def _expert_ffn(xs, lists, layer, w_gate, w_up, w_down):
    src, ch_off, cs, cn = lists
    grid_spec = pltpu.PrefetchScalarGridSpec(
        num_scalar_prefetch=4,
        grid=(N_EXPERTS,),
        in_specs=[
            pl.BlockSpec(memory_space=pl.ANY),
            pl.BlockSpec((None, 1, D, D_EXPERT), lambda e, *_: (layer, e, 0, 0)),
            pl.BlockSpec((None, 1, D, D_EXPERT), lambda e, *_: (layer, e, 0, 0)),
            pl.BlockSpec((None, 1, D_EXPERT, D), lambda e, *_: (layer, e, 0, 0)),
        ],
        out_specs=pl.BlockSpec(memory_space=pl.ANY),
        scratch_shapes=[
            pltpu.VMEM((N_SLOTS, G_PIECES, PIECE, D), BF16),
            pltpu.VMEM((N_SLOTS, G_PIECES, PIECE, D), BF16),
            pltpu.VMEM((D, D_EXPERT), BF16),
            pltpu.VMEM((D, D_EXPERT), BF16),
            pltpu.VMEM((D_EXPERT, D), BF16),
            pltpu.SemaphoreType.DMA((N_SLOTS,)),
            pltpu.SemaphoreType.DMA((N_SLOTS,)),
        ],
    )
    out = pl.pallas_call(
        _ffn_kernel,
        grid_spec=grid_spec,
        out_shape=jax.ShapeDtypeStruct((P_TOT, PIECE, D), BF16),
        input_output_aliases={4: 0},
        compiler_params=_cparams(("arbitrary",)),
        name="moe_ffn",
    )(src, ch_off, cs, cn, xs.reshape(P_TOT, PIECE, D), w_gate, w_up, w_down)
    return out.reshape(P_TOT * PIECE, D)


def _combine_kernel(used_ref, ys_ref, pos_ref, wt_ref, x_ref, nw_ref, mod_ref, wsg_ref, wsu_ref, wsd_ref,
                    fw_ref, *rest, final):
    o_refs, acc_ref = rest[:-1], rest[-1]
    i = pl.program_id(0)
    x = x_ref[...]
    h = _norm_mod(x, nw_ref[...], mod_ref[0, 3:4, :], mod_ref[0, 4:5, :]).astype(BF16)
    shared = _dot((_silu(_dot(h, wsg_ref[...])) * _dot(h, wsu_ref[...])).astype(BF16), wsd_ref[...])

    pos16 = pos_ref[...].astype(jnp.int16)
    wt16 = wt_ref[...].astype(BF16)

    def block(r0, n):
        j16 = (lax.broadcasted_iota(I32, (MOE_TM, n), 1) + r0).astype(jnp.int16)
        wm = jnp.zeros((MOE_TM, n), BF16)
        for k in range(TOP_K):
            wm = jnp.where(j16 == pos16[:, k:k + 1], wt16[:, k:k + 1], wm)
        return _dot(wm, ys_ref[r0:r0 + n, :])

    acc_ref[...] = block(0, ROWS_MAIN)
    for r0 in range(ROWS_MAIN, R1, ROW_BLK):
        @pl.when(used_ref[i] > r0)
        def _(r0=r0):
            acc_ref[...] += block(r0, ROW_BLK)

    y = x + mod_ref[0, 5:6, :] * (acc_ref[...] + shared)
    if not final:
        o_refs[0][...] = y
        return
    ms = jnp.mean(y * y, axis=-1, keepdims=True)
    y = y * lax.rsqrt(ms + EPS) * fw_ref[...]
    is_prompt = i < NP // MOE_TM

    @pl.when(is_prompt)
    def _():
        o_refs[0][...] = y

    @pl.when(jnp.logical_not(is_prompt))
    def _():
        o_refs[1][...] = y


def _combine(ys, used, pos_t, wt_t, x, nw, mod, wsg, wsu, wsd, fw, final):
    n_p = NP // MOE_TM
    if final:
        out_specs = [pl.BlockSpec((MOE_TM, D), lambda i, u: (jnp.minimum(i, n_p - 1), 0)),
                     pl.BlockSpec((MOE_TM, D), lambda i, u: (jnp.maximum(i - n_p, 0), 0))]
        out_shape = [jax.ShapeDtypeStruct((NP, D), F32), jax.ShapeDtypeStruct((NS, D), F32)]
    else:
        out_specs = pl.BlockSpec((MOE_TM, D), lambda i, u: (i, 0))
        out_shape = jax.ShapeDtypeStruct((NT, D), F32)
    grid_spec = pltpu.PrefetchScalarGridSpec(
        num_scalar_prefetch=1,
        grid=(N_TILES,),
        in_specs=[
            pl.BlockSpec((R1, D), lambda i, u: (i, 0)),
            pl.BlockSpec((MOE_TM, SUBLANES), lambda i, u: (i, 0)),
            pl.BlockSpec((MOE_TM, SUBLANES), lambda i, u: (i, 0)),
            pl.BlockSpec((MOE_TM, D), lambda i, u: (i, 0)),
            pl.BlockSpec((1, D), lambda i, u: (0, 0)),
            pl.BlockSpec((1, 6, D), lambda i, u: (_mod_group(i, MOE_TM), 0, 0)),
            pl.BlockSpec((D, D_SHARED), lambda i, u: (0, 0)),
            pl.BlockSpec((D, D_SHARED), lambda i, u: (0, 0)),
            pl.BlockSpec((D_SHARED, D), lambda i, u: (0, 0)),
            pl.BlockSpec((1, D), lambda i, u: (0, 0)),
        ],
        out_specs=out_specs,
        scratch_shapes=[pltpu.VMEM((MOE_TM, D), F32)],
    )
    return pl.pallas_call(
        functools.partial(_combine_kernel, final=final),
        grid_spec=grid_spec,
        out_shape=out_shape,
        compiler_params=_cparams(("arbitrary",)),
        name="moe_combine",
    )(used, ys, pos_t, wt_t, x, nw.reshape(1, D), mod, wsg, wsu, wsd, fw.reshape(1, D))


def _moe(mix, nw, mod, w_router, rbias, layer, w_gate, w_up, w_down, ws_gate, ws_up, ws_down, fw, final):
    wr = w_router.astype(F32).T
    wr_hi = wr.astype(BF16)
    wr_lo = (wr - wr_hi.astype(F32)).astype(BF16)
    x, xs, pos, wt, npc = _route(mix, nw, mod, wr_hi, wr_lo, rbias.astype(F32))
    npc = npc[:, :, 0]
    lists = _piece_lists(npc)
    ys = _expert_ffn(xs, lists, layer, w_gate, w_up, w_down)
    used = (npc.sum(axis=1) * PIECE).astype(I32)
    return _combine(ys, used, pos.T, wt.T, x, nw, mod, ws_gate.astype(BF16), ws_up.astype(BF16),
                    ws_down.astype(BF16), fw, final)


def kernel(x_prompt, x_sample, state_hgrn_fwd, state_hgrn_bwd, cache_na_k, cache_na_v, c, c_ctx,
           norm1_w, norm2_w, ada_w, ada_b, hgrn_w_in, hgrn_lb_logits, hgrn_gn_w, hgrn_w_out,
           na_w_qkv, na_rpb, na_w_o, moe_w_router, moe_router_bias, moe_w_gate, moe_w_up, moe_w_down,
           shared_w_gate, shared_w_up, shared_w_down, final_norm_w):
    x = (x_prompt.reshape(NP, D), x_sample.reshape(NS, D))
    cvec = jnp.concatenate([c_ctx[None, :], c, jnp.zeros((N_MOD - 1 - DEC_BATCH, D), F32)], axis=0)
    mod = _modulation(cvec, ada_w, ada_b)
    lb_table = jnp.cumsum(jax.nn.softmax(hgrn_lb_logits.astype(F32), axis=0), axis=0)
    hk = HG_HEADS * HG_DK

    sf = sb = k_c = v_c = None
    for l in range(DEPTH):
        if l % 2 == 0:
            a = l // 2
            proj, chunk_tot = _hgrn_proj(x, norm1_w[l], mod[l], lb_table[l], hgrn_w_in[a].astype(BF16))
            o_f, o_b, sf, sb = _gla(proj, chunk_tot,
                                    state_hgrn_fwd[:, a].reshape(DEC_BATCH, hk, HG_DV),
                                    state_hgrn_bwd[:, a].reshape(DEC_BATCH, hk, HG_DV))
            mix = _hgrn_mix_inputs(o_f, o_b, proj, x, hgrn_gn_w[a], hgrn_w_out[a].astype(BF16))
        else:
            n = l // 2
            w_qkv = na_w_qkv[n].astype(BF16)
            qkv_p, k_p, v_p = _norm_proj(x, norm1_w[l], mod[l], w_qkv, 0, NP // TM, n_copy=2, out_dtype=BF16)
            qkv_s = _norm_proj(x, norm1_w[l], mod[l], w_qkv, NP // TM, NS // TM, out_dtype=BF16)
            att_p = _attn_ctx(qkv_p)
            att_s = _attn_lat(qkv_s, cache_na_k[:, n].reshape(DEC_BATCH, PAST_LEN, D),
                              cache_na_v[:, n].reshape(DEC_BATCH, PAST_LEN, D),
                              *_latent_bias_tables(na_rpb[n]))
            mix = _attn_mix_inputs(att_p, att_s, x, na_w_o[n].astype(BF16))
            k_c = k_p.reshape(BATCH, SEQ, NA_HEADS, NA_HD)
            v_c = v_p.reshape(BATCH, SEQ, NA_HEADS, NA_HD)
        x = _moe(mix, norm2_w[l], mod[l], moe_w_router[l], moe_router_bias[l], l, moe_w_gate, moe_w_up,
                 moe_w_down, shared_w_gate[l], shared_w_up[l], shared_w_down[l], final_norm_w,
                 final=(l == DEPTH - 1))

    y_prompt = x[0].reshape(BATCH, SEQ, D)
    y_sample = x[1].reshape(DEC_BATCH, DEC_SEQ, D)
    new_sf = sf.reshape(BATCH, 1, HG_HEADS, HG_DK, HG_DV)
    new_sb = sb.reshape(BATCH, 1, HG_HEADS, HG_DK, HG_DV)
    return (y_prompt, y_sample, new_sf, new_sb, k_c[:, None], v_c[:, None])
```

```python
import functools

import numpy as np
import jax
import jax.numpy as jnp
from jax import lax
from jax.experimental import pallas as pl
from jax.experimental.pallas import tpu as pltpu

F32 = jnp.float32
BF16 = jnp.bfloat16
I32 = jnp.int32

D = 1024
BATCH = 32
SEQ = 256
DEPTH = 2
DEC_BATCH = 8
DEC_SEQ = 1024
PAST_LEN = 512
GRID_W = 64
HG_HEADS = 8
HG_DK = 128
HG_DV = 128
CHUNK = 64
NA_HEADS = 16
NA_HD = 64
WIN_R = 8
WIN_C = 16
N_EXPERTS = 64
TOP_K = 6
N_GROUPS = 8
TOPK_GROUPS = 4
D_EXPERT = 256
D_SHARED = 256
ROUTED_SCALE = 2.5
EPS = 1e-6

NP = BATCH * SEQ
NS = DEC_BATCH * DEC_SEQ
NT = NP + NS
N_MOD = 16
STRIP_W = 2048
assert (2 * (DEC_SEQ // GRID_W) - 1) * GRID_W <= STRIP_W
NEG = -1e30

LANES = 128
SUBLANES = 8
BF16_ROWS = 16
VMEM_LIMIT = 56 * 1024 * 1024

TM = 256
SUB = 16
GLA_FAST_MAX = 60.0
GLA_CPS = 4
GLA_BLK = GLA_CPS * CHUNK
assert SEQ % GLA_BLK == 0 and DEC_SEQ % GLA_BLK == 0
MOE_TM = 256
PIECE = BF16_ROWS
R1 = MOE_TM * TOP_K + N_EXPERTS * (PIECE - 1) + 64
assert R1 % PIECE == 0 and R1 % LANES == 0
R1B = R1 // PIECE
ROWS_MAIN = 2048
ROW_BLK = 256
assert (R1 - ROWS_MAIN) % ROW_BLK == 0
N_TILES = NT // MOE_TM
P_MAX = N_TILES * R1B
G_PIECES = 32
NCH = P_MAX // G_PIECES + N_EXPERTS
N_SLOTS = 3
P_TOT = P_MAX + R1B
assert N_SLOTS * G_PIECES <= R1B


def _cparams(sem):
    return pltpu.CompilerParams(dimension_semantics=sem, vmem_limit_bytes=VMEM_LIMIT)


def _dot(a, b):
    return jnp.dot(a, b, preferred_element_type=F32)


def _dot_nt(a, b):
    return lax.dot_general(a, b, (((1,), (1,)), ((), ())), preferred_element_type=F32)


def _dot_tn(a, b):
    return lax.dot_general(a, b, (((0,), (0,)), ((), ())), preferred_element_type=F32)


def _silu(x):
    return x * jax.nn.sigmoid(x)


def _mod_group(i, tm):
    r = i * tm
    return jnp.where(r < NP, 0, 1 + (r - NP) // DEC_SEQ)


def _norm_mod(x, nw, shift, scale):
    ms = jnp.mean(x * x, axis=-1, keepdims=True)
    y = x * lax.rsqrt(ms + EPS) * nw
    return y * (1.0 + scale) + shift


def _mod_kernel(c_ref, w_ref, b_ref, o_ref):
    s = _silu(c_ref[...]).astype(BF16)
    o_ref[0] = _dot(s, w_ref[0].astype(BF16)) + b_ref[0]


def _modulation(cvec, ada_w, ada_b):
    cw = 1536
    n = ada_w.shape[-1]
    out = pl.pallas_call(
        _mod_kernel,
        grid=(DEPTH, n // cw),
        in_specs=[
            pl.BlockSpec((N_MOD, D), lambda l, j: (0, 0)),
            pl.BlockSpec((1, D, cw), lambda l, j: (l, 0, j)),
            pl.BlockSpec((1, 1, cw), lambda l, j: (l, 0, j)),
        ],
        out_specs=pl.BlockSpec((1, N_MOD, cw), lambda l, j: (l, 0, j)),
        out_shape=jax.ShapeDtypeStruct((DEPTH, N_MOD, n), F32),
        compiler_params=_cparams(("arbitrary", "arbitrary")),
        name="modulation",
    )(cvec, ada_w, ada_b.reshape(DEPTH, 1, n))
    return out.reshape(DEPTH, N_MOD, 6, D)


def _stream_specs(x, tm):
    if not isinstance(x, tuple):
        return [pl.BlockSpec((tm, D), lambda i: (i, 0))], (x,)
    n_p = NP // tm
    return [pl.BlockSpec((tm, D), lambda i: (jnp.minimum(i, n_p - 1), 0)),
            pl.BlockSpec((tm, D), lambda i: (jnp.maximum(i - n_p, 0), 0))], x


def _stream_tile(refs, tm):
    if len(refs) == 1:
        return refs[0][...]
    return jnp.where(pl.program_id(0) < NP // tm, refs[0][...], refs[1][...])


def _proj_kernel(x_ref, nw_ref, mod_ref, w_ref, o_ref, *copy_refs, cw):
    h = _norm_mod(x_ref[...], nw_ref[...], mod_ref[0, 0:1, :], mod_ref[0, 1:2, :]).astype(BF16)
    for j in range(w_ref.shape[1] // cw):
        val = _dot(h, w_ref[:, j * cw:(j + 1) * cw])
        o_ref[:, j * cw:(j + 1) * cw] = val.astype(o_ref.dtype)
        col = j * cw - D
        if 0 <= col < len(copy_refs) * D:
            copy_refs[col // D][:, col % D:col % D + cw] = val


def _norm_proj(x, nw, mod, w, tile0, n_tiles, n_copy=0, out_dtype=F32):
    n = w.shape[1]
    rows = n_tiles * TM
    outs = pl.pallas_call(
        functools.partial(_proj_kernel, cw=512),
        grid=(n_tiles,),
        in_specs=[
            pl.BlockSpec((TM, D), lambda i: (tile0 + i, 0)),
            pl.BlockSpec((1, D), lambda i: (0, 0)),
            pl.BlockSpec((1, 6, D), lambda i: (_mod_group(tile0 + i, TM), 0, 0)),
            pl.BlockSpec((D, n), lambda i: (0, 0)),
        ],
        out_specs=[pl.BlockSpec((TM, n), lambda i: (i, 0))] + [pl.BlockSpec((TM, D), lambda i: (i, 0))] * n_copy,
        out_shape=[jax.ShapeDtypeStruct((rows, n), out_dtype)] + [jax.ShapeDtypeStruct((rows, D), F32)] * n_copy,
        compiler_params=_cparams(("arbitrary",)),
        name="norm_proj",
    )(x, nw.reshape(1, D), mod, w)
    return outs if n_copy else outs[0]


def _hgrn_proj_kernel(*refs, cw, n_x):
    nw_ref, mod_ref, lb_ref, w_ref, o_ref, tot_ref = refs[n_x:]
    x = _stream_tile(refs[:n_x], TM)
    h = _norm_mod(x, nw_ref[...], mod_ref[0, 0:1, :], mod_ref[0, 1:2, :]).astype(BF16)
    n_ck = TM // CHUNK
    mins = {}
    for j in range(w_ref.shape[1] // cw):
        val = _dot(h, w_ref[:, j * cw:(j + 1) * cw])
        sec, col = divmod(j * cw, D)
        if sec in (1, 2):
            lb = lb_ref[:, col:col + cw]
            val = jnp.log(lb + (1.0 - lb) * jax.nn.sigmoid(val))
            for c in range(n_ck):
                tot = jnp.sum(val[c * CHUNK:(c + 1) * CHUNK], axis=0, keepdims=True)
                m = jnp.min(tot, axis=-1, keepdims=True)
                key = (sec - 1, c)
                mins[key] = m if key not in mins else jnp.minimum(mins[key], m)
        o_ref[:, j * cw:(j + 1) * cw] = val
    tot_ref[0] = jnp.concatenate(
        [jnp.broadcast_to(mins[(d, c)], (1, LANES)) for c in range(n_ck) for d in range(2)], axis=0)


def _hgrn_proj(x, nw, mod, lb, w):
    n = w.shape[1]
    n_ck = TM // CHUNK
    assert 2 * n_ck == SUBLANES
    x_specs, xs = _stream_specs(x, TM)
    proj, tot = pl.pallas_call(
        functools.partial(_hgrn_proj_kernel, cw=512, n_x=len(xs)),
        grid=(NT // TM,),
        in_specs=x_specs + [
            pl.BlockSpec((1, D), lambda i: (0, 0)),
            pl.BlockSpec((1, 6, D), lambda i: (_mod_group(i, TM), 0, 0)),
            pl.BlockSpec((1, HG_HEADS * HG_DK), lambda i: (0, 0)),
            pl.BlockSpec((D, n), lambda i: (0, 0)),
        ],
        out_specs=[pl.BlockSpec((TM, n), lambda i: (i, 0)),
                   pl.BlockSpec((1, SUBLANES, LANES), lambda i: (i, 0, 0))],
        out_shape=[jax.ShapeDtypeStruct((NT, n), F32),
                   jax.ShapeDtypeStruct((NT // TM, SUBLANES, LANES), F32)],
        compiler_params=_cparams(("arbitrary",)),
        name="hgrn_proj",
    )(*xs, nw.reshape(1, D), mod, lb.reshape(1, HG_HEADS * HG_DK), w)
    return proj, tot[:, :, 0].reshape(NT // CHUNK, 2)


def _split3(x):
    hi = x.astype(BF16)
    r = x - hi.astype(F32)
    mid = r.astype(BF16)
    lo = (r - mid.astype(F32)).astype(BF16)
    return hi, mid, lo


def _gla_safe(q, kk, v, g3, st_ref, row_refs, rev):
    bs_ref, ks_ref, kd8_ref = row_refs

    n_stack = 4
    t_io = lax.broadcasted_iota(I32, (n_stack * CHUNK, CHUNK), 0)
    u_io = lax.broadcasted_iota(I32, (n_stack * CHUNK, CHUNK), 1)
    which = t_io // CHUNK
    tt = t_io - which * CHUNK
    b16 = (tt // SUB) * SUB
    b8 = (tt // SUBLANES) * SUBLANES
    if rev:
        lim = jnp.where(which == 0, tt, jnp.where(which == 1, b16 + SUB, jnp.where(which == 2, b8 + SUBLANES, b8)))
        pick = u_io >= lim
    else:
        lim = jnp.where(which == 0, tt, jnp.where(which == 1, b16 - 1,
                                                  jnp.where(which == 2, b8 - 1, b8 + SUBLANES - 1)))
        pick = u_io <= lim
    tri = jnp.where(pick, 1.0, 0.0).astype(BF16)
    hi, mid, lo = g3
    cs = _dot(tri, hi) + _dot(tri, mid) + _dot(tri, lo)
    b, r16, r8, e8 = (cs[j * CHUNK:(j + 1) * CHUNK] for j in range(n_stack))
    last = 0 if rev else CHUNK - 1
    tot = b[last:last + 1]

    qe = (q * jnp.exp(b)).astype(BF16)
    qd = (q * jnp.exp(b - r16)).astype(BF16)
    qd8 = q * jnp.exp(b - r8)
    kdec = (kk * jnp.exp(tot - b)).astype(BF16)
    vb = v.astype(BF16)
    dec_tot = jnp.exp(tot)
    kd8 = kk * jnp.exp(e8 - b)
    for h in range(HG_HEADS):
        sl = slice(h * LANES, (h + 1) * LANES)
        bs_ref[h] = b[:, sl]
        ks_ref[h] = kk[:, sl]
        kd8_ref[h] = kd8[:, sl]

    def row(ref, h, s):
        return jnp.broadcast_to(ref[h, s:s + 1, :], (SUBLANES, LANES))

    lane = lax.broadcasted_iota(I32, (SUBLANES, LANES), 1)
    row8 = lax.broadcasted_iota(I32, (SUBLANES, LANES), 0)
    a_in = [[] for _ in range(HG_HEADS)]
    for tb in range(CHUNK // SUBLANES):
        t0 = tb * SUBLANES
        blk0 = (t0 // SUB) * SUB
        other = blk0 + SUBLANES if t0 == blk0 else blk0
        keep = (row8 + t0 <= lane) if rev else (row8 + t0 >= lane)
        for h in range(HG_HEADS):
            sl = slice(h * LANES, (h + 1) * LANES)
            q_t = q[t0:t0 + SUBLANES, sl]
            b_t = b[t0:t0 + SUBLANES, sl]
            acc = jnp.zeros((SUBLANES, LANES), F32)
            for s in range(t0, t0 + SUBLANES):
                p = q_t * jnp.exp(b_t - row(bs_ref, h, s)) * row(ks_ref, h, s)
                acc = jnp.where(lane == s, jnp.sum(p, axis=-1, keepdims=True), acc)
            acc = jnp.where(keep, acc, 0.0)
            if (t0 == blk0) == rev:
                qd8_t = qd8[t0:t0 + SUBLANES, sl]
                for s in range(other, other + SUBLANES):
                    p = qd8_t * row(kd8_ref, h, s)
                    acc = jnp.where(lane == s, jnp.sum(p, axis=-1, keepdims=True), acc)
            a_in[h].append(acc)

    n_blk = CHUNK // SUB
    o_heads = []
    for h in range(HG_HEADS):
        sl = slice(h * LANES, (h + 1) * LANES)
        st = st_ref[sl, :]
        o_h = _dot_nt(qe[:, sl], st.astype(BF16))
        rows = []
        for i in range(n_blk):
            lo_r, hi_r = ((i + 1) * SUB, CHUNK) if rev else (0, i * SUB)
            if hi_r == lo_r:
                rows.append(jnp.zeros((SUB, CHUNK), F32))
                continue
            r_i = r16[i * SUB:i * SUB + 1, sl]
            kd = kk[lo_r:hi_r, sl] * jnp.exp(r_i - b[lo_r:hi_r, sl])
            pads = [jnp.zeros((lo_r, LANES), F32)] if lo_r else []
            pads_hi = [jnp.zeros((CHUNK - hi_r, LANES), F32)] if hi_r < CHUNK else []
            kd = jnp.concatenate(pads + [kd] + pads_hi, axis=0).astype(BF16)
            rows.append(_dot_nt(qd[i * SUB:(i + 1) * SUB, sl], kd))
        a1 = jnp.concatenate(rows, axis=0) + jnp.concatenate(a_in[h], axis=0)[:, :CHUNK]
        o_h = o_h + _dot(a1.astype(BF16), vb[:, sl])
        o_heads.append(o_h)
        st_ref[sl, :] = st * dec_tot[:, sl] + _dot_tn(vb[:, sl], kdec[:, sl])
    return jnp.concatenate(o_heads, axis=1)


def _gla_fast(q, kk, v, b, st_ref, rev):
    qe = (q * jnp.exp(b)).astype(BF16)
    ke32 = kk * jnp.exp(-b)
    ke = ke32.astype(BF16)
    vb = v.astype(BF16)
    t_io = lax.broadcasted_iota(I32, (GLA_BLK, GLA_BLK), 0)
    s_io = lax.broadcasted_iota(I32, (GLA_BLK, GLA_BLK), 1)
    same = (t_io // CHUNK) == (s_io // CHUNK)
    keep = jnp.logical_and(same, (s_io >= t_io) if rev else (s_io <= t_io))
    order = range(GLA_CPS - 1, -1, -1) if rev else range(GLA_CPS)
    o_heads = []
    for h in range(HG_HEADS):
        sl = slice(h * LANES, (h + 1) * LANES)
        a = jnp.where(keep, _dot_nt(qe[:, sl], ke[:, sl]), 0.0).astype(BF16)
        o_h = _dot(a, vb[:, sl])
        st = st_ref[sl, :]
        inter = [None] * GLA_CPS
        for k in order:
            r = slice(k * CHUNK, (k + 1) * CHUNK)
            last = k * CHUNK if rev else (k + 1) * CHUNK - 1
            dec_tot = jnp.exp(b[last:last + 1, sl])
            inter[k] = _dot_nt(qe[r, sl], st.astype(BF16))
            kdec = (ke32[r, sl] * dec_tot).astype(BF16)
            st = st * dec_tot + _dot_tn(vb[r, sl], kdec)
        st_ref[sl, :] = st
        o_heads.append(o_h + jnp.concatenate(inter, axis=0))
    return jnp.concatenate(o_heads, axis=1)


def _gla_prep(g, rev):
    kk = 1.0 - jnp.exp(g)
    g3 = _split3(g)
    t_io = lax.broadcasted_iota(I32, (GLA_BLK, GLA_BLK), 0)
    u_io = lax.broadcasted_iota(I32, (GLA_BLK, GLA_BLK), 1)
    same = (t_io // CHUNK) == (u_io // CHUNK)
    tri = jnp.where(jnp.logical_and(same, (u_io >= t_io) if rev else (u_io <= t_io)), 1.0, 0.0).astype(BF16)
    b = _dot(tri, g3[0]) + _dot(tri, g3[1]) + _dot(tri, g3[2])
    return kk, g3, b


def _gla_kernel(mild_ref, qf_ref, gf_ref, vf_ref, qb_ref, gb_ref, vb_ref, s0f_ref, s0b_ref,
                of_ref, ob_ref, sf_ref, sb_ref, stf, stb, bs, ks, kd8):
    i = pl.program_id(0)
    n_p = NP // GLA_BLK
    is_prompt = i < n_p
    c = jnp.where(is_prompt, i % (SEQ // GLA_BLK), (i - n_p) % (DEC_SEQ // GLA_BLK))
    n_c = jnp.where(is_prompt, SEQ // GLA_BLK, DEC_SEQ // GLA_BLK)

    @pl.when(jnp.logical_and(c == 0, is_prompt))
    def _():
        stf[...] = jnp.zeros_like(stf)
        stb[...] = jnp.zeros_like(stb)

    @pl.when(jnp.logical_and(c == 0, jnp.logical_not(is_prompt)))
    def _():
        for h in range(HG_HEADS):
            sl = slice(h * LANES, (h + 1) * LANES)
            stf[sl, :] = s0f_ref[0, sl, :].T
            stb[sl, :] = s0b_ref[0, sl, :].T

    rows = (bs, ks, kd8)
    rf = [slice(k * CHUNK, (k + 1) * CHUNK) for k in range(GLA_CPS)]
    rb = rf[::-1]
    kk_f, g3_f, b_f = _gla_prep(gf_ref[...], False)
    kk_b, g3_b, b_b = _gla_prep(gb_ref[...], True)
    mild = mild_ref[i] != 0

    @pl.when(mild)
    def _():
        of_ref[...] = _gla_fast(qf_ref[...], kk_f, vf_ref[...], b_f, stf, False)
        ob_ref[...] = _gla_fast(qb_ref[...], kk_b, vb_ref[...], b_b, stb, True)

    @pl.when(jnp.logical_not(mild))
    def _():
        for k in range(GLA_CPS):
            r = rf[k]
            of_ref[r, :] = _gla_safe(qf_ref[r, :], kk_f[r], vf_ref[r, :], [p[r] for p in g3_f], stf, rows, False)
            r = rb[k]
            ob_ref[r, :] = _gla_safe(qb_ref[r, :], kk_b[r], vb_ref[r, :], [p[r] for p in g3_b], stb, rows, True)

    @pl.when(jnp.logical_and(c == n_c - 1, is_prompt))
    def _():
        for h in range(HG_HEADS):
            sl = slice(h * LANES, (h + 1) * LANES)
            sf_ref[0, sl, :] = stf[sl, :].T
            sb_ref[0, sl, :] = stb[sl, :].T


def _gla(proj, chunk_tot, s0f, s0b):
    n_p = NP // GLA_BLK
    cp = SEQ // GLA_BLK
    cs = DEC_SEQ // GLA_BLK

    def bwd_blk(i):
        jp = (i // cp) * cp + (cp - 1 - i % cp)
        j = i - n_p
        js = n_p + (j // cs) * cs + (cs - 1 - j % cs)
        return jnp.where(i < n_p, jp, js)

    def req(i):
        return jnp.maximum(i - n_p, 0) // cs

    def preq(i):
        return jnp.minimum(i // cp, BATCH - 1)

    steps = jnp.arange(NT // GLA_BLK, dtype=I32)
    blk_tot = chunk_tot.reshape(NT // GLA_BLK, GLA_CPS, 2).min(axis=1)
    mild = jnp.logical_and(blk_tot[:, 0] >= -GLA_FAST_MAX,
                           blk_tot[bwd_blk(steps), 1] >= -GLA_FAST_MAX).astype(I32)
    hk = HG_HEADS * HG_DK
    grid_spec = pltpu.PrefetchScalarGridSpec(
        num_scalar_prefetch=1,
        grid=(NT // GLA_BLK,),
        in_specs=[
            pl.BlockSpec((GLA_BLK, D), lambda i, m: (i, 0)),
            pl.BlockSpec((GLA_BLK, D), lambda i, m: (i, 1)),
            pl.BlockSpec((GLA_BLK, D), lambda i, m: (i, 3)),
            pl.BlockSpec((GLA_BLK, D), lambda i, m: (bwd_blk(i), 0)),
            pl.BlockSpec((GLA_BLK, D), lambda i, m: (bwd_blk(i), 2)),
            pl.BlockSpec((GLA_BLK, D), lambda i, m: (bwd_blk(i), 3)),
            pl.BlockSpec((1, hk, HG_DV), lambda i, m: (req(i), 0, 0)),
            pl.BlockSpec((1, hk, HG_DV), lambda i, m: (req(i), 0, 0)),
        ],
        out_specs=[
            pl.BlockSpec((GLA_BLK, D), lambda i, m: (i, 0)),
            pl.BlockSpec((GLA_BLK, D), lambda i, m: (bwd_blk(i), 0)),
            pl.BlockSpec((1, hk, HG_DV), lambda i, m: (preq(i), 0, 0)),
            pl.BlockSpec((1, hk, HG_DV), lambda i, m: (preq(i), 0, 0)),
        ],
        scratch_shapes=[pltpu.VMEM((hk, HG_DV), F32), pltpu.VMEM((hk, HG_DV), F32)]
        + [pltpu.VMEM((HG_HEADS, CHUNK, HG_DK), F32)] * 3,
    )
    return pl.pallas_call(
        _gla_kernel,
        grid_spec=grid_spec,
        out_shape=[
            jax.ShapeDtypeStruct((NT, D), F32),
            jax.ShapeDtypeStruct((NT, D), F32),
            jax.ShapeDtypeStruct((BATCH, hk, HG_DV), F32),
            jax.ShapeDtypeStruct((BATCH, hk, HG_DV), F32),
        ],
        compiler_params=_cparams(("arbitrary",)),
        name="gla_scan",
    )(mild, proj, proj, proj, proj, proj, proj, s0f, s0b)


def _head_pair_masks():
    upper = lax.broadcasted_iota(I32, (1, LANES), 1) >= NA_HD
    return upper


def _attn_ctx_kernel(q_ref, k_ref, v_ref, o_ref):
    upper = _head_pair_masks()
    for p in range(NA_HEADS // 2):
        sl = slice(p * LANES, (p + 1) * LANES)
        qp = q_ref[:, sl] * (NA_HD ** -0.5)
        kp = k_ref[:, sl].astype(BF16)
        vp = v_ref[:, sl].astype(BF16)
        outs = []
        for u in range(2):
            qm = jnp.where(upper if u else jnp.logical_not(upper), qp, 0.0).astype(BF16)
            s = _dot_nt(qm, kp)
            e = jnp.exp(s - jnp.max(s, axis=-1, keepdims=True))
            outs.append(_dot(e.astype(BF16), vp) / jnp.sum(e, axis=-1, keepdims=True))
        o_ref[:, sl] = jnp.where(upper, outs[1], outs[0]).astype(o_ref.dtype)


def _attn_ctx(qkv):
    return pl.pallas_call(
        _attn_ctx_kernel,
        grid=(BATCH,),
        in_specs=[
            pl.BlockSpec((SEQ, D), lambda b: (b, 0)),
            pl.BlockSpec((SEQ, D), lambda b: (b, 1)),
            pl.BlockSpec((SEQ, D), lambda b: (b, 2)),
        ],
        out_specs=pl.BlockSpec((SEQ, D), lambda b: (b, 0)),
        out_shape=jax.ShapeDtypeStruct((NP, D), BF16),
        compiler_params=_cparams(("arbitrary",)),
        name="attn_ctx",
    )(qkv, qkv, qkv)


def _attn_lat_kernel(q_ref, k_ref, v_ref, kc_ref, vc_ref, strip_ref, rowmask_ref, o_ref, *, tq):
    rows = DEC_SEQ // GRID_W
    kr = min(WIN_R, rows)
    rpt = tq // GRID_W

    def bias_rows(u, r, k_lo, k_hi):
        first = rows - 1 - r + k_lo
        var = first % 2
        strip = strip_ref[u, var, :, (first - var) * GRID_W:(first - var + k_hi - k_lo) * GRID_W]
        return strip + rowmask_ref[r:r + 1, k_lo * GRID_W:k_hi * GRID_W]

    upper = _head_pair_masks()
    mine = (jnp.logical_not(upper), upper)
    kl = k_ref[...].astype(BF16)
    kc = kc_ref[0].astype(BF16)
    vl = [jnp.where(mine[u], v_ref[...], 1.0).astype(BF16) for u in range(2)]
    vc = [jnp.where(mine[u], vc_ref[0], 1.0).astype(BF16) for u in range(2)]
    for t in range(DEC_SEQ // tq):
        rs = slice(t * tq, (t + 1) * tq)
        k_lo = min(max(t * rpt - kr // 2, 0), rows - kr) // 2 * 2
        k_hi = -(-(min(max(t * rpt + rpt - 1 - kr // 2, 0), rows - kr) + kr) // 2) * 2
        ks = slice(k_lo * GRID_W, k_hi * GRID_W)
        qp = q_ref[rs, :] * (NA_HD ** -0.5)
        outs = []
        for u in range(2):
            qm = jnp.where(mine[u], qp, 0.0).astype(BF16)
            bias = jnp.concatenate([bias_rows(u, t * rpt + a, k_lo, k_hi) for a in range(rpt)], axis=0)
            s_l = _dot_nt(qm, kl[ks]) + bias
            s_c = _dot_nt(qm, kc)
            m = jnp.maximum(jnp.max(s_l, axis=-1, keepdims=True), jnp.max(s_c, axis=-1, keepdims=True))
            acc = (_dot(jnp.exp(s_l - m).astype(BF16), vl[u][ks])
                   + _dot(jnp.exp(s_c - m).astype(BF16), vc[u]))
            outs.append(acc / pltpu.roll(acc, NA_HD, 1))
        o_ref[rs, :] = jnp.where(upper, outs[1], outs[0]).astype(o_ref.dtype)


def _attn_lat(qkv, k_ctx, v_ctx, strips, rowmask):
    npair = NA_HEADS // 2
    return pl.pallas_call(
        functools.partial(_attn_lat_kernel, tq=256),
        grid=(npair, DEC_BATCH),
        in_specs=[
            pl.BlockSpec((DEC_SEQ, LANES), lambda p, b: (b, p)),
            pl.BlockSpec((DEC_SEQ, LANES), lambda p, b: (b, npair + p)),
            pl.BlockSpec((DEC_SEQ, LANES), lambda p, b: (b, 2 * npair + p)),
            pl.BlockSpec((1, PAST_LEN, LANES), lambda p, b: (b, 0, p)),
            pl.BlockSpec((1, PAST_LEN, LANES), lambda p, b: (b, 0, p)),
            pl.BlockSpec((2, 2, GRID_W, STRIP_W), lambda p, b: (p, 0, 0, 0)),
            pl.BlockSpec((DEC_SEQ // GRID_W, DEC_SEQ), lambda p, b: (0, 0)),
        ],
        out_specs=pl.BlockSpec((DEC_SEQ, LANES), lambda p, b: (b, p)),
        out_shape=jax.ShapeDtypeStruct((NS, D), BF16),
        compiler_params=_cparams(("arbitrary", "arbitrary")),
        name="attn_lat",
    )(qkv, qkv, qkv, k_ctx, v_ctx, strips, rowmask)


def _latent_bias_tables(rpb):
    rows = DEC_SEQ // GRID_W
    kr = min(WIN_R, rows)
    ndr, ndc = 2 * WIN_R - 1, 2 * WIN_C - 1
    qc = np.arange(GRID_W)
    kc = np.arange(GRID_W)
    ws = np.clip(qc - WIN_C // 2, 0, GRID_W - WIN_C)
    col_ok = (kc[None, :] >= ws[:, None]) & (kc[None, :] < ws[:, None] + WIN_C)
    dc = np.clip(kc[None, :] - qc[:, None] + WIN_C - 1, 0, ndc - 1)
    onehot = (dc[None, :, :] == np.arange(ndc)[:, None, None]).astype(np.float32)
    t2 = jnp.einsum('hrc,cqk->hqrk', rpb.astype(F32), jnp.asarray(onehot),
                    precision=lax.Precision.HIGHEST)
    t2 = jnp.where(jnp.asarray(col_ok)[None, :, None, :], t2, NEG)
    lead = rows - WIN_R
    n_tiles = 2 * rows - 1

    def neg(n):
        return jnp.full((NA_HEADS, GRID_W, n, GRID_W), NEG, F32)

    strip = jnp.concatenate([neg(lead), t2, neg(n_tiles - lead - ndr)], axis=2)
    strip = strip.reshape(NA_HEADS, GRID_W, n_tiles * GRID_W)

    def pad(a):
        return jnp.pad(a, ((0, 0), (0, 0), (0, STRIP_W - a.shape[-1])), constant_values=NEG)

    strips = jnp.stack([pad(strip), pad(strip[:, :, GRID_W:])], axis=1)
    r = np.arange(rows)
    k0 = np.clip(r - kr // 2, 0, rows - kr)
    krow = np.arange(DEC_SEQ) // GRID_W
    row_ok = (krow[None, :] >= k0[:, None]) & (krow[None, :] < k0[:, None] + kr)
    rowmask = jnp.asarray(np.where(row_ok, 0.0, NEG).astype(np.float32))
    return strips, rowmask


def _hgrn_mix_out(refs, mod_ref, tile):
    of_ref, ob_ref, gate_ref, xp_ref, xs_ref, gn_ref, w_ref = refs
    x = jnp.where(tile < NP // MOE_TM, xp_ref[...], xs_ref[...])
    o = of_ref[...] + ob_ref[...]
    gn = gn_ref[...]
    segs = []
    for h in range(HG_HEADS):
        seg = o[:, h * LANES:(h + 1) * LANES]
        ms = jnp.mean(seg * seg, axis=-1, keepdims=True)
        segs.append(seg * lax.rsqrt(ms + EPS) * gn)
    y = (jnp.concatenate(segs, axis=1) * _silu(gate_ref[...])).astype(BF16)
    return x + mod_ref[0, 2:3, :] * _dot(y, w_ref[...])


def _attn_mix_out(refs, mod_ref, tile):
    ap_ref, as_ref, x_ref, w_ref = refs
    a = jnp.where(tile < NP // MOE_TM, ap_ref[...], as_ref[...])
    return x_ref[...] + mod_ref[0, 2:3, :] * _dot(a, w_ref[...])


def _route_kernel(*refs, mix_out, n_mix):
    mix_refs = refs[:n_mix]
    (nw_ref, mod_ref, wrh_ref, wrl_ref, rb_ref, xnew_ref, xs_ref, pos_ref, wt_ref, npc_ref,
     hb_s, pos_s, used_s) = refs[n_mix:]
    i = pl.program_id(0)
    cur = lax.rem(i, 2)
    tile = jnp.minimum(i, N_TILES - 1)

    @pl.when(i == 0)
    def _():
        hb_s[...] = jnp.zeros_like(hb_s)
        pos_s[...] = jnp.full(pos_s.shape, -1, I32)
        used_s[0] = 0.0
        used_s[1] = 0.0

    live = i < N_TILES

    def step(slot):
        x = mix_out(mix_refs, mod_ref, tile)
        xnew_ref[...] = x
        hb, pos, used = _route_select(x, nw_ref, mod_ref, wrh_ref, wrl_ref, rb_ref, pos_ref, wt_ref, npc_ref)
        hb_s[slot] = hb
        pos_s[slot] = jnp.where(live, pos, -1)
        used_s[slot] = jnp.where(live, used, 0.0)
        _route_sort(xs_ref, hb_s[1 - slot], pos_s[1 - slot], used_s[1 - slot])

    for slot in range(2):
        pl.when(cur == slot)(functools.partial(step, slot))


def _route_sort(xs_ref, hb, pos, used):
    pos16 = [pos[k:k + 1].astype(jnp.int16) for k in range(TOP_K)]

    def fill(r0, n):
        j16 = (lax.broadcasted_iota(I32, (n, MOE_TM), 0) + r0).astype(jnp.int16)
        onehot = jnp.zeros((n, MOE_TM), BF16)
        for k in range(TOP_K):
            onehot = jnp.where(j16 == pos16[k], jnp.ones((), BF16), onehot)
        xs_ref[r0:r0 + n, :] = _dot(onehot, hb).astype(BF16)

    fill(0, ROWS_MAIN)
    for r0 in range(ROWS_MAIN, R1, ROW_BLK):
        pl.when(used > r0)(functools.partial(fill, r0, ROW_BLK))

        @pl.when(used <= r0)
        def _(r0=r0):
            xs_ref[r0:r0 + ROW_BLK, :] = jnp.zeros((ROW_BLK, D), BF16)


def _route_select(x, nw_ref, mod_ref, wrh_ref, wrl_ref, rb_ref, pos_ref, wt_ref, npc_ref):
    h = _norm_mod(x, nw_ref[...], mod_ref[0, 3:4, :], mod_ref[0, 4:5, :])
    hb = h.astype(BF16)
    hl = (h - hb.astype(F32)).astype(BF16)
    wrh = wrh_ref[...]
    logits = _dot_nt(wrh, hb) + _dot_nt(wrh, hl) + _dot_nt(wrl_ref[...], hb)
    scores = jax.nn.sigmoid(logits)
    sel = scores + rb_ref[...]

    gsz = N_EXPERTS // N_GROUPS
    sub = lax.broadcasted_iota(I32, (gsz, MOE_TM), 0)
    ninf = -jnp.inf
    gs_rows = []
    for gi in range(N_GROUPS):
        blk = sel[gi * gsz:(gi + 1) * gsz]
        m1 = jnp.max(blk, axis=0, keepdims=True)
        first = jnp.min(jnp.where(blk == m1, sub, gsz), axis=0, keepdims=True)
        m2 = jnp.max(jnp.where(sub == first, ninf, blk), axis=0, keepdims=True)
        gs_rows.append(m1 + m2)
    cur = jnp.concatenate(gs_rows, axis=0)
    gidx = lax.broadcasted_iota(I32, (N_GROUPS, MOE_TM), 0)
    gsel = jnp.zeros((N_GROUPS, MOE_TM), F32)
    for _ in range(TOPK_GROUPS):
        m = jnp.max(cur, axis=0, keepdims=True)
        first = jnp.min(jnp.where(cur == m, gidx, N_GROUPS), axis=0, keepdims=True)
        hit = gidx == first
        gsel = jnp.where(hit, 1.0, gsel)
        cur = jnp.where(hit, ninf, cur)
    emask = jnp.concatenate(
        [jnp.broadcast_to(gsel[gi:gi + 1], (gsz, MOE_TM)) for gi in range(N_GROUPS)], axis=0)
    masked = jnp.where(emask > 0.5, sel, ninf)
    eidx = lax.broadcasted_iota(I32, (N_EXPERTS, MOE_TM), 0)
    chosen = jnp.zeros((N_EXPERTS, MOE_TM), F32)
    hits, wsel = [], []
    for _ in range(TOP_K):
        m = jnp.max(masked, axis=0, keepdims=True)
        first = jnp.min(jnp.where(masked == m, eidx, N_EXPERTS), axis=0, keepdims=True)
        hit = eidx == first
        hits.append(hit)
        wsel.append(jnp.sum(jnp.where(hit, scores, 0.0), axis=0, keepdims=True))
        chosen = jnp.where(hit, 1.0, chosen)
        masked = jnp.where(hit, ninf, masked)
    wsum = wsel[0]
    for w in wsel[1:]:
        wsum = wsum + w

    n_io = lax.broadcasted_iota(I32, (MOE_TM, MOE_TM), 0)
    m_io = lax.broadcasted_iota(I32, (MOE_TM, MOE_TM), 1)
    earlier = jnp.where(n_io < m_io, 1.0, 0.0).astype(BF16)
    rank = _dot(chosen.astype(BF16), earlier)
    cnt = jnp.sum(chosen, axis=1, keepdims=True)
    npc = jnp.floor((cnt + (PIECE - 1)) * (1.0 / PIECE))
    e_io = lax.broadcasted_iota(I32, (N_EXPERTS, N_EXPERTS), 0)
    f_io = lax.broadcasted_iota(I32, (N_EXPERTS, N_EXPERTS), 1)
    below = jnp.where(f_io < e_io, 1.0, 0.0).astype(BF16)
    npc_l = jnp.broadcast_to(npc, (N_EXPERTS, LANES))
    start = _dot(below, npc_l.astype(BF16))[:, 0:1] * PIECE
    slot = start + rank

    pos_rows, wt_rows = [], []
    for k in range(TOP_K):
        pos_k = jnp.sum(jnp.where(hits[k], slot, 0.0), axis=0, keepdims=True).astype(I32)
        pos_rows.append(pos_k)
        wt_rows.append(wsel[k] / wsum * ROUTED_SCALE)

    pad = SUBLANES - TOP_K
    pos = jnp.concatenate(pos_rows + [jnp.full((pad, MOE_TM), -1, I32)], axis=0)
    pos_ref[...] = pos
    wt_ref[...] = jnp.concatenate(wt_rows + [jnp.zeros((pad, MOE_TM), F32)], axis=0)
    npc_ref[0] = npc_l.astype(I32)
    return hb, pos, jnp.sum(npc) * PIECE


def _hgrn_mix_inputs(o_f, o_b, proj, x_pair, gn_w, w_out):
    n_p = NP // MOE_TM

    def specs(tile):
        return [pl.BlockSpec((MOE_TM, D), lambda i: (tile(i), 0)),
                pl.BlockSpec((MOE_TM, D), lambda i: (tile(i), 0)),
                pl.BlockSpec((MOE_TM, D), lambda i: (tile(i), 4)),
                pl.BlockSpec((MOE_TM, D), lambda i: (jnp.minimum(tile(i), n_p - 1), 0)),
                pl.BlockSpec((MOE_TM, D), lambda i: (jnp.maximum(tile(i) - n_p, 0), 0)),
                pl.BlockSpec((1, HG_DV), lambda i: (0, 0)),
                pl.BlockSpec((D, D), lambda i: (0, 0))]

    return _hgrn_mix_out, (o_f, o_b, proj, x_pair[0], x_pair[1], gn_w.reshape(1, HG_DV), w_out), specs


def _attn_mix_inputs(a_p, a_s, x, w_o):
    n_p = NP // MOE_TM

    def specs(tile):
        return [pl.BlockSpec((MOE_TM, D), lambda i: (jnp.minimum(tile(i), n_p - 1), 0)),
                pl.BlockSpec((MOE_TM, D), lambda i: (jnp.maximum(tile(i) - n_p, 0), 0)),
                pl.BlockSpec((MOE_TM, D), lambda i: (tile(i), 0)),
                pl.BlockSpec((D, D), lambda i: (0, 0))]

    return _attn_mix_out, (a_p, a_s, x, w_o), specs


def _route(mix, nw, mod, wr_hi, wr_lo, rbias):
    mix_out, mix_args, mix_specs = mix

    def tile(i):
        return jnp.minimum(i, N_TILES - 1)

    return pl.pallas_call(
        functools.partial(_route_kernel, mix_out=mix_out, n_mix=len(mix_args)),
        grid=(N_TILES + 2,),
        in_specs=mix_specs(tile) + [
            pl.BlockSpec((1, D), lambda i: (0, 0)),
            pl.BlockSpec((1, 6, D), lambda i: (_mod_group(tile(i), MOE_TM), 0, 0)),
            pl.BlockSpec((N_EXPERTS, D), lambda i: (0, 0)),
            pl.BlockSpec((N_EXPERTS, D), lambda i: (0, 0)),
            pl.BlockSpec((N_EXPERTS, 1), lambda i: (0, 0)),
        ],
        out_specs=[
            pl.BlockSpec((MOE_TM, D), lambda i: (tile(i), 0)),
            pl.BlockSpec((R1, D), lambda i: (jnp.maximum(i - 1, 0), 0)),
            pl.BlockSpec((SUBLANES, MOE_TM), lambda i: (0, tile(i))),
            pl.BlockSpec((SUBLANES, MOE_TM), lambda i: (0, tile(i))),
            pl.BlockSpec((1, N_EXPERTS, LANES), lambda i: (tile(i), 0, 0)),
        ],
        out_shape=[
            jax.ShapeDtypeStruct((NT, D), F32),
            jax.ShapeDtypeStruct((P_TOT * PIECE, D), BF16),
            jax.ShapeDtypeStruct((SUBLANES, NT), I32),
            jax.ShapeDtypeStruct((SUBLANES, NT), F32),
            jax.ShapeDtypeStruct((N_TILES, N_EXPERTS, LANES), I32),
        ],
        scratch_shapes=[pltpu.VMEM((2, MOE_TM, D), BF16), pltpu.VMEM((2, SUBLANES, MOE_TM), I32),
                        pltpu.SMEM((2,), F32)],
        compiler_params=_cparams(("arbitrary",)),
        name="moe_route",
    )(*mix_args, nw.reshape(1, D), mod, wr_hi, wr_lo, rbias.reshape(N_EXPERTS, 1))


def _piece_lists(npc):
    t, e = npc.shape
    hp = lax.Precision.HIGHEST
    npc_t = npc.T.astype(F32)
    start_t = (jnp.cumsum(npc, axis=1) - npc).T.astype(F32)
    tile_end = jnp.cumsum(npc_t, axis=1)
    n_e = tile_end[:, -1]
    pe_end = jnp.cumsum(n_e)
    pe_off = pe_end - n_e
    p = jnp.arange(P_MAX, dtype=F32)
    e_p = jnp.minimum(jnp.sum((pe_end[None, :] <= p[:, None]).astype(I32), axis=1), e - 1)
    oh_e = (e_p[:, None] == jnp.arange(e, dtype=I32)[None, :]).astype(F32)
    tab = jnp.concatenate([tile_end, start_t, npc_t, pe_off[:, None]], axis=1)
    row = jnp.dot(oh_e, tab, precision=hp)
    te_p, st_p, np_p, off_p = row[:, :t], row[:, t:2 * t], row[:, 2 * t:3 * t], row[:, 3 * t]
    local = p - off_p
    t_p = jnp.minimum(jnp.sum((te_p <= local[:, None]).astype(I32), axis=1), t - 1)
    oh_t = t_p[:, None] == jnp.arange(t, dtype=I32)[None, :]

    def pick(a):
        return jnp.sum(jnp.where(oh_t, a, 0.0), axis=1)

    src = t_p * R1B + (pick(st_p) + local - (pick(te_p) - pick(np_p))).astype(I32)
    src = jnp.concatenate([jnp.clip(src, 0, P_MAX - 1), jnp.zeros((G_PIECES,), I32)])

    nch = jnp.floor((n_e + (G_PIECES - 1)) * (1.0 / G_PIECES))
    ch_end = jnp.cumsum(nch)
    ch_off = jnp.concatenate([jnp.zeros((1,), F32), ch_end]).astype(I32)
    c = jnp.arange(-N_SLOTS, NCH + 2, dtype=F32)
    ce = jnp.minimum(jnp.sum((ch_end[None, :] <= c[:, None]).astype(I32), axis=1), e - 1)
    oh_c = (ce[:, None] == jnp.arange(e, dtype=I32)[None, :]).astype(F32)
    crow = jnp.dot(oh_c, jnp.stack([ch_end - nch, pe_off, n_e], axis=1), precision=hp)
    k_in = c - crow[:, 0]
    live = jnp.logical_and(c >= 0, c < ch_end[-1])
    cn = jnp.where(live, jnp.clip(crow[:, 2] - G_PIECES * k_in, 0, G_PIECES), 0.0)
    cs = jnp.where(cn > 0, crow[:, 1] + G_PIECES * k_in, 0.0)
    return src, ch_off, cs.astype(I32), cn.astype(I32)


def _ffn_kernel(src_ref, choff_ref, cs_ref, cn_ref, xs_in, wg_ref, wu_ref, wd_ref, xs_out,
                xbuf, ybuf, wgb, wub, wdb, gsem, ssem):
    e = pl.program_id(0)
    total = choff_ref[N_EXPERTS]

    def start_gather(ch):
        sl = lax.rem(ch + N_SLOTS, N_SLOTS)
        base = cs_ref[ch + N_SLOTS]
        n = cn_ref[ch + N_SLOTS]
        for i in range(G_PIECES):
            idx = src_ref[base + jnp.where(i < n, i, 0)]
            pltpu.make_async_copy(xs_in.at[idx], xbuf.at[sl, i], gsem.at[sl]).start(priority=i % 2)

    def wait_gather(ch):
        sl = lax.rem(ch + N_SLOTS, N_SLOTS)
        pltpu.make_async_copy(xs_in.at[pl.ds(0, G_PIECES)], xbuf.at[sl], gsem.at[sl]).wait()

    def start_scatter(ch):
        sl = lax.rem(ch + N_SLOTS, N_SLOTS)
        base = cs_ref[ch + N_SLOTS]
        n = cn_ref[ch + N_SLOTS]
        for i in range(G_PIECES):
            idx = jnp.where(i < n, src_ref[base + i], P_MAX + sl * G_PIECES + i)
            pltpu.make_async_copy(ybuf.at[sl, i], xs_out.at[idx], ssem.at[sl]).start(priority=i % 2)

    def wait_scatter(ch):
        sl = lax.rem(ch + N_SLOTS, N_SLOTS)
        pltpu.make_async_copy(ybuf.at[sl], xs_out.at[pl.ds(0, G_PIECES)], ssem.at[sl]).wait()

    @pl.when(e == 0)
    def _():
        ybuf[...] = jnp.zeros_like(ybuf)
        start_gather(0)
        start_gather(1)
        start_scatter(-3)
        start_scatter(-2)

    wgb[...] = wg_ref[0].astype(BF16)
    wub[...] = wu_ref[0].astype(BF16)
    wdb[...] = wd_ref[0].astype(BF16)

    def chunk(c, carry):
        sl = lax.rem(c, N_SLOTS)
        wait_gather(c)
        wait_scatter(c - 3)
        x = xbuf[sl].reshape(G_PIECES * PIECE, D)
        hid = (_silu(_dot(x, wgb[...])) * _dot(x, wub[...])).astype(BF16)
        start_gather(c + 2)
        start_scatter(c - 1)
        ybuf[sl] = _dot(hid, wdb[...]).astype(BF16).reshape(G_PIECES, PIECE, D)
        return carry

    lax.fori_loop(choff_ref[e], choff_ref[e + 1], chunk, 0)

    @pl.when(e == N_EXPERTS - 1)
    def _():
        start_scatter(total - 1)
        wait_gather(total)
        wait_gather(total + 1)
        wait_scatter(total - 3)
        wait_scatter(total - 2)
        wait_scatter(total - 1)


def _expert_ffn(xs, lists, layer, w_gate, w_up, w_down):
    src, ch_off, cs, cn = lists
    grid_spec = pltpu.PrefetchScalarGridSpec(
        num_scalar_prefetch=4,
        grid=(N_EXPERTS,),
        in_specs=[
            pl.BlockSpec(memory_space=pl.ANY),
            pl.BlockSpec((None, 1, D, D_EXPERT), lambda e, *_: (layer, e, 0, 0)),
            pl.BlockSpec((None, 1, D, D_EXPERT), lambda e, *_: (layer, e, 0, 0)),
            pl.BlockSpec((None, 1, D_EXPERT, D), lambda e, *_: (layer, e, 0, 0)),
        ],
        out_specs=pl.BlockSpec(memory_space=pl.ANY),
        scratch_shapes=[
            pltpu.VMEM((N_SLOTS, G_PIECES, PIECE, D), BF16),
            pltpu.VMEM((N_SLOTS, G_PIECES, PIECE, D), BF16),
            pltpu.VMEM((D, D_EXPERT), BF16),
            pltpu.VMEM((D, D_EXPERT), BF16),
            pltpu.VMEM((D_EXPERT, D), BF16),
            pltpu.SemaphoreType.DMA((N_SLOTS,)),
            pltpu.SemaphoreType.DMA((N_SLOTS,)),
        ],
    )
    out = pl.pallas_call(
        _ffn_kernel,
        grid_spec=grid_spec,
        out_shape=jax.ShapeDtypeStruct((P_TOT, PIECE, D), BF16),
        input_output_aliases={4: 0},
        compiler_params=_cparams(("arbitrary",)),
        name="moe_ffn",
    )(src, ch_off, cs, cn, xs.reshape(P_TOT, PIECE, D), w_gate, w_up, w_down)
    return out.reshape(P_TOT * PIECE, D)


def _combine_kernel(used_ref, ys_ref, pos_ref, wt_ref, x_ref, nw_ref, mod_ref, wsg_ref, wsu_ref, wsd_ref,
                    fw_ref, *rest, final):
    o_refs, acc_ref = rest[:-1], rest[-1]
    i = pl.program_id(0)
    x = x_ref[...]
    h = _norm_mod(x, nw_ref[...], mod_ref[0, 3:4, :], mod_ref[0, 4:5, :]).astype(BF16)
    shared = _dot((_silu(_dot(h, wsg_ref[...])) * _dot(h, wsu_ref[...])).astype(BF16), wsd_ref[...])

    pos16 = pos_ref[...].astype(jnp.int16)
    wt16 = wt_ref[...].astype(BF16)

    def block(r0, n):
        j16 = (lax.broadcasted_iota(I32, (MOE_TM, n), 1) + r0).astype(jnp.int16)
        wm = jnp.zeros((MOE_TM, n), BF16)
        for k in range(TOP_K):
            wm = jnp.where(j16 == pos16[:, k:k + 1], wt16[:, k:k + 1], wm)
        return _dot(wm, ys_ref[r0:r0 + n, :])

    acc_ref[...] = block(0, ROWS_MAIN)
    for r0 in range(ROWS_MAIN, R1, ROW_BLK):
        @pl.when(used_ref[i] > r0)
        def _(r0=r0):
            acc_ref[...] += block(r0, ROW_BLK)

    y = x + mod_ref[0, 5:6, :] * (acc_ref[...] + shared)
    if not final:
        o_refs[0][...] = y
        return
    ms = jnp.mean(y * y, axis=-1, keepdims=True)
    y = y * lax.rsqrt(ms + EPS) * fw_ref[...]
    is_prompt = i < NP // MOE_TM

    @pl.when(is_prompt)
    def _():
        o_refs[0][...] = y

    @pl.when(jnp.logical_not(is_prompt))
    def _():
        o_refs[1][...] = y


def _combine(ys, used, pos_t, wt_t, x, nw, mod, wsg, wsu, wsd, fw, final):
    n_p = NP // MOE_TM
    if final:
        out_specs = [pl.BlockSpec((MOE_TM, D), lambda i, u: (jnp.minimum(i, n_p - 1), 0)),
                     pl.BlockSpec((MOE_TM, D), lambda i, u: (jnp.maximum(i - n_p, 0), 0))]
        out_shape = [jax.ShapeDtypeStruct((NP, D), F32), jax.ShapeDtypeStruct((NS, D), F32)]
    else:
        out_specs = pl.BlockSpec((MOE_TM, D), lambda i, u: (i, 0))
        out_shape = jax.ShapeDtypeStruct((NT, D), F32)
    grid_spec = pltpu.PrefetchScalarGridSpec(
        num_scalar_prefetch=1,
        grid=(N_TILES,),
        in_specs=[
            pl.BlockSpec((R1, D), lambda i, u: (i, 0)),
            pl.BlockSpec((MOE_TM, SUBLANES), lambda i, u: (i, 0)),
            pl.BlockSpec((MOE_TM, SUBLANES), lambda i, u: (i, 0)),
            pl.BlockSpec((MOE_TM, D), lambda i, u: (i, 0)),
            pl.BlockSpec((1, D), lambda i, u: (0, 0)),
            pl.BlockSpec((1, 6, D), lambda i, u: (_mod_group(i, MOE_TM), 0, 0)),
            pl.BlockSpec((D, D_SHARED), lambda i, u: (0, 0)),
            pl.BlockSpec((D, D_SHARED), lambda i, u: (0, 0)),
            pl.BlockSpec((D_SHARED, D), lambda i, u: (0, 0)),
            pl.BlockSpec((1, D), lambda i, u: (0, 0)),
        ],
        out_specs=out_specs,
        scratch_shapes=[pltpu.VMEM((MOE_TM, D), F32)],
    )
    return pl.pallas_call(
        functools.partial(_combine_kernel, final=final),
        grid_spec=grid_spec,
        out_shape=out_shape,
        compiler_params=_cparams(("arbitrary",)),
        name="moe_combine",
    )(used, ys, pos_t, wt_t, x, nw.reshape(1, D), mod, wsg, wsu, wsd, fw.reshape(1, D))


def _moe(mix, nw, mod, w_router, rbias, layer, w_gate, w_up, w_down, ws_gate, ws_up, ws_down, fw, final):
    wr = w_router.astype(F32).T
    wr_hi = wr.astype(BF16)
    wr_lo = (wr - wr_hi.astype(F32)).astype(BF16)
    x, xs, pos, wt, npc = _route(mix, nw, mod, wr_hi, wr_lo, rbias.astype(F32))
    npc = npc[:, :, 0]
    lists = _piece_lists(npc)
    ys = _expert_ffn(xs, lists, layer, w_gate, w_up, w_down)
    used = (npc.sum(axis=1) * PIECE).astype(I32)
    return _combine(ys, used, pos.T, wt.T, x, nw, mod, ws_gate.astype(BF16), ws_up.astype(BF16),
                    ws_down.astype(BF16), fw, final)


def kernel(x_prompt, x_sample, state_hgrn_fwd, state_hgrn_bwd, cache_na_k, cache_na_v, c, c_ctx,
           norm1_w, norm2_w, ada_w, ada_b, hgrn_w_in, hgrn_lb_logits, hgrn_gn_w, hgrn_w_out,
           na_w_qkv, na_rpb, na_w_o, moe_w_router, moe_router_bias, moe_w_gate, moe_w_up, moe_w_down,
           shared_w_gate, shared_w_up, shared_w_down, final_norm_w):
    x = (x_prompt.reshape(NP, D), x_sample.reshape(NS, D))
    cvec = jnp.concatenate([c_ctx[None, :], c, jnp.zeros((N_MOD - 1 - DEC_BATCH, D), F32)], axis=0)
    mod = _modulation(cvec, ada_w, ada_b)
    lb_table = jnp.cumsum(jax.nn.softmax(hgrn_lb_logits.astype(F32), axis=0), axis=0)
    hk = HG_HEADS * HG_DK

    sf = sb = k_c = v_c = None
    for l in range(DEPTH):
        if l % 2 == 0:
            a = l // 2
            proj, chunk_tot = _hgrn_proj(x, norm1_w[l], mod[l], lb_table[l], hgrn_w_in[a].astype(BF16))
            o_f, o_b, sf, sb = _gla(proj, chunk_tot,
                                    state_hgrn_fwd[:, a].reshape(DEC_BATCH, hk, HG_DV),
                                    state_hgrn_bwd[:, a].reshape(DEC_BATCH, hk, HG_DV))
            mix = _hgrn_mix_inputs(o_f, o_b, proj, x, hgrn_gn_w[a], hgrn_w_out[a].astype(BF16))
        else:
            n = l // 2
            w_qkv = na_w_qkv[n].astype(BF16)
            qkv_p, k_p, v_p = _norm_proj(x, norm1_w[l], mod[l], w_qkv, 0, NP // TM, n_copy=2, out_dtype=BF16)
            qkv_s = _norm_proj(x, norm1_w[l], mod[l], w_qkv, NP // TM, NS // TM, out_dtype=BF16)
            att_p = _attn_ctx(qkv_p)
            att_s = _attn_lat(qkv_s, cache_na_k[:, n].reshape(DEC_BATCH, PAST_LEN, D),
                              cache_na_v[:, n].reshape(DEC_BATCH, PAST_LEN, D),
                              *_latent_bias_tables(na_rpb[n]))
            mix = _attn_mix_inputs(att_p, att_s, x, na_w_o[n].astype(BF16))
            k_c = k_p.reshape(BATCH, SEQ, NA_HEADS, NA_HD)
            v_c = v_p.reshape(BATCH, SEQ, NA_HEADS, NA_HD)
        x = _moe(mix, norm2_w[l], mod[l], moe_w_router[l], moe_router_bias[l], l, moe_w_gate, moe_w_up,
                 moe_w_down, shared_w_gate[l], shared_w_up[l], shared_w_down[l], final_norm_w,
                 final=(l == DEPTH - 1))

    y_prompt = x[0].reshape(BATCH, SEQ, D)
    y_sample = x[1].reshape(DEC_BATCH, DEC_SEQ, D)
    new_sf = sf.reshape(BATCH, 1, HG_HEADS, HG_DK, HG_DV)
    new_sb = sb.reshape(BATCH, 1, HG_HEADS, HG_DK, HG_DV)
    return (y_prompt, y_sample, new_sf, new_sb, k_c[:, None], v_c[:, None])
```

```python
import functools

import numpy as np
import jax
import jax.numpy as jnp
from jax import lax
from jax.experimental import pallas as pl
from jax.experimental.pallas import tpu as pltpu

F32 = jnp.float32
BF16 = jnp.bfloat16
I32 = jnp.int32

D = 1024
BATCH = 32
SEQ = 256
DEPTH = 2
DEC_BATCH = 8
DEC_SEQ = 1024
PAST_LEN = 512
GRID_W = 64
HG_HEADS = 8
HG_DK = 128
HG_DV = 128
CHUNK = 64
NA_HEADS = 16
NA_HD = 64
WIN_R = 8
WIN_C = 16
N_EXPERTS = 64
TOP_K = 6
N_GROUPS = 8
TOPK_GROUPS = 4
D_EXPERT = 256
D_SHARED = 256
ROUTED_SCALE = 2.5
EPS = 1e-6

NP = BATCH * SEQ
NS = DEC_BATCH * DEC_SEQ
NT = NP + NS
N_MOD = 16
STRIP_W = 2048
assert (2 * (DEC_SEQ // GRID_W) - 1) * GRID_W <= STRIP_W
NEG = -1e30

LANES = 128
SUBLANES = 8
BF16_ROWS = 16
VMEM_LIMIT = 56 * 1024 * 1024

TM = 256
SUB = 16
GLA_FAST_MAX = 60.0
GLA_CPS = 4
GLA_BLK = GLA_CPS * CHUNK
assert SEQ % GLA_BLK == 0 and DEC_SEQ % GLA_BLK == 0
MOE_TM = 256
PIECE = BF16_ROWS
R1 = MOE_TM * TOP_K + N_EXPERTS * (PIECE - 1) + 64
assert R1 % PIECE == 0 and R1 % LANES == 0
R1B = R1 // PIECE
ROWS_MAIN = 2048
ROW_BLK = 256
assert (R1 - ROWS_MAIN) % ROW_BLK == 0
N_TILES = NT // MOE_TM
P_MAX = N_TILES * R1B
G_PIECES = 48
NCH = P_MAX // G_PIECES + N_EXPERTS
N_SLOTS = 3
P_TOT = P_MAX + R1B
assert N_SLOTS * G_PIECES <= R1B


def _cparams(sem):
    return pltpu.CompilerParams(dimension_semantics=sem, vmem_limit_bytes=VMEM_LIMIT)


def _dot(a, b):
    return jnp.dot(a, b, preferred_element_type=F32)


def _dot_nt(a, b):
    return lax.dot_general(a, b, (((1,), (1,)), ((), ())), preferred_element_type=F32)


def _dot_tn(a, b):
    return lax.dot_general(a, b, (((0,), (0,)), ((), ())), preferred_element_type=F32)


def _silu(x):
    return x * jax.nn.sigmoid(x)


def _mod_group(i, tm):
    r = i * tm
    return jnp.where(r < NP, 0, 1 + (r - NP) // DEC_SEQ)


def _norm_mod(x, nw, shift, scale):
    ms = jnp.mean(x * x, axis=-1, keepdims=True)
    y = x * lax.rsqrt(ms + EPS) * nw
    return y * (1.0 + scale) + shift


def _mod_kernel(c_ref, w_ref, b_ref, o_ref):
    s = _silu(c_ref[...]).astype(BF16)
    o_ref[0] = _dot(s, w_ref[0].astype(BF16)) + b_ref[0]


def _modulation(cvec, ada_w, ada_b):
    cw = 1536
    n = ada_w.shape[-1]
    out = pl.pallas_call(
        _mod_kernel,
        grid=(DEPTH, n // cw),
        in_specs=[
            pl.BlockSpec((N_MOD, D), lambda l, j: (0, 0)),
            pl.BlockSpec((1, D, cw), lambda l, j: (l, 0, j)),
            pl.BlockSpec((1, 1, cw), lambda l, j: (l, 0, j)),
        ],
        out_specs=pl.BlockSpec((1, N_MOD, cw), lambda l, j: (l, 0, j)),
        out_shape=jax.ShapeDtypeStruct((DEPTH, N_MOD, n), F32),
        compiler_params=_cparams(("arbitrary", "arbitrary")),
        name="modulation",
    )(cvec, ada_w, ada_b.reshape(DEPTH, 1, n))
    return out.reshape(DEPTH, N_MOD, 6, D)


def _stream_specs(x, tm):
    if not isinstance(x, tuple):
        return [pl.BlockSpec((tm, D), lambda i: (i, 0))], (x,)
    n_p = NP // tm
    return [pl.BlockSpec((tm, D), lambda i: (jnp.minimum(i, n_p - 1), 0)),
            pl.BlockSpec((tm, D), lambda i: (jnp.maximum(i - n_p, 0), 0))], x


def _stream_tile(refs, tm):
    if len(refs) == 1:
        return refs[0][...]
    return jnp.where(pl.program_id(0) < NP // tm, refs[0][...], refs[1][...])


def _proj_kernel(x_ref, nw_ref, mod_ref, w_ref, o_ref, *copy_refs, cw):
    h = _norm_mod(x_ref[...], nw_ref[...], mod_ref[0, 0:1, :], mod_ref[0, 1:2, :]).astype(BF16)
    for j in range(w_ref.shape[1] // cw):
        val = _dot(h, w_ref[:, j * cw:(j + 1) * cw])
        o_ref[:, j * cw:(j + 1) * cw] = val.astype(o_ref.dtype)
        col = j * cw - D
        if 0 <= col < len(copy_refs) * D:
            copy_refs[col // D][:, col % D:col % D + cw] = val


def _norm_proj(x, nw, mod, w, tile0, n_tiles, n_copy=0, out_dtype=F32):
    n = w.shape[1]
    rows = n_tiles * TM
    outs = pl.pallas_call(
        functools.partial(_proj_kernel, cw=512),
        grid=(n_tiles,),
        in_specs=[
            pl.BlockSpec((TM, D), lambda i: (tile0 + i, 0)),
            pl.BlockSpec((1, D), lambda i: (0, 0)),
            pl.BlockSpec((1, 6, D), lambda i: (_mod_group(tile0 + i, TM), 0, 0)),
            pl.BlockSpec((D, n), lambda i: (0, 0)),
        ],
        out_specs=[pl.BlockSpec((TM, n), lambda i: (i, 0))] + [pl.BlockSpec((TM, D), lambda i: (i, 0))] * n_copy,
        out_shape=[jax.ShapeDtypeStruct((rows, n), out_dtype)] + [jax.ShapeDtypeStruct((rows, D), F32)] * n_copy,
        compiler_params=_cparams(("arbitrary",)),
        name="norm_proj",
    )(x, nw.reshape(1, D), mod, w)
    return outs if n_copy else outs[0]


def _hgrn_proj_kernel(*refs, cw, n_x):
    nw_ref, mod_ref, lb_ref, w_ref, o_ref, tot_ref = refs[n_x:]
    x = _stream_tile(refs[:n_x], TM)
    h = _norm_mod(x, nw_ref[...], mod_ref[0, 0:1, :], mod_ref[0, 1:2, :]).astype(BF16)
    n_ck = TM // CHUNK
    mins = {}
    for j in range(w_ref.shape[1] // cw):
        val = _dot(h, w_ref[:, j * cw:(j + 1) * cw])
        sec, col = divmod(j * cw, D)
        if sec in (1, 2):
            lb = lb_ref[:, col:col + cw]
            val = jnp.log(lb + (1.0 - lb) * jax.nn.sigmoid(val))
            for c in range(n_ck):
                tot = jnp.sum(val[c * CHUNK:(c + 1) * CHUNK], axis=0, keepdims=True)
                m = jnp.min(tot, axis=-1, keepdims=True)
                key = (sec - 1, c)
                mins[key] = m if key not in mins else jnp.minimum(mins[key], m)
        o_ref[:, j * cw:(j + 1) * cw] = val
    tot_ref[0] = jnp.concatenate(
        [jnp.broadcast_to(mins[(d, c)], (1, LANES)) for c in range(n_ck) for d in range(2)], axis=0)


def _hgrn_proj(x, nw, mod, lb, w):
    n = w.shape[1]
    n_ck = TM // CHUNK
    assert 2 * n_ck == SUBLANES
    x_specs, xs = _stream_specs(x, TM)
    proj, tot = pl.pallas_call(
        functools.partial(_hgrn_proj_kernel, cw=512, n_x=len(xs)),
        grid=(NT // TM,),
        in_specs=x_specs + [
            pl.BlockSpec((1, D), lambda i: (0, 0)),
            pl.BlockSpec((1, 6, D), lambda i: (_mod_group(i, TM), 0, 0)),
            pl.BlockSpec((1, HG_HEADS * HG_DK), lambda i: (0, 0)),
            pl.BlockSpec((D, n), lambda i: (0, 0)),
        ],
        out_specs=[pl.BlockSpec((TM, n), lambda i: (i, 0)),
                   pl.BlockSpec((1, SUBLANES, LANES), lambda i: (i, 0, 0))],
        out_shape=[jax.ShapeDtypeStruct((NT, n), F32),
                   jax.ShapeDtypeStruct((NT // TM, SUBLANES, LANES), F32)],
        compiler_params=_cparams(("arbitrary",)),
        name="hgrn_proj",
    )(*xs, nw.reshape(1, D), mod, lb.reshape(1, HG_HEADS * HG_DK), w)
    return proj, tot[:, :, 0].reshape(NT // CHUNK, 2)


def _split3(x):
    hi = x.astype(BF16)
    r = x - hi.astype(F32)
    mid = r.astype(BF16)
    lo = (r - mid.astype(F32)).astype(BF16)
    return hi, mid, lo


def _gla_safe(q, kk, v, g3, st_ref, row_refs, rev):
    bs_ref, ks_ref, kd8_ref = row_refs

    n_stack = 4
    t_io = lax.broadcasted_iota(I32, (n_stack * CHUNK, CHUNK), 0)
    u_io = lax.broadcasted_iota(I32, (n_stack * CHUNK, CHUNK), 1)
    which = t_io // CHUNK
    tt = t_io - which * CHUNK
    b16 = (tt // SUB) * SUB
    b8 = (tt // SUBLANES) * SUBLANES
    if rev:
        lim = jnp.where(which == 0, tt, jnp.where(which == 1, b16 + SUB, jnp.where(which == 2, b8 + SUBLANES, b8)))
        pick = u_io >= lim
    else:
        lim = jnp.where(which == 0, tt, jnp.where(which == 1, b16 - 1,
                                                  jnp.where(which == 2, b8 - 1, b8 + SUBLANES - 1)))
        pick = u_io <= lim
    tri = jnp.where(pick, 1.0, 0.0).astype(BF16)
    hi, mid, lo = g3
    cs = _dot(tri, hi) + _dot(tri, mid) + _dot(tri, lo)
    b, r16, r8, e8 = (cs[j * CHUNK:(j + 1) * CHUNK] for j in range(n_stack))
    last = 0 if rev else CHUNK - 1
    tot = b[last:last + 1]

    qe = (q * jnp.exp(b)).astype(BF16)
    qd = (q * jnp.exp(b - r16)).astype(BF16)
    qd8 = q * jnp.exp(b - r8)
    kdec = (kk * jnp.exp(tot - b)).astype(BF16)
    vb = v.astype(BF16)
    dec_tot = jnp.exp(tot)
    kd8 = kk * jnp.exp(e8 - b)
    for h in range(HG_HEADS):
        sl = slice(h * LANES, (h + 1) * LANES)
        bs_ref[h] = b[:, sl]
        ks_ref[h] = kk[:, sl]
        kd8_ref[h] = kd8[:, sl]

    def row(ref, h, s):
        return jnp.broadcast_to(ref[h, s:s + 1, :], (SUBLANES, LANES))

    lane = lax.broadcasted_iota(I32, (SUBLANES, LANES), 1)
    row8 = lax.broadcasted_iota(I32, (SUBLANES, LANES), 0)
    a_in = [[] for _ in range(HG_HEADS)]
    for tb in range(CHUNK // SUBLANES):
        t0 = tb * SUBLANES
        blk0 = (t0 // SUB) * SUB
        other = blk0 + SUBLANES if t0 == blk0 else blk0
        keep = (row8 + t0 <= lane) if rev else (row8 + t0 >= lane)
        for h in range(HG_HEADS):
            sl = slice(h * LANES, (h + 1) * LANES)
            q_t = q[t0:t0 + SUBLANES, sl]
            b_t = b[t0:t0 + SUBLANES, sl]
            acc = jnp.zeros((SUBLANES, LANES), F32)
            for s in range(t0, t0 + SUBLANES):
                p = q_t * jnp.exp(b_t - row(bs_ref, h, s)) * row(ks_ref, h, s)
                acc = jnp.where(lane == s, jnp.sum(p, axis=-1, keepdims=True), acc)
            acc = jnp.where(keep, acc, 0.0)
            if (t0 == blk0) == rev:
                qd8_t = qd8[t0:t0 + SUBLANES, sl]
                for s in range(other, other + SUBLANES):
                    p = qd8_t * row(kd8_ref, h, s)
                    acc = jnp.where(lane == s, jnp.sum(p, axis=-1, keepdims=True), acc)
            a_in[h].append(acc)

    n_blk = CHUNK // SUB
    o_heads = []
    for h in range(HG_HEADS):
        sl = slice(h * LANES, (h + 1) * LANES)
        st = st_ref[sl, :]
        o_h = _dot_nt(qe[:, sl], st.astype(BF16))
        rows = []
        for i in range(n_blk):
            lo_r, hi_r = ((i + 1) * SUB, CHUNK) if rev else (0, i * SUB)
            if hi_r == lo_r:
                rows.append(jnp.zeros((SUB, CHUNK), F32))
                continue
            r_i = r16[i * SUB:i * SUB + 1, sl]
            kd = kk[lo_r:hi_r, sl] * jnp.exp(r_i - b[lo_r:hi_r, sl])
            pads = [jnp.zeros((lo_r, LANES), F32)] if lo_r else []
            pads_hi = [jnp.zeros((CHUNK - hi_r, LANES), F32)] if hi_r < CHUNK else []
            kd = jnp.concatenate(pads + [kd] + pads_hi, axis=0).astype(BF16)
            rows.append(_dot_nt(qd[i * SUB:(i + 1) * SUB, sl], kd))
        a1 = jnp.concatenate(rows, axis=0) + jnp.concatenate(a_in[h], axis=0)[:, :CHUNK]
        o_h = o_h + _dot(a1.astype(BF16), vb[:, sl])
        o_heads.append(o_h)
        st_ref[sl, :] = st * dec_tot[:, sl] + _dot_tn(vb[:, sl], kdec[:, sl])
    return jnp.concatenate(o_heads, axis=1)


def _gla_fast(q, kk, v, b, st_ref, rev):
    qe = (q * jnp.exp(b)).astype(BF16)
    ke32 = kk * jnp.exp(-b)
    ke = ke32.astype(BF16)
    vb = v.astype(BF16)
    t_io = lax.broadcasted_iota(I32, (GLA_BLK, GLA_BLK), 0)
    s_io = lax.broadcasted_iota(I32, (GLA_BLK, GLA_BLK), 1)
    same = (t_io // CHUNK) == (s_io // CHUNK)
    keep = jnp.logical_and(same, (s_io >= t_io) if rev else (s_io <= t_io))
    order = range(GLA_CPS - 1, -1, -1) if rev else range(GLA_CPS)
    o_heads = []
    for h in range(HG_HEADS):
        sl = slice(h * LANES, (h + 1) * LANES)
        a = jnp.where(keep, _dot_nt(qe[:, sl], ke[:, sl]), 0.0).astype(BF16)
        o_h = _dot(a, vb[:, sl])
        st = st_ref[sl, :]
        inter = [None] * GLA_CPS
        for k in order:
            r = slice(k * CHUNK, (k + 1) * CHUNK)
            last = k * CHUNK if rev else (k + 1) * CHUNK - 1
            dec_tot = jnp.exp(b[last:last + 1, sl])
            inter[k] = _dot_nt(qe[r, sl], st.astype(BF16))
            kdec = (ke32[r, sl] * dec_tot).astype(BF16)
            st = st * dec_tot + _dot_tn(vb[r, sl], kdec)
        st_ref[sl, :] = st
        o_heads.append(o_h + jnp.concatenate(inter, axis=0))
    return jnp.concatenate(o_heads, axis=1)


def _gla_prep(g, rev):
    kk = 1.0 - jnp.exp(g)
    g3 = _split3(g)
    t_io = lax.broadcasted_iota(I32, (GLA_BLK, GLA_BLK), 0)
    u_io = lax.broadcasted_iota(I32, (GLA_BLK, GLA_BLK), 1)
    same = (t_io // CHUNK) == (u_io // CHUNK)
    tri = jnp.where(jnp.logical_and(same, (u_io >= t_io) if rev else (u_io <= t_io)), 1.0, 0.0).astype(BF16)
    b = _dot(tri, g3[0]) + _dot(tri, g3[1]) + _dot(tri, g3[2])
    return kk, g3, b


def _gla_kernel(mild_ref, qf_ref, gf_ref, vf_ref, qb_ref, gb_ref, vb_ref, s0f_ref, s0b_ref,
                of_ref, ob_ref, sf_ref, sb_ref, stf, stb, bs, ks, kd8):
    i = pl.program_id(0)
    n_p = NP // GLA_BLK
    is_prompt = i < n_p
    c = jnp.where(is_prompt, i % (SEQ // GLA_BLK), (i - n_p) % (DEC_SEQ // GLA_BLK))
    n_c = jnp.where(is_prompt, SEQ // GLA_BLK, DEC_SEQ // GLA_BLK)

    @pl.when(jnp.logical_and(c == 0, is_prompt))
    def _():
        stf[...] = jnp.zeros_like(stf)
        stb[...] = jnp.zeros_like(stb)

    @pl.when(jnp.logical_and(c == 0, jnp.logical_not(is_prompt)))
    def _():
        for h in range(HG_HEADS):
            sl = slice(h * LANES, (h + 1) * LANES)
            stf[sl, :] = s0f_ref[0, sl, :].T
            stb[sl, :] = s0b_ref[0, sl, :].T

    rows = (bs, ks, kd8)
    rf = [slice(k * CHUNK, (k + 1) * CHUNK) for k in range(GLA_CPS)]
    rb = rf[::-1]
    kk_f, g3_f, b_f = _gla_prep(gf_ref[...], False)
    kk_b, g3_b, b_b = _gla_prep(gb_ref[...], True)
    mild = mild_ref[i] != 0

    @pl.when(mild)
    def _():
        of_ref[...] = _gla_fast(qf_ref[...], kk_f, vf_ref[...], b_f, stf, False)
        ob_ref[...] = _gla_fast(qb_ref[...], kk_b, vb_ref[...], b_b, stb, True)

    @pl.when(jnp.logical_not(mild))
    def _():
        for k in range(GLA_CPS):
            r = rf[k]
            of_ref[r, :] = _gla_safe(qf_ref[r, :], kk_f[r], vf_ref[r, :], [p[r] for p in g3_f], stf, rows, False)
            r = rb[k]
            ob_ref[r, :] = _gla_safe(qb_ref[r, :], kk_b[r], vb_ref[r, :], [p[r] for p in g3_b], stb, rows, True)

    @pl.when(jnp.logical_and(c == n_c - 1, is_prompt))
    def _():
        for h in range(HG_HEADS):
            sl = slice(h * LANES, (h + 1) * LANES)
            sf_ref[0, sl, :] = stf[sl, :].T
            sb_ref[0, sl, :] = stb[sl, :].T


def _gla(proj, chunk_tot, s0f, s0b):
    n_p = NP // GLA_BLK
    cp = SEQ // GLA_BLK
    cs = DEC_SEQ // GLA_BLK

    def bwd_blk(i):
        jp = (i // cp) * cp + (cp - 1 - i % cp)
        j = i - n_p
        js = n_p + (j // cs) * cs + (cs - 1 - j % cs)
        return jnp.where(i < n_p, jp, js)

    def req(i):
        return jnp.maximum(i - n_p, 0) // cs

    def preq(i):
        return jnp.minimum(i // cp, BATCH - 1)

    steps = jnp.arange(NT // GLA_BLK, dtype=I32)
    blk_tot = chunk_tot.reshape(NT // GLA_BLK, GLA_CPS, 2).min(axis=1)
    mild = jnp.logical_and(blk_tot[:, 0] >= -GLA_FAST_MAX,
                           blk_tot[bwd_blk(steps), 1] >= -GLA_FAST_MAX).astype(I32)
    hk = HG_HEADS * HG_DK
    grid_spec = pltpu.PrefetchScalarGridSpec(
        num_scalar_prefetch=1,
        grid=(NT // GLA_BLK,),
        in_specs=[
            pl.BlockSpec((GLA_BLK, D), lambda i, m: (i, 0)),
            pl.BlockSpec((GLA_BLK, D), lambda i, m: (i, 1)),
            pl.BlockSpec((GLA_BLK, D), lambda i, m: (i, 3)),
            pl.BlockSpec((GLA_BLK, D), lambda i, m: (bwd_blk(i), 0)),
            pl.BlockSpec((GLA_BLK, D), lambda i, m: (bwd_blk(i), 2)),
            pl.BlockSpec((GLA_BLK, D), lambda i, m: (bwd_blk(i), 3)),
            pl.BlockSpec((1, hk, HG_DV), lambda i, m: (req(i), 0, 0)),
            pl.BlockSpec((1, hk, HG_DV), lambda i, m: (req(i), 0, 0)),
        ],
        out_specs=[
            pl.BlockSpec((GLA_BLK, D), lambda i, m: (i, 0)),
            pl.BlockSpec((GLA_BLK, D), lambda i, m: (bwd_blk(i), 0)),
            pl.BlockSpec((1, hk, HG_DV), lambda i, m: (preq(i), 0, 0)),
            pl.BlockSpec((1, hk, HG_DV), lambda i, m: (preq(i), 0, 0)),
        ],
        scratch_shapes=[pltpu.VMEM((hk, HG_DV), F32), pltpu.VMEM((hk, HG_DV), F32)]
        + [pltpu.VMEM((HG_HEADS, CHUNK, HG_DK), F32)] * 3,
    )
    return pl.pallas_call(
        _gla_kernel,
        grid_spec=grid_spec,
        out_shape=[
            jax.ShapeDtypeStruct((NT, D), F32),
            jax.ShapeDtypeStruct((NT, D), F32),
            jax.ShapeDtypeStruct((BATCH, hk, HG_DV), F32),
            jax.ShapeDtypeStruct((BATCH, hk, HG_DV), F32),
        ],
        compiler_params=_cparams(("arbitrary",)),
        name="gla_scan",
    )(mild, proj, proj, proj, proj, proj, proj, s0f, s0b)


def _head_pair_masks():
    upper = lax.broadcasted_iota(I32, (1, LANES), 1) >= NA_HD
    return upper


def _attn_ctx_kernel(q_ref, k_ref, v_ref, o_ref):
    upper = _head_pair_masks()
    for p in range(NA_HEADS // 2):
        sl = slice(p * LANES, (p + 1) * LANES)
        qp = q_ref[:, sl] * (NA_HD ** -0.5)
        kp = k_ref[:, sl].astype(BF16)
        vp = v_ref[:, sl].astype(BF16)
        outs = []
        for u in range(2):
            qm = jnp.where(upper if u else jnp.logical_not(upper), qp, 0.0).astype(BF16)
            s = _dot_nt(qm, kp)
            e = jnp.exp(s - jnp.max(s, axis=-1, keepdims=True))
            outs.append(_dot(e.astype(BF16), vp) / jnp.sum(e, axis=-1, keepdims=True))
        o_ref[:, sl] = jnp.where(upper, outs[1], outs[0]).astype(o_ref.dtype)


def _attn_ctx(qkv):
    return pl.pallas_call(
        _attn_ctx_kernel,
        grid=(BATCH,),
        in_specs=[
            pl.BlockSpec((SEQ, D), lambda b: (b, 0)),
            pl.BlockSpec((SEQ, D), lambda b: (b, 1)),
            pl.BlockSpec((SEQ, D), lambda b: (b, 2)),
        ],
        out_specs=pl.BlockSpec((SEQ, D), lambda b: (b, 0)),
        out_shape=jax.ShapeDtypeStruct((NP, D), BF16),
        compiler_params=_cparams(("arbitrary",)),
        name="attn_ctx",
    )(qkv, qkv, qkv)


def _attn_lat_kernel(q_ref, k_ref, v_ref, kc_ref, vc_ref, strip_ref, rowmask_ref, o_ref, *, tq):
    rows = DEC_SEQ // GRID_W
    kr = min(WIN_R, rows)
    rpt = tq // GRID_W

    def bias_rows(u, r, k_lo, k_hi):
        first = rows - 1 - r + k_lo
        var = first % 2
        strip = strip_ref[u, var, :, (first - var) * GRID_W:(first - var + k_hi - k_lo) * GRID_W]
        return strip + rowmask_ref[r:r + 1, k_lo * GRID_W:k_hi * GRID_W]

    upper = _head_pair_masks()
    mine = (jnp.logical_not(upper), upper)
    kl = k_ref[...].astype(BF16)
    kc = kc_ref[0].astype(BF16)
    vl = [jnp.where(mine[u], v_ref[...], 1.0).astype(BF16) for u in range(2)]
    vc = [jnp.where(mine[u], vc_ref[0], 1.0).astype(BF16) for u in range(2)]
    for t in range(DEC_SEQ // tq):
        rs = slice(t * tq, (t + 1) * tq)
        k_lo = min(max(t * rpt - kr // 2, 0), rows - kr) // 2 * 2
        k_hi = -(-(min(max(t * rpt + rpt - 1 - kr // 2, 0), rows - kr) + kr) // 2) * 2
        ks = slice(k_lo * GRID_W, k_hi * GRID_W)
        qp = q_ref[rs, :] * (NA_HD ** -0.5)
        outs = []
        for u in range(2):
            qm = jnp.where(mine[u], qp, 0.0).astype(BF16)
            bias = jnp.concatenate([bias_rows(u, t * rpt + a, k_lo, k_hi) for a in range(rpt)], axis=0)
            s_l = _dot_nt(qm, kl[ks]) + bias
            s_c = _dot_nt(qm, kc)
            m = jnp.maximum(jnp.max(s_l, axis=-1, keepdims=True), jnp.max(s_c, axis=-1, keepdims=True))
            acc = (_dot(jnp.exp(s_l - m).astype(BF16), vl[u][ks])
                   + _dot(jnp.exp(s_c - m).astype(BF16), vc[u]))
            outs.append(acc / pltpu.roll(acc, NA_HD, 1))
        o_ref[rs, :] = jnp.where(upper, outs[1], outs[0]).astype(o_ref.dtype)


def _attn_lat(qkv, k_ctx, v_ctx, strips, rowmask):
    npair = NA_HEADS // 2
    return pl.pallas_call(
        functools.partial(_attn_lat_kernel, tq=256),
        grid=(npair, DEC_BATCH),
        in_specs=[
            pl.BlockSpec((DEC_SEQ, LANES), lambda p, b: (b, p)),
            pl.BlockSpec((DEC_SEQ, LANES), lambda p, b: (b, npair + p)),
            pl.BlockSpec((DEC_SEQ, LANES), lambda p, b: (b, 2 * npair + p)),
            pl.BlockSpec((1, PAST_LEN, LANES), lambda p, b: (b, 0, p)),
            pl.BlockSpec((1, PAST_LEN, LANES), lambda p, b: (b, 0, p)),
            pl.BlockSpec((2, 2, GRID_W, STRIP_W), lambda p, b: (p, 0, 0, 0)),
            pl.BlockSpec((DEC_SEQ // GRID_W, DEC_SEQ), lambda p, b: (0, 0)),
        ],
        out_specs=pl.BlockSpec((DEC_SEQ, LANES), lambda p, b: (b, p)),
        out_shape=jax.ShapeDtypeStruct((NS, D), BF16),
        compiler_params=_cparams(("arbitrary", "arbitrary")),
        name="attn_lat",
    )(qkv, qkv, qkv, k_ctx, v_ctx, strips, rowmask)


def _latent_bias_tables(rpb):
    rows = DEC_SEQ // GRID_W
    kr = min(WIN_R, rows)
    ndr, ndc = 2 * WIN_R - 1, 2 * WIN_C - 1
    qc = np.arange(GRID_W)
    kc = np.arange(GRID_W)
    ws = np.clip(qc - WIN_C // 2, 0, GRID_W - WIN_C)
    col_ok = (kc[None, :] >= ws[:, None]) & (kc[None, :] < ws[:, None] + WIN_C)
    dc = np.clip(kc[None, :] - qc[:, None] + WIN_C - 1, 0, ndc - 1)
    onehot = (dc[None, :, :] == np.arange(ndc)[:, None, None]).astype(np.float32)
    t2 = jnp.einsum('hrc,cqk->hqrk', rpb.astype(F32), jnp.asarray(onehot),
                    precision=lax.Precision.HIGHEST)
    t2 = jnp.where(jnp.asarray(col_ok)[None, :, None, :], t2, NEG)
    lead = rows - WIN_R
    n_tiles = 2 * rows - 1

    def neg(n):
        return jnp.full((NA_HEADS, GRID_W, n, GRID_W), NEG, F32)

    strip = jnp.concatenate([neg(lead), t2, neg(n_tiles - lead - ndr)], axis=2)
    strip = strip.reshape(NA_HEADS, GRID_W, n_tiles * GRID_W)

    def pad(a):
        return jnp.pad(a, ((0, 0), (0, 0), (0, STRIP_W - a.shape[-1])), constant_values=NEG)

    strips = jnp.stack([pad(strip), pad(strip[:, :, GRID_W:])], axis=1)
    r = np.arange(rows)
    k0 = np.clip(r - kr // 2, 0, rows - kr)
    krow = np.arange(DEC_SEQ) // GRID_W
    row_ok = (krow[None, :] >= k0[:, None]) & (krow[None, :] < k0[:, None] + kr)
    rowmask = jnp.asarray(np.where(row_ok, 0.0, NEG).astype(np.float32))
    return strips, rowmask


def _hgrn_mix_out(refs, mod_ref, tile):
    of_ref, ob_ref, gate_ref, xp_ref, xs_ref, gn_ref, w_ref = refs
    x = jnp.where(tile < NP // MOE_TM, xp_ref[...], xs_ref[...])
    o = of_ref[...] + ob_ref[...]
    gn = gn_ref[...]
    segs = []
    for h in range(HG_HEADS):
        seg = o[:, h * LANES:(h + 1) * LANES]
        ms = jnp.mean(seg * seg, axis=-1, keepdims=True)
        segs.append(seg * lax.rsqrt(ms + EPS) * gn)
    y = (jnp.concatenate(segs, axis=1) * _silu(gate_ref[...])).astype(BF16)
    return x + mod_ref[0, 2:3, :] * _dot(y, w_ref[...])


def _attn_mix_out(refs, mod_ref, tile):
    ap_ref, as_ref, x_ref, w_ref = refs
    a = jnp.where(tile < NP // MOE_TM, ap_ref[...], as_ref[...])
    return x_ref[...] + mod_ref[0, 2:3, :] * _dot(a, w_ref[...])


def _route_kernel(*refs, mix_out, n_mix):
    mix_refs = refs[:n_mix]
    (nw_ref, mod_ref, wrh_ref, wrl_ref, rb_ref, xnew_ref, xs_ref, pos_ref, wt_ref, npc_ref,
     hb_s, pos_s, used_s) = refs[n_mix:]
    i = pl.program_id(0)
    cur = lax.rem(i, 2)
    tile = jnp.minimum(i, N_TILES - 1)

    @pl.when(i == 0)
    def _():
        hb_s[...] = jnp.zeros_like(hb_s)
        pos_s[...] = jnp.full(pos_s.shape, -1, I32)
        used_s[0] = 0.0
        used_s[1] = 0.0

    live = i < N_TILES

    def step(slot):
        x = mix_out(mix_refs, mod_ref, tile)
        xnew_ref[...] = x
        hb, pos, used = _route_select(x, nw_ref, mod_ref, wrh_ref, wrl_ref, rb_ref, pos_ref, wt_ref, npc_ref)
        hb_s[slot] = hb
        pos_s[slot] = jnp.where(live, pos, -1)
        used_s[slot] = jnp.where(live, used, 0.0)
        _route_sort(xs_ref, hb_s[1 - slot], pos_s[1 - slot], used_s[1 - slot])

    for slot in range(2):
        pl.when(cur == slot)(functools.partial(step, slot))


def _route_sort(xs_ref, hb, pos, used):
    pos16 = [pos[k:k + 1].astype(jnp.int16) for k in range(TOP_K)]

    def fill(r0, n):
        j16 = (lax.broadcasted_iota(I32, (n, MOE_TM), 0) + r0).astype(jnp.int16)
        onehot = jnp.zeros((n, MOE_TM), BF16)
        for k in range(TOP_K):
            onehot = jnp.where(j16 == pos16[k], jnp.ones((), BF16), onehot)
        xs_ref[r0:r0 + n, :] = _dot(onehot, hb).astype(BF16)

    fill(0, ROWS_MAIN)
    for r0 in range(ROWS_MAIN, R1, ROW_BLK):
        pl.when(used > r0)(functools.partial(fill, r0, ROW_BLK))

        @pl.when(used <= r0)
        def _(r0=r0):
            xs_ref[r0:r0 + ROW_BLK, :] = jnp.zeros((ROW_BLK, D), BF16)


def _route_select(x, nw_ref, mod_ref, wrh_ref, wrl_ref, rb_ref, pos_ref, wt_ref, npc_ref):
    h = _norm_mod(x, nw_ref[...], mod_ref[0, 3:4, :], mod_ref[0, 4:5, :])
    hb = h.astype(BF16)
    hl = (h - hb.astype(F32)).astype(BF16)
    wrh = wrh_ref[...]
    logits = _dot_nt(wrh, hb) + _dot_nt(wrh, hl) + _dot_nt(wrl_ref[...], hb)
    scores = jax.nn.sigmoid(logits)
    sel = scores + rb_ref[...]

    gsz = N_EXPERTS // N_GROUPS
    sub = lax.broadcasted_iota(I32, (gsz, MOE_TM), 0)
    ninf = -jnp.inf
    gs_rows = []
    for gi in range(N_GROUPS):
        blk = sel[gi * gsz:(gi + 1) * gsz]
        m1 = jnp.max(blk, axis=0, keepdims=True)
        first = jnp.min(jnp.where(blk == m1, sub, gsz), axis=0, keepdims=True)
        m2 = jnp.max(jnp.where(sub == first, ninf, blk), axis=0, keepdims=True)
        gs_rows.append(m1 + m2)
    cur = jnp.concatenate(gs_rows, axis=0)
    gidx = lax.broadcasted_iota(I32, (N_GROUPS, MOE_TM), 0)
    gsel = jnp.zeros((N_GROUPS, MOE_TM), F32)
    for _ in range(TOPK_GROUPS):
        m = jnp.max(cur, axis=0, keepdims=True)
        first = jnp.min(jnp.where(cur == m, gidx, N_GROUPS), axis=0, keepdims=True)
        hit = gidx == first
        gsel = jnp.where(hit, 1.0, gsel)
        cur = jnp.where(hit, ninf, cur)
    emask = jnp.concatenate(
        [jnp.broadcast_to(gsel[gi:gi + 1], (gsz, MOE_TM)) for gi in range(N_GROUPS)], axis=0)
    masked = jnp.where(emask > 0.5, sel, ninf)
    eidx = lax.broadcasted_iota(I32, (N_EXPERTS, MOE_TM), 0)
    chosen = jnp.zeros((N_EXPERTS, MOE_TM), F32)
    hits, wsel = [], []
    for _ in range(TOP_K):
        m = jnp.max(masked, axis=0, keepdims=True)
        first = jnp.min(jnp.where(masked == m, eidx, N_EXPERTS), axis=0, keepdims=True)
        hit = eidx == first
        hits.append(hit)
        wsel.append(jnp.sum(jnp.where(hit, scores, 0.0), axis=0, keepdims=True))
        chosen = jnp.where(hit, 1.0, chosen)
        masked = jnp.where(hit, ninf, masked)
    wsum = wsel[0]
    for w in wsel[1:]:
        wsum = wsum + w

    n_io = lax.broadcasted_iota(I32, (MOE_TM, MOE_TM), 0)
    m_io = lax.broadcasted_iota(I32, (MOE_TM, MOE_TM), 1)
    earlier = jnp.where(n_io < m_io, 1.0, 0.0).astype(BF16)
    rank = _dot(chosen.astype(BF16), earlier)
    cnt = jnp.sum(chosen, axis=1, keepdims=True)
    npc = jnp.floor((cnt + (PIECE - 1)) * (1.0 / PIECE))
    e_io = lax.broadcasted_iota(I32, (N_EXPERTS, N_EXPERTS), 0)
    f_io = lax.broadcasted_iota(I32, (N_EXPERTS, N_EXPERTS), 1)
    below = jnp.where(f_io < e_io, 1.0, 0.0).astype(BF16)
    npc_l = jnp.broadcast_to(npc, (N_EXPERTS, LANES))
    start = _dot(below, npc_l.astype(BF16))[:, 0:1] * PIECE
    slot = start + rank

    pos_rows, wt_rows = [], []
    for k in range(TOP_K):
        pos_k = jnp.sum(jnp.where(hits[k], slot, 0.0), axis=0, keepdims=True).astype(I32)
        pos_rows.append(pos_k)
        wt_rows.append(wsel[k] / wsum * ROUTED_SCALE)

    pad = SUBLANES - TOP_K
    pos = jnp.concatenate(pos_rows + [jnp.full((pad, MOE_TM), -1, I32)], axis=0)
    pos_ref[...] = pos
    wt_ref[...] = jnp.concatenate(wt_rows + [jnp.zeros((pad, MOE_TM), F32)], axis=0)
    npc_ref[0] = npc_l.astype(I32)
    return hb, pos, jnp.sum(npc) * PIECE


def _hgrn_mix_inputs(o_f, o_b, proj, x_pair, gn_w, w_out):
    n_p = NP // MOE_TM

    def specs(tile):
        return [pl.BlockSpec((MOE_TM, D), lambda i: (tile(i), 0)),
                pl.BlockSpec((MOE_TM, D), lambda i: (tile(i), 0)),
                pl.BlockSpec((MOE_TM, D), lambda i: (tile(i), 4)),
                pl.BlockSpec((MOE_TM, D), lambda i: (jnp.minimum(tile(i), n_p - 1), 0)),
                pl.BlockSpec((MOE_TM, D), lambda i: (jnp.maximum(tile(i) - n_p, 0), 0)),
                pl.BlockSpec((1, HG_DV), lambda i: (0, 0)),
                pl.BlockSpec((D, D), lambda i: (0, 0))]

    return _hgrn_mix_out, (o_f, o_b, proj, x_pair[0], x_pair[1], gn_w.reshape(1, HG_DV), w_out), specs


def _attn_mix_inputs(a_p, a_s, x, w_o):
    n_p = NP // MOE_TM

    def specs(tile):
        return [pl.BlockSpec((MOE_TM, D), lambda i: (jnp.minimum(tile(i), n_p - 1), 0)),
                pl.BlockSpec((MOE_TM, D), lambda i: (jnp.maximum(tile(i) - n_p, 0), 0)),
                pl.BlockSpec((MOE_TM, D), lambda i: (tile(i), 0)),
                pl.BlockSpec((D, D), lambda i: (0, 0))]

    return _attn_mix_out, (a_p, a_s, x, w_o), specs


def _route(mix, nw, mod, wr_hi, wr_lo, rbias):
    mix_out, mix_args, mix_specs = mix

    def tile(i):
        return jnp.minimum(i, N_TILES - 1)

    return pl.pallas_call(
        functools.partial(_route_kernel, mix_out=mix_out, n_mix=len(mix_args)),
        grid=(N_TILES + 2,),
        in_specs=mix_specs(tile) + [
            pl.BlockSpec((1, D), lambda i: (0, 0)),
            pl.BlockSpec((1, 6, D), lambda i: (_mod_group(tile(i), MOE_TM), 0, 0)),
            pl.BlockSpec((N_EXPERTS, D), lambda i: (0, 0)),
            pl.BlockSpec((N_EXPERTS, D), lambda i: (0, 0)),
            pl.BlockSpec((N_EXPERTS, 1), lambda i: (0, 0)),
        ],
        out_specs=[
            pl.BlockSpec((MOE_TM, D), lambda i: (tile(i), 0)),
            pl.BlockSpec((R1, D), lambda i: (jnp.maximum(i - 1, 0), 0)),
            pl.BlockSpec((SUBLANES, MOE_TM), lambda i: (0, tile(i))),
            pl.BlockSpec((SUBLANES, MOE_TM), lambda i: (0, tile(i))),
            pl.BlockSpec((1, N_EXPERTS, LANES), lambda i: (tile(i), 0, 0)),
        ],
        out_shape=[
            jax.ShapeDtypeStruct((NT, D), F32),
            jax.ShapeDtypeStruct((P_TOT * PIECE, D), BF16),
            jax.ShapeDtypeStruct((SUBLANES, NT), I32),
            jax.ShapeDtypeStruct((SUBLANES, NT), F32),
            jax.ShapeDtypeStruct((N_TILES, N_EXPERTS, LANES), I32),
        ],
        scratch_shapes=[pltpu.VMEM((2, MOE_TM, D), BF16), pltpu.VMEM((2, SUBLANES, MOE_TM), I32),
                        pltpu.SMEM((2,), F32)],
        compiler_params=_cparams(("arbitrary",)),
        name="moe_route",
    )(*mix_args, nw.reshape(1, D), mod, wr_hi, wr_lo, rbias.reshape(N_EXPERTS, 1))


def _piece_lists(npc):
    t, e = npc.shape
    hp = lax.Precision.HIGHEST
    npc_t = npc.T.astype(F32)
    start_t = (jnp.cumsum(npc, axis=1) - npc).T.astype(F32)
    tile_end = jnp.cumsum(npc_t, axis=1)
    n_e = tile_end[:, -1]
    pe_end = jnp.cumsum(n_e)
    pe_off = pe_end - n_e
    p = jnp.arange(P_MAX, dtype=F32)
    e_p = jnp.minimum(jnp.sum((pe_end[None, :] <= p[:, None]).astype(I32), axis=1), e - 1)
    oh_e = (e_p[:, None] == jnp.arange(e, dtype=I32)[None, :]).astype(F32)
    tab = jnp.concatenate([tile_end, start_t, npc_t, pe_off[:, None]], axis=1)
    row = jnp.dot(oh_e, tab, precision=hp)
    te_p, st_p, np_p, off_p = row[:, :t], row[:, t:2 * t], row[:, 2 * t:3 * t], row[:, 3 * t]
    local = p - off_p
    t_p = jnp.minimum(jnp.sum((te_p <= local[:, None]).astype(I32), axis=1), t - 1)
    oh_t = t_p[:, None] == jnp.arange(t, dtype=I32)[None, :]

    def pick(a):
        return jnp.sum(jnp.where(oh_t, a, 0.0), axis=1)

    src = t_p * R1B + (pick(st_p) + local - (pick(te_p) - pick(np_p))).astype(I32)
    src = jnp.concatenate([jnp.clip(src, 0, P_MAX - 1), jnp.zeros((G_PIECES,), I32)])

    nch = jnp.floor((n_e + (G_PIECES - 1)) * (1.0 / G_PIECES))
    ch_end = jnp.cumsum(nch)
    ch_off = jnp.concatenate([jnp.zeros((1,), F32), ch_end]).astype(I32)
    c = jnp.arange(-N_SLOTS, NCH + 2, dtype=F32)
    ce = jnp.minimum(jnp.sum((ch_end[None, :] <= c[:, None]).astype(I32), axis=1), e - 1)
    oh_c = (ce[:, None] == jnp.arange(e, dtype=I32)[None, :]).astype(F32)
    crow = jnp.dot(oh_c, jnp.stack([ch_end - nch, pe_off, n_e], axis=1), precision=hp)
    k_in = c - crow[:, 0]
    live = jnp.logical_and(c >= 0, c < ch_end[-1])
    cn = jnp.where(live, jnp.clip(crow[:, 2] - G_PIECES * k_in, 0, G_PIECES), 0.0)
    cs = jnp.where(cn > 0, crow[:, 1] + G_PIECES * k_in, 0.0)
    return src, ch_off, cs.astype(I32), cn.astype(I32)


def _ffn_kernel(src_ref, choff_ref, cs_ref, cn_ref, xs_in, wg_ref, wu_ref, wd_ref, xs_out,
                xbuf, ybuf, wgb, wub, wdb, gsem, ssem):
    e = pl.program_id(0)
    total = choff_ref[N_EXPERTS]

    def start_gather(ch):
        sl = lax.rem(ch + N_SLOTS, N_SLOTS)
        base = cs_ref[ch + N_SLOTS]
        n = cn_ref[ch + N_SLOTS]
        for i in range(G_PIECES):
            idx = src_ref[base + jnp.where(i < n, i, 0)]
            pltpu.make_async_copy(xs_in.at[idx], xbuf.at[sl, i], gsem.at[sl]).start(priority=i % 2)

    def wait_gather(ch):
        sl = lax.rem(ch + N_SLOTS, N_SLOTS)
        pltpu.make_async_copy(xs_in.at[pl.ds(0, G_PIECES)], xbuf.at[sl], gsem.at[sl]).wait()

    def start_scatter(ch):
        sl = lax.rem(ch + N_SLOTS, N_SLOTS)
        base = cs_ref[ch + N_SLOTS]
        n = cn_ref[ch + N_SLOTS]
        for i in range(G_PIECES):
            idx = jnp.where(i < n, src_ref[base + i], P_MAX + sl * G_PIECES + i)
            pltpu.make_async_copy(ybuf.at[sl, i], xs_out.at[idx], ssem.at[sl]).start(priority=i % 2)

    def wait_scatter(ch):
        sl = lax.rem(ch + N_SLOTS, N_SLOTS)
        pltpu.make_async_copy(ybuf.at[sl], xs_out.at[pl.ds(0, G_PIECES)], ssem.at[sl]).wait()

    @pl.when(e == 0)
    def _():
        ybuf[...] = jnp.zeros_like(ybuf)
        start_gather(0)
        start_gather(1)
        start_scatter(-3)
        start_scatter(-2)

    wgb[...] = wg_ref[0].astype(BF16)
    wub[...] = wu_ref[0].astype(BF16)
    wdb[...] = wd_ref[0].astype(BF16)

    def chunk(c, carry):
        sl = lax.rem(c, N_SLOTS)
        wait_gather(c)
        wait_scatter(c - 3)
        x = xbuf[sl].reshape(G_PIECES * PIECE, D)
        hid = (_silu(_dot(x, wgb[...])) * _dot(x, wub[...])).astype(BF16)
        start_gather(c + 2)
        start_scatter(c - 1)
        ybuf[sl] = _dot(hid, wdb[...]).astype(BF16).reshape(G_PIECES, PIECE, D)
        return carry

    lax.fori_loop(choff_ref[e], choff_ref[e + 1], chunk, 0)

    @pl.when(e == N_EXPERTS - 1)
    def _():
        start_scatter(total - 1)
        wait_gather(total)
        wait_gather(total + 1)
        wait_scatter(total - 3)
        wait_scatter(total - 2)
        wait_scatter(total - 1)


def _expert_ffn(xs, lists, layer, w_gate, w_up, w_down):
    src, ch_off, cs, cn = lists
    grid_spec = pltpu.PrefetchScalarGridSpec(
        num_scalar_prefetch=4,
        grid=(N_EXPERTS,),
        in_specs=[
            pl.BlockSpec(memory_space=pl.ANY),
            pl.BlockSpec((None, 1, D, D_EXPERT), lambda e, *_: (layer, e, 0, 0)),
            pl.BlockSpec((None, 1, D, D_EXPERT), lambda e, *_: (layer, e, 0, 0)),
            pl.BlockSpec((None, 1, D_EXPERT, D), lambda e, *_: (layer, e, 0, 0)),
        ],
        out_specs=pl.BlockSpec(memory_space=pl.ANY),
        scratch_shapes=[
            pltpu.VMEM((N_SLOTS, G_PIECES, PIECE, D), BF16),
            pltpu.VMEM((N_SLOTS, G_PIECES, PIECE, D), BF16),
            pltpu.VMEM((D, D_EXPERT), BF16),
            pltpu.VMEM((D, D_EXPERT), BF16),
            pltpu.VMEM((D_EXPERT, D), BF16),
            pltpu.SemaphoreType.DMA((N_SLOTS,)),
            pltpu.SemaphoreType.DMA((N_SLOTS,)),
        ],
    )
    out = pl.pallas_call(
        _ffn_kernel,
        grid_spec=grid_spec,
        out_shape=jax.ShapeDtypeStruct((P_TOT, PIECE, D), BF16),
        input_output_aliases={4: 0},
        compiler_params=_cparams(("arbitrary",)),
        name="moe_ffn",
    )(src, ch_off, cs, cn, xs.reshape(P_TOT, PIECE, D), w_gate, w_up, w_down)
    return out.reshape(P_TOT * PIECE, D)


def _combine_kernel(used_ref, ys_ref, pos_ref, wt_ref, x_ref, nw_ref, mod_ref, wsg_ref, wsu_ref, wsd_ref,
                    fw_ref, *rest, final):
    o_refs, acc_ref = rest[:-1], rest[-1]
    i = pl.program_id(0)
    x = x_ref[...]
    h = _norm_mod(x, nw_ref[...], mod_ref[0, 3:4, :], mod_ref[0, 4:5, :]).astype(BF16)
    shared = _dot((_silu(_dot(h, wsg_ref[...])) * _dot(h, wsu_ref[...])).astype(BF16), wsd_ref[...])

    pos16 = pos_ref[...].astype(jnp.int16)
    wt16 = wt_ref[...].astype(BF16)

    def block(r0, n):
        j16 = (lax.broadcasted_iota(I32, (MOE_TM, n), 1) + r0).astype(jnp.int16)
        wm = jnp.zeros((MOE_TM, n), BF16)
        for k in range(TOP_K):
            wm = jnp.where(j16 == pos16[:, k:k + 1], wt16[:, k:k + 1], wm)
        return _dot(wm, ys_ref[r0:r0 + n, :])

    acc_ref[...] = block(0, ROWS_MAIN)
    for r0 in range(ROWS_MAIN, R1, ROW_BLK):
        @pl.when(used_ref[i] > r0)
        def _(r0=r0):
            acc_ref[...] += block(r0, ROW_BLK)

    y = x + mod_ref[0, 5:6, :] * (acc_ref[...] + shared)
    if not final:
        o_refs[0][...] = y
        return
    ms = jnp.mean(y * y, axis=-1, keepdims=True)
    y = y * lax.rsqrt(ms + EPS) * fw_ref[...]
    is_prompt = i < NP // MOE_TM

    @pl.when(is_prompt)
    def _():
        o_refs[0][...] = y

    @pl.when(jnp.logical_not(is_prompt))
    def _():
        o_refs[1][...] = y


def _combine(ys, used, pos_t, wt_t, x, nw, mod, wsg, wsu, wsd, fw, final):
    n_p = NP // MOE_TM
    if final:
        out_specs = [pl.BlockSpec((MOE_TM, D), lambda i, u: (jnp.minimum(i, n_p - 1), 0)),
                     pl.BlockSpec((MOE_TM, D), lambda i, u: (jnp.maximum(i - n_p, 0), 0))]
        out_shape = [jax.ShapeDtypeStruct((NP, D), F32), jax.ShapeDtypeStruct((NS, D), F32)]
    else:
        out_specs = pl.BlockSpec((MOE_TM, D), lambda i, u: (i, 0))
        out_shape = jax.ShapeDtypeStruct((NT, D), F32)
    grid_spec = pltpu.PrefetchScalarGridSpec(
        num_scalar_prefetch=1,
        grid=(N_TILES,),
        in_specs=[
            pl.BlockSpec((R1, D), lambda i, u: (i, 0)),
            pl.BlockSpec((MOE_TM, SUBLANES), lambda i, u: (i, 0)),
            pl.BlockSpec((MOE_TM, SUBLANES), lambda i, u: (i, 0)),
            pl.BlockSpec((MOE_TM, D), lambda i, u: (i, 0)),
            pl.BlockSpec((1, D), lambda i, u: (0, 0)),
            pl.BlockSpec((1, 6, D), lambda i, u: (_mod_group(i, MOE_TM), 0, 0)),
            pl.BlockSpec((D, D_SHARED), lambda i, u: (0, 0)),
            pl.BlockSpec((D, D_SHARED), lambda i, u: (0, 0)),
            pl.BlockSpec((D_SHARED, D), lambda i, u: (0, 0)),
            pl.BlockSpec((1, D), lambda i, u: (0, 0)),
        ],
        out_specs=out_specs,
        scratch_shapes=[pltpu.VMEM((MOE_TM, D), F32)],
    )
    return pl.pallas_call(
        functools.partial(_combine_kernel, final=final),
        grid_spec=grid_spec,
        out_shape=out_shape,
        compiler_params=_cparams(("arbitrary",)),
        name="moe_combine",
    )(used, ys, pos_t, wt_t, x, nw.reshape(1, D), mod, wsg, wsu, wsd, fw.reshape(1, D))


def _moe(mix, nw, mod, w_router, rbias, layer, w_gate, w_up, w_down, ws_gate, ws_up, ws_down, fw, final):
    wr = w_router.astype(F32).T
    wr_hi = wr.astype(BF16)
    wr_lo = (wr - wr_hi.astype(F32)).astype(BF16)
    x, xs, pos, wt, npc = _route(mix, nw, mod, wr_hi, wr_lo, rbias.astype(F32))
    npc = npc[:, :, 0]
    lists = _piece_lists(npc)
    ys = _expert_ffn(xs, lists, layer, w_gate, w_up, w_down)
    used = (npc.sum(axis=1) * PIECE).astype(I32)
    return _combine(ys, used, pos.T, wt.T, x, nw, mod, ws_gate.astype(BF16), ws_up.astype(BF16),
                    ws_down.astype(BF16), fw, final)


def kernel(x_prompt, x_sample, state_hgrn_fwd, state_hgrn_bwd, cache_na_k, cache_na_v, c, c_ctx,
           norm1_w, norm2_w, ada_w, ada_b, hgrn_w_in, hgrn_lb_logits, hgrn_gn_w, hgrn_w_out,
           na_w_qkv, na_rpb, na_w_o, moe_w_router, moe_router_bias, moe_w_gate, moe_w_up, moe_w_down,
           shared_w_gate, shared_w_up, shared_w_down, final_norm_w):
    x = (x_prompt.reshape(NP, D), x_sample.reshape(NS, D))
    cvec = jnp.concatenate([c_ctx[None, :], c, jnp.zeros((N_MOD - 1 - DEC_BATCH, D), F32)], axis=0)
    mod = _modulation(cvec, ada_w, ada_b)
    lb_table = jnp.cumsum(jax.nn.softmax(hgrn_lb_logits.astype(F32), axis=0), axis=0)
    hk = HG_HEADS * HG_DK

    sf = sb = k_c = v_c = None
    for l in range(DEPTH):
        if l % 2 == 0:
            a = l // 2
            proj, chunk_tot = _hgrn_proj(x, norm1_w[l], mod[l], lb_table[l], hgrn_w_in[a].astype(BF16))
            o_f, o_b, sf, sb = _gla(proj, chunk_tot,
                                    state_hgrn_fwd[:, a].reshape(DEC_BATCH, hk, HG_DV),
                                    state_hgrn_bwd[:, a].reshape(DEC_BATCH, hk, HG_DV))
            mix = _hgrn_mix_inputs(o_f, o_b, proj, x, hgrn_gn_w[a], hgrn_w_out[a].astype(BF16))
        else:
            n = l // 2
            w_qkv = na_w_qkv[n].astype(BF16)
            qkv_p, k_p, v_p = _norm_proj(x, norm1_w[l], mod[l], w_qkv, 0, NP // TM, n_copy=2, out_dtype=BF16)
            qkv_s = _norm_proj(x, norm1_w[l], mod[l], w_qkv, NP // TM, NS // TM, out_dtype=BF16)
            att_p = _attn_ctx(qkv_p)
            att_s = _attn_lat(qkv_s, cache_na_k[:, n].reshape(DEC_BATCH, PAST_LEN, D),
                              cache_na_v[:, n].reshape(DEC_BATCH, PAST_LEN, D),
                              *_latent_bias_tables(na_rpb[n]))
            mix = _attn_mix_inputs(att_p, att_s, x, na_w_o[n].astype(BF16))
            k_c = k_p.reshape(BATCH, SEQ, NA_HEADS, NA_HD)
            v_c = v_p.reshape(BATCH, SEQ, NA_HEADS, NA_HD)
        x = _moe(mix, norm2_w[l], mod[l], moe_w_router[l], moe_router_bias[l], l, moe_w_gate, moe_w_up,
                 moe_w_down, shared_w_gate[l], shared_w_up[l], shared_w_down[l], final_norm_w,
                 final=(l == DEPTH - 1))

    y_prompt = x[0].reshape(BATCH, SEQ, D)
    y_sample = x[1].reshape(DEC_BATCH, DEC_SEQ, D)
    new_sf = sf.reshape(BATCH, 1, HG_HEADS, HG_DK, HG_DV)
    new_sb = sb.reshape(BATCH, 1, HG_HEADS, HG_DK, HG_DV)
    return (y_prompt, y_sample, new_sf, new_sb, k_c[:, None], v_c[:, None])
```

```python
import functools

import numpy as np
import jax
import jax.numpy as jnp
from jax import lax
from jax.experimental import pallas as pl
from jax.experimental.pallas import tpu as pltpu

F32 = jnp.float32
BF16 = jnp.bfloat16
I32 = jnp.int32

D = 1024
BATCH = 32
SEQ = 256
DEPTH = 2
DEC_BATCH = 8
DEC_SEQ = 1024
PAST_LEN = 512
GRID_W = 64
HG_HEADS = 8
HG_DK = 128
HG_DV = 128
CHUNK = 64
NA_HEADS = 16
NA_HD = 64
WIN_R = 8
WIN_C = 16
N_EXPERTS = 64
TOP_K = 6
N_GROUPS = 8
TOPK_GROUPS = 4
D_EXPERT = 256
D_SHARED = 256
ROUTED_SCALE = 2.5
EPS = 1e-6

NP = BATCH * SEQ
NS = DEC_BATCH * DEC_SEQ
NT = NP + NS
N_MOD = 16
STRIP_W = 2048
assert (2 * (DEC_SEQ // GRID_W) - 1) * GRID_W <= STRIP_W
NEG = -1e30

LANES = 128
SUBLANES = 8
BF16_ROWS = 16
VMEM_LIMIT = 56 * 1024 * 1024

TM = 256
SUB = 16
GLA_FAST_MAX = 60.0
GLA_CPS = 4
GLA_BLK = GLA_CPS * CHUNK
assert SEQ % GLA_BLK == 0 and DEC_SEQ % GLA_BLK == 0
MOE_TM = 256
PIECE = BF16_ROWS
R1 = MOE_TM * TOP_K + N_EXPERTS * (PIECE - 1) + 64
assert R1 % PIECE == 0 and R1 % LANES == 0
R1B = R1 // PIECE
ROWS_MAIN = 2048
ROW_BLK = 256
assert (R1 - ROWS_MAIN) % ROW_BLK == 0
N_TILES = NT // MOE_TM
P_MAX = N_TILES * R1B
G_PIECES = 32
NCH = P_MAX // G_PIECES + N_EXPERTS
N_SLOTS = 3
P_TOT = P_MAX + R1B
assert N_SLOTS * G_PIECES <= R1B


def _cparams(sem):
    return pltpu.CompilerParams(dimension_semantics=sem, vmem_limit_bytes=VMEM_LIMIT)


def _dot(a, b):
    return jnp.dot(a, b, preferred_element_type=F32)


def _dot_nt(a, b):
    return lax.dot_general(a, b, (((1,), (1,)), ((), ())), preferred_element_type=F32)


def _dot_tn(a, b):
    return lax.dot_general(a, b, (((0,), (0,)), ((), ())), preferred_element_type=F32)


def _silu(x):
    return x * jax.nn.sigmoid(x)


def _mod_group(i, tm):
    r = i * tm
    return jnp.where(r < NP, 0, 1 + (r - NP) // DEC_SEQ)


def _norm_mod(x, nw, shift, scale):
    ms = jnp.mean(x * x, axis=-1, keepdims=True)
    y = x * lax.rsqrt(ms + EPS) * nw
    return y * (1.0 + scale) + shift


def _mod_kernel(c_ref, w_ref, b_ref, o_ref):
    s = _silu(c_ref[...]).astype(BF16)
    o_ref[0] = _dot(s, w_ref[0].astype(BF16)) + b_ref[0]


def _modulation(cvec, ada_w, ada_b):
    cw = 1536
    n = ada_w.shape[-1]
    out = pl.pallas_call(
        _mod_kernel,
        grid=(DEPTH, n // cw),
        in_specs=[
            pl.BlockSpec((N_MOD, D), lambda l, j: (0, 0)),
            pl.BlockSpec((1, D, cw), lambda l, j: (l, 0, j)),
            pl.BlockSpec((1, 1, cw), lambda l, j: (l, 0, j)),
        ],
        out_specs=pl.BlockSpec((1, N_MOD, cw), lambda l, j: (l, 0, j)),
        out_shape=jax.ShapeDtypeStruct((DEPTH, N_MOD, n), F32),
        compiler_params=_cparams(("arbitrary", "arbitrary")),
        name="modulation",
    )(cvec, ada_w, ada_b.reshape(DEPTH, 1, n))
    return out.reshape(DEPTH, N_MOD, 6, D)


def _stream_specs(x, tm):
    if not isinstance(x, tuple):
        return [pl.BlockSpec((tm, D), lambda i: (i, 0))], (x,)
    n_p = NP // tm
    return [pl.BlockSpec((tm, D), lambda i: (jnp.minimum(i, n_p - 1), 0)),
            pl.BlockSpec((tm, D), lambda i: (jnp.maximum(i - n_p, 0), 0))], x


def _stream_tile(refs, tm):
    if len(refs) == 1:
        return refs[0][...]
    return jnp.where(pl.program_id(0) < NP // tm, refs[0][...], refs[1][...])


def _proj_kernel(x_ref, nw_ref, mod_ref, w_ref, o_ref, *copy_refs, cw):
    h = _norm_mod(x_ref[...], nw_ref[...], mod_ref[0, 0:1, :], mod_ref[0, 1:2, :]).astype(BF16)
    for j in range(w_ref.shape[1] // cw):
        val = _dot(h, w_ref[:, j * cw:(j + 1) * cw])
        o_ref[:, j * cw:(j + 1) * cw] = val.astype(o_ref.dtype)
        col = j * cw - D
        if 0 <= col < len(copy_refs) * D:
            copy_refs[col // D][:, col % D:col % D + cw] = val


def _norm_proj(x, nw, mod, w, tile0, n_tiles, n_copy=0, out_dtype=F32):
    n = w.shape[1]
    rows = n_tiles * TM
    outs = pl.pallas_call(
        functools.partial(_proj_kernel, cw=512),
        grid=(n_tiles,),
        in_specs=[
            pl.BlockSpec((TM, D), lambda i: (tile0 + i, 0)),
            pl.BlockSpec((1, D), lambda i: (0, 0)),
            pl.BlockSpec((1, 6, D), lambda i: (_mod_group(tile0 + i, TM), 0, 0)),
            pl.BlockSpec((D, n), lambda i: (0, 0)),
        ],
        out_specs=[pl.BlockSpec((TM, n), lambda i: (i, 0))] + [pl.BlockSpec((TM, D), lambda i: (i, 0))] * n_copy,
        out_shape=[jax.ShapeDtypeStruct((rows, n), out_dtype)] + [jax.ShapeDtypeStruct((rows, D), F32)] * n_copy,
        compiler_params=_cparams(("arbitrary",)),
        name="norm_proj",
    )(x, nw.reshape(1, D), mod, w)
    return outs if n_copy else outs[0]


def _hgrn_proj_kernel(*refs, cw, n_x):
    nw_ref, mod_ref, lb_ref, w_ref, o_ref, tot_ref = refs[n_x:]
    x = _stream_tile(refs[:n_x], TM)
    h = _norm_mod(x, nw_ref[...], mod_ref[0, 0:1, :], mod_ref[0, 1:2, :]).astype(BF16)
    n_ck = TM // CHUNK
    mins = {}
    for j in range(w_ref.shape[1] // cw):
        val = _dot(h, w_ref[:, j * cw:(j + 1) * cw])
        sec, col = divmod(j * cw, D)
        if sec in (1, 2):
            lb = lb_ref[:, col:col + cw]
            val = jnp.log(lb + (1.0 - lb) * jax.nn.sigmoid(val))
            for c in range(n_ck):
                tot = jnp.sum(val[c * CHUNK:(c + 1) * CHUNK], axis=0, keepdims=True)
                m = jnp.min(tot, axis=-1, keepdims=True)
                key = (sec - 1, c)
                mins[key] = m if key not in mins else jnp.minimum(mins[key], m)
        o_ref[:, j * cw:(j + 1) * cw] = val
    tot_ref[0] = jnp.concatenate(
        [jnp.broadcast_to(mins[(d, c)], (1, LANES)) for c in range(n_ck) for d in range(2)], axis=0)


def _hgrn_proj(x, nw, mod, lb, w):
    n = w.shape[1]
    n_ck = TM // CHUNK
    assert 2 * n_ck == SUBLANES
    x_specs, xs = _stream_specs(x, TM)
    proj, tot = pl.pallas_call(
        functools.partial(_hgrn_proj_kernel, cw=512, n_x=len(xs)),
        grid=(NT // TM,),
        in_specs=x_specs + [
            pl.BlockSpec((1, D), lambda i: (0, 0)),
            pl.BlockSpec((1, 6, D), lambda i: (_mod_group(i, TM), 0, 0)),
            pl.BlockSpec((1, HG_HEADS * HG_DK), lambda i: (0, 0)),
            pl.BlockSpec((D, n), lambda i: (0, 0)),
        ],
        out_specs=[pl.BlockSpec((TM, n), lambda i: (i, 0)),
                   pl.BlockSpec((1, SUBLANES, LANES), lambda i: (i, 0, 0))],
        out_shape=[jax.ShapeDtypeStruct((NT, n), F32),
                   jax.ShapeDtypeStruct((NT // TM, SUBLANES, LANES), F32)],
        compiler_params=_cparams(("arbitrary",)),
        name="hgrn_proj",
    )(*xs, nw.reshape(1, D), mod, lb.reshape(1, HG_HEADS * HG_DK), w)
    return proj, tot[:, :, 0].reshape(NT // CHUNK, 2)


def _split3(x):
    hi = x.astype(BF16)
    r = x - hi.astype(F32)
    mid = r.astype(BF16)
    lo = (r - mid.astype(F32)).astype(BF16)
    return hi, mid, lo


def _gla_safe(q, kk, v, g3, st_ref, row_refs, rev):
    bs_ref, ks_ref, kd8_ref = row_refs

    n_stack = 4
    t_io = lax.broadcasted_iota(I32, (n_stack * CHUNK, CHUNK), 0)
    u_io = lax.broadcasted_iota(I32, (n_stack * CHUNK, CHUNK), 1)
    which = t_io // CHUNK
    tt = t_io - which * CHUNK
    b16 = (tt // SUB) * SUB
    b8 = (tt // SUBLANES) * SUBLANES
    if rev:
        lim = jnp.where(which == 0, tt, jnp.where(which == 1, b16 + SUB, jnp.where(which == 2, b8 + SUBLANES, b8)))
        pick = u_io >= lim
    else:
        lim = jnp.where(which == 0, tt, jnp.where(which == 1, b16 - 1,
                                                  jnp.where(which == 2, b8 - 1, b8 + SUBLANES - 1)))
        pick = u_io <= lim
    tri = jnp.where(pick, 1.0, 0.0).astype(BF16)
    hi, mid, lo = g3
    cs = _dot(tri, hi) + _dot(tri, mid) + _dot(tri, lo)
    b, r16, r8, e8 = (cs[j * CHUNK:(j + 1) * CHUNK] for j in range(n_stack))
    last = 0 if rev else CHUNK - 1
    tot = b[last:last + 1]

    qe = (q * jnp.exp(b)).astype(BF16)
    qd = (q * jnp.exp(b - r16)).astype(BF16)
    qd8 = q * jnp.exp(b - r8)
    kdec = (kk * jnp.exp(tot - b)).astype(BF16)
    vb = v.astype(BF16)
    dec_tot = jnp.exp(tot)
    kd8 = kk * jnp.exp(e8 - b)
    for h in range(HG_HEADS):
        sl = slice(h * LANES, (h + 1) * LANES)
        bs_ref[h] = b[:, sl]
        ks_ref[h] = kk[:, sl]
        kd8_ref[h] = kd8[:, sl]

    def row(ref, h, s):
        return jnp.broadcast_to(ref[h, s:s + 1, :], (SUBLANES, LANES))

    lane = lax.broadcasted_iota(I32, (SUBLANES, LANES), 1)
    row8 = lax.broadcasted_iota(I32, (SUBLANES, LANES), 0)
    a_in = [[] for _ in range(HG_HEADS)]
    for tb in range(CHUNK // SUBLANES):
        t0 = tb * SUBLANES
        blk0 = (t0 // SUB) * SUB
        other = blk0 + SUBLANES if t0 == blk0 else blk0
        keep = (row8 + t0 <= lane) if rev else (row8 + t0 >= lane)
        for h in range(HG_HEADS):
            sl = slice(h * LANES, (h + 1) * LANES)
            q_t = q[t0:t0 + SUBLANES, sl]
            b_t = b[t0:t0 + SUBLANES, sl]
            acc = jnp.zeros((SUBLANES, LANES), F32)
            for s in range(t0, t0 + SUBLANES):
                p = q_t * jnp.exp(b_t - row(bs_ref, h, s)) * row(ks_ref, h, s)
                acc = jnp.where(lane == s, jnp.sum(p, axis=-1, keepdims=True), acc)
            acc = jnp.where(keep, acc, 0.0)
            if (t0 == blk0) == rev:
                qd8_t = qd8[t0:t0 + SUBLANES, sl]
                for s in range(other, other + SUBLANES):
                    p = qd8_t * row(kd8_ref, h, s)
                    acc = jnp.where(lane == s, jnp.sum(p, axis=-1, keepdims=True), acc)
            a_in[h].append(acc)

    n_blk = CHUNK // SUB
    o_heads = []
    for h in range(HG_HEADS):
        sl = slice(h * LANES, (h + 1) * LANES)
        st = st_ref[sl, :]
        o_h = _dot_nt(qe[:, sl], st.astype(BF16))
        rows = []
        for i in range(n_blk):
            lo_r, hi_r = ((i + 1) * SUB, CHUNK) if rev else (0, i * SUB)
            if hi_r == lo_r:
                rows.append(jnp.zeros((SUB, CHUNK), F32))
                continue
            r_i = r16[i * SUB:i * SUB + 1, sl]
            kd = kk[lo_r:hi_r, sl] * jnp.exp(r_i - b[lo_r:hi_r, sl])
            pads = [jnp.zeros((lo_r, LANES), F32)] if lo_r else []
            pads_hi = [jnp.zeros((CHUNK - hi_r, LANES), F32)] if hi_r < CHUNK else []
            kd = jnp.concatenate(pads + [kd] + pads_hi, axis=0).astype(BF16)
            rows.append(_dot_nt(qd[i * SUB:(i + 1) * SUB, sl], kd))
        a1 = jnp.concatenate(rows, axis=0) + jnp.concatenate(a_in[h], axis=0)[:, :CHUNK]
        o_h = o_h + _dot(a1.astype(BF16), vb[:, sl])
        o_heads.append(o_h)
        st_ref[sl, :] = st * dec_tot[:, sl] + _dot_tn(vb[:, sl], kdec[:, sl])
    return jnp.concatenate(o_heads, axis=1)


def _gla_fast(q, kk, v, b, st_ref, rev):
    qe = (q * jnp.exp(b)).astype(BF16)
    ke32 = kk * jnp.exp(-b)
    ke = ke32.astype(BF16)
    vb = v.astype(BF16)
    t_io = lax.broadcasted_iota(I32, (GLA_BLK, GLA_BLK), 0)
    s_io = lax.broadcasted_iota(I32, (GLA_BLK, GLA_BLK), 1)
    same = (t_io // CHUNK) == (s_io // CHUNK)
    keep = jnp.logical_and(same, (s_io >= t_io) if rev else (s_io <= t_io))
    order = range(GLA_CPS - 1, -1, -1) if rev else range(GLA_CPS)
    o_heads = []
    for h in range(HG_HEADS):
        sl = slice(h * LANES, (h + 1) * LANES)
        a = jnp.where(keep, _dot_nt(qe[:, sl], ke[:, sl]), 0.0).astype(BF16)
        o_h = _dot(a, vb[:, sl])
        st = st_ref[sl, :]
        inter = [None] * GLA_CPS
        for k in order:
            r = slice(k * CHUNK, (k + 1) * CHUNK)
            last = k * CHUNK if rev else (k + 1) * CHUNK - 1
            dec_tot = jnp.exp(b[last:last + 1, sl])
            inter[k] = _dot_nt(qe[r, sl], st.astype(BF16))
            kdec = (ke32[r, sl] * dec_tot).astype(BF16)
            st = st * dec_tot + _dot_tn(vb[r, sl], kdec)
        st_ref[sl, :] = st
        o_heads.append(o_h + jnp.concatenate(inter, axis=0))
    return jnp.concatenate(o_heads, axis=1)


def _gla_prep(g, rev):
    kk = 1.0 - jnp.exp(g)
    g3 = _split3(g)
    t_io = lax.broadcasted_iota(I32, (GLA_BLK, GLA_BLK), 0)
    u_io = lax.broadcasted_iota(I32, (GLA_BLK, GLA_BLK), 1)
    same = (t_io // CHUNK) == (u_io // CHUNK)
    tri = jnp.where(jnp.logical_and(same, (u_io >= t_io) if rev else (u_io <= t_io)), 1.0, 0.0).astype(BF16)
    b = _dot(tri, g3[0]) + _dot(tri, g3[1]) + _dot(tri, g3[2])
    return kk, g3, b


def _gla_kernel(mild_ref, qf_ref, gf_ref, vf_ref, qb_ref, gb_ref, vb_ref, s0f_ref, s0b_ref,
                of_ref, ob_ref, sf_ref, sb_ref, stf, stb, bs, ks, kd8):
    i = pl.program_id(0)
    n_p = NP // GLA_BLK
    is_prompt = i < n_p
    c = jnp.where(is_prompt, i % (SEQ // GLA_BLK), (i - n_p) % (DEC_SEQ // GLA_BLK))
    n_c = jnp.where(is_prompt, SEQ // GLA_BLK, DEC_SEQ // GLA_BLK)

    @pl.when(jnp.logical_and(c == 0, is_prompt))
    def _():
        stf[...] = jnp.zeros_like(stf)
        stb[...] = jnp.zeros_like(stb)

    @pl.when(jnp.logical_and(c == 0, jnp.logical_not(is_prompt)))
    def _():
        for h in range(HG_HEADS):
            sl = slice(h * LANES, (h + 1) * LANES)
            stf[sl, :] = s0f_ref[0, sl, :].T
            stb[sl, :] = s0b_ref[0, sl, :].T

    rows = (bs, ks, kd8)
    rf = [slice(k * CHUNK, (k + 1) * CHUNK) for k in range(GLA_CPS)]
    rb = rf[::-1]
    kk_f, g3_f, b_f = _gla_prep(gf_ref[...], False)
    kk_b, g3_b, b_b = _gla_prep(gb_ref[...], True)
    mild = mild_ref[i] != 0

    @pl.when(mild)
    def _():
        of_ref[...] = _gla_fast(qf_ref[...], kk_f, vf_ref[...], b_f, stf, False)
        ob_ref[...] = _gla_fast(qb_ref[...], kk_b, vb_ref[...], b_b, stb, True)

    @pl.when(jnp.logical_not(mild))
    def _():
        for k in range(GLA_CPS):
            r = rf[k]
            of_ref[r, :] = _gla_safe(qf_ref[r, :], kk_f[r], vf_ref[r, :], [p[r] for p in g3_f], stf, rows, False)
            r = rb[k]
            ob_ref[r, :] = _gla_safe(qb_ref[r, :], kk_b[r], vb_ref[r, :], [p[r] for p in g3_b], stb, rows, True)

    @pl.when(jnp.logical_and(c == n_c - 1, is_prompt))
    def _():
        for h in range(HG_HEADS):
            sl = slice(h * LANES, (h + 1) * LANES)
            sf_ref[0, sl, :] = stf[sl, :].T
            sb_ref[0, sl, :] = stb[sl, :].T


def _gla(proj, chunk_tot, s0f, s0b):
    n_p = NP // GLA_BLK
    cp = SEQ // GLA_BLK
    cs = DEC_SEQ // GLA_BLK

    def bwd_blk(i):
        jp = (i // cp) * cp + (cp - 1 - i % cp)
        j = i - n_p
        js = n_p + (j // cs) * cs + (cs - 1 - j % cs)
        return jnp.where(i < n_p, jp, js)

    def req(i):
        return jnp.maximum(i - n_p, 0) // cs

    def preq(i):
        return jnp.minimum(i // cp, BATCH - 1)

    steps = jnp.arange(NT // GLA_BLK, dtype=I32)
    blk_tot = chunk_tot.reshape(NT // GLA_BLK, GLA_CPS, 2).min(axis=1)
    mild = jnp.logical_and(blk_tot[:, 0] >= -GLA_FAST_MAX,
                           blk_tot[bwd_blk(steps), 1] >= -GLA_FAST_MAX).astype(I32)
    hk = HG_HEADS * HG_DK
    grid_spec = pltpu.PrefetchScalarGridSpec(
        num_scalar_prefetch=1,
        grid=(NT // GLA_BLK,),
        in_specs=[
            pl.BlockSpec((GLA_BLK, D), lambda i, m: (i, 0)),
            pl.BlockSpec((GLA_BLK, D), lambda i, m: (i, 1)),
            pl.BlockSpec((GLA_BLK, D), lambda i, m: (i, 3)),
            pl.BlockSpec((GLA_BLK, D), lambda i, m: (bwd_blk(i), 0)),
            pl.BlockSpec((GLA_BLK, D), lambda i, m: (bwd_blk(i), 2)),
            pl.BlockSpec((GLA_BLK, D), lambda i, m: (bwd_blk(i), 3)),
            pl.BlockSpec((1, hk, HG_DV), lambda i, m: (req(i), 0, 0)),
            pl.BlockSpec((1, hk, HG_DV), lambda i, m: (req(i), 0, 0)),
        ],
        out_specs=[
            pl.BlockSpec((GLA_BLK, D), lambda i, m: (i, 0)),
            pl.BlockSpec((GLA_BLK, D), lambda i, m: (bwd_blk(i), 0)),
            pl.BlockSpec((1, hk, HG_DV), lambda i, m: (preq(i), 0, 0)),
            pl.BlockSpec((1, hk, HG_DV), lambda i, m: (preq(i), 0, 0)),
        ],
        scratch_shapes=[pltpu.VMEM((hk, HG_DV), F32), pltpu.VMEM((hk, HG_DV), F32)]
        + [pltpu.VMEM((HG_HEADS, CHUNK, HG_DK), F32)] * 3,
    )
    return pl.pallas_call(
        _gla_kernel,
        grid_spec=grid_spec,
        out_shape=[
            jax.ShapeDtypeStruct((NT, D), F32),
            jax.ShapeDtypeStruct((NT, D), F32),
            jax.ShapeDtypeStruct((BATCH, hk, HG_DV), F32),
            jax.ShapeDtypeStruct((BATCH, hk, HG_DV), F32),
        ],
        compiler_params=_cparams(("arbitrary",)),
        name="gla_scan",
    )(mild, proj, proj, proj, proj, proj, proj, s0f, s0b)


def _head_pair_masks():
    upper = lax.broadcasted_iota(I32, (1, LANES), 1) >= NA_HD
    return upper


def _attn_ctx_kernel(q_ref, k_ref, v_ref, o_ref):
    upper = _head_pair_masks()
    for p in range(NA_HEADS // 2):
        sl = slice(p * LANES, (p + 1) * LANES)
        qp = q_ref[:, sl] * (NA_HD ** -0.5)
        kp = k_ref[:, sl].astype(BF16)
        vp = v_ref[:, sl].astype(BF16)
        outs = []
        for u in range(2):
            qm = jnp.where(upper if u else jnp.logical_not(upper), qp, 0.0).astype(BF16)
            s = _dot_nt(qm, kp)
            e = jnp.exp(s - jnp.max(s, axis=-1, keepdims=True))
            outs.append(_dot(e.astype(BF16), vp) / jnp.sum(e, axis=-1, keepdims=True))
        o_ref[:, sl] = jnp.where(upper, outs[1], outs[0]).astype(o_ref.dtype)


def _attn_ctx(qkv):
    return pl.pallas_call(
        _attn_ctx_kernel,
        grid=(BATCH,),
        in_specs=[
            pl.BlockSpec((SEQ, D), lambda b: (b, 0)),
            pl.BlockSpec((SEQ, D), lambda b: (b, 1)),
            pl.BlockSpec((SEQ, D), lambda b: (b, 2)),
        ],
        out_specs=pl.BlockSpec((SEQ, D), lambda b: (b, 0)),
        out_shape=jax.ShapeDtypeStruct((NP, D), BF16),
        compiler_params=_cparams(("arbitrary",)),
        name="attn_ctx",
    )(qkv, qkv, qkv)


def _attn_lat_kernel(q_ref, k_ref, v_ref, kc_ref, vc_ref, strip_ref, rowmask_ref, o_ref, *, tq):
    rows = DEC_SEQ // GRID_W
    kr = min(WIN_R, rows)
    rpt = tq // GRID_W

    def bias_rows(u, r, k_lo, k_hi):
        first = rows - 1 - r + k_lo
        var = first % 2
        strip = strip_ref[u, var, :, (first - var) * GRID_W:(first - var + k_hi - k_lo) * GRID_W]
        return strip + rowmask_ref[r:r + 1, k_lo * GRID_W:k_hi * GRID_W]

    upper = _head_pair_masks()
    mine = (jnp.logical_not(upper), upper)
    kl = k_ref[...].astype(BF16)
    kc = kc_ref[0].astype(BF16)
    vl = [jnp.where(mine[u], v_ref[...], 1.0).astype(BF16) for u in range(2)]
    vc = [jnp.where(mine[u], vc_ref[0], 1.0).astype(BF16) for u in range(2)]
    for t in range(DEC_SEQ // tq):
        rs = slice(t * tq, (t + 1) * tq)
        k_lo = min(max(t * rpt - kr // 2, 0), rows - kr) // 2 * 2
        k_hi = -(-(min(max(t * rpt + rpt - 1 - kr // 2, 0), rows - kr) + kr) // 2) * 2
        ks = slice(k_lo * GRID_W, k_hi * GRID_W)
        qp = q_ref[rs, :] * (NA_HD ** -0.5)
        outs = []
        for u in range(2):
            qm = jnp.where(mine[u], qp, 0.0).astype(BF16)
            bias = jnp.concatenate([bias_rows(u, t * rpt + a, k_lo, k_hi) for a in range(rpt)], axis=0)
            s_l = _dot_nt(qm, kl[ks]) + bias
            s_c = _dot_nt(qm, kc)
            m = jnp.maximum(jnp.max(s_l, axis=-1, keepdims=True), jnp.max(s_c, axis=-1, keepdims=True))
            acc = (_dot(jnp.exp(s_l - m).astype(BF16), vl[u][ks])
                   + _dot(jnp.exp(s_c - m).astype(BF16), vc[u]))
            outs.append(acc / pltpu.roll(acc, NA_HD, 1))
        o_ref[rs, :] = jnp.where(upper, outs[1], outs[0]).astype(o_ref.dtype)


def _attn_lat(qkv, k_ctx, v_ctx, strips, rowmask):
    npair = NA_HEADS // 2
    return pl.pallas_call(
        functools.partial(_attn_lat_kernel, tq=256),
        grid=(npair, DEC_BATCH),
        in_specs=[
            pl.BlockSpec((DEC_SEQ, LANES), lambda p, b: (b, p)),
            pl.BlockSpec((DEC_SEQ, LANES), lambda p, b: (b, npair + p)),
            pl.BlockSpec((DEC_SEQ, LANES), lambda p, b: (b, 2 * npair + p)),
            pl.BlockSpec((1, PAST_LEN, LANES), lambda p, b: (b, 0, p)),
            pl.BlockSpec((1, PAST_LEN, LANES), lambda p, b: (b, 0, p)),
            pl.BlockSpec((2, 2, GRID_W, STRIP_W), lambda p, b: (p, 0, 0, 0)),
            pl.BlockSpec((DEC_SEQ // GRID_W, DEC_SEQ), lambda p, b: (0, 0)),
        ],
        out_specs=pl.BlockSpec((DEC_SEQ, LANES), lambda p, b: (b, p)),
        out_shape=jax.ShapeDtypeStruct((NS, D), BF16),
        compiler_params=_cparams(("arbitrary", "arbitrary")),
        name="attn_lat",
    )(qkv, qkv, qkv, k_ctx, v_ctx, strips, rowmask)


def _latent_bias_tables(rpb):
    rows = DEC_SEQ // GRID_W
    kr = min(WIN_R, rows)
    ndr, ndc = 2 * WIN_R - 1, 2 * WIN_C - 1
    qc = np.arange(GRID_W)
    kc = np.arange(GRID_W)
    ws = np.clip(qc - WIN_C // 2, 0, GRID_W - WIN_C)
    col_ok = (kc[None, :] >= ws[:, None]) & (kc[None, :] < ws[:, None] + WIN_C)
    dc = np.clip(kc[None, :] - qc[:, None] + WIN_C - 1, 0, ndc - 1)
    onehot = (dc[None, :, :] == np.arange(ndc)[:, None, None]).astype(np.float32)
    t2 = jnp.einsum('hrc,cqk->hqrk', rpb.astype(F32), jnp.asarray(onehot),
                    precision=lax.Precision.HIGHEST)
    t2 = jnp.where(jnp.asarray(col_ok)[None, :, None, :], t2, NEG)
    lead = rows - WIN_R
    n_tiles = 2 * rows - 1

    def neg(n):
        return jnp.full((NA_HEADS, GRID_W, n, GRID_W), NEG, F32)

    strip = jnp.concatenate([neg(lead), t2, neg(n_tiles - lead - ndr)], axis=2)
    strip = strip.reshape(NA_HEADS, GRID_W, n_tiles * GRID_W)

    def pad(a):
        return jnp.pad(a, ((0, 0), (0, 0), (0, STRIP_W - a.shape[-1])), constant_values=NEG)

    strips = jnp.stack([pad(strip), pad(strip[:, :, GRID_W:])], axis=1)
    r = np.arange(rows)
    k0 = np.clip(r - kr // 2, 0, rows - kr)
    krow = np.arange(DEC_SEQ) // GRID_W
    row_ok = (krow[None, :] >= k0[:, None]) & (krow[None, :] < k0[:, None] + kr)
    rowmask = jnp.asarray(np.where(row_ok, 0.0, NEG).astype(np.float32))
    return strips, rowmask


def _hgrn_mix_out(refs, mod_ref, tile):
    of_ref, ob_ref, gate_ref, xp_ref, xs_ref, gn_ref, w_ref = refs
    x = jnp.where(tile < NP // MOE_TM, xp_ref[...], xs_ref[...])
    o = of_ref[...] + ob_ref[...]
    gn = gn_ref[...]
    segs = []
    for h in range(HG_HEADS):
        seg = o[:, h * LANES:(h + 1) * LANES]
        ms = jnp.mean(seg * seg, axis=-1, keepdims=True)
        segs.append(seg * lax.rsqrt(ms + EPS) * gn)
    y = (jnp.concatenate(segs, axis=1) * _silu(gate_ref[...])).astype(BF16)
    return x + mod_ref[0, 2:3, :] * _dot(y, w_ref[...])


def _attn_mix_out(refs, mod_ref, tile):
    ap_ref, as_ref, x_ref, w_ref = refs
    a = jnp.where(tile < NP // MOE_TM, ap_ref[...], as_ref[...])
    return x_ref[...] + mod_ref[0, 2:3, :] * _dot(a, w_ref[...])


def _route_kernel(*refs, mix_out, n_mix):
    mix_refs = refs[:n_mix]
    (nw_ref, mod_ref, wrh_ref, wrl_ref, rb_ref, xnew_ref, xs_ref, pos_ref, wt_ref, npc_ref,
     hb_s, pos_s, used_s) = refs[n_mix:]
    i = pl.program_id(0)
    cur = lax.rem(i, 2)
    tile = jnp.minimum(i, N_TILES - 1)

    @pl.when(i == 0)
    def _():
        hb_s[...] = jnp.zeros_like(hb_s)
        pos_s[...] = jnp.full(pos_s.shape, -1, I32)
        used_s[0] = 0.0
        used_s[1] = 0.0

    live = i < N_TILES

    def step(slot):
        x = mix_out(mix_refs, mod_ref, tile)
        xnew_ref[...] = x
        hb, pos, used = _route_select(x, nw_ref, mod_ref, wrh_ref, wrl_ref, rb_ref, pos_ref, wt_ref, npc_ref)
        hb_s[slot] = hb
        pos_s[slot] = jnp.where(live, pos, -1)
        used_s[slot] = jnp.where(live, used, 0.0)
        _route_sort(xs_ref, hb_s[1 - slot], pos_s[1 - slot], used_s[1 - slot])

    for slot in range(2):
        pl.when(cur == slot)(functools.partial(step, slot))


def _route_sort(xs_ref, hb, pos, used):
    pos16 = [pos[k:k + 1].astype(jnp.int16) for k in range(TOP_K)]

    def fill(r0, n):
        j16 = (lax.broadcasted_iota(I32, (n, MOE_TM), 0) + r0).astype(jnp.int16)
        onehot = jnp.zeros((n, MOE_TM), BF16)
        for k in range(TOP_K):
            onehot = jnp.where(j16 == pos16[k], jnp.ones((), BF16), onehot)
        xs_ref[r0:r0 + n, :] = _dot(onehot, hb).astype(BF16)

    fill(0, ROWS_MAIN)
    for r0 in range(ROWS_MAIN, R1, ROW_BLK):
        pl.when(used > r0)(functools.partial(fill, r0, ROW_BLK))

        @pl.when(used <= r0)
        def _(r0=r0):
            xs_ref[r0:r0 + ROW_BLK, :] = jnp.zeros((ROW_BLK, D), BF16)


def _route_select(x, nw_ref, mod_ref, wrh_ref, wrl_ref, rb_ref, pos_ref, wt_ref, npc_ref):
    h = _norm_mod(x, nw_ref[...], mod_ref[0, 3:4, :], mod_ref[0, 4:5, :])
    hb = h.astype(BF16)
    hl = (h - hb.astype(F32)).astype(BF16)
    wrh = wrh_ref[...]
    logits = _dot_nt(wrh, hb) + _dot_nt(wrh, hl) + _dot_nt(wrl_ref[...], hb)
    scores = jax.nn.sigmoid(logits)
    sel = scores + rb_ref[...]

    gsz = N_EXPERTS // N_GROUPS
    sub = lax.broadcasted_iota(I32, (gsz, MOE_TM), 0)
    ninf = -jnp.inf
    gs_rows = []
    for gi in range(N_GROUPS):
        blk = sel[gi * gsz:(gi + 1) * gsz]
        m1 = jnp.max(blk, axis=0, keepdims=True)
        first = jnp.min(jnp.where(blk == m1, sub, gsz), axis=0, keepdims=True)
        m2 = jnp.max(jnp.where(sub == first, ninf, blk), axis=0, keepdims=True)
        gs_rows.append(m1 + m2)
    cur = jnp.concatenate(gs_rows, axis=0)
    gidx = lax.broadcasted_iota(I32, (N_GROUPS, MOE_TM), 0)
    gsel = jnp.zeros((N_GROUPS, MOE_TM), F32)
    for _ in range(TOPK_GROUPS):
        m = jnp.max(cur, axis=0, keepdims=True)
        first = jnp.min(jnp.where(cur == m, gidx, N_GROUPS), axis=0, keepdims=True)
        hit = gidx == first
        gsel = jnp.where(hit, 1.0, gsel)
        cur = jnp.where(hit, ninf, cur)
    emask = jnp.concatenate(
        [jnp.broadcast_to(gsel[gi:gi + 1], (gsz, MOE_TM)) for gi in range(N_GROUPS)], axis=0)
    masked = jnp.where(emask > 0.5, sel, ninf)
    eidx = lax.broadcasted_iota(I32, (N_EXPERTS, MOE_TM), 0)
    chosen = jnp.zeros((N_EXPERTS, MOE_TM), F32)
    hits, wsel = [], []
    for _ in range(TOP_K):
        m = jnp.max(masked, axis=0, keepdims=True)
        first = jnp.min(jnp.where(masked == m, eidx, N_EXPERTS), axis=0, keepdims=True)
        hit = eidx == first
        hits.append(hit)
        wsel.append(jnp.sum(jnp.where(hit, scores, 0.0), axis=0, keepdims=True))
        chosen = jnp.where(hit, 1.0, chosen)
        masked = jnp.where(hit, ninf, masked)
    wsum = wsel[0]
    for w in wsel[1:]:
        wsum = wsum + w

    n_io = lax.broadcasted_iota(I32, (MOE_TM, MOE_TM), 0)
    m_io = lax.broadcasted_iota(I32, (MOE_TM, MOE_TM), 1)
    earlier = jnp.where(n_io < m_io, 1.0, 0.0).astype(BF16)
    rank = _dot(chosen.astype(BF16), earlier)
    cnt = jnp.sum(chosen, axis=1, keepdims=True)
    npc = jnp.floor((cnt + (PIECE - 1)) * (1.0 / PIECE))
    e_io = lax.broadcasted_iota(I32, (N_EXPERTS, N_EXPERTS), 0)
    f_io = lax.broadcasted_iota(I32, (N_EXPERTS, N_EXPERTS), 1)
    below = jnp.where(f_io < e_io, 1.0, 0.0).astype(BF16)
    npc_l = jnp.broadcast_to(npc, (N_EXPERTS, LANES))
    start = _dot(below, npc_l.astype(BF16))[:, 0:1] * PIECE
    slot = start + rank

    pos_rows, wt_rows = [], []
    for k in range(TOP_K):
        pos_k = jnp.sum(jnp.where(hits[k], slot, 0.0), axis=0, keepdims=True).astype(I32)
        pos_rows.append(pos_k)
        wt_rows.append(wsel[k] / wsum * ROUTED_SCALE)

    pad = SUBLANES - TOP_K
    pos = jnp.concatenate(pos_rows + [jnp.full((pad, MOE_TM), -1, I32)], axis=0)
    pos_ref[...] = pos
    wt_ref[...] = jnp.concatenate(wt_rows + [jnp.zeros((pad, MOE_TM), F32)], axis=0)
    npc_ref[0] = npc_l.astype(I32)
    return hb, pos, jnp.sum(npc) * PIECE


def _hgrn_mix_inputs(o_f, o_b, proj, x_pair, gn_w, w_out):
    n_p = NP // MOE_TM

    def specs(tile):
        return [pl.BlockSpec((MOE_TM, D), lambda i: (tile(i), 0)),
                pl.BlockSpec((MOE_TM, D), lambda i: (tile(i), 0)),
                pl.BlockSpec((MOE_TM, D), lambda i: (tile(i), 4)),
                pl.BlockSpec((MOE_TM, D), lambda i: (jnp.minimum(tile(i), n_p - 1), 0)),
                pl.BlockSpec((MOE_TM, D), lambda i: (jnp.maximum(tile(i) - n_p, 0), 0)),
                pl.BlockSpec((1, HG_DV), lambda i: (0, 0)),
                pl.BlockSpec((D, D), lambda i: (0, 0))]

    return _hgrn_mix_out, (o_f, o_b, proj, x_pair[0], x_pair[1], gn_w.reshape(1, HG_DV), w_out), specs


def _attn_mix_inputs(a_p, a_s, x, w_o):
    n_p = NP // MOE_TM

    def specs(tile):
        return [pl.BlockSpec((MOE_TM, D), lambda i: (jnp.minimum(tile(i), n_p - 1), 0)),
                pl.BlockSpec((MOE_TM, D), lambda i: (jnp.maximum(tile(i) - n_p, 0), 0)),
                pl.BlockSpec((MOE_TM, D), lambda i: (tile(i), 0)),
                pl.BlockSpec((D, D), lambda i: (0, 0))]

    return _attn_mix_out, (a_p, a_s, x, w_o), specs


def _route(mix, nw, mod, wr_hi, wr_lo, rbias):
    mix_out, mix_args, mix_specs = mix

    def tile(i):
        return jnp.minimum(i, N_TILES - 1)

    return pl.pallas_call(
        functools.partial(_route_kernel, mix_out=mix_out, n_mix=len(mix_args)),
        grid=(N_TILES + 2,),
        in_specs=mix_specs(tile) + [
            pl.BlockSpec((1, D), lambda i: (0, 0)),
            pl.BlockSpec((1, 6, D), lambda i: (_mod_group(tile(i), MOE_TM), 0, 0)),
            pl.BlockSpec((N_EXPERTS, D), lambda i: (0, 0)),
            pl.BlockSpec((N_EXPERTS, D), lambda i: (0, 0)),
            pl.BlockSpec((N_EXPERTS, 1), lambda i: (0, 0)),
        ],
        out_specs=[
            pl.BlockSpec((MOE_TM, D), lambda i: (tile(i), 0)),
            pl.BlockSpec((R1, D), lambda i: (jnp.maximum(i - 1, 0), 0)),
            pl.BlockSpec((SUBLANES, MOE_TM), lambda i: (0, tile(i))),
            pl.BlockSpec((SUBLANES, MOE_TM), lambda i: (0, tile(i))),
            pl.BlockSpec((1, N_EXPERTS, LANES), lambda i: (tile(i), 0, 0)),
        ],
        out_shape=[
            jax.ShapeDtypeStruct((NT, D), F32),
            jax.ShapeDtypeStruct((P_TOT * PIECE, D), BF16),
            jax.ShapeDtypeStruct((SUBLANES, NT), I32),
            jax.ShapeDtypeStruct((SUBLANES, NT), F32),
            jax.ShapeDtypeStruct((N_TILES, N_EXPERTS, LANES), I32),
        ],
        scratch_shapes=[pltpu.VMEM((2, MOE_TM, D), BF16), pltpu.VMEM((2, SUBLANES, MOE_TM), I32),
                        pltpu.SMEM((2,), F32)],
        compiler_params=_cparams(("arbitrary",)),
        name="moe_route",
    )(*mix_args, nw.reshape(1, D), mod, wr_hi, wr_lo, rbias.reshape(N_EXPERTS, 1))


def _piece_lists(npc):
    t, e = npc.shape
    hp = lax.Precision.HIGHEST
    npc_t = npc.T.astype(F32)
    start_t = (jnp.cumsum(npc, axis=1) - npc).T.astype(F32)
    tile_end = jnp.cumsum(npc_t, axis=1)
    n_e = tile_end[:, -1]
    pe_end = jnp.cumsum(n_e)
    pe_off = pe_end - n_e
    p = jnp.arange(P_MAX, dtype=F32)
    e_p = jnp.minimum(jnp.sum((pe_end[None, :] <= p[:, None]).astype(I32), axis=1), e - 1)
    oh_e = (e_p[:, None] == jnp.arange(e, dtype=I32)[None, :]).astype(F32)
    tab = jnp.concatenate([tile_end, start_t, npc_t, pe_off[:, None]], axis=1)
    row = jnp.dot(oh_e, tab, precision=hp)
    te_p, st_p, np_p, off_p = row[:, :t], row[:, t:2 * t], row[:, 2 * t:3 * t], row[:, 3 * t]
    local = p - off_p
    t_p = jnp.minimum(jnp.sum((te_p <= local[:, None]).astype(I32), axis=1), t - 1)
    oh_t = t_p[:, None] == jnp.arange(t, dtype=I32)[None, :]

    def pick(a):
        return jnp.sum(jnp.where(oh_t, a, 0.0), axis=1)

    src = t_p * R1B + (pick(st_p) + local - (pick(te_p) - pick(np_p))).astype(I32)
    src = jnp.concatenate([jnp.clip(src, 0, P_MAX - 1), jnp.zeros((G_PIECES,), I32)])

    nch = jnp.floor((n_e + (G_PIECES - 1)) * (1.0 / G_PIECES))
    ch_end = jnp.cumsum(nch)
    ch_off = jnp.concatenate([jnp.zeros((1,), F32), ch_end]).astype(I32)
    c = jnp.arange(-N_SLOTS, NCH + 2, dtype=F32)
    ce = jnp.minimum(jnp.sum((ch_end[None, :] <= c[:, None]).astype(I32), axis=1), e - 1)
    oh_c = (ce[:, None] == jnp.arange(e, dtype=I32)[None, :]).astype(F32)
    crow = jnp.dot(oh_c, jnp.stack([ch_end - nch, pe_off, n_e], axis=1), precision=hp)
    k_in = c - crow[:, 0]
    live = jnp.logical_and(c >= 0, c < ch_end[-1])
    cn = jnp.where(live, jnp.clip(crow[:, 2] - G_PIECES * k_in, 0, G_PIECES), 0.0)
    cs = jnp.where(cn > 0, crow[:, 1] + G_PIECES * k_in, 0.0)
    return src, ch_off, cs.astype(I32), cn.astype(I32)


def _ffn_kernel(src_ref, choff_ref, cs_ref, cn_ref, xs_in, wg_ref, wu_ref, wd_ref, xs_out,
                xbuf, ybuf, wgb, wub, wdb, gsem, ssem):
    e = pl.program_id(0)
    total = choff_ref[N_EXPERTS]

    def start_gather(ch):
        sl = lax.rem(ch + N_SLOTS, N_SLOTS)
        base = cs_ref[ch + N_SLOTS]
        n = cn_ref[ch + N_SLOTS]
        for i in range(G_PIECES):
            idx = src_ref[base + jnp.where(i < n, i, 0)]
            pltpu.make_async_copy(xs_in.at[idx], xbuf.at[sl, i], gsem.at[sl]).start(priority=1)

    def wait_gather(ch):
        sl = lax.rem(ch + N_SLOTS, N_SLOTS)
        pltpu.make_async_copy(xs_in.at[pl.ds(0, G_PIECES)], xbuf.at[sl], gsem.at[sl]).wait()

    def start_scatter(ch):
        sl = lax.rem(ch + N_SLOTS, N_SLOTS)
        base = cs_ref[ch + N_SLOTS]
        n = cn_ref[ch + N_SLOTS]
        for i in range(G_PIECES):
            idx = jnp.where(i < n, src_ref[base + i], P_MAX + sl * G_PIECES + i)
            pltpu.make_async_copy(ybuf.at[sl, i], xs_out.at[idx], ssem.at[sl]).start(priority=1)

    def wait_scatter(ch):
        sl = lax.rem(ch + N_SLOTS, N_SLOTS)
        pltpu.make_async_copy(ybuf.at[sl], xs_out.at[pl.ds(0, G_PIECES)], ssem.at[sl]).wait()

    @pl.when(e == 0)
    def _():
        ybuf[...] = jnp.zeros_like(ybuf)
        start_gather(0)
        start_gather(1)
        start_scatter(-3)
        start_scatter(-2)

    wgb[...] = wg_ref[0].astype(BF16)
    wub[...] = wu_ref[0].astype(BF16)
    wdb[...] = wd_ref[0].astype(BF16)

    def chunk(c, carry):
        sl = lax.rem(c, N_SLOTS)
        wait_gather(c)
        wait_scatter(c - 3)
        x = xbuf[sl].reshape(G_PIECES * PIECE, D)
        hid = (_silu(_dot(x, wgb[...])) * _dot(x, wub[...])).astype(BF16)
        start_gather(c + 2)
        start_scatter(c - 1)
        ybuf[sl] = _dot(hid, wdb[...]).astype(BF16).reshape(G_PIECES, PIECE, D)
        return carry

    lax.fori_loop(choff_ref[e], choff_ref[e + 1], chunk, 0)

    @pl.when(e == N_EXPERTS - 1)
    def _():
        start_scatter(total - 1)
        wait_gather(total)
        wait_gather(total + 1)
        wait_scatter(total - 3)
        wait_scatter(total - 2)
        wait_scatter(total - 1)


def _expert_ffn(xs, lists, layer, w_gate, w_up, w_down):
    src, ch_off, cs, cn = lists
    grid_spec = pltpu.PrefetchScalarGridSpec(
        num_scalar_prefetch=4,
        grid=(N_EXPERTS,),
        in_specs=[
            pl.BlockSpec(memory_space=pl.ANY),
            pl.BlockSpec((None, 1, D, D_EXPERT), lambda e, *_: (layer, e, 0, 0)),
            pl.BlockSpec((None, 1, D, D_EXPERT), lambda e, *_: (layer, e, 0, 0)),
            pl.BlockSpec((None, 1, D_EXPERT, D), lambda e, *_: (layer, e, 0, 0)),
        ],
        out_specs=pl.BlockSpec(memory_space=pl.ANY),
        scratch_shapes=[
            pltpu.VMEM((N_SLOTS, G_PIECES, PIECE, D), BF16),
            pltpu.VMEM((N_SLOTS, G_PIECES, PIECE, D), BF16),
            pltpu.VMEM((D, D_EXPERT), BF16),
            pltpu.VMEM((D, D_EXPERT), BF16),
            pltpu.VMEM((D_EXPERT, D), BF16),
            pltpu.SemaphoreType.DMA((N_SLOTS,)),
            pltpu.SemaphoreType.DMA((N_SLOTS,)),
        ],
    )
    out = pl.pallas_call(
        _ffn_kernel,
        grid_spec=grid_spec,
        out_shape=jax.ShapeDtypeStruct((P_TOT, PIECE, D), BF16),
        input_output_aliases={4: 0},
        compiler_params=_cparams(("arbitrary",)),
        name="moe_ffn",
    )(src, ch_off, cs, cn, xs.reshape(P_TOT, PIECE, D), w_gate, w_up, w_down)
    return out.reshape(P_TOT * PIECE, D)


def _combine_kernel(used_ref, ys_ref, pos_ref, wt_ref, x_ref, nw_ref, mod_ref, wsg_ref, wsu_ref, wsd_ref,
                    fw_ref, *rest, final):
    o_refs, acc_ref = rest[:-1], rest[-1]
    i = pl.program_id(0)
    x = x_ref[...]
    h = _norm_mod(x, nw_ref[...], mod_ref[0, 3:4, :], mod_ref[0, 4:5, :]).astype(BF16)
    shared = _dot((_silu(_dot(h, wsg_ref[...])) * _dot(h, wsu_ref[...])).astype(BF16), wsd_ref[...])

    pos16 = pos_ref[...].astype(jnp.int16)
    wt16 = wt_ref[...].astype(BF16)

    def block(r0, n):
        j16 = (lax.broadcasted_iota(I32, (MOE_TM, n), 1) + r0).astype(jnp.int16)
        wm = jnp.zeros((MOE_TM, n), BF16)
        for k in range(TOP_K):
            wm = jnp.where(j16 == pos16[:, k:k + 1], wt16[:, k:k + 1], wm)
        return _dot(wm, ys_ref[r0:r0 + n, :])

    acc_ref[...] = block(0, ROWS_MAIN)
    for r0 in range(ROWS_MAIN, R1, ROW_BLK):
        @pl.when(used_ref[i] > r0)
        def _(r0=r0):
            acc_ref[...] += block(r0, ROW_BLK)

    y = x + mod_ref[0, 5:6, :] * (acc_ref[...] + shared)
    if not final:
        o_refs[0][...] = y
        return
    ms = jnp.mean(y * y, axis=-1, keepdims=True)
    y = y * lax.rsqrt(ms + EPS) * fw_ref[...]
    is_prompt = i < NP // MOE_TM

    @pl.when(is_prompt)
    def _():
        o_refs[0][...] = y

    @pl.when(jnp.logical_not(is_prompt))
    def _():
        o_refs[1][...] = y


def _combine(ys, used, pos_t, wt_t, x, nw, mod, wsg, wsu, wsd, fw, final):
    n_p = NP // MOE_TM
    if final:
        out_specs = [pl.BlockSpec((MOE_TM, D), lambda i, u: (jnp.minimum(i, n_p - 1), 0)),
                     pl.BlockSpec((MOE_TM, D), lambda i, u: (jnp.maximum(i - n_p, 0), 0))]
        out_shape = [jax.ShapeDtypeStruct((NP, D), F32), jax.ShapeDtypeStruct((NS, D), F32)]
    else:
        out_specs = pl.BlockSpec((MOE_TM, D), lambda i, u: (i, 0))
        out_shape = jax.ShapeDtypeStruct((NT, D), F32)
    grid_spec = pltpu.PrefetchScalarGridSpec(
        num_scalar_prefetch=1,
        grid=(N_TILES,),
        in_specs=[
            pl.BlockSpec((R1, D), lambda i, u: (i, 0)),
            pl.BlockSpec((MOE_TM, SUBLANES), lambda i, u: (i, 0)),
            pl.BlockSpec((MOE_TM, SUBLANES), lambda i, u: (i, 0)),
            pl.BlockSpec((MOE_TM, D), lambda i, u: (i, 0)),
            pl.BlockSpec((1, D), lambda i, u: (0, 0)),
            pl.BlockSpec((1, 6, D), lambda i, u: (_mod_group(i, MOE_TM), 0, 0)),
            pl.BlockSpec((D, D_SHARED), lambda i, u: (0, 0)),
            pl.BlockSpec((D, D_SHARED), lambda i, u: (0, 0)),
            pl.BlockSpec((D_SHARED, D), lambda i, u: (0, 0)),
            pl.BlockSpec((1, D), lambda i, u: (0, 0)),
        ],
        out_specs=out_specs,
        scratch_shapes=[pltpu.VMEM((MOE_TM, D), F32)],
    )
    return pl.pallas_call(
        functools.partial(_combine_kernel, final=final),
        grid_spec=grid_spec,
        out_shape=out_shape,
        compiler_params=_cparams(("arbitrary",)),
        name="moe_combine",
    )(used, ys, pos_t, wt_t, x, nw.reshape(1, D), mod, wsg, wsu, wsd, fw.reshape(1, D))


def _moe(mix, nw, mod, w_router, rbias, layer, w_gate, w_up, w_down, ws_gate, ws_up, ws_down, fw, final):
    wr = w_router.astype(F32).T
    wr_hi = wr.astype(BF16)
    wr_lo = (wr - wr_hi.astype(F32)).astype(BF16)
    x, xs, pos, wt, npc = _route(mix, nw, mod, wr_hi, wr_lo, rbias.astype(F32))
    npc = npc[:, :, 0]
    lists = _piece_lists(npc)
    ys = _expert_ffn(xs, lists, layer, w_gate, w_up, w_down)
    used = (npc.sum(axis=1) * PIECE).astype(I32)
    return _combine(ys, used, pos.T, wt.T, x, nw, mod, ws_gate.astype(BF16), ws_up.astype(BF16),
                    ws_down.astype(BF16), fw, final)


def kernel(x_prompt, x_sample, state_hgrn_fwd, state_hgrn_bwd, cache_na_k, cache_na_v, c, c_ctx,
           norm1_w, norm2_w, ada_w, ada_b, hgrn_w_in, hgrn_lb_logits, hgrn_gn_w, hgrn_w_out,
           na_w_qkv, na_rpb, na_w_o, moe_w_router, moe_router_bias, moe_w_gate, moe_w_up, moe_w_down,
           shared_w_gate, shared_w_up, shared_w_down, final_norm_w):
    x = (x_prompt.reshape(NP, D), x_sample.reshape(NS, D))
    cvec = jnp.concatenate([c_ctx[None, :], c, jnp.zeros((N_MOD - 1 - DEC_BATCH, D), F32)], axis=0)
    mod = _modulation(cvec, ada_w, ada_b)
    lb_table = jnp.cumsum(jax.nn.softmax(hgrn_lb_logits.astype(F32), axis=0), axis=0)
    hk = HG_HEADS * HG_DK

    sf = sb = k_c = v_c = None
    for l in range(DEPTH):
        if l % 2 == 0:
            a = l // 2
            proj, chunk_tot = _hgrn_proj(x, norm1_w[l], mod[l], lb_table[l], hgrn_w_in[a].astype(BF16))
            o_f, o_b, sf, sb = _gla(proj, chunk_tot,
                                    state_hgrn_fwd[:, a].reshape(DEC_BATCH, hk, HG_DV),
                                    state_hgrn_bwd[:, a].reshape(DEC_BATCH, hk, HG_DV))
            mix = _hgrn_mix_inputs(o_f, o_b, proj, x, hgrn_gn_w[a], hgrn_w_out[a].astype(BF16))
        else:
            n = l // 2
            w_qkv = na_w_qkv[n].astype(BF16)
            qkv_p, k_p, v_p = _norm_proj(x, norm1_w[l], mod[l], w_qkv, 0, NP // TM, n_copy=2, out_dtype=BF16)
            qkv_s = _norm_proj(x, norm1_w[l], mod[l], w_qkv, NP // TM, NS // TM, out_dtype=BF16)
            att_p = _attn_ctx(qkv_p)
            att_s = _attn_lat(qkv_s, cache_na_k[:, n].reshape(DEC_BATCH, PAST_LEN, D),
                              cache_na_v[:, n].reshape(DEC_BATCH, PAST_LEN, D),
                              *_latent_bias_tables(na_rpb[n]))
            mix = _attn_mix_inputs(att_p, att_s, x, na_w_o[n].astype(BF16))
            k_c = k_p.reshape(BATCH, SEQ, NA_HEADS, NA_HD)
            v_c = v_p.reshape(BATCH, SEQ, NA_HEADS, NA_HD)
        x = _moe(mix, norm2_w[l], mod[l], moe_w_router[l], moe_router_bias[l], l, moe_w_gate, moe_w_up,
                 moe_w_down, shared_w_gate[l], shared_w_up[l], shared_w_down[l], final_norm_w,
                 final=(l == DEPTH - 1))

    y_prompt = x[0].reshape(BATCH, SEQ, D)
    y_sample = x[1].reshape(DEC_BATCH, DEC_SEQ, D)
    new_sf = sf.reshape(BATCH, 1, HG_HEADS, HG_DK, HG_DV)
    new_sb = sb.reshape(BATCH, 1, HG_HEADS, HG_DK, HG_DV)
    return (y_prompt, y_sample, new_sf, new_sb, k_c[:, None], v_c[:, None])
```

```python
import functools

import numpy as np
import jax
import jax.numpy as jnp
from jax import lax
from jax.experimental import pallas as pl
from jax.experimental.pallas import tpu as pltpu

F32 = jnp.float32
BF16 = jnp.bfloat16
I32 = jnp.int32

D = 1024
BATCH = 32
SEQ = 256
DEPTH = 2
DEC_BATCH = 8
DEC_SEQ = 1024
PAST_LEN = 512
GRID_W = 64
HG_HEADS = 8
HG_DK = 128
HG_DV = 128
CHUNK = 64
NA_HEADS = 16
NA_HD = 64
WIN_R = 8
WIN_C = 16
N_EXPERTS = 64
TOP_K = 6
N_GROUPS = 8
TOPK_GROUPS = 4
D_EXPERT = 256
D_SHARED = 256
ROUTED_SCALE = 2.5
EPS = 1e-6

NP = BATCH * SEQ
NS = DEC_BATCH * DEC_SEQ
NT = NP + NS
N_MOD = 16
STRIP_W = 2048
assert (2 * (DEC_SEQ // GRID_W) - 1) * GRID_W <= STRIP_W
NEG = -1e30

LANES = 128
SUBLANES = 8
BF16_ROWS = 16
VMEM_LIMIT = 56 * 1024 * 1024

TM = 256
SUB = 16
GLA_FAST_MAX = 60.0
GLA_CPS = 4
GLA_BLK = GLA_CPS * CHUNK
assert SEQ % GLA_BLK == 0 and DEC_SEQ % GLA_BLK == 0
MOE_TM = 256
PIECE = BF16_ROWS
R1 = MOE_TM * TOP_K + N_EXPERTS * (PIECE - 1) + 64
assert R1 % PIECE == 0 and R1 % LANES == 0
R1B = R1 // PIECE
ROWS_MAIN = 2048
ROW_BLK = 256
FETCH_BLK = 512
assert (R1 - ROWS_MAIN) % ROW_BLK == 0 and R1 - ROWS_MAIN == FETCH_BLK and ROWS_MAIN % FETCH_BLK == 0
N_TILES = NT // MOE_TM
P_MAX = N_TILES * R1B
G_PIECES = 32
NCH = P_MAX // G_PIECES + N_EXPERTS
N_SLOTS = 3
P_TOT = P_MAX + R1B
assert N_SLOTS * G_PIECES <= R1B


def _cparams(sem):
    return pltpu.CompilerParams(dimension_semantics=sem, vmem_limit_bytes=VMEM_LIMIT)


def _dot(a, b):
    return jnp.dot(a, b, preferred_element_type=F32)


def _dot_nt(a, b):
    return lax.dot_general(a, b, (((1,), (1,)), ((), ())), preferred_element_type=F32)


def _dot_tn(a, b):
    return lax.dot_general(a, b, (((0,), (0,)), ((), ())), preferred_element_type=F32)


def _silu(x):
    return x * jax.nn.sigmoid(x)


def _mod_group(i, tm):
    r = i * tm
    return jnp.where(r < NP, 0, 1 + (r - NP) // DEC_SEQ)


def _norm_mod(x, nw, shift, scale):
    ms = jnp.mean(x * x, axis=-1, keepdims=True)
    y = x * lax.rsqrt(ms + EPS) * nw
    return y * (1.0 + scale) + shift


def _mod_kernel(c_ref, w_ref, b_ref, o_ref):
    s = _silu(c_ref[...]).astype(BF16)
    o_ref[0] = _dot(s, w_ref[0].astype(BF16)) + b_ref[0]


def _modulation(cvec, ada_w, ada_b):
    cw = 1536
    n = ada_w.shape[-1]
    out = pl.pallas_call(
        _mod_kernel,
        grid=(DEPTH, n // cw),
        in_specs=[
            pl.BlockSpec((N_MOD, D), lambda l, j: (0, 0)),
            pl.BlockSpec((1, D, cw), lambda l, j: (l, 0, j)),
            pl.BlockSpec((1, 1, cw), lambda l, j: (l, 0, j)),
        ],
        out_specs=pl.BlockSpec((1, N_MOD, cw), lambda l, j: (l, 0, j)),
        out_shape=jax.ShapeDtypeStruct((DEPTH, N_MOD, n), F32),
        compiler_params=_cparams(("arbitrary", "arbitrary")),
        name="modulation",
    )(cvec, ada_w, ada_b.reshape(DEPTH, 1, n))
    return out.reshape(DEPTH, N_MOD, 6, D)


def _stream_specs(x, tm):
    if not isinstance(x, tuple):
        return [pl.BlockSpec((tm, D), lambda i: (i, 0))], (x,)
    n_p = NP // tm
    return [pl.BlockSpec((tm, D), lambda i: (jnp.minimum(i, n_p - 1), 0)),
            pl.BlockSpec((tm, D), lambda i: (jnp.maximum(i - n_p, 0), 0))], x


def _stream_tile(refs, tm):
    if len(refs) == 1:
        return refs[0][...]
    return jnp.where(pl.program_id(0) < NP // tm, refs[0][...], refs[1][...])


def _proj_kernel(x_ref, nw_ref, mod_ref, w_ref, o_ref, *copy_refs, cw):
    h = _norm_mod(x_ref[...], nw_ref[...], mod_ref[0, 0:1, :], mod_ref[0, 1:2, :]).astype(BF16)
    for j in range(w_ref.shape[1] // cw):
        val = _dot(h, w_ref[:, j * cw:(j + 1) * cw])
        o_ref[:, j * cw:(j + 1) * cw] = val.astype(o_ref.dtype)
        col = j * cw - D
        if 0 <= col < len(copy_refs) * D:
            copy_refs[col // D][:, col % D:col % D + cw] = val


def _norm_proj(x, nw, mod, w, tile0, n_tiles, n_copy=0, out_dtype=F32):
    n = w.shape[1]
    rows = n_tiles * TM
    outs = pl.pallas_call(
        functools.partial(_proj_kernel, cw=512),
        grid=(n_tiles,),
        in_specs=[
            pl.BlockSpec((TM, D), lambda i: (tile0 + i, 0)),
            pl.BlockSpec((1, D), lambda i: (0, 0)),
            pl.BlockSpec((1, 6, D), lambda i: (_mod_group(tile0 + i, TM), 0, 0)),
            pl.BlockSpec((D, n), lambda i: (0, 0)),
        ],
        out_specs=[pl.BlockSpec((TM, n), lambda i: (i, 0))] + [pl.BlockSpec((TM, D), lambda i: (i, 0))] * n_copy,
        out_shape=[jax.ShapeDtypeStruct((rows, n), out_dtype)] + [jax.ShapeDtypeStruct((rows, D), F32)] * n_copy,
        compiler_params=_cparams(("arbitrary",)),
        name="norm_proj",
    )(x, nw.reshape(1, D), mod, w)
    return outs if n_copy else outs[0]


def _hgrn_proj_kernel(*refs, cw, n_x):
    nw_ref, mod_ref, lb_ref, w_ref, o_ref, tot_ref = refs[n_x:]
    x = _stream_tile(refs[:n_x], TM)
    h = _norm_mod(x, nw_ref[...], mod_ref[0, 0:1, :], mod_ref[0, 1:2, :]).astype(BF16)
    n_ck = TM // CHUNK
    mins = {}
    for j in range(w_ref.shape[1] // cw):
        val = _dot(h, w_ref[:, j * cw:(j + 1) * cw])
        sec, col = divmod(j * cw, D)
        if sec in (1, 2):
            lb = lb_ref[:, col:col + cw]
            val = jnp.log(lb + (1.0 - lb) * jax.nn.sigmoid(val))
            for c in range(n_ck):
                tot = jnp.sum(val[c * CHUNK:(c + 1) * CHUNK], axis=0, keepdims=True)
                m = jnp.min(tot, axis=-1, keepdims=True)
                key = (sec - 1, c)
                mins[key] = m if key not in mins else jnp.minimum(mins[key], m)
        o_ref[:, j * cw:(j + 1) * cw] = val
    tot_ref[0] = jnp.concatenate(
        [jnp.broadcast_to(mins[(d, c)], (1, LANES)) for c in range(n_ck) for d in range(2)], axis=0)


def _hgrn_proj(x, nw, mod, lb, w):
    n = w.shape[1]
    n_ck = TM // CHUNK
    assert 2 * n_ck == SUBLANES
    x_specs, xs = _stream_specs(x, TM)
    proj, tot = pl.pallas_call(
        functools.partial(_hgrn_proj_kernel, cw=512, n_x=len(xs)),
        grid=(NT // TM,),
        in_specs=x_specs + [
            pl.BlockSpec((1, D), lambda i: (0, 0)),
            pl.BlockSpec((1, 6, D), lambda i: (_mod_group(i, TM), 0, 0)),
            pl.BlockSpec((1, HG_HEADS * HG_DK), lambda i: (0, 0)),
            pl.BlockSpec((D, n), lambda i: (0, 0)),
        ],
        out_specs=[pl.BlockSpec((TM, n), lambda i: (i, 0)),
                   pl.BlockSpec((1, SUBLANES, LANES), lambda i: (i, 0, 0))],
        out_shape=[jax.ShapeDtypeStruct((NT, n), F32),
                   jax.ShapeDtypeStruct((NT // TM, SUBLANES, LANES), F32)],
        compiler_params=_cparams(("arbitrary",)),
        name="hgrn_proj",
    )(*xs, nw.reshape(1, D), mod, lb.reshape(1, HG_HEADS * HG_DK), w)
    return proj, tot[:, :, 0].reshape(NT // CHUNK, 2)


def _split3(x):
    hi = x.astype(BF16)
    r = x - hi.astype(F32)
    mid = r.astype(BF16)
    lo = (r - mid.astype(F32)).astype(BF16)
    return hi, mid, lo


def _gla_safe(q, kk, v, g3, st_ref, row_refs, rev):
    bs_ref, ks_ref, kd8_ref = row_refs

    n_stack = 4
    t_io = lax.broadcasted_iota(I32, (n_stack * CHUNK, CHUNK), 0)
    u_io = lax.broadcasted_iota(I32, (n_stack * CHUNK, CHUNK), 1)
    which = t_io // CHUNK
    tt = t_io - which * CHUNK
    b16 = (tt // SUB) * SUB
    b8 = (tt // SUBLANES) * SUBLANES
    if rev:
        lim = jnp.where(which == 0, tt, jnp.where(which == 1, b16 + SUB, jnp.where(which == 2, b8 + SUBLANES, b8)))
        pick = u_io >= lim
    else:
        lim = jnp.where(which == 0, tt, jnp.where(which == 1, b16 - 1,
                                                  jnp.where(which == 2, b8 - 1, b8 + SUBLANES - 1)))
        pick = u_io <= lim
    tri = jnp.where(pick, 1.0, 0.0).astype(BF16)
    hi, mid, lo = g3
    cs = _dot(tri, hi) + _dot(tri, mid) + _dot(tri, lo)
    b, r16, r8, e8 = (cs[j * CHUNK:(j + 1) * CHUNK] for j in range(n_stack))
    last = 0 if rev else CHUNK - 1
    tot = b[last:last + 1]

    qe = (q * jnp.exp(b)).astype(BF16)
    qd = (q * jnp.exp(b - r16)).astype(BF16)
    qd8 = q * jnp.exp(b - r8)
    kdec = (kk * jnp.exp(tot - b)).astype(BF16)
    vb = v.astype(BF16)
    dec_tot = jnp.exp(tot)
    kd8 = kk * jnp.exp(e8 - b)
    for h in range(HG_HEADS):
        sl = slice(h * LANES, (h + 1) * LANES)
        bs_ref[h] = b[:, sl]
        ks_ref[h] = kk[:, sl]
        kd8_ref[h] = kd8[:, sl]

    def row(ref, h, s):
        return jnp.broadcast_to(ref[h, s:s + 1, :], (SUBLANES, LANES))

    lane = lax.broadcasted_iota(I32, (SUBLANES, LANES), 1)
    row8 = lax.broadcasted_iota(I32, (SUBLANES, LANES), 0)
    a_in = [[] for _ in range(HG_HEADS)]
    for tb in range(CHUNK // SUBLANES):
        t0 = tb * SUBLANES
        blk0 = (t0 // SUB) * SUB
        other = blk0 + SUBLANES if t0 == blk0 else blk0
        keep = (row8 + t0 <= lane) if rev else (row8 + t0 >= lane)
        for h in range(HG_HEADS):
            sl = slice(h * LANES, (h + 1) * LANES)
            q_t = q[t0:t0 + SUBLANES, sl]
            b_t = b[t0:t0 + SUBLANES, sl]
            acc = jnp.zeros((SUBLANES, LANES), F32)
            for s in range(t0, t0 + SUBLANES):
                p = q_t * jnp.exp(b_t - row(bs_ref, h, s)) * row(ks_ref, h, s)
                acc = jnp.where(lane == s, jnp.sum(p, axis=-1, keepdims=True), acc)
            acc = jnp.where(keep, acc, 0.0)
            if (t0 == blk0) == rev:
                qd8_t = qd8[t0:t0 + SUBLANES, sl]
                for s in range(other, other + SUBLANES):
                    p = qd8_t * row(kd8_ref, h, s)
                    acc = jnp.where(lane == s, jnp.sum(p, axis=-1, keepdims=True), acc)
            a_in[h].append(acc)

    n_blk = CHUNK // SUB
    o_heads = []
    for h in range(HG_HEADS):
        sl = slice(h * LANES, (h + 1) * LANES)
        st = st_ref[sl, :]
        o_h = _dot_nt(qe[:, sl], st.astype(BF16))
        rows = []
        for i in range(n_blk):
            lo_r, hi_r = ((i + 1) * SUB, CHUNK) if rev else (0, i * SUB)
            if hi_r == lo_r:
                rows.append(jnp.zeros((SUB, CHUNK), F32))
                continue
            r_i = r16[i * SUB:i * SUB + 1, sl]
            kd = kk[lo_r:hi_r, sl] * jnp.exp(r_i - b[lo_r:hi_r, sl])
            pads = [jnp.zeros((lo_r, LANES), F32)] if lo_r else []
            pads_hi = [jnp.zeros((CHUNK - hi_r, LANES), F32)] if hi_r < CHUNK else []
            kd = jnp.concatenate(pads + [kd] + pads_hi, axis=0).astype(BF16)
            rows.append(_dot_nt(qd[i * SUB:(i + 1) * SUB, sl], kd))
        a1 = jnp.concatenate(rows, axis=0) + jnp.concatenate(a_in[h], axis=0)[:, :CHUNK]
        o_h = o_h + _dot(a1.astype(BF16), vb[:, sl])
        o_heads.append(o_h)
        st_ref[sl, :] = st * dec_tot[:, sl] + _dot_tn(vb[:, sl], kdec[:, sl])
    return jnp.concatenate(o_heads, axis=1)


def _gla_fast(q, kk, v, b, st_ref, rev):
    qe = (q * jnp.exp(b)).astype(BF16)
    ke32 = kk * jnp.exp(-b)
    ke = ke32.astype(BF16)
    vb = v.astype(BF16)
    t_io = lax.broadcasted_iota(I32, (GLA_BLK, GLA_BLK), 0)
    s_io = lax.broadcasted_iota(I32, (GLA_BLK, GLA_BLK), 1)
    same = (t_io // CHUNK) == (s_io // CHUNK)
    keep = jnp.logical_and(same, (s_io >= t_io) if rev else (s_io <= t_io))
    order = range(GLA_CPS - 1, -1, -1) if rev else range(GLA_CPS)
    o_heads = []
    for h in range(HG_HEADS):
        sl = slice(h * LANES, (h + 1) * LANES)
        a = jnp.where(keep, _dot_nt(qe[:, sl], ke[:, sl]), 0.0).astype(BF16)
        o_h = _dot(a, vb[:, sl])
        st = st_ref[sl, :]
        inter = [None] * GLA_CPS
        for k in order:
            r = slice(k * CHUNK, (k + 1) * CHUNK)
            last = k * CHUNK if rev else (k + 1) * CHUNK - 1
            dec_tot = jnp.exp(b[last:last + 1, sl])
            inter[k] = _dot_nt(qe[r, sl], st.astype(BF16))
            kdec = (ke32[r, sl] * dec_tot).astype(BF16)
            st = st * dec_tot + _dot_tn(vb[r, sl], kdec)
        st_ref[sl, :] = st
        o_heads.append(o_h + jnp.concatenate(inter, axis=0))
    return jnp.concatenate(o_heads, axis=1)


def _gla_prep(g, rev):
    kk = 1.0 - jnp.exp(g)
    g3 = _split3(g)
    t_io = lax.broadcasted_iota(I32, (GLA_BLK, GLA_BLK), 0)
    u_io = lax.broadcasted_iota(I32, (GLA_BLK, GLA_BLK), 1)
    same = (t_io // CHUNK) == (u_io // CHUNK)
    tri = jnp.where(jnp.logical_and(same, (u_io >= t_io) if rev else (u_io <= t_io)), 1.0, 0.0).astype(BF16)
    b = _dot(tri, g3[0]) + _dot(tri, g3[1]) + _dot(tri, g3[2])
    return kk, g3, b


def _gla_kernel(mild_ref, qf_ref, gf_ref, vf_ref, qb_ref, gb_ref, vb_ref, s0f_ref, s0b_ref,
                of_ref, ob_ref, sf_ref, sb_ref, stf, stb, bs, ks, kd8):
    i = pl.program_id(0)
    n_p = NP // GLA_BLK
    is_prompt = i < n_p
    c = jnp.where(is_prompt, i % (SEQ // GLA_BLK), (i - n_p) % (DEC_SEQ // GLA_BLK))
    n_c = jnp.where(is_prompt, SEQ // GLA_BLK, DEC_SEQ // GLA_BLK)

    @pl.when(jnp.logical_and(c == 0, is_prompt))
    def _():
        stf[...] = jnp.zeros_like(stf)
        stb[...] = jnp.zeros_like(stb)

    @pl.when(jnp.logical_and(c == 0, jnp.logical_not(is_prompt)))
    def _():
        for h in range(HG_HEADS):
            sl = slice(h * LANES, (h + 1) * LANES)
            stf[sl, :] = s0f_ref[0, sl, :].T
            stb[sl, :] = s0b_ref[0, sl, :].T

    rows = (bs, ks, kd8)
    rf = [slice(k * CHUNK, (k + 1) * CHUNK) for k in range(GLA_CPS)]
    rb = rf[::-1]
    kk_f, g3_f, b_f = _gla_prep(gf_ref[...], False)
    kk_b, g3_b, b_b = _gla_prep(gb_ref[...], True)
    mild = mild_ref[i] != 0

    @pl.when(mild)
    def _():
        of_ref[...] = _gla_fast(qf_ref[...], kk_f, vf_ref[...], b_f, stf, False)
        ob_ref[...] = _gla_fast(qb_ref[...], kk_b, vb_ref[...], b_b, stb, True)

    @pl.when(jnp.logical_not(mild))
    def _():
        for k in range(GLA_CPS):
            r = rf[k]
            of_ref[r, :] = _gla_safe(qf_ref[r, :], kk_f[r], vf_ref[r, :], [p[r] for p in g3_f], stf, rows, False)
            r = rb[k]
            ob_ref[r, :] = _gla_safe(qb_ref[r, :], kk_b[r], vb_ref[r, :], [p[r] for p in g3_b], stb, rows, True)

    @pl.when(jnp.logical_and(c == n_c - 1, is_prompt))
    def _():
        for h in range(HG_HEADS):
            sl = slice(h * LANES, (h + 1) * LANES)
            sf_ref[0, sl, :] = stf[sl, :].T
            sb_ref[0, sl, :] = stb[sl, :].T


def _gla(proj, chunk_tot, s0f, s0b):
    n_p = NP // GLA_BLK
    cp = SEQ // GLA_BLK
    cs = DEC_SEQ // GLA_BLK

    def bwd_blk(i):
        jp = (i // cp) * cp + (cp - 1 - i % cp)
        j = i - n_p
        js = n_p + (j // cs) * cs + (cs - 1 - j % cs)
        return jnp.where(i < n_p, jp, js)

    def req(i):
        return jnp.maximum(i - n_p, 0) // cs

    def preq(i):
        return jnp.minimum(i // cp, BATCH - 1)

    steps = jnp.arange(NT // GLA_BLK, dtype=I32)
    blk_tot = chunk_tot.reshape(NT // GLA_BLK, GLA_CPS, 2).min(axis=1)
    mild = jnp.logical_and(blk_tot[:, 0] >= -GLA_FAST_MAX,
                           blk_tot[bwd_blk(steps), 1] >= -GLA_FAST_MAX).astype(I32)
    hk = HG_HEADS * HG_DK
    grid_spec = pltpu.PrefetchScalarGridSpec(
        num_scalar_prefetch=1,
        grid=(NT // GLA_BLK,),
        in_specs=[
            pl.BlockSpec((GLA_BLK, D), lambda i, m: (i, 0)),
            pl.BlockSpec((GLA_BLK, D), lambda i, m: (i, 1)),
            pl.BlockSpec((GLA_BLK, D), lambda i, m: (i, 3)),
            pl.BlockSpec((GLA_BLK, D), lambda i, m: (bwd_blk(i), 0)),
            pl.BlockSpec((GLA_BLK, D), lambda i, m: (bwd_blk(i), 2)),
            pl.BlockSpec((GLA_BLK, D), lambda i, m: (bwd_blk(i), 3)),
            pl.BlockSpec((1, hk, HG_DV), lambda i, m: (req(i), 0, 0)),
            pl.BlockSpec((1, hk, HG_DV), lambda i, m: (req(i), 0, 0)),
        ],
        out_specs=[
            pl.BlockSpec((GLA_BLK, D), lambda i, m: (i, 0)),
            pl.BlockSpec((GLA_BLK, D), lambda i, m: (bwd_blk(i), 0)),
            pl.BlockSpec((1, hk, HG_DV), lambda i, m: (preq(i), 0, 0)),
            pl.BlockSpec((1, hk, HG_DV), lambda i, m: (preq(i), 0, 0)),
        ],
        scratch_shapes=[pltpu.VMEM((hk, HG_DV), F32), pltpu.VMEM((hk, HG_DV), F32)]
        + [pltpu.VMEM((HG_HEADS, CHUNK, HG_DK), F32)] * 3,
    )
    return pl.pallas_call(
        _gla_kernel,
        grid_spec=grid_spec,
        out_shape=[
            jax.ShapeDtypeStruct((NT, D), F32),
            jax.ShapeDtypeStruct((NT, D), F32),
            jax.ShapeDtypeStruct((BATCH, hk, HG_DV), F32),
            jax.ShapeDtypeStruct((BATCH, hk, HG_DV), F32),
        ],
        compiler_params=_cparams(("arbitrary",)),
        name="gla_scan",
    )(mild, proj, proj, proj, proj, proj, proj, s0f, s0b)


def _head_pair_masks():
    upper = lax.broadcasted_iota(I32, (1, LANES), 1) >= NA_HD
    return upper


def _attn_ctx_kernel(q_ref, k_ref, v_ref, o_ref):
    upper = _head_pair_masks()
    for p in range(NA_HEADS // 2):
        sl = slice(p * LANES, (p + 1) * LANES)
        qp = q_ref[:, sl] * (NA_HD ** -0.5)
        kp = k_ref[:, sl].astype(BF16)
        vp = v_ref[:, sl].astype(BF16)
        outs = []
        for u in range(2):
            qm = jnp.where(upper if u else jnp.logical_not(upper), qp, 0.0).astype(BF16)
            s = _dot_nt(qm, kp)
            e = jnp.exp(s - jnp.max(s, axis=-1, keepdims=True))
            outs.append(_dot(e.astype(BF16), vp) / jnp.sum(e, axis=-1, keepdims=True))
        o_ref[:, sl] = jnp.where(upper, outs[1], outs[0]).astype(o_ref.dtype)


def _attn_ctx(qkv):
    return pl.pallas_call(
        _attn_ctx_kernel,
        grid=(BATCH,),
        in_specs=[
            pl.BlockSpec((SEQ, D), lambda b: (b, 0)),
            pl.BlockSpec((SEQ, D), lambda b: (b, 1)),
            pl.BlockSpec((SEQ, D), lambda b: (b, 2)),
        ],
        out_specs=pl.BlockSpec((SEQ, D), lambda b: (b, 0)),
        out_shape=jax.ShapeDtypeStruct((NP, D), BF16),
        compiler_params=_cparams(("arbitrary",)),
        name="attn_ctx",
    )(qkv, qkv, qkv)


def _attn_lat_kernel(q_ref, k_ref, v_ref, kc_ref, vc_ref, strip_ref, rowmask_ref, o_ref, *, tq):
    rows = DEC_SEQ // GRID_W
    kr = min(WIN_R, rows)
    rpt = tq // GRID_W

    def bias_rows(u, r, k_lo, k_hi):
        first = rows - 1 - r + k_lo
        var = first % 2
        strip = strip_ref[u, var, :, (first - var) * GRID_W:(first - var + k_hi - k_lo) * GRID_W]
        return strip + rowmask_ref[r:r + 1, k_lo * GRID_W:k_hi * GRID_W]

    upper = _head_pair_masks()
    mine = (jnp.logical_not(upper), upper)
    kl = k_ref[...].astype(BF16)
    kc = kc_ref[0].astype(BF16)
    vl = [jnp.where(mine[u], v_ref[...], 1.0).astype(BF16) for u in range(2)]
    vc = [jnp.where(mine[u], vc_ref[0], 1.0).astype(BF16) for u in range(2)]
    for t in range(DEC_SEQ // tq):
        rs = slice(t * tq, (t + 1) * tq)
        k_lo = min(max(t * rpt - kr // 2, 0), rows - kr) // 2 * 2
        k_hi = -(-(min(max(t * rpt + rpt - 1 - kr // 2, 0), rows - kr) + kr) // 2) * 2
        ks = slice(k_lo * GRID_W, k_hi * GRID_W)
        qp = q_ref[rs, :] * (NA_HD ** -0.5)
        outs = []
        for u in range(2):
            qm = jnp.where(mine[u], qp, 0.0).astype(BF16)
            bias = jnp.concatenate([bias_rows(u, t * rpt + a, k_lo, k_hi) for a in range(rpt)], axis=0)
            s_l = _dot_nt(qm, kl[ks]) + bias
            s_c = _dot_nt(qm, kc)
            m = jnp.maximum(jnp.max(s_l, axis=-1, keepdims=True), jnp.max(s_c, axis=-1, keepdims=True))
            acc = (_dot(jnp.exp(s_l - m).astype(BF16), vl[u][ks])
                   + _dot(jnp.exp(s_c - m).astype(BF16), vc[u]))
            outs.append(acc / pltpu.roll(acc, NA_HD, 1))
        o_ref[rs, :] = jnp.where(upper, outs[1], outs[0]).astype(o_ref.dtype)


def _attn_lat(qkv, k_ctx, v_ctx, strips, rowmask):
    npair = NA_HEADS // 2
    return pl.pallas_call(
        functools.partial(_attn_lat_kernel, tq=256),
        grid=(npair, DEC_BATCH),
        in_specs=[
            pl.BlockSpec((DEC_SEQ, LANES), lambda p, b: (b, p)),
            pl.BlockSpec((DEC_SEQ, LANES), lambda p, b: (b, npair + p)),
            pl.BlockSpec((DEC_SEQ, LANES), lambda p, b: (b, 2 * npair + p)),
            pl.BlockSpec((1, PAST_LEN, LANES), lambda p, b: (b, 0, p)),
            pl.BlockSpec((1, PAST_LEN, LANES), lambda p, b: (b, 0, p)),
            pl.BlockSpec((2, 2, GRID_W, STRIP_W), lambda p, b: (p, 0, 0, 0)),
            pl.BlockSpec((DEC_SEQ // GRID_W, DEC_SEQ), lambda p, b: (0, 0)),
        ],
        out_specs=pl.BlockSpec((DEC_SEQ, LANES), lambda p, b: (b, p)),
        out_shape=jax.ShapeDtypeStruct((NS, D), BF16),
        compiler_params=_cparams(("arbitrary", "arbitrary")),
        name="attn_lat",
    )(qkv, qkv, qkv, k_ctx, v_ctx, strips, rowmask)


def _latent_bias_tables(rpb):
    rows = DEC_SEQ // GRID_W
    kr = min(WIN_R, rows)
    ndr, ndc = 2 * WIN_R - 1, 2 * WIN_C - 1
    qc = np.arange(GRID_W)
    kc = np.arange(GRID_W)
    ws = np.clip(qc - WIN_C // 2, 0, GRID_W - WIN_C)
    col_ok = (kc[None, :] >= ws[:, None]) & (kc[None, :] < ws[:, None] + WIN_C)
    dc = np.clip(kc[None, :] - qc[:, None] + WIN_C - 1, 0, ndc - 1)
    onehot = (dc[None, :, :] == np.arange(ndc)[:, None, None]).astype(np.float32)
    t2 = jnp.einsum('hrc,cqk->hqrk', rpb.astype(F32), jnp.asarray(onehot),
                    precision=lax.Precision.HIGHEST)
    t2 = jnp.where(jnp.asarray(col_ok)[None, :, None, :], t2, NEG)
    lead = rows - WIN_R
    n_tiles = 2 * rows - 1

    def neg(n):
        return jnp.full((NA_HEADS, GRID_W, n, GRID_W), NEG, F32)

    strip = jnp.concatenate([neg(lead), t2, neg(n_tiles - lead - ndr)], axis=2)
    strip = strip.reshape(NA_HEADS, GRID_W, n_tiles * GRID_W)

    def pad(a):
        return jnp.pad(a, ((0, 0), (0, 0), (0, STRIP_W - a.shape[-1])), constant_values=NEG)

    strips = jnp.stack([pad(strip), pad(strip[:, :, GRID_W:])], axis=1)
    r = np.arange(rows)
    k0 = np.clip(r - kr // 2, 0, rows - kr)
    krow = np.arange(DEC_SEQ) // GRID_W
    row_ok = (krow[None, :] >= k0[:, None]) & (krow[None, :] < k0[:, None] + kr)
    rowmask = jnp.asarray(np.where(row_ok, 0.0, NEG).astype(np.float32))
    return strips, rowmask


def _hgrn_mix_out(refs, mod_ref, tile):
    of_ref, ob_ref, gate_ref, xp_ref, xs_ref, gn_ref, w_ref = refs
    x = jnp.where(tile < NP // MOE_TM, xp_ref[...], xs_ref[...])
    o = of_ref[...] + ob_ref[...]
    gn = gn_ref[...]
    segs = []
    for h in range(HG_HEADS):
        seg = o[:, h * LANES:(h + 1) * LANES]
        ms = jnp.mean(seg * seg, axis=-1, keepdims=True)
        segs.append(seg * lax.rsqrt(ms + EPS) * gn)
    y = (jnp.concatenate(segs, axis=1) * _silu(gate_ref[...])).astype(BF16)
    return x + mod_ref[0, 2:3, :] * _dot(y, w_ref[...])


def _attn_mix_out(refs, mod_ref, tile):
    ap_ref, as_ref, x_ref, w_ref = refs
    a = jnp.where(tile < NP // MOE_TM, ap_ref[...], as_ref[...])
    return x_ref[...] + mod_ref[0, 2:3, :] * _dot(a, w_ref[...])


def _route_kernel(*refs, mix_out, n_mix):
    mix_refs = refs[:n_mix]
    (nw_ref, mod_ref, wrh_ref, wrl_ref, rb_ref, xnew_ref, xs_ref, pos_ref, wt_ref, npc_ref,
     hb_s, pos_s, used_s) = refs[n_mix:]
    i = pl.program_id(0)
    cur = lax.rem(i, 2)
    tile = jnp.minimum(i, N_TILES - 1)

    @pl.when(i == 0)
    def _():
        hb_s[...] = jnp.zeros_like(hb_s)
        pos_s[...] = jnp.full(pos_s.shape, -1, I32)
        used_s[0] = 0.0
        used_s[1] = 0.0

    live = i < N_TILES

    def step(slot):
        x = mix_out(mix_refs, mod_ref, tile)
        xnew_ref[...] = x
        hb, pos, used = _route_select(x, nw_ref, mod_ref, wrh_ref, wrl_ref, rb_ref, pos_ref, wt_ref, npc_ref)
        hb_s[slot] = hb
        pos_s[slot] = jnp.where(live, pos, -1)
        used_s[slot] = jnp.where(live, used, 0.0)
        _route_sort(xs_ref, hb_s[1 - slot], pos_s[1 - slot], used_s[1 - slot])

    for slot in range(2):
        pl.when(cur == slot)(functools.partial(step, slot))


def _route_sort(xs_ref, hb, pos, used):
    pos16 = [pos[k:k + 1].astype(jnp.int16) for k in range(TOP_K)]

    def fill(r0, n):
        j16 = (lax.broadcasted_iota(I32, (n, MOE_TM), 0) + r0).astype(jnp.int16)
        onehot = jnp.zeros((n, MOE_TM), BF16)
        for k in range(TOP_K):
            onehot = jnp.where(j16 == pos16[k], jnp.ones((), BF16), onehot)
        xs_ref[r0:r0 + n, :] = _dot(onehot, hb).astype(BF16)

    fill(0, ROWS_MAIN)
    for r0 in range(ROWS_MAIN, R1, ROW_BLK):
        pl.when(used > r0)(functools.partial(fill, r0, ROW_BLK))

        @pl.when(used <= r0)
        def _(r0=r0):
            xs_ref[r0:r0 + ROW_BLK, :] = jnp.zeros((ROW_BLK, D), BF16)


def _route_select(x, nw_ref, mod_ref, wrh_ref, wrl_ref, rb_ref, pos_ref, wt_ref, npc_ref):
    h = _norm_mod(x, nw_ref[...], mod_ref[0, 3:4, :], mod_ref[0, 4:5, :])
    hb = h.astype(BF16)
    hl = (h - hb.astype(F32)).astype(BF16)
    wrh = wrh_ref[...]
    logits = _dot_nt(wrh, hb) + _dot_nt(wrh, hl) + _dot_nt(wrl_ref[...], hb)
    scores = jax.nn.sigmoid(logits)
    sel = scores + rb_ref[...]

    gsz = N_EXPERTS // N_GROUPS
    sub = lax.broadcasted_iota(I32, (gsz, MOE_TM), 0)
    ninf = -jnp.inf
    gs_rows = []
    for gi in range(N_GROUPS):
        blk = sel[gi * gsz:(gi + 1) * gsz]
        m1 = jnp.max(blk, axis=0, keepdims=True)
        first = jnp.min(jnp.where(blk == m1, sub, gsz), axis=0, keepdims=True)
        m2 = jnp.max(jnp.where(sub == first, ninf, blk), axis=0, keepdims=True)
        gs_rows.append(m1 + m2)
    cur = jnp.concatenate(gs_rows, axis=0)
    gidx = lax.broadcasted_iota(I32, (N_GROUPS, MOE_TM), 0)
    gsel = jnp.zeros((N_GROUPS, MOE_TM), F32)
    for _ in range(TOPK_GROUPS):
        m = jnp.max(cur, axis=0, keepdims=True)
        first = jnp.min(jnp.where(cur == m, gidx, N_GROUPS), axis=0, keepdims=True)
        hit = gidx == first
        gsel = jnp.where(hit, 1.0, gsel)
        cur = jnp.where(hit, ninf, cur)
    emask = jnp.concatenate(
        [jnp.broadcast_to(gsel[gi:gi + 1], (gsz, MOE_TM)) for gi in range(N_GROUPS)], axis=0)
    masked = jnp.where(emask > 0.5, sel, ninf)
    eidx = lax.broadcasted_iota(I32, (N_EXPERTS, MOE_TM), 0)
    chosen = jnp.zeros((N_EXPERTS, MOE_TM), F32)
    hits, wsel = [], []
    for _ in range(TOP_K):
        m = jnp.max(masked, axis=0, keepdims=True)
        first = jnp.min(jnp.where(masked == m, eidx, N_EXPERTS), axis=0, keepdims=True)
        hit = eidx == first
        hits.append(hit)
        wsel.append(jnp.sum(jnp.where(hit, scores, 0.0), axis=0, keepdims=True))
        chosen = jnp.where(hit, 1.0, chosen)
        masked = jnp.where(hit, ninf, masked)
    wsum = wsel[0]
    for w in wsel[1:]:
        wsum = wsum + w

    n_io = lax.broadcasted_iota(I32, (MOE_TM, MOE_TM), 0)
    m_io = lax.broadcasted_iota(I32, (MOE_TM, MOE_TM), 1)
    earlier = jnp.where(n_io < m_io, 1.0, 0.0).astype(BF16)
    rank = _dot(chosen.astype(BF16), earlier)
    cnt = jnp.sum(chosen, axis=1, keepdims=True)
    npc = jnp.floor((cnt + (PIECE - 1)) * (1.0 / PIECE))
    e_io = lax.broadcasted_iota(I32, (N_EXPERTS, N_EXPERTS), 0)
    f_io = lax.broadcasted_iota(I32, (N_EXPERTS, N_EXPERTS), 1)
    below = jnp.where(f_io < e_io, 1.0, 0.0).astype(BF16)
    npc_l = jnp.broadcast_to(npc, (N_EXPERTS, LANES))
    start = _dot(below, npc_l.astype(BF16))[:, 0:1] * PIECE
    slot = start + rank

    pos_rows, wt_rows = [], []
    for k in range(TOP_K):
        pos_k = jnp.sum(jnp.where(hits[k], slot, 0.0), axis=0, keepdims=True).astype(I32)
        pos_rows.append(pos_k)
        wt_rows.append(wsel[k] / wsum * ROUTED_SCALE)

    pad = SUBLANES - TOP_K
    pos = jnp.concatenate(pos_rows + [jnp.full((pad, MOE_TM), -1, I32)], axis=0)
    pos_ref[...] = pos
    wt_ref[...] = jnp.concatenate(wt_rows + [jnp.zeros((pad, MOE_TM), F32)], axis=0)
    npc_ref[0] = npc_l.astype(I32)
    return hb, pos, jnp.sum(npc) * PIECE


def _hgrn_mix_inputs(o_f, o_b, proj, x_pair, gn_w, w_out):
    n_p = NP // MOE_TM

    def specs(tile):
        return [pl.BlockSpec((MOE_TM, D), lambda i: (tile(i), 0)),
                pl.BlockSpec((MOE_TM, D), lambda i: (tile(i), 0)),
                pl.BlockSpec((MOE_TM, D), lambda i: (tile(i), 4)),
                pl.BlockSpec((MOE_TM, D), lambda i: (jnp.minimum(tile(i), n_p - 1), 0)),
                pl.BlockSpec((MOE_TM, D), lambda i: (jnp.maximum(tile(i) - n_p, 0), 0)),
                pl.BlockSpec((1, HG_DV), lambda i: (0, 0)),
                pl.BlockSpec((D, D), lambda i: (0, 0))]

    return _hgrn_mix_out, (o_f, o_b, proj, x_pair[0], x_pair[1], gn_w.reshape(1, HG_DV), w_out), specs


def _attn_mix_inputs(a_p, a_s, x, w_o):
    n_p = NP // MOE_TM

    def specs(tile):
        return [pl.BlockSpec((MOE_TM, D), lambda i: (jnp.minimum(tile(i), n_p - 1), 0)),
                pl.BlockSpec((MOE_TM, D), lambda i: (jnp.maximum(tile(i) - n_p, 0), 0)),
                pl.BlockSpec((MOE_TM, D), lambda i: (tile(i), 0)),
                pl.BlockSpec((D, D), lambda i: (0, 0))]

    return _attn_mix_out, (a_p, a_s, x, w_o), specs


def _route(mix, nw, mod, wr_hi, wr_lo, rbias):
    mix_out, mix_args, mix_specs = mix

    def tile(i):
        return jnp.minimum(i, N_TILES - 1)

    return pl.pallas_call(
        functools.partial(_route_kernel, mix_out=mix_out, n_mix=len(mix_args)),
        grid=(N_TILES + 2,),
        in_specs=mix_specs(tile) + [
            pl.BlockSpec((1, D), lambda i: (0, 0)),
            pl.BlockSpec((1, 6, D), lambda i: (_mod_group(tile(i), MOE_TM), 0, 0)),
            pl.BlockSpec((N_EXPERTS, D), lambda i: (0, 0)),
            pl.BlockSpec((N_EXPERTS, D), lambda i: (0, 0)),
            pl.BlockSpec((N_EXPERTS, 1), lambda i: (0, 0)),
        ],
        out_specs=[
            pl.BlockSpec((MOE_TM, D), lambda i: (tile(i), 0)),
            pl.BlockSpec((R1, D), lambda i: (jnp.maximum(i - 1, 0), 0)),
            pl.BlockSpec((SUBLANES, MOE_TM), lambda i: (0, tile(i))),
            pl.BlockSpec((SUBLANES, MOE_TM), lambda i: (0, tile(i))),
            pl.BlockSpec((1, N_EXPERTS, LANES), lambda i: (tile(i), 0, 0)),
        ],
        out_shape=[
            jax.ShapeDtypeStruct((NT, D), F32),
            jax.ShapeDtypeStruct((P_TOT * PIECE, D), BF16),
            jax.ShapeDtypeStruct((SUBLANES, NT), I32),
            jax.ShapeDtypeStruct((SUBLANES, NT), F32),
            jax.ShapeDtypeStruct((N_TILES, N_EXPERTS, LANES), I32),
        ],
        scratch_shapes=[pltpu.VMEM((2, MOE_TM, D), BF16), pltpu.VMEM((2, SUBLANES, MOE_TM), I32),
                        pltpu.SMEM((2,), F32)],
        compiler_params=_cparams(("arbitrary",)),
        name="moe_route",
    )(*mix_args, nw.reshape(1, D), mod, wr_hi, wr_lo, rbias.reshape(N_EXPERTS, 1))


def _piece_lists(npc):
    t, e = npc.shape
    hp = lax.Precision.HIGHEST
    npc_t = npc.T.astype(F32)
    start_t = (jnp.cumsum(npc, axis=1) - npc).T.astype(F32)
    tile_end = jnp.cumsum(npc_t, axis=1)
    n_e = tile_end[:, -1]
    pe_end = jnp.cumsum(n_e)
    pe_off = pe_end - n_e
    p = jnp.arange(P_MAX, dtype=F32)
    e_p = jnp.minimum(jnp.sum((pe_end[None, :] <= p[:, None]).astype(I32), axis=1), e - 1)
    oh_e = (e_p[:, None] == jnp.arange(e, dtype=I32)[None, :]).astype(F32)
    tab = jnp.concatenate([tile_end, start_t, npc_t, pe_off[:, None]], axis=1)
    row = jnp.dot(oh_e, tab, precision=hp)
    te_p, st_p, np_p, off_p = row[:, :t], row[:, t:2 * t], row[:, 2 * t:3 * t], row[:, 3 * t]
    local = p - off_p
    t_p = jnp.minimum(jnp.sum((te_p <= local[:, None]).astype(I32), axis=1), t - 1)
    oh_t = t_p[:, None] == jnp.arange(t, dtype=I32)[None, :]

    def pick(a):
        return jnp.sum(jnp.where(oh_t, a, 0.0), axis=1)

    src = t_p * R1B + (pick(st_p) + local - (pick(te_p) - pick(np_p))).astype(I32)
    src = jnp.concatenate([jnp.clip(src, 0, P_MAX - 1), jnp.zeros((G_PIECES,), I32)])

    nch = jnp.floor((n_e + (G_PIECES - 1)) * (1.0 / G_PIECES))
    ch_end = jnp.cumsum(nch)
    ch_off = jnp.concatenate([jnp.zeros((1,), F32), ch_end]).astype(I32)
    c = jnp.arange(-N_SLOTS, NCH + 2, dtype=F32)
    ce = jnp.minimum(jnp.sum((ch_end[None, :] <= c[:, None]).astype(I32), axis=1), e - 1)
    oh_c = (ce[:, None] == jnp.arange(e, dtype=I32)[None, :]).astype(F32)
    crow = jnp.dot(oh_c, jnp.stack([ch_end - nch, pe_off, n_e], axis=1), precision=hp)
    k_in = c - crow[:, 0]
    live = jnp.logical_and(c >= 0, c < ch_end[-1])
    cn = jnp.where(live, jnp.clip(crow[:, 2] - G_PIECES * k_in, 0, G_PIECES), 0.0)
    cs = jnp.where(cn > 0, crow[:, 1] + G_PIECES * k_in, 0.0)
    return src, ch_off, cs.astype(I32), cn.astype(I32)


def _ffn_kernel(src_ref, choff_ref, cs_ref, cn_ref, xs_in, wg_ref, wu_ref, wd_ref, xs_out,
                xbuf, ybuf, wgb, wub, wdb, gsem, ssem):
    e = pl.program_id(0)
    total = choff_ref[N_EXPERTS]

    def start_gather(ch):
        sl = lax.rem(ch + N_SLOTS, N_SLOTS)
        base = cs_ref[ch + N_SLOTS]
        n = cn_ref[ch + N_SLOTS]
        for i in range(G_PIECES):
            idx = src_ref[base + jnp.where(i < n, i, 0)]
            pltpu.make_async_copy(xs_in.at[idx], xbuf.at[sl, i], gsem.at[sl]).start(priority=i % 2)

    def wait_gather(ch):
        sl = lax.rem(ch + N_SLOTS, N_SLOTS)
        pltpu.make_async_copy(xs_in.at[pl.ds(0, G_PIECES)], xbuf.at[sl], gsem.at[sl]).wait()

    def start_scatter(ch):
        sl = lax.rem(ch + N_SLOTS, N_SLOTS)
        base = cs_ref[ch + N_SLOTS]
        n = cn_ref[ch + N_SLOTS]
        for i in range(G_PIECES):
            idx = jnp.where(i < n, src_ref[base + i], P_MAX + sl * G_PIECES + i)
            pltpu.make_async_copy(ybuf.at[sl, i], xs_out.at[idx], ssem.at[sl]).start(priority=i % 2)

    def wait_scatter(ch):
        sl = lax.rem(ch + N_SLOTS, N_SLOTS)
        pltpu.make_async_copy(ybuf.at[sl], xs_out.at[pl.ds(0, G_PIECES)], ssem.at[sl]).wait()

    @pl.when(e == 0)
    def _():
        ybuf[...] = jnp.zeros_like(ybuf)
        start_gather(0)
        start_gather(1)
        start_scatter(-3)
        start_scatter(-2)

    wgb[...] = wg_ref[0].astype(BF16)
    wub[...] = wu_ref[0].astype(BF16)
    wdb[...] = wd_ref[0].astype(BF16)

    def chunk(c, carry):
        sl = lax.rem(c, N_SLOTS)
        wait_gather(c)
        wait_scatter(c - 3)
        x = xbuf[sl].reshape(G_PIECES * PIECE, D)
        hid = (_silu(_dot(x, wgb[...])) * _dot(x, wub[...])).astype(BF16)
        start_gather(c + 2)
        start_scatter(c - 1)
        ybuf[sl] = _dot(hid, wdb[...]).astype(BF16).reshape(G_PIECES, PIECE, D)
        return carry

    lax.fori_loop(choff_ref[e], choff_ref[e + 1], chunk, 0)

    @pl.when(e == N_EXPERTS - 1)
    def _():
        start_scatter(total - 1)
        wait_gather(total)
        wait_gather(total + 1)
        wait_scatter(total - 3)
        wait_scatter(total - 2)
        wait_scatter(total - 1)


def _expert_ffn(xs, lists, layer, w_gate, w_up, w_down):
    src, ch_off, cs, cn = lists
    grid_spec = pltpu.PrefetchScalarGridSpec(
        num_scalar_prefetch=4,
        grid=(N_EXPERTS,),
        in_specs=[
            pl.BlockSpec(memory_space=pl.ANY),
            pl.BlockSpec((None, 1, D, D_EXPERT), lambda e, *_: (layer, e, 0, 0)),
            pl.BlockSpec((None, 1, D, D_EXPERT), lambda e, *_: (layer, e, 0, 0)),
            pl.BlockSpec((None, 1, D_EXPERT, D), lambda e, *_: (layer, e, 0, 0)),
        ],
        out_specs=pl.BlockSpec(memory_space=pl.ANY),
        scratch_shapes=[
            pltpu.VMEM((N_SLOTS, G_PIECES, PIECE, D), BF16),
            pltpu.VMEM((N_SLOTS, G_PIECES, PIECE, D), BF16),
            pltpu.VMEM((D, D_EXPERT), BF16),
            pltpu.VMEM((D, D_EXPERT), BF16),
            pltpu.VMEM((D_EXPERT, D), BF16),
            pltpu.SemaphoreType.DMA((N_SLOTS,)),
            pltpu.SemaphoreType.DMA((N_SLOTS,)),
        ],
    )
    out = pl.pallas_call(
        _ffn_kernel,
        grid_spec=grid_spec,
        out_shape=jax.ShapeDtypeStruct((P_TOT, PIECE, D), BF16),
        input_output_aliases={4: 0},
        compiler_params=_cparams(("arbitrary",)),
        name="moe_ffn",
    )(src, ch_off, cs, cn, xs.reshape(P_TOT, PIECE, D), w_gate, w_up, w_down)
    return out.reshape(P_TOT * PIECE, D)


def _combine_kernel(used_ref, tail_ref, *refs, final):
    del tail_ref
    n_main = ROWS_MAIN // FETCH_BLK
    ys_refs = refs[:n_main + 1]
    (pos_ref, wt_ref, x_ref, nw_ref, mod_ref, wsg_ref, wsu_ref, wsd_ref, fw_ref) = refs[n_main + 1:n_main + 10]
    rest = refs[n_main + 10:]
    o_refs, acc_ref = rest[:-1], rest[-1]
    i = pl.program_id(0)
    x = x_ref[...]
    h = _norm_mod(x, nw_ref[...], mod_ref[0, 3:4, :], mod_ref[0, 4:5, :]).astype(BF16)
    shared = _dot((_silu(_dot(h, wsg_ref[...])) * _dot(h, wsu_ref[...])).astype(BF16), wsd_ref[...])

    pos16 = pos_ref[...].astype(jnp.int16)
    wt16 = wt_ref[...].astype(BF16)

    def block(r0, n):
        j16 = (lax.broadcasted_iota(I32, (MOE_TM, n), 1) + r0).astype(jnp.int16)
        wm = jnp.zeros((MOE_TM, n), BF16)
        for k in range(TOP_K):
            wm = jnp.where(j16 == pos16[:, k:k + 1], wt16[:, k:k + 1], wm)
        ref, off = ys_refs[r0 // FETCH_BLK], r0 % FETCH_BLK
        return _dot(wm, ref[off:off + n, :])

    routed = block(0, FETCH_BLK)
    for m in range(1, n_main):
        routed = routed + block(m * FETCH_BLK, FETCH_BLK)
    acc_ref[...] = routed
    for r0 in range(ROWS_MAIN, R1, ROW_BLK):
        @pl.when(used_ref[i] > r0)
        def _(r0=r0):
            acc_ref[...] += block(r0, ROW_BLK)

    y = x + mod_ref[0, 5:6, :] * (acc_ref[...] + shared)
    if not final:
        o_refs[0][...] = y
        return
    ms = jnp.mean(y * y, axis=-1, keepdims=True)
    y = y * lax.rsqrt(ms + EPS) * fw_ref[...]
    is_prompt = i < NP // MOE_TM

    @pl.when(is_prompt)
    def _():
        o_refs[0][...] = y

    @pl.when(jnp.logical_not(is_prompt))
    def _():
        o_refs[1][...] = y


def _combine(ys, used, pos_t, wt_t, x, nw, mod, wsg, wsu, wsd, fw, final):
    n_p = NP // MOE_TM
    if final:
        out_specs = [pl.BlockSpec((MOE_TM, D), lambda i, u, t: (jnp.minimum(i, n_p - 1), 0)),
                     pl.BlockSpec((MOE_TM, D), lambda i, u, t: (jnp.maximum(i - n_p, 0), 0))]
        out_shape = [jax.ShapeDtypeStruct((NP, D), F32), jax.ShapeDtypeStruct((NS, D), F32)]
    else:
        out_specs = pl.BlockSpec((MOE_TM, D), lambda i, u, t: (i, 0))
        out_shape = jax.ShapeDtypeStruct((NT, D), F32)
    per_tile = R1 // FETCH_BLK
    n_main = ROWS_MAIN // FETCH_BLK
    assert per_tile == n_main + 1
    tiles = jnp.arange(N_TILES, dtype=I32)
    cand = jnp.where(used > ROWS_MAIN, tiles * per_tile + n_main, n_main)
    tail = jnp.max(jnp.where(tiles[None, :] <= tiles[:, None], cand[None, :], n_main), axis=1).astype(I32)
    grid_spec = pltpu.PrefetchScalarGridSpec(
        num_scalar_prefetch=2,
        grid=(N_TILES,),
        in_specs=[pl.BlockSpec((FETCH_BLK, D), functools.partial(lambda m, i, u, t: (i * per_tile + m, 0), m))
                  for m in range(n_main)] + [
            pl.BlockSpec((FETCH_BLK, D), lambda i, u, t: (t[i], 0)),
            pl.BlockSpec((MOE_TM, SUBLANES), lambda i, u, t: (i, 0)),
            pl.BlockSpec((MOE_TM, SUBLANES), lambda i, u, t: (i, 0)),
            pl.BlockSpec((MOE_TM, D), lambda i, u, t: (i, 0)),
            pl.BlockSpec((1, D), lambda i, u, t: (0, 0)),
            pl.BlockSpec((1, 6, D), lambda i, u, t: (_mod_group(i, MOE_TM), 0, 0)),
            pl.BlockSpec((D, D_SHARED), lambda i, u, t: (0, 0)),
            pl.BlockSpec((D, D_SHARED), lambda i, u, t: (0, 0)),
            pl.BlockSpec((D_SHARED, D), lambda i, u, t: (0, 0)),
            pl.BlockSpec((1, D), lambda i, u, t: (0, 0)),
        ],
        out_specs=out_specs,
        scratch_shapes=[pltpu.VMEM((MOE_TM, D), F32)],
    )
    return pl.pallas_call(
        functools.partial(_combine_kernel, final=final),
        grid_spec=grid_spec,
        out_shape=out_shape,
        compiler_params=_cparams(("arbitrary",)),
        name="moe_combine",
    )(used, tail, *([ys] * (n_main + 1)), pos_t, wt_t, x, nw.reshape(1, D), mod, wsg, wsu, wsd, fw.reshape(1, D))


def _moe(mix, nw, mod, w_router, rbias, layer, w_gate, w_up, w_down, ws_gate, ws_up, ws_down, fw, final):
    wr = w_router.astype(F32).T
    wr_hi = wr.astype(BF16)
    wr_lo = (wr - wr_hi.astype(F32)).astype(BF16)
    x, xs, pos, wt, npc = _route(mix, nw, mod, wr_hi, wr_lo, rbias.astype(F32))
    npc = npc[:, :, 0]
    lists = _piece_lists(npc)
    ys = _expert_ffn(xs, lists, layer, w_gate, w_up, w_down)
    used = (npc.sum(axis=1) * PIECE).astype(I32)
    return _combine(ys, used, pos.T, wt.T, x, nw, mod, ws_gate.astype(BF16), ws_up.astype(BF16),
                    ws_down.astype(BF16), fw, final)


def kernel(x_prompt, x_sample, state_hgrn_fwd, state_hgrn_bwd, cache_na_k, cache_na_v, c, c_ctx,
           norm1_w, norm2_w, ada_w, ada_b, hgrn_w_in, hgrn_lb_logits, hgrn_gn_w, hgrn_w_out,
           na_w_qkv, na_rpb, na_w_o, moe_w_router, moe_router_bias, moe_w_gate, moe_w_up, moe_w_down,
           shared_w_gate, shared_w_up, shared_w_down, final_norm_w):
    x = (x_prompt.reshape(NP, D), x_sample.reshape(NS, D))
    cvec = jnp.concatenate([c_ctx[None, :], c, jnp.zeros((N_MOD - 1 - DEC_BATCH, D), F32)], axis=0)
    mod = _modulation(cvec, ada_w, ada_b)
    lb_table = jnp.cumsum(jax.nn.softmax(hgrn_lb_logits.astype(F32), axis=0), axis=0)
    hk = HG_HEADS * HG_DK

    sf = sb = k_c = v_c = None
    for l in range(DEPTH):
        if l % 2 == 0:
            a = l // 2
            proj, chunk_tot = _hgrn_proj(x, norm1_w[l], mod[l], lb_table[l], hgrn_w_in[a].astype(BF16))
            o_f, o_b, sf, sb = _gla(proj, chunk_tot,
                                    state_hgrn_fwd[:, a].reshape(DEC_BATCH, hk, HG_DV),
                                    state_hgrn_bwd[:, a].reshape(DEC_BATCH, hk, HG_DV))
            mix = _hgrn_mix_inputs(o_f, o_b, proj, x, hgrn_gn_w[a], hgrn_w_out[a].astype(BF16))
        else:
            n = l // 2
            w_qkv = na_w_qkv[n].astype(BF16)
            qkv_p, k_p, v_p = _norm_proj(x, norm1_w[l], mod[l], w_qkv, 0, NP // TM, n_copy=2, out_dtype=BF16)
            qkv_s = _norm_proj(x, norm1_w[l], mod[l], w_qkv, NP // TM, NS // TM, out_dtype=BF16)
            att_p = _attn_ctx(qkv_p)
            att_s = _attn_lat(qkv_s, cache_na_k[:, n].reshape(DEC_BATCH, PAST_LEN, D),
                              cache_na_v[:, n].reshape(DEC_BATCH, PAST_LEN, D),
                              *_latent_bias_tables(na_rpb[n]))
            mix = _attn_mix_inputs(att_p, att_s, x, na_w_o[n].astype(BF16))
            k_c = k_p.reshape(BATCH, SEQ, NA_HEADS, NA_HD)
            v_c = v_p.reshape(BATCH, SEQ, NA_HEADS, NA_HD)
        x = _moe(mix, norm2_w[l], mod[l], moe_w_router[l], moe_router_bias[l], l, moe_w_gate, moe_w_up,
                 moe_w_down, shared_w_gate[l], shared_w_up[l], shared_w_down[l], final_norm_w,
                 final=(l == DEPTH - 1))

    y_prompt = x[0].reshape(BATCH, SEQ, D)
    y_sample = x[1].reshape(DEC_BATCH, DEC_SEQ, D)
    new_sf = sf.reshape(BATCH, 1, HG_HEADS, HG_DK, HG_DV)
    new_sb = sb.reshape(BATCH, 1, HG_HEADS, HG_DK, HG_DV)
    return (y_prompt, y_sample, new_sf, new_sb, k_c[:, None], v_c[:, None])
```

```python
import functools

import numpy as np
import jax
import jax.numpy as jnp
from jax import lax
from jax.experimental import pallas as pl
from jax.experimental.pallas import tpu as pltpu

F32 = jnp.float32
BF16 = jnp.bfloat16
I32 = jnp.int32

D = 1024
BATCH = 32
SEQ = 256
DEPTH = 2
DEC_BATCH = 8
DEC_SEQ = 1024
PAST_LEN = 512
GRID_W = 64
HG_HEADS = 8
HG_DK = 128
HG_DV = 128
CHUNK = 64
NA_HEADS = 16
NA_HD = 64
WIN_R = 8
WIN_C = 16
N_EXPERTS = 64
TOP_K = 6
N_GROUPS = 8
TOPK_GROUPS = 4
D_EXPERT = 256
D_SHARED = 256
ROUTED_SCALE = 2.5
EPS = 1e-6

NP = BATCH * SEQ
NS = DEC_BATCH * DEC_SEQ
NT = NP + NS
N_MOD = 16
STRIP_W = 2048
assert (2 * (DEC_SEQ // GRID_W) - 1) * GRID_W <= STRIP_W
NEG = -1e30

LANES = 128
SUBLANES = 8
BF16_ROWS = 16
VMEM_LIMIT = 56 * 1024 * 1024

TM = 256
SUB = 16
GLA_FAST_MAX = 60.0
GLA_CPS = 4
GLA_BLK = GLA_CPS * CHUNK
assert SEQ % GLA_BLK == 0 and DEC_SEQ % GLA_BLK == 0
MOE_TM = 256
PIECE = BF16_ROWS
R1 = MOE_TM * TOP_K + N_EXPERTS * (PIECE - 1) + 64
assert R1 % PIECE == 0 and R1 % LANES == 0
R1B = R1 // PIECE
ROWS_MAIN = 2048
ROW_BLK = 256
FETCH_BLK = 512
assert (R1 - ROWS_MAIN) % ROW_BLK == 0 and R1 - ROWS_MAIN == FETCH_BLK and ROWS_MAIN % FETCH_BLK == 0
N_TILES = NT // MOE_TM
P_MAX = N_TILES * R1B
G_PIECES = 32
NCH = P_MAX // G_PIECES + N_EXPERTS
N_SLOTS = 3
P_TOT = P_MAX + R1B
assert N_SLOTS * G_PIECES <= R1B


def _cparams(sem):
    return pltpu.CompilerParams(dimension_semantics=sem, vmem_limit_bytes=VMEM_LIMIT)


def _dot(a, b):
    return jnp.dot(a, b, preferred_element_type=F32)


def _dot_nt(a, b):
    return lax.dot_general(a, b, (((1,), (1,)), ((), ())), preferred_element_type=F32)


def _dot_tn(a, b):
    return lax.dot_general(a, b, (((0,), (0,)), ((), ())), preferred_element_type=F32)


def _silu(x):
    return x * jax.nn.sigmoid(x)


def _mod_group(i, tm):
    r = i * tm
    return jnp.where(r < NP, 0, 1 + (r - NP) // DEC_SEQ)


def _norm_mod(x, nw, shift, scale):
    ms = jnp.mean(x * x, axis=-1, keepdims=True)
    y = x * lax.rsqrt(ms + EPS) * nw
    return y * (1.0 + scale) + shift


def _mod_kernel(c_ref, w_ref, b_ref, o_ref):
    s = _silu(c_ref[...]).astype(BF16)
    o_ref[0] = _dot(s, w_ref[0].astype(BF16)) + b_ref[0]


def _modulation(cvec, ada_w, ada_b):
    cw = 1536
    n = ada_w.shape[-1]
    out = pl.pallas_call(
        _mod_kernel,
        grid=(DEPTH, n // cw),
        in_specs=[
            pl.BlockSpec((N_MOD, D), lambda l, j: (0, 0)),
            pl.BlockSpec((1, D, cw), lambda l, j: (l, 0, j)),
            pl.BlockSpec((1, 1, cw), lambda l, j: (l, 0, j)),
        ],
        out_specs=pl.BlockSpec((1, N_MOD, cw), lambda l, j: (l, 0, j)),
        out_shape=jax.ShapeDtypeStruct((DEPTH, N_MOD, n), F32),
        compiler_params=_cparams(("arbitrary", "arbitrary")),
        name="modulation",
    )(cvec, ada_w, ada_b.reshape(DEPTH, 1, n))
    return out.reshape(DEPTH, N_MOD, 6, D)


def _stream_specs(x, tm):
    if not isinstance(x, tuple):
        return [pl.BlockSpec((tm, D), lambda i: (i, 0))], (x,)
    n_p = NP // tm
    return [pl.BlockSpec((tm, D), lambda i: (jnp.minimum(i, n_p - 1), 0)),
            pl.BlockSpec((tm, D), lambda i: (jnp.maximum(i - n_p, 0), 0))], x


def _stream_tile(refs, tm):
    if len(refs) == 1:
        return refs[0][...]
    return jnp.where(pl.program_id(0) < NP // tm, refs[0][...], refs[1][...])


def _proj_kernel(x_ref, nw_ref, mod_ref, w_ref, o_ref, *copy_refs, cw):
    h = _norm_mod(x_ref[...], nw_ref[...], mod_ref[0, 0:1, :], mod_ref[0, 1:2, :]).astype(BF16)
    for j in range(w_ref.shape[1] // cw):
        val = _dot(h, w_ref[:, j * cw:(j + 1) * cw])
        o_ref[:, j * cw:(j + 1) * cw] = val.astype(o_ref.dtype)
        col = j * cw - D
        if 0 <= col < len(copy_refs) * D:
            copy_refs[col // D][:, col % D:col % D + cw] = val


def _norm_proj(x, nw, mod, w, tile0, n_tiles, n_copy=0, out_dtype=F32):
    n = w.shape[1]
    rows = n_tiles * TM
    outs = pl.pallas_call(
        functools.partial(_proj_kernel, cw=512),
        grid=(n_tiles,),
        in_specs=[
            pl.BlockSpec((TM, D), lambda i: (tile0 + i, 0)),
            pl.BlockSpec((1, D), lambda i: (0, 0)),
            pl.BlockSpec((1, 6, D), lambda i: (_mod_group(tile0 + i, TM), 0, 0)),
            pl.BlockSpec((D, n), lambda i: (0, 0)),
        ],
        out_specs=[pl.BlockSpec((TM, n), lambda i: (i, 0))] + [pl.BlockSpec((TM, D), lambda i: (i, 0))] * n_copy,
        out_shape=[jax.ShapeDtypeStruct((rows, n), out_dtype)] + [jax.ShapeDtypeStruct((rows, D), F32)] * n_copy,
        compiler_params=_cparams(("arbitrary",)),
        name="norm_proj",
    )(x, nw.reshape(1, D), mod, w)
    return outs if n_copy else outs[0]


def _hgrn_proj_kernel(*refs, cw, n_x):
    nw_ref, mod_ref, lb_ref, w_ref, o_ref, tot_ref = refs[n_x:]
    x = _stream_tile(refs[:n_x], TM)
    h = _norm_mod(x, nw_ref[...], mod_ref[0, 0:1, :], mod_ref[0, 1:2, :]).astype(BF16)
    n_ck = TM // CHUNK
    mins = {}
    for j in range(w_ref.shape[1] // cw):
        val = _dot(h, w_ref[:, j * cw:(j + 1) * cw])
        sec, col = divmod(j * cw, D)
        if sec in (1, 2):
            lb = lb_ref[:, col:col + cw]
            val = jnp.log(lb + (1.0 - lb) * jax.nn.sigmoid(val))
            for c in range(n_ck):
                tot = jnp.sum(val[c * CHUNK:(c + 1) * CHUNK], axis=0, keepdims=True)
                m = jnp.min(tot, axis=-1, keepdims=True)
                key = (sec - 1, c)
                mins[key] = m if key not in mins else jnp.minimum(mins[key], m)
        o_ref[:, j * cw:(j + 1) * cw] = val
    tot_ref[0] = jnp.concatenate(
        [jnp.broadcast_to(mins[(d, c)], (1, LANES)) for c in range(n_ck) for d in range(2)], axis=0)


def _hgrn_proj(x, nw, mod, lb, w):
    n = w.shape[1]
    n_ck = TM // CHUNK
    assert 2 * n_ck == SUBLANES
    x_specs, xs = _stream_specs(x, TM)
    proj, tot = pl.pallas_call(
        functools.partial(_hgrn_proj_kernel, cw=512, n_x=len(xs)),
        grid=(NT // TM,),
        in_specs=x_specs + [
            pl.BlockSpec((1, D), lambda i: (0, 0)),
            pl.BlockSpec((1, 6, D), lambda i: (_mod_group(i, TM), 0, 0)),
            pl.BlockSpec((1, HG_HEADS * HG_DK), lambda i: (0, 0)),
            pl.BlockSpec((D, n), lambda i: (0, 0)),
        ],
        out_specs=[pl.BlockSpec((TM, n), lambda i: (i, 0)),
                   pl.BlockSpec((1, SUBLANES, LANES), lambda i: (i, 0, 0))],
        out_shape=[jax.ShapeDtypeStruct((NT, n), F32),
                   jax.ShapeDtypeStruct((NT // TM, SUBLANES, LANES), F32)],
        compiler_params=_cparams(("arbitrary",)),
        name="hgrn_proj",
    )(*xs, nw.reshape(1, D), mod, lb.reshape(1, HG_HEADS * HG_DK), w)
    return proj, tot[:, :, 0].reshape(NT // CHUNK, 2)


def _split3(x):
    hi = x.astype(BF16)
    r = x - hi.astype(F32)
    mid = r.astype(BF16)
    lo = (r - mid.astype(F32)).astype(BF16)
    return hi, mid, lo


def _gla_safe(q, kk, v, g3, st_ref, row_refs, rev):
    bs_ref, ks_ref, kd8_ref = row_refs

    n_stack = 4
    t_io = lax.broadcasted_iota(I32, (n_stack * CHUNK, CHUNK), 0)
    u_io = lax.broadcasted_iota(I32, (n_stack * CHUNK, CHUNK), 1)
    which = t_io // CHUNK
    tt = t_io - which * CHUNK
    b16 = (tt // SUB) * SUB
    b8 = (tt // SUBLANES) * SUBLANES
    if rev:
        lim = jnp.where(which == 0, tt, jnp.where(which == 1, b16 + SUB, jnp.where(which == 2, b8 + SUBLANES, b8)))
        pick = u_io >= lim
    else:
        lim = jnp.where(which == 0, tt, jnp.where(which == 1, b16 - 1,
                                                  jnp.where(which == 2, b8 - 1, b8 + SUBLANES - 1)))
        pick = u_io <= lim
    tri = jnp.where(pick, 1.0, 0.0).astype(BF16)
    hi, mid, lo = g3
    cs = _dot(tri, hi) + _dot(tri, mid) + _dot(tri, lo)
    b, r16, r8, e8 = (cs[j * CHUNK:(j + 1) * CHUNK] for j in range(n_stack))
    last = 0 if rev else CHUNK - 1
    tot = b[last:last + 1]

    qe = (q * jnp.exp(b)).astype(BF16)
    qd = (q * jnp.exp(b - r16)).astype(BF16)
    qd8 = q * jnp.exp(b - r8)
    kdec = (kk * jnp.exp(tot - b)).astype(BF16)
    vb = v.astype(BF16)
    dec_tot = jnp.exp(tot)
    kd8 = kk * jnp.exp(e8 - b)
    for h in range(HG_HEADS):
        sl = slice(h * LANES, (h + 1) * LANES)
        bs_ref[h] = b[:, sl]
        ks_ref[h] = kk[:, sl]
        kd8_ref[h] = kd8[:, sl]

    def row(ref, h, s):
        return jnp.broadcast_to(ref[h, s:s + 1, :], (SUBLANES, LANES))

    lane = lax.broadcasted_iota(I32, (SUBLANES, LANES), 1)
    row8 = lax.broadcasted_iota(I32, (SUBLANES, LANES), 0)
    a_in = [[] for _ in range(HG_HEADS)]
    for tb in range(CHUNK // SUBLANES):
        t0 = tb * SUBLANES
        blk0 = (t0 // SUB) * SUB
        other = blk0 + SUBLANES if t0 == blk0 else blk0
        keep = (row8 + t0 <= lane) if rev else (row8 + t0 >= lane)
        for h in range(HG_HEADS):
            sl = slice(h * LANES, (h + 1) * LANES)
            q_t = q[t0:t0 + SUBLANES, sl]
            b_t = b[t0:t0 + SUBLANES, sl]
            acc = jnp.zeros((SUBLANES, LANES), F32)
            for s in range(t0, t0 + SUBLANES):
                p = q_t * jnp.exp(b_t - row(bs_ref, h, s)) * row(ks_ref, h, s)
                acc = jnp.where(lane == s, jnp.sum(p, axis=-1, keepdims=True), acc)
            acc = jnp.where(keep, acc, 0.0)
            if (t0 == blk0) == rev:
                qd8_t = qd8[t0:t0 + SUBLANES, sl]
                for s in range(other, other + SUBLANES):
                    p = qd8_t * row(kd8_ref, h, s)
                    acc = jnp.where(lane == s, jnp.sum(p, axis=-1, keepdims=True), acc)
            a_in[h].append(acc)

    n_blk = CHUNK // SUB
    o_heads = []
    for h in range(HG_HEADS):
        sl = slice(h * LANES, (h + 1) * LANES)
        st = st_ref[sl, :]
        o_h = _dot_nt(qe[:, sl], st.astype(BF16))
        rows = []
        for i in range(n_blk):
            lo_r, hi_r = ((i + 1) * SUB, CHUNK) if rev else (0, i * SUB)
            if hi_r == lo_r:
                rows.append(jnp.zeros((SUB, CHUNK), F32))
                continue
            r_i = r16[i * SUB:i * SUB + 1, sl]
            kd = kk[lo_r:hi_r, sl] * jnp.exp(r_i - b[lo_r:hi_r, sl])
            pads = [jnp.zeros((lo_r, LANES), F32)] if lo_r else []
            pads_hi = [jnp.zeros((CHUNK - hi_r, LANES), F32)] if hi_r < CHUNK else []
            kd = jnp.concatenate(pads + [kd] + pads_hi, axis=0).astype(BF16)
            rows.append(_dot_nt(qd[i * SUB:(i + 1) * SUB, sl], kd))
        a1 = jnp.concatenate(rows, axis=0) + jnp.concatenate(a_in[h], axis=0)[:, :CHUNK]
        o_h = o_h + _dot(a1.astype(BF16), vb[:, sl])
        o_heads.append(o_h)
        st_ref[sl, :] = st * dec_tot[:, sl] + _dot_tn(vb[:, sl], kdec[:, sl])
    return jnp.concatenate(o_heads, axis=1)


def _gla_fast(q, kk, v, b, st_ref, rev):
    qe = (q * jnp.exp(b)).astype(BF16)
    ke32 = kk * jnp.exp(-b)
    ke = ke32.astype(BF16)
    vb = v.astype(BF16)
    t_io = lax.broadcasted_iota(I32, (GLA_BLK, GLA_BLK), 0)
    s_io = lax.broadcasted_iota(I32, (GLA_BLK, GLA_BLK), 1)
    same = (t_io // CHUNK) == (s_io // CHUNK)
    keep = jnp.logical_and(same, (s_io >= t_io) if rev else (s_io <= t_io))
    order = range(GLA_CPS - 1, -1, -1) if rev else range(GLA_CPS)
    o_heads = []
    for h in range(HG_HEADS):
        sl = slice(h * LANES, (h + 1) * LANES)
        a = jnp.where(keep, _dot_nt(qe[:, sl], ke[:, sl]), 0.0).astype(BF16)
        o_h = _dot(a, vb[:, sl])
        st = st_ref[sl, :]
        inter = [None] * GLA_CPS
        for k in order:
            r = slice(k * CHUNK, (k + 1) * CHUNK)
            last = k * CHUNK if rev else (k + 1) * CHUNK - 1
            dec_tot = jnp.exp(b[last:last + 1, sl])
            inter[k] = _dot_nt(qe[r, sl], st.astype(BF16))
            kdec = (ke32[r, sl] * dec_tot).astype(BF16)
            st = st * dec_tot + _dot_tn(vb[r, sl], kdec)
        st_ref[sl, :] = st
        o_heads.append(o_h + jnp.concatenate(inter, axis=0))
    return jnp.concatenate(o_heads, axis=1)


def _gla_prep(g, rev):
    kk = 1.0 - jnp.exp(g)
    g3 = _split3(g)
    t_io = lax.broadcasted_iota(I32, (GLA_BLK, GLA_BLK), 0)
    u_io = lax.broadcasted_iota(I32, (GLA_BLK, GLA_BLK), 1)
    same = (t_io // CHUNK) == (u_io // CHUNK)
    tri = jnp.where(jnp.logical_and(same, (u_io >= t_io) if rev else (u_io <= t_io)), 1.0, 0.0).astype(BF16)
    b = _dot(tri, g3[0]) + _dot(tri, g3[1]) + _dot(tri, g3[2])
    return kk, g3, b


def _gla_kernel(mild_ref, qf_ref, gf_ref, vf_ref, qb_ref, gb_ref, vb_ref, s0f_ref, s0b_ref,
                of_ref, ob_ref, sf_ref, sb_ref, stf, stb, bs, ks, kd8):
    i = pl.program_id(0)
    n_p = NP // GLA_BLK
    is_prompt = i < n_p
    c = jnp.where(is_prompt, i % (SEQ // GLA_BLK), (i - n_p) % (DEC_SEQ // GLA_BLK))
    n_c = jnp.where(is_prompt, SEQ // GLA_BLK, DEC_SEQ // GLA_BLK)

    @pl.when(jnp.logical_and(c == 0, is_prompt))
    def _():
        stf[...] = jnp.zeros_like(stf)
        stb[...] = jnp.zeros_like(stb)

    @pl.when(jnp.logical_and(c == 0, jnp.logical_not(is_prompt)))
    def _():
        for h in range(HG_HEADS):
            sl = slice(h * LANES, (h + 1) * LANES)
            stf[sl, :] = s0f_ref[0, sl, :].T
            stb[sl, :] = s0b_ref[0, sl, :].T

    rows = (bs, ks, kd8)
    rf = [slice(k * CHUNK, (k + 1) * CHUNK) for k in range(GLA_CPS)]
    rb = rf[::-1]
    kk_f, g3_f, b_f = _gla_prep(gf_ref[...], False)
    kk_b, g3_b, b_b = _gla_prep(gb_ref[...], True)
    mild = mild_ref[i] != 0

    @pl.when(mild)
    def _():
        of_ref[...] = _gla_fast(qf_ref[...], kk_f, vf_ref[...], b_f, stf, False)
        ob_ref[...] = _gla_fast(qb_ref[...], kk_b, vb_ref[...], b_b, stb, True)

    @pl.when(jnp.logical_not(mild))
    def _():
        for k in range(GLA_CPS):
            r = rf[k]
            of_ref[r, :] = _gla_safe(qf_ref[r, :], kk_f[r], vf_ref[r, :], [p[r] for p in g3_f], stf, rows, False)
            r = rb[k]
            ob_ref[r, :] = _gla_safe(qb_ref[r, :], kk_b[r], vb_ref[r, :], [p[r] for p in g3_b], stb, rows, True)

    @pl.when(jnp.logical_and(c == n_c - 1, is_prompt))
    def _():
        for h in range(HG_HEADS):
            sl = slice(h * LANES, (h + 1) * LANES)
            sf_ref[0, sl, :] = stf[sl, :].T
            sb_ref[0, sl, :] = stb[sl, :].T


def _gla(proj, chunk_tot, s0f, s0b):
    n_p = NP // GLA_BLK
    cp = SEQ // GLA_BLK
    cs = DEC_SEQ // GLA_BLK

    def bwd_blk(i):
        jp = (i // cp) * cp + (cp - 1 - i % cp)
        j = i - n_p
        js = n_p + (j // cs) * cs + (cs - 1 - j % cs)
        return jnp.where(i < n_p, jp, js)

    def req(i):
        return jnp.maximum(i - n_p, 0) // cs

    def preq(i):
        return jnp.minimum(i // cp, BATCH - 1)

    steps = jnp.arange(NT // GLA_BLK, dtype=I32)
    blk_tot = chunk_tot.reshape(NT // GLA_BLK, GLA_CPS, 2).min(axis=1)
    mild = jnp.logical_and(blk_tot[:, 0] >= -GLA_FAST_MAX,
                           blk_tot[bwd_blk(steps), 1] >= -GLA_FAST_MAX).astype(I32)
    hk = HG_HEADS * HG_DK
    grid_spec = pltpu.PrefetchScalarGridSpec(
        num_scalar_prefetch=1,
        grid=(NT // GLA_BLK,),
        in_specs=[
            pl.BlockSpec((GLA_BLK, D), lambda i, m: (i, 0)),
            pl.BlockSpec((GLA_BLK, D), lambda i, m: (i, 1)),
            pl.BlockSpec((GLA_BLK, D), lambda i, m: (i, 3)),
            pl.BlockSpec((GLA_BLK, D), lambda i, m: (bwd_blk(i), 0)),
            pl.BlockSpec((GLA_BLK, D), lambda i, m: (bwd_blk(i), 2)),
            pl.BlockSpec((GLA_BLK, D), lambda i, m: (bwd_blk(i), 3)),
            pl.BlockSpec((1, hk, HG_DV), lambda i, m: (req(i), 0, 0)),
            pl.BlockSpec((1, hk, HG_DV), lambda i, m: (req(i), 0, 0)),
        ],
        out_specs=[
            pl.BlockSpec((GLA_BLK, D), lambda i, m: (i, 0)),
            pl.BlockSpec((GLA_BLK, D), lambda i, m: (bwd_blk(i), 0)),
            pl.BlockSpec((1, hk, HG_DV), lambda i, m: (preq(i), 0, 0)),
            pl.BlockSpec((1, hk, HG_DV), lambda i, m: (preq(i), 0, 0)),
        ],
        scratch_shapes=[pltpu.VMEM((hk, HG_DV), F32), pltpu.VMEM((hk, HG_DV), F32)]
        + [pltpu.VMEM((HG_HEADS, CHUNK, HG_DK), F32)] * 3,
    )
    return pl.pallas_call(
        _gla_kernel,
        grid_spec=grid_spec,
        out_shape=[
            jax.ShapeDtypeStruct((NT, D), F32),
            jax.ShapeDtypeStruct((NT, D), F32),
            jax.ShapeDtypeStruct((BATCH, hk, HG_DV), F32),
            jax.ShapeDtypeStruct((BATCH, hk, HG_DV), F32),
        ],
        compiler_params=_cparams(("arbitrary",)),
        name="gla_scan",
    )(mild, proj, proj, proj, proj, proj, proj, s0f, s0b)


def _head_pair_masks():
    upper = lax.broadcasted_iota(I32, (1, LANES), 1) >= NA_HD
    return upper


def _attn_ctx_kernel(q_ref, k_ref, v_ref, o_ref):
    upper = _head_pair_masks()
    for p in range(NA_HEADS // 2):
        sl = slice(p * LANES, (p + 1) * LANES)
        qp = q_ref[:, sl] * (NA_HD ** -0.5)
        kp = k_ref[:, sl].astype(BF16)
        vp = v_ref[:, sl].astype(BF16)
        outs = []
        for u in range(2):
            qm = jnp.where(upper if u else jnp.logical_not(upper), qp, 0.0).astype(BF16)
            s = _dot_nt(qm, kp)
            e = jnp.exp(s - jnp.max(s, axis=-1, keepdims=True))
            outs.append(_dot(e.astype(BF16), vp) / jnp.sum(e, axis=-1, keepdims=True))
        o_ref[:, sl] = jnp.where(upper, outs[1], outs[0]).astype(o_ref.dtype)


def _attn_ctx(qkv):
    return pl.pallas_call(
        _attn_ctx_kernel,
        grid=(BATCH,),
        in_specs=[
            pl.BlockSpec((SEQ, D), lambda b: (b, 0)),
            pl.BlockSpec((SEQ, D), lambda b: (b, 1)),
            pl.BlockSpec((SEQ, D), lambda b: (b, 2)),
        ],
        out_specs=pl.BlockSpec((SEQ, D), lambda b: (b, 0)),
        out_shape=jax.ShapeDtypeStruct((NP, D), BF16),
        compiler_params=_cparams(("arbitrary",)),
        name="attn_ctx",
    )(qkv, qkv, qkv)


def _attn_lat_kernel(q_ref, k_ref, v_ref, kc_ref, vc_ref, strip_ref, rowmask_ref, o_ref, *, tq):
    rows = DEC_SEQ // GRID_W
    kr = min(WIN_R, rows)
    rpt = tq // GRID_W

    def bias_rows(u, r, k_lo, k_hi):
        first = rows - 1 - r + k_lo
        var = first % 2
        strip = strip_ref[u, var, :, (first - var) * GRID_W:(first - var + k_hi - k_lo) * GRID_W]
        return strip + rowmask_ref[r:r + 1, k_lo * GRID_W:k_hi * GRID_W]

    upper = _head_pair_masks()
    mine = (jnp.logical_not(upper), upper)
    kl = k_ref[...].astype(BF16)
    kc = kc_ref[0].astype(BF16)
    vl = [jnp.where(mine[u], v_ref[...], 1.0).astype(BF16) for u in range(2)]
    vc = [jnp.where(mine[u], vc_ref[0], 1.0).astype(BF16) for u in range(2)]
    for t in range(DEC_SEQ // tq):
        rs = slice(t * tq, (t + 1) * tq)
        k_lo = min(max(t * rpt - kr // 2, 0), rows - kr) // 2 * 2
        k_hi = -(-(min(max(t * rpt + rpt - 1 - kr // 2, 0), rows - kr) + kr) // 2) * 2
        ks = slice(k_lo * GRID_W, k_hi * GRID_W)
        qp = q_ref[rs, :] * (NA_HD ** -0.5)
        outs = []
        qm2 = jnp.concatenate([jnp.where(mine[u], qp, 0.0).astype(BF16) for u in range(2)], axis=0)
        s_l2 = _dot_nt(qm2, kl[ks])
        s_c2 = _dot_nt(qm2, kc)
        for u in range(2):
            bias = jnp.concatenate([bias_rows(u, t * rpt + a, k_lo, k_hi) for a in range(rpt)], axis=0)
            s_l = s_l2[u * tq:(u + 1) * tq] + bias
            s_c = s_c2[u * tq:(u + 1) * tq]
            m = jnp.maximum(jnp.max(s_l, axis=-1, keepdims=True), jnp.max(s_c, axis=-1, keepdims=True))
            acc = (_dot(jnp.exp(s_l - m).astype(BF16), vl[u][ks])
                   + _dot(jnp.exp(s_c - m).astype(BF16), vc[u]))
            outs.append(acc / pltpu.roll(acc, NA_HD, 1))
        o_ref[rs, :] = jnp.where(upper, outs[1], outs[0]).astype(o_ref.dtype)


def _attn_lat(qkv, k_ctx, v_ctx, strips, rowmask):
    npair = NA_HEADS // 2
    return pl.pallas_call(
        functools.partial(_attn_lat_kernel, tq=256),
        grid=(npair, DEC_BATCH),
        in_specs=[
            pl.BlockSpec((DEC_SEQ, LANES), lambda p, b: (b, p)),
            pl.BlockSpec((DEC_SEQ, LANES), lambda p, b: (b, npair + p)),
            pl.BlockSpec((DEC_SEQ, LANES), lambda p, b: (b, 2 * npair + p)),
            pl.BlockSpec((1, PAST_LEN, LANES), lambda p, b: (b, 0, p)),
            pl.BlockSpec((1, PAST_LEN, LANES), lambda p, b: (b, 0, p)),
            pl.BlockSpec((2, 2, GRID_W, STRIP_W), lambda p, b: (p, 0, 0, 0)),
            pl.BlockSpec((DEC_SEQ // GRID_W, DEC_SEQ), lambda p, b: (0, 0)),
        ],
        out_specs=pl.BlockSpec((DEC_SEQ, LANES), lambda p, b: (b, p)),
        out_shape=jax.ShapeDtypeStruct((NS, D), BF16),
        compiler_params=_cparams(("arbitrary", "arbitrary")),
        name="attn_lat",
    )(qkv, qkv, qkv, k_ctx, v_ctx, strips, rowmask)


def _latent_bias_tables(rpb):
    rows = DEC_SEQ // GRID_W
    kr = min(WIN_R, rows)
    ndr, ndc = 2 * WIN_R - 1, 2 * WIN_C - 1
    qc = np.arange(GRID_W)
    kc = np.arange(GRID_W)
    ws = np.clip(qc - WIN_C // 2, 0, GRID_W - WIN_C)
    col_ok = (kc[None, :] >= ws[:, None]) & (kc[None, :] < ws[:, None] + WIN_C)
    dc = np.clip(kc[None, :] - qc[:, None] + WIN_C - 1, 0, ndc - 1)
    onehot = (dc[None, :, :] == np.arange(ndc)[:, None, None]).astype(np.float32)
    t2 = jnp.einsum('hrc,cqk->hqrk', rpb.astype(F32), jnp.asarray(onehot),
                    precision=lax.Precision.HIGHEST)
    t2 = jnp.where(jnp.asarray(col_ok)[None, :, None, :], t2, NEG)
    lead = rows - WIN_R
    n_tiles = 2 * rows - 1

    def neg(n):
        return jnp.full((NA_HEADS, GRID_W, n, GRID_W), NEG, F32)

    strip = jnp.concatenate([neg(lead), t2, neg(n_tiles - lead - ndr)], axis=2)
    strip = strip.reshape(NA_HEADS, GRID_W, n_tiles * GRID_W)

    def pad(a):
        return jnp.pad(a, ((0, 0), (0, 0), (0, STRIP_W - a.shape[-1])), constant_values=NEG)

    strips = jnp.stack([pad(strip), pad(strip[:, :, GRID_W:])], axis=1)
    r = np.arange(rows)
    k0 = np.clip(r - kr // 2, 0, rows - kr)
    krow = np.arange(DEC_SEQ) // GRID_W
    row_ok = (krow[None, :] >= k0[:, None]) & (krow[None, :] < k0[:, None] + kr)
    rowmask = jnp.asarray(np.where(row_ok, 0.0, NEG).astype(np.float32))
    return strips, rowmask


def _hgrn_mix_out(refs, mod_ref, tile):
    of_ref, ob_ref, gate_ref, xp_ref, xs_ref, gn_ref, w_ref = refs
    x = jnp.where(tile < NP // MOE_TM, xp_ref[...], xs_ref[...])
    o = of_ref[...] + ob_ref[...]
    gn = gn_ref[...]
    segs = []
    for h in range(HG_HEADS):
        seg = o[:, h * LANES:(h + 1) * LANES]
        ms = jnp.mean(seg * seg, axis=-1, keepdims=True)
        segs.append(seg * lax.rsqrt(ms + EPS) * gn)
    y = (jnp.concatenate(segs, axis=1) * _silu(gate_ref[...])).astype(BF16)
    return x + mod_ref[0, 2:3, :] * _dot(y, w_ref[...])


def _attn_mix_out(refs, mod_ref, tile):
    ap_ref, as_ref, x_ref, w_ref = refs
    a = jnp.where(tile < NP // MOE_TM, ap_ref[...], as_ref[...])
    return x_ref[...] + mod_ref[0, 2:3, :] * _dot(a, w_ref[...])


def _route_kernel(*refs, mix_out, n_mix):
    mix_refs = refs[:n_mix]
    (nw_ref, mod_ref, wrh_ref, wrl_ref, rb_ref, xnew_ref, xs_ref, pos_ref, wt_ref, npc_ref,
     hb_s, pos_s, used_s) = refs[n_mix:]
    i = pl.program_id(0)
    cur = lax.rem(i, 2)
    tile = jnp.minimum(i, N_TILES - 1)

    @pl.when(i == 0)
    def _():
        hb_s[...] = jnp.zeros_like(hb_s)
        pos_s[...] = jnp.full(pos_s.shape, -1, I32)
        used_s[0] = 0.0
        used_s[1] = 0.0

    live = i < N_TILES

    def step(slot):
        x = mix_out(mix_refs, mod_ref, tile)
        xnew_ref[...] = x
        hb, pos, used = _route_select(x, nw_ref, mod_ref, wrh_ref, wrl_ref, rb_ref, pos_ref, wt_ref, npc_ref)
        hb_s[slot] = hb
        pos_s[slot] = jnp.where(live, pos, -1)
        used_s[slot] = jnp.where(live, used, 0.0)
        _route_sort(xs_ref, hb_s[1 - slot], pos_s[1 - slot], used_s[1 - slot])

    for slot in range(2):
        pl.when(cur == slot)(functools.partial(step, slot))


def _route_sort(xs_ref, hb, pos, used):
    pos16 = [pos[k:k + 1].astype(jnp.int16) for k in range(TOP_K)]

    def fill(r0, n):
        j16 = (lax.broadcasted_iota(I32, (n, MOE_TM), 0) + r0).astype(jnp.int16)
        onehot = jnp.zeros((n, MOE_TM), BF16)
        for k in range(TOP_K):
            onehot = jnp.where(j16 == pos16[k], jnp.ones((), BF16), onehot)
        xs_ref[r0:r0 + n, :] = _dot(onehot, hb).astype(BF16)

    fill(0, ROWS_MAIN)
    for r0 in range(ROWS_MAIN, R1, ROW_BLK):
        pl.when(used > r0)(functools.partial(fill, r0, ROW_BLK))

        @pl.when(used <= r0)
        def _(r0=r0):
            xs_ref[r0:r0 + ROW_BLK, :] = jnp.zeros((ROW_BLK, D), BF16)


def _route_select(x, nw_ref, mod_ref, wrh_ref, wrl_ref, rb_ref, pos_ref, wt_ref, npc_ref):
    h = _norm_mod(x, nw_ref[...], mod_ref[0, 3:4, :], mod_ref[0, 4:5, :])
    hb = h.astype(BF16)
    hl = (h - hb.astype(F32)).astype(BF16)
    wrh = wrh_ref[...]
    logits = _dot_nt(wrh, hb) + _dot_nt(wrh, hl) + _dot_nt(wrl_ref[...], hb)
    scores = jax.nn.sigmoid(logits)
    sel = scores + rb_ref[...]

    gsz = N_EXPERTS // N_GROUPS
    sub = lax.broadcasted_iota(I32, (gsz, MOE_TM), 0)
    ninf = -jnp.inf
    gs_rows = []
    for gi in range(N_GROUPS):
        blk = sel[gi * gsz:(gi + 1) * gsz]
        m1 = jnp.max(blk, axis=0, keepdims=True)
        first = jnp.min(jnp.where(blk == m1, sub, gsz), axis=0, keepdims=True)
        m2 = jnp.max(jnp.where(sub == first, ninf, blk), axis=0, keepdims=True)
        gs_rows.append(m1 + m2)
    cur = jnp.concatenate(gs_rows, axis=0)
    gidx = lax.broadcasted_iota(I32, (N_GROUPS, MOE_TM), 0)
    gsel = jnp.zeros((N_GROUPS, MOE_TM), F32)
    for _ in range(TOPK_GROUPS):
        m = jnp.max(cur, axis=0, keepdims=True)
        first = jnp.min(jnp.where(cur == m, gidx, N_GROUPS), axis=0, keepdims=True)
        hit = gidx == first
        gsel = jnp.where(hit, 1.0, gsel)
        cur = jnp.where(hit, ninf, cur)
    emask = jnp.concatenate(
        [jnp.broadcast_to(gsel[gi:gi + 1], (gsz, MOE_TM)) for gi in range(N_GROUPS)], axis=0)
    masked = jnp.where(emask > 0.5, sel, ninf)
    eidx = lax.broadcasted_iota(I32, (N_EXPERTS, MOE_TM), 0)
    chosen = jnp.zeros((N_EXPERTS, MOE_TM), F32)
    hits, wsel = [], []
    for _ in range(TOP_K):
        m = jnp.max(masked, axis=0, keepdims=True)
        first = jnp.min(jnp.where(masked == m, eidx, N_EXPERTS), axis=0, keepdims=True)
        hit = eidx == first
        hits.append(hit)
        wsel.append(jnp.sum(jnp.where(hit, scores, 0.0), axis=0, keepdims=True))
        chosen = jnp.where(hit, 1.0, chosen)
        masked = jnp.where(hit, ninf, masked)
    wsum = wsel[0]
    for w in wsel[1:]:
        wsum = wsum + w

    n_io = lax.broadcasted_iota(I32, (MOE_TM, MOE_TM), 0)
    m_io = lax.broadcasted_iota(I32, (MOE_TM, MOE_TM), 1)
    earlier = jnp.where(n_io < m_io, 1.0, 0.0).astype(BF16)
    rank = _dot(chosen.astype(BF16), earlier)
    cnt = jnp.sum(chosen, axis=1, keepdims=True)
    npc = jnp.floor((cnt + (PIECE - 1)) * (1.0 / PIECE))
    e_io = lax.broadcasted_iota(I32, (N_EXPERTS, N_EXPERTS), 0)
    f_io = lax.broadcasted_iota(I32, (N_EXPERTS, N_EXPERTS), 1)
    below = jnp.where(f_io < e_io, 1.0, 0.0).astype(BF16)
    npc_l = jnp.broadcast_to(npc, (N_EXPERTS, LANES))
    start = _dot(below, npc_l.astype(BF16))[:, 0:1] * PIECE
    slot = start + rank

    pos_rows, wt_rows = [], []
    for k in range(TOP_K):
        pos_k = jnp.sum(jnp.where(hits[k], slot, 0.0), axis=0, keepdims=True).astype(I32)
        pos_rows.append(pos_k)
        wt_rows.append(wsel[k] / wsum * ROUTED_SCALE)

    pad = SUBLANES - TOP_K
    pos = jnp.concatenate(pos_rows + [jnp.full((pad, MOE_TM), -1, I32)], axis=0)
    pos_ref[...] = pos
    wt_ref[...] = jnp.concatenate(wt_rows + [jnp.zeros((pad, MOE_TM), F32)], axis=0)
    npc_ref[0] = npc_l.astype(I32)
    return hb, pos, jnp.sum(npc) * PIECE


def _hgrn_mix_inputs(o_f, o_b, proj, x_pair, gn_w, w_out):
    n_p = NP // MOE_TM

    def specs(tile):
        return [pl.BlockSpec((MOE_TM, D), lambda i: (tile(i), 0)),
                pl.BlockSpec((MOE_TM, D), lambda i: (tile(i), 0)),
                pl.BlockSpec((MOE_TM, D), lambda i: (tile(i), 4)),
                pl.BlockSpec((MOE_TM, D), lambda i: (jnp.minimum(tile(i), n_p - 1), 0)),
                pl.BlockSpec((MOE_TM, D), lambda i: (jnp.maximum(tile(i) - n_p, 0), 0)),
                pl.BlockSpec((1, HG_DV), lambda i: (0, 0)),
                pl.BlockSpec((D, D), lambda i: (0, 0))]

    return _hgrn_mix_out, (o_f, o_b, proj, x_pair[0], x_pair[1], gn_w.reshape(1, HG_DV), w_out), specs


def _attn_mix_inputs(a_p, a_s, x, w_o):
    n_p = NP // MOE_TM

    def specs(tile):
        return [pl.BlockSpec((MOE_TM, D), lambda i: (jnp.minimum(tile(i), n_p - 1), 0)),
                pl.BlockSpec((MOE_TM, D), lambda i: (jnp.maximum(tile(i) - n_p, 0), 0)),
                pl.BlockSpec((MOE_TM, D), lambda i: (tile(i), 0)),
                pl.BlockSpec((D, D), lambda i: (0, 0))]

    return _attn_mix_out, (a_p, a_s, x, w_o), specs


def _route(mix, nw, mod, wr_hi, wr_lo, rbias):
    mix_out, mix_args, mix_specs = mix

    def tile(i):
        return jnp.minimum(i, N_TILES - 1)

    return pl.pallas_call(
        functools.partial(_route_kernel, mix_out=mix_out, n_mix=len(mix_args)),
        grid=(N_TILES + 2,),
        in_specs=mix_specs(tile) + [
            pl.BlockSpec((1, D), lambda i: (0, 0)),
            pl.BlockSpec((1, 6, D), lambda i: (_mod_group(tile(i), MOE_TM), 0, 0)),
            pl.BlockSpec((N_EXPERTS, D), lambda i: (0, 0)),
            pl.BlockSpec((N_EXPERTS, D), lambda i: (0, 0)),
            pl.BlockSpec((N_EXPERTS, 1), lambda i: (0, 0)),
        ],
        out_specs=[
            pl.BlockSpec((MOE_TM, D), lambda i: (tile(i), 0)),
            pl.BlockSpec((R1, D), lambda i: (jnp.maximum(i - 1, 0), 0)),
            pl.BlockSpec((SUBLANES, MOE_TM), lambda i: (0, tile(i))),
            pl.BlockSpec((SUBLANES, MOE_TM), lambda i: (0, tile(i))),
            pl.BlockSpec((1, N_EXPERTS, LANES), lambda i: (tile(i), 0, 0)),
        ],
        out_shape=[
            jax.ShapeDtypeStruct((NT, D), F32),
            jax.ShapeDtypeStruct((P_TOT * PIECE, D), BF16),
            jax.ShapeDtypeStruct((SUBLANES, NT), I32),
            jax.ShapeDtypeStruct((SUBLANES, NT), F32),
            jax.ShapeDtypeStruct((N_TILES, N_EXPERTS, LANES), I32),
        ],
        scratch_shapes=[pltpu.VMEM((2, MOE_TM, D), BF16), pltpu.VMEM((2, SUBLANES, MOE_TM), I32),
                        pltpu.SMEM((2,), F32)],
        compiler_params=_cparams(("arbitrary",)),
        name="moe_route",
    )(*mix_args, nw.reshape(1, D), mod, wr_hi, wr_lo, rbias.reshape(N_EXPERTS, 1))


def _piece_lists(npc):
    t, e = npc.shape
    hp = lax.Precision.HIGHEST
    npc_t = npc.T.astype(F32)
    start_t = (jnp.cumsum(npc, axis=1) - npc).T.astype(F32)
    tile_end = jnp.cumsum(npc_t, axis=1)
    n_e = tile_end[:, -1]
    pe_end = jnp.cumsum(n_e)
    pe_off = pe_end - n_e
    p = jnp.arange(P_MAX, dtype=F32)
    e_p = jnp.minimum(jnp.sum((pe_end[None, :] <= p[:, None]).astype(I32), axis=1), e - 1)
    oh_e = (e_p[:, None] == jnp.arange(e, dtype=I32)[None, :]).astype(F32)
    tab = jnp.concatenate([tile_end, start_t, npc_t, pe_off[:, None]], axis=1)
    row = jnp.dot(oh_e, tab, precision=hp)
    te_p, st_p, np_p, off_p = row[:, :t], row[:, t:2 * t], row[:, 2 * t:3 * t], row[:, 3 * t]
    local = p - off_p
    t_p = jnp.minimum(jnp.sum((te_p <= local[:, None]).astype(I32), axis=1), t - 1)
    oh_t = t_p[:, None] == jnp.arange(t, dtype=I32)[None, :]

    def pick(a):
        return jnp.sum(jnp.where(oh_t, a, 0.0), axis=1)

    src = t_p * R1B + (pick(st_p) + local - (pick(te_p) - pick(np_p))).astype(I32)
    src = jnp.concatenate([jnp.clip(src, 0, P_MAX - 1), jnp.zeros((G_PIECES,), I32)])

    nch = jnp.floor((n_e + (G_PIECES - 1)) * (1.0 / G_PIECES))
    ch_end = jnp.cumsum(nch)
    ch_off = jnp.concatenate([jnp.zeros((1,), F32), ch_end]).astype(I32)
    c = jnp.arange(-N_SLOTS, NCH + 2, dtype=F32)
    ce = jnp.minimum(jnp.sum((ch_end[None, :] <= c[:, None]).astype(I32), axis=1), e - 1)
    oh_c = (ce[:, None] == jnp.arange(e, dtype=I32)[None, :]).astype(F32)
    crow = jnp.dot(oh_c, jnp.stack([ch_end - nch, pe_off, n_e], axis=1), precision=hp)
    k_in = c - crow[:, 0]
    live = jnp.logical_and(c >= 0, c < ch_end[-1])
    cn = jnp.where(live, jnp.clip(crow[:, 2] - G_PIECES * k_in, 0, G_PIECES), 0.0)
    cs = jnp.where(cn > 0, crow[:, 1] + G_PIECES * k_in, 0.0)
    return src, ch_off, cs.astype(I32), cn.astype(I32)


def _ffn_kernel(src_ref, choff_ref, cs_ref, cn_ref, xs_in, wg_ref, wu_ref, wd_ref, xs_out,
                xbuf, ybuf, wgb, wub, wdb, gsem, ssem):
    e = pl.program_id(0)
    total = choff_ref[N_EXPERTS]

    def start_gather(ch):
        sl = lax.rem(ch + N_SLOTS, N_SLOTS)
        base = cs_ref[ch + N_SLOTS]
        n = cn_ref[ch + N_SLOTS]
        for i in range(G_PIECES):
            idx = src_ref[base + jnp.where(i < n, i, 0)]
            pltpu.make_async_copy(xs_in.at[idx], xbuf.at[sl, i], gsem.at[sl]).start(priority=i % 2)

    def wait_gather(ch):
        sl = lax.rem(ch + N_SLOTS, N_SLOTS)
        pltpu.make_async_copy(xs_in.at[pl.ds(0, G_PIECES)], xbuf.at[sl], gsem.at[sl]).wait()

    def start_scatter(ch):
        sl = lax.rem(ch + N_SLOTS, N_SLOTS)
        base = cs_ref[ch + N_SLOTS]
        n = cn_ref[ch + N_SLOTS]
        for i in range(G_PIECES):
            idx = jnp.where(i < n, src_ref[base + i], P_MAX + sl * G_PIECES + i)
            pltpu.make_async_copy(ybuf.at[sl, i], xs_out.at[idx], ssem.at[sl]).start(priority=i % 2)

    def wait_scatter(ch):
        sl = lax.rem(ch + N_SLOTS, N_SLOTS)
        pltpu.make_async_copy(ybuf.at[sl], xs_out.at[pl.ds(0, G_PIECES)], ssem.at[sl]).wait()

    @pl.when(e == 0)
    def _():
        ybuf[...] = jnp.zeros_like(ybuf)
        start_gather(0)
        start_gather(1)
        start_scatter(-3)
        start_scatter(-2)

    wgb[...] = wg_ref[0].astype(BF16)
    wub[...] = wu_ref[0].astype(BF16)
    wdb[...] = wd_ref[0].astype(BF16)

    def chunk(c, carry):
        sl = lax.rem(c, N_SLOTS)
        wait_gather(c)
        wait_scatter(c - 3)
        x = xbuf[sl].reshape(G_PIECES * PIECE, D)
        hid = (_silu(_dot(x, wgb[...])) * _dot(x, wub[...])).astype(BF16)
        start_gather(c + 2)
        start_scatter(c - 1)
        ybuf[sl] = _dot(hid, wdb[...]).astype(BF16).reshape(G_PIECES, PIECE, D)
        return carry

    lax.fori_loop(choff_ref[e], choff_ref[e + 1], chunk, 0)

    @pl.when(e == N_EXPERTS - 1)
    def _():
        start_scatter(total - 1)
        wait_gather(total)
        wait_gather(total + 1)
        wait_scatter(total - 3)
        wait_scatter(total - 2)
        wait_scatter(total - 1)


def _expert_ffn(xs, lists, layer, w_gate, w_up, w_down):
    src, ch_off, cs, cn = lists
    grid_spec = pltpu.PrefetchScalarGridSpec(
        num_scalar_prefetch=4,
        grid=(N_EXPERTS,),
        in_specs=[
            pl.BlockSpec(memory_space=pl.ANY),
            pl.BlockSpec((None, 1, D, D_EXPERT), lambda e, *_: (layer, e, 0, 0)),
            pl.BlockSpec((None, 1, D, D_EXPERT), lambda e, *_: (layer, e, 0, 0)),
            pl.BlockSpec((None, 1, D_EXPERT, D), lambda e, *_: (layer, e, 0, 0)),
        ],
        out_specs=pl.BlockSpec(memory_space=pl.ANY),
        scratch_shapes=[
            pltpu.VMEM((N_SLOTS, G_PIECES, PIECE, D), BF16),
            pltpu.VMEM((N_SLOTS, G_PIECES, PIECE, D), BF16),
            pltpu.VMEM((D, D_EXPERT), BF16),
            pltpu.VMEM((D, D_EXPERT), BF16),
            pltpu.VMEM((D_EXPERT, D), BF16),
            pltpu.SemaphoreType.DMA((N_SLOTS,)),
            pltpu.SemaphoreType.DMA((N_SLOTS,)),
        ],
    )
    out = pl.pallas_call(
        _ffn_kernel,
        grid_spec=grid_spec,
        out_shape=jax.ShapeDtypeStruct((P_TOT, PIECE, D), BF16),
        input_output_aliases={4: 0},
        compiler_params=_cparams(("arbitrary",)),
        name="moe_ffn",
    )(src, ch_off, cs, cn, xs.reshape(P_TOT, PIECE, D), w_gate, w_up, w_down)
    return out.reshape(P_TOT * PIECE, D)


def _combine_kernel(used_ref, tail_ref, *refs, final):
    del tail_ref
    n_main = ROWS_MAIN // FETCH_BLK
    ys_refs = refs[:n_main + 1]
    (pos_ref, wt_ref, x_ref, nw_ref, mod_ref, wsg_ref, wsu_ref, wsd_ref, fw_ref) = refs[n_main + 1:n_main + 10]
    rest = refs[n_main + 10:]
    o_refs, acc_ref = rest[:-1], rest[-1]
    i = pl.program_id(0)
    x = x_ref[...]
    h = _norm_mod(x, nw_ref[...], mod_ref[0, 3:4, :], mod_ref[0, 4:5, :]).astype(BF16)
    shared = _dot((_silu(_dot(h, wsg_ref[...])) * _dot(h, wsu_ref[...])).astype(BF16), wsd_ref[...])

    pos16 = pos_ref[...].astype(jnp.int16)
    wt16 = wt_ref[...].astype(BF16)

    def block(r0, n):
        j16 = (lax.broadcasted_iota(I32, (MOE_TM, n), 1) + r0).astype(jnp.int16)
        wm = jnp.zeros((MOE_TM, n), BF16)
        for k in range(TOP_K):
            wm = jnp.where(j16 == pos16[:, k:k + 1], wt16[:, k:k + 1], wm)
        ref, off = ys_refs[r0 // FETCH_BLK], r0 % FETCH_BLK
        return _dot(wm, ref[off:off + n, :])

    routed = block(0, FETCH_BLK)
    for m in range(1, n_main):
        routed = routed + block(m * FETCH_BLK, FETCH_BLK)
    acc_ref[...] = routed
    for r0 in range(ROWS_MAIN, R1, ROW_BLK):
        @pl.when(used_ref[i] > r0)
        def _(r0=r0):
            acc_ref[...] += block(r0, ROW_BLK)

    y = x + mod_ref[0, 5:6, :] * (acc_ref[...] + shared)
    if not final:
        o_refs[0][...] = y
        return
    ms = jnp.mean(y * y, axis=-1, keepdims=True)
    y = y * lax.rsqrt(ms + EPS) * fw_ref[...]
    is_prompt = i < NP // MOE_TM

    @pl.when(is_prompt)
    def _():
        o_refs[0][...] = y

    @pl.when(jnp.logical_not(is_prompt))
    def _():
        o_refs[1][...] = y


def _combine(ys, used, pos_t, wt_t, x, nw, mod, wsg, wsu, wsd, fw, final):
    n_p = NP // MOE_TM
    if final:
        out_specs = [pl.BlockSpec((MOE_TM, D), lambda i, u, t: (jnp.minimum(i, n_p - 1), 0)),
                     pl.BlockSpec((MOE_TM, D), lambda i, u, t: (jnp.maximum(i - n_p, 0), 0))]
        out_shape = [jax.ShapeDtypeStruct((NP, D), F32), jax.ShapeDtypeStruct((NS, D), F32)]
    else:
        out_specs = pl.BlockSpec((MOE_TM, D), lambda i, u, t: (i, 0))
        out_shape = jax.ShapeDtypeStruct((NT, D), F32)
    per_tile = R1 // FETCH_BLK
    n_main = ROWS_MAIN // FETCH_BLK
    assert per_tile == n_main + 1
    tiles = jnp.arange(N_TILES, dtype=I32)
    cand = jnp.where(used > ROWS_MAIN, tiles * per_tile + n_main, n_main)
    tail = jnp.max(jnp.where(tiles[None, :] <= tiles[:, None], cand[None, :], n_main), axis=1).astype(I32)
    grid_spec = pltpu.PrefetchScalarGridSpec(
        num_scalar_prefetch=2,
        grid=(N_TILES,),
        in_specs=[pl.BlockSpec((FETCH_BLK, D), functools.partial(lambda m, i, u, t: (i * per_tile + m, 0), m))
                  for m in range(n_main)] + [
            pl.BlockSpec((FETCH_BLK, D), lambda i, u, t: (t[i], 0)),
            pl.BlockSpec((MOE_TM, SUBLANES), lambda i, u, t: (i, 0)),
            pl.BlockSpec((MOE_TM, SUBLANES), lambda i, u, t: (i, 0)),
            pl.BlockSpec((MOE_TM, D), lambda i, u, t: (i, 0)),
            pl.BlockSpec((1, D), lambda i, u, t: (0, 0)),
            pl.BlockSpec((1, 6, D), lambda i, u, t: (_mod_group(i, MOE_TM), 0, 0)),
            pl.BlockSpec((D, D_SHARED), lambda i, u, t: (0, 0)),
            pl.BlockSpec((D, D_SHARED), lambda i, u, t: (0, 0)),
            pl.BlockSpec((D_SHARED, D), lambda i, u, t: (0, 0)),
            pl.BlockSpec((1, D), lambda i, u, t: (0, 0)),
        ],
        out_specs=out_specs,
        scratch_shapes=[pltpu.VMEM((MOE_TM, D), F32)],
    )
    return pl.pallas_call(
        functools.partial(_combine_kernel, final=final),
        grid_spec=grid_spec,
        out_shape=out_shape,
        compiler_params=_cparams(("arbitrary",)),
        name="moe_combine",
    )(used, tail, *([ys] * (n_main + 1)), pos_t, wt_t, x, nw.reshape(1, D), mod, wsg, wsu, wsd, fw.reshape(1, D))


def _moe(mix, nw, mod, w_router, rbias, layer, w_gate, w_up, w_down, ws_gate, ws_up, ws_down, fw, final):
    wr = w_router.astype(F32).T
    wr_hi = wr.astype(BF16)
    wr_lo = (wr - wr_hi.astype(F32)).astype(BF16)
    x, xs, pos, wt, npc = _route(mix, nw, mod, wr_hi, wr_lo, rbias.astype(F32))
    npc = npc[:, :, 0]
    lists = _piece_lists(npc)
    ys = _expert_ffn(xs, lists, layer, w_gate, w_up, w_down)
    used = (npc.sum(axis=1) * PIECE).astype(I32)
    return _combine(ys, used, pos.T, wt.T, x, nw, mod, ws_gate.astype(BF16), ws_up.astype(BF16),
                    ws_down.astype(BF16), fw, final)


def kernel(x_prompt, x_sample, state_hgrn_fwd, state_hgrn_bwd, cache_na_k, cache_na_v, c, c_ctx,
           norm1_w, norm2_w, ada_w, ada_b, hgrn_w_in, hgrn_lb_logits, hgrn_gn_w, hgrn_w_out,
           na_w_qkv, na_rpb, na_w_o, moe_w_router, moe_router_bias, moe_w_gate, moe_w_up, moe_w_down,
           shared_w_gate, shared_w_up, shared_w_down, final_norm_w):
    x = (x_prompt.reshape(NP, D), x_sample.reshape(NS, D))
    cvec = jnp.concatenate([c_ctx[None, :], c, jnp.zeros((N_MOD - 1 - DEC_BATCH, D), F32)], axis=0)
    mod = _modulation(cvec, ada_w, ada_b)
    lb_table = jnp.cumsum(jax.nn.softmax(hgrn_lb_logits.astype(F32), axis=0), axis=0)
    hk = HG_HEADS * HG_DK

    sf = sb = k_c = v_c = None
    for l in range(DEPTH):
        if l % 2 == 0:
            a = l // 2
            proj, chunk_tot = _hgrn_proj(x, norm1_w[l], mod[l], lb_table[l], hgrn_w_in[a].astype(BF16))
            o_f, o_b, sf, sb = _gla(proj, chunk_tot,
                                    state_hgrn_fwd[:, a].reshape(DEC_BATCH, hk, HG_DV),
                                    state_hgrn_bwd[:, a].reshape(DEC_BATCH, hk, HG_DV))
            mix = _hgrn_mix_inputs(o_f, o_b, proj, x, hgrn_gn_w[a], hgrn_w_out[a].astype(BF16))
        else:
            n = l // 2
            w_qkv = na_w_qkv[n].astype(BF16)
            qkv_p, k_p, v_p = _norm_proj(x, norm1_w[l], mod[l], w_qkv, 0, NP // TM, n_copy=2, out_dtype=BF16)
            qkv_s = _norm_proj(x, norm1_w[l], mod[l], w_qkv, NP // TM, NS // TM, out_dtype=BF16)
            att_p = _attn_ctx(qkv_p)
            att_s = _attn_lat(qkv_s, cache_na_k[:, n].reshape(DEC_BATCH, PAST_LEN, D),
                              cache_na_v[:, n].reshape(DEC_BATCH, PAST_LEN, D),
                              *_latent_bias_tables(na_rpb[n]))
            mix = _attn_mix_inputs(att_p, att_s, x, na_w_o[n].astype(BF16))
            k_c = k_p.reshape(BATCH, SEQ, NA_HEADS, NA_HD)
            v_c = v_p.reshape(BATCH, SEQ, NA_HEADS, NA_HD)
        x = _moe(mix, norm2_w[l], mod[l], moe_w_router[l], moe_router_bias[l], l, moe_w_gate, moe_w_up,
                 moe_w_down, shared_w_gate[l], shared_w_up[l], shared_w_down[l], final_norm_w,
                 final=(l == DEPTH - 1))

    y_prompt = x[0].reshape(BATCH, SEQ, D)
    y_sample = x[1].reshape(DEC_BATCH, DEC_SEQ, D)
    new_sf = sf.reshape(BATCH, 1, HG_HEADS, HG_DK, HG_DV)
    new_sb = sb.reshape(BATCH, 1, HG_HEADS, HG_DK, HG_DV)
    return (y_prompt, y_sample, new_sf, new_sb, k_c[:, None], v_c[:, None])
```
